```python
import math
import jax, jax.numpy as jnp
from jax import lax
import numpy as np

D_MODEL = 1024
BATCH = 16
SEQ = 2048
DEPTH = 1

N_META = 16
D_MIX = D_MODEL
N_HEADS = 8
QK_NOPE = 64
QK_ROPE = 32
QK_HEAD = QK_NOPE + QK_ROPE
V_HEAD = 64
D_ATTN = N_HEADS * V_HEAD
Q_LORA = 384
KV_LORA = 256
D_RNN = D_MIX - D_ATTN
RNN_BLOCKS = 8
RNN_BW = D_RNN // RNN_BLOCKS
CONV_W = 4
CONV_PAD = (2, 1)
LRU_C = 8.0
ROPE_THETA = 10000.0
Q_BLOCK = 128
OFF_CQ = Q_LORA
OFF_CKV = OFF_CQ + KV_LORA
OFF_KR = OFF_CKV + QK_ROPE
OFF_XR = OFF_KR + D_RNN
IN_COLS = OFF_XR + D_RNN
D_FF = int(math.ceil(8 * D_MODEL / 3 / 256) * 256)
EPS = 1e-6

kernel_name = "hymba_mla_rglru_hybrid_encoder"


def rms_norm(x, g):
    xf = x.astype(jnp.float32)
    y = xf * lax.rsqrt(jnp.mean(xf * xf, axis=-1, keepdims=True) + EPS)
    return (y * g.astype(jnp.float32)).astype(x.dtype)


def rope(x, pos):
    half = x.shape[-1] // 2
    freqs = 1.0 / (ROPE_THETA ** (jnp.arange(half, dtype=jnp.float32) / half))
    ang = pos[:, None] * freqs[None, :]
    cos = jnp.cos(ang)[None, :, None, :]
    sin = jnp.sin(ang)[None, :, None, :]
    xf = x.astype(jnp.float32)
    x1, x2 = xf[..., :half], xf[..., half:]
    out = jnp.concatenate([x1 * cos - x2 * sin, x1 * sin + x2 * cos], axis=-1)
    return out.astype(x.dtype)


def attend_block(q_blk, k, v):
    s = jnp.einsum('bhqd,bhkd->bhqk', q_blk, k).astype(jnp.float32) * (QK_HEAD ** -0.5)
    p = jax.nn.softmax(s, axis=-1)
    return jnp.einsum('bhqk,bhkd->bhqd', p.astype(v.dtype), v)


def mla_group(c_q, c_kv, k_r, q_a_g, w_uq, kv_a_g, w_ukv, q_g, k_g, pos):
    B, T, _ = c_q.shape
    q = (rms_norm(c_q, q_a_g) @ w_uq).reshape(B, T, N_HEADS, QK_HEAD)
    kv = (rms_norm(c_kv, kv_a_g) @ w_ukv).reshape(B, T, N_HEADS, QK_NOPE + V_HEAD)
    k_nope, v = kv[..., :QK_NOPE], kv[..., QK_NOPE:]
    k = jnp.concatenate([k_nope, jnp.broadcast_to(k_r[:, :, None, :], (B, T, N_HEADS, QK_ROPE))], axis=-1)
    q = rms_norm(q, q_g)
    k = rms_norm(k, k_g)
    q = jnp.concatenate([q[..., :QK_NOPE], rope(q[..., QK_NOPE:], pos)], axis=-1)
    k = jnp.concatenate([k[..., :QK_NOPE], rope(k[..., QK_NOPE:], pos)], axis=-1)
    q = q.transpose(0, 2, 1, 3)
    k = k.transpose(0, 2, 1, 3)
    v = v.transpose(0, 2, 1, 3)
    o_meta = attend_block(q[:, :, :N_META], k, v)
    q_real = q[:, :, N_META:]
    n_blk = q_real.shape[2] // Q_BLOCK
    q_blocks = q_real.reshape(B, N_HEADS, n_blk, Q_BLOCK, QK_HEAD).transpose(2, 0, 1, 3, 4)
    o_blocks = lax.map(lambda qb: attend_block(qb, k, v), q_blocks)
    o_real = o_blocks.transpose(1, 2, 0, 3, 4).reshape(B, N_HEADS, n_blk * Q_BLOCK, V_HEAD)
    o = jnp.concatenate([o_meta, o_real], axis=2)
    return o.transpose(0, 2, 1, 3).reshape(B, T, D_ATTN)


def _linear_combine(c1, c2):
    a1, b1 = c1
    a2, b2 = c2
    return a1 * a2, a2 * b1 + b2


def rg_lru(xc, wa, ba, wi, bi, lam, reverse):
    B, T, _ = xc.shape
    xg = xc.reshape(B, T, RNN_BLOCKS, RNN_BW)
    r = jax.nn.sigmoid((jnp.einsum('btgi,gij->btgj', xg, wa).reshape(B, T, D_RNN) + ba).astype(jnp.float32))
    i = jax.nn.sigmoid((jnp.einsum('btgi,gij->btgj', xg, wi).reshape(B, T, D_RNN) + bi).astype(jnp.float32))
    log_a = -LRU_C * r * jax.nn.softplus(-lam.astype(jnp.float32))
    a = jnp.exp(log_a)
    b = jnp.sqrt(jnp.maximum(-jnp.expm1(2.0 * log_a), 0.0)) * (i * xc.astype(jnp.float32))
    _, h = lax.associative_scan(_linear_combine, (a, b), axis=1, reverse=reverse)
    return h.astype(xc.dtype)


def rglru_group(x_r, x_gate, conv_w, conv_b, wa, ba, wi, bi, lam):
    xc = lax.conv_general_dilated(
        x_r, conv_w[:, None, :], window_strides=(1,), padding=[CONV_PAD],
        dimension_numbers=('NWC', 'WIO', 'NWC'), feature_group_count=D_RNN) + conv_b
    y = rg_lru(xc, wa[0], ba[0], wi[0], bi[0], lam[0], reverse=False) \
        + rg_lru(xc, wa[1], ba[1], wi[1], bi[1], lam[1], reverse=True)
    return y * jax.nn.gelu(x_gate)


def setup_inputs(seed: int = 0) -> dict:
    key = jax.random.key(seed)
    ks = iter(jax.random.split(key, 40))
    L = DEPTH
    f32 = jnp.float32

    def nrm(shape, fan_in):
        return jax.random.normal(next(ks), shape, f32) * (fan_in ** -0.5)

    def gain(shape):
        return 1.0 + 0.02 * jax.random.normal(next(ks), shape, f32)

    def bias(shape):
        return 0.01 * jax.random.normal(next(ks), shape, f32)

    x = jax.random.normal(next(ks), (BATCH, SEQ, D_MODEL), f32)
    meta_tokens = jax.random.normal(next(ks), (N_META, D_MODEL), f32)
    u = jax.random.uniform(next(ks), (L, 2, D_RNN), f32, 0.9, 0.999)
    s = u ** (1.0 / LRU_C)
    lru_lambda = jnp.log(s) - jnp.log1p(-s)
    return {
        "x": x,
        "meta_tokens": meta_tokens,
        "ln1_g": gain((L, D_MODEL)),
        "w_in": nrm((L, D_MODEL, IN_COLS), D_MODEL),
        "q_a_norm_g": gain((L, Q_LORA)),
        "w_uq": nrm((L, Q_LORA, N_HEADS * QK_HEAD), Q_LORA),
        "kv_a_norm_g": gain((L, KV_LORA)),
        "w_ukv": nrm((L, KV_LORA, N_HEADS * (QK_NOPE + V_HEAD)), KV_LORA),
        "q_norm_g": gain((L, QK_HEAD)),
        "k_norm_g": gain((L, QK_HEAD)),
        "conv_w": nrm((L, CONV_W, D_RNN), CONV_W),
        "conv_b": bias((L, D_RNN)),
        "lru_wa": nrm((L, 2, RNN_BLOCKS, RNN_BW, RNN_BW), RNN_BW),
        "lru_ba": bias((L, 2, D_RNN)),
        "lru_wi": nrm((L, 2, RNN_BLOCKS, RNN_BW, RNN_BW), RNN_BW),
        "lru_bi": bias((L, 2, D_RNN)),
        "lru_lambda": lru_lambda,
        "attn_out_g": gain((L, D_ATTN)),
        "rnn_out_g": gain((L, D_RNN)),
        "w_out": nrm((L, D_MIX, D_MODEL), D_MIX),
        "ln2_g": gain((L, D_MODEL)),
        "w_gate": nrm((L, D_MODEL, D_FF), D_MODEL),
        "w_up": nrm((L, D_MODEL, D_FF), D_MODEL),
        "w_down": nrm((L, D_FF, D_MODEL), D_FF),
    }


def reference(x, meta_tokens, ln1_g, w_in, q_a_norm_g, w_uq, kv_a_norm_g, w_ukv,
              q_norm_g, k_norm_g, conv_w, conv_b, lru_wa, lru_ba, lru_wi, lru_bi,
              lru_lambda, attn_out_g, rnn_out_g, w_out, ln2_g, w_gate, w_up, w_down):
    B = x.shape[0]
    meta = jnp.broadcast_to(meta_tokens[None].astype(x.dtype), (B, N_META, x.shape[-1]))
    h = jnp.concatenate([meta, x], axis=1)
    T = h.shape[1]
    pos = jnp.arange(T, dtype=jnp.float32)
    for l in range(DEPTH):
        hn = rms_norm(h, ln1_g[l])
        p = hn @ w_in[l]
        c_q = p[..., :OFF_CQ]
        c_kv = p[..., OFF_CQ:OFF_CKV]
        k_r = p[..., OFF_CKV:OFF_KR]
        x_r = p[..., OFF_KR:OFF_XR]
        x_gate = p[..., OFF_XR:]
        o_attn = mla_group(c_q, c_kv, k_r, q_a_norm_g[l], w_uq[l], kv_a_norm_g[l],
                           w_ukv[l], q_norm_g[l], k_norm_g[l], pos)
        o_rnn = rglru_group(x_r, x_gate, conv_w[l], conv_b[l], lru_wa[l], lru_ba[l],
                            lru_wi[l], lru_bi[l], lru_lambda[l])
        mix = jnp.concatenate([rms_norm(o_attn, attn_out_g[l]), rms_norm(o_rnn, rnn_out_g[l])], axis=-1)
        h = h + mix @ w_out[l]
        hn = rms_norm(h, ln2_g[l])
        h = h + (jax.nn.silu(hn @ w_gate[l]) * (hn @ w_up[l])) @ w_down[l]
    return h[:, N_META:]
```

```python
import functools
import math

import jax
import jax.numpy as jnp
from jax import lax
from jax.experimental import pallas as pl
from jax.experimental.pallas import tpu as pltpu

F32 = jnp.float32
BF16 = jnp.bfloat16

N_META_TOK = 16
N_HEAD = 8
NOPE = 64
ROPE = 32
HEAD_QK = NOPE + ROPE
HEAD_V = 64
LRU_SCALE = 8.0
THETA = 10000.0
NORM_EPS = 1e-6
CONV_TAPS = 4
CONV_LEFT = 2

LANES = 128
SUBLANES = 8
VMEM_LIMIT = 56 * 1024 * 1024

LOG2E = 1.4426950408889634


def _rsqrt_mean(x, n):
    return lax.rsqrt(jnp.sum(x * x, axis=-1, keepdims=True) * (1.0 / n) + NORM_EPS)


def _proj_kernel(x_ref, ln1_ref, win_ref, qag_ref, wuq_ref, kvag_ref, wuk_ref, wuv_ref,
                 cos_ref, sin_ref, gq_ref, gk_ref,
                 q_out, k_out, v_out, xr_out, xg_out, *, q_lora, kv_lora, d_rnn):
    x = x_ref[0]
    d_model = x.shape[-1]
    hn = (x * _rsqrt_mean(x, d_model) * ln1_ref[...]).astype(BF16)
    p = jnp.dot(hn, win_ref[...], preferred_element_type=F32)
    o_kv = q_lora
    o_kr = o_kv + kv_lora
    o_xr = o_kr + LANES
    o_xg = o_xr + d_rnn
    c_q = p[:, :o_kv]
    c_kv = p[:, o_kv:o_kr]
    kr = p[:, o_kr:o_xr]
    n_slab = d_rnn // LANES
    for s in range(n_slab):
        xr_out[0, s] = p[:, o_xr + s * LANES:o_xr + (s + 1) * LANES]
        xg_out[0, s] = p[:, o_xg + s * LANES:o_xg + (s + 1) * LANES]

    lane = lax.broadcasted_iota(jnp.int32, (1, LANES), 1)
    qk_mask = (lane < HEAD_QK).astype(F32)
    cos_t = cos_ref[...]
    sin_t = sin_ref[...]
    inv_hd = 1.0 / HEAD_QK

    cqn = (c_q * _rsqrt_mean(c_q, q_lora) * qag_ref[...]).astype(BF16)
    q = jnp.dot(cqn, wuq_ref[...], preferred_element_type=F32)
    cq = cos_t * gq_ref[0:1, :]
    sq = sin_t * gq_ref[1:2, :]
    q_scale = (HEAD_QK ** -0.5) * LOG2E
    for h in range(N_HEAD):
        qh = q[:, h * LANES:(h + 1) * LANES]
        ss = jnp.sum(qh * qh * qk_mask, axis=-1, keepdims=True)
        sc = lax.rsqrt(ss * inv_hd + NORM_EPS) * q_scale
        qr = pltpu.roll(qh, LANES - ROPE, axis=1)
        q_out[0, :, h * LANES:(h + 1) * LANES] = ((qh * cq + qr * sq) * sc).astype(BF16)

    ckn = (c_kv * _rsqrt_mean(c_kv, kv_lora) * kvag_ref[...]).astype(BF16)
    kn = jnp.dot(ckn, wuk_ref[...], preferred_element_type=F32)
    v_out[0] = jnp.dot(ckn, wuv_ref[...], preferred_element_type=F32).astype(BF16)
    ck = cos_t * gk_ref[0:1, :]
    sk = sin_t * gk_ref[1:2, :]
    ss_r = jnp.sum(kr * kr * qk_mask, axis=-1, keepdims=True)
    kr_rot = kr * ck + pltpu.roll(kr, LANES - ROPE, axis=1) * sk
    for h in range(N_HEAD):
        kh = kn[:, h * LANES:(h + 1) * LANES]
        ss = jnp.sum(kh * kh, axis=-1, keepdims=True) + ss_r
        sc = lax.rsqrt(ss * inv_hd + NORM_EPS)
        k_out[0, :, h * LANES:(h + 1) * LANES] = ((kh * ck + kr_rot) * sc).astype(BF16)


def _const_spec(shape):
    nd = len(shape)
    return pl.BlockSpec(shape, lambda *_: (0,) * nd)


def _project(x3, tm, cos_t, sin_t, wts, dims):
    q_lora, kv_lora, d_rnn = dims
    b, s, d = x3.shape
    n_slab = d_rnn // LANES
    nt = s // tm
    ln1, win, qag, wuq, kvag, wuk, wuv, gq, gk = wts
    row3 = lambda i, j: (i, j, 0)
    row4 = lambda i, j: (i, 0, j, 0)
    tab = lambda i, j: (j, 0)
    in_specs = [pl.BlockSpec((1, tm, d), row3)]
    in_specs += [_const_spec(w.shape) for w in (ln1, win, qag, wuq, kvag, wuk, wuv)]
    in_specs += [pl.BlockSpec((tm, LANES), tab), pl.BlockSpec((tm, LANES), tab)]
    in_specs += [_const_spec(gq.shape), _const_spec(gk.shape)]
    out_shape = (
        jax.ShapeDtypeStruct((b, s, N_HEAD * LANES), BF16),
        jax.ShapeDtypeStruct((b, s, N_HEAD * LANES), BF16),
        jax.ShapeDtypeStruct((b, s, N_HEAD * HEAD_V), BF16),
        jax.ShapeDtypeStruct((b, n_slab, s, LANES), F32),
        jax.ShapeDtypeStruct((b, n_slab, s, LANES), F32),
    )
    out_specs = (
        pl.BlockSpec((1, tm, N_HEAD * LANES), row3),
        pl.BlockSpec((1, tm, N_HEAD * LANES), row3),
        pl.BlockSpec((1, tm, N_HEAD * HEAD_V), row3),
        pl.BlockSpec((1, n_slab, tm, LANES), row4),
        pl.BlockSpec((1, n_slab, tm, LANES), row4),
    )
    return pl.pallas_call(
        functools.partial(_proj_kernel, q_lora=q_lora, kv_lora=kv_lora, d_rnn=d_rnn),
        grid=(b, nt),
        in_specs=in_specs,
        out_specs=out_specs,
        out_shape=out_shape,
        compiler_params=pltpu.CompilerParams(
            dimension_semantics=("arbitrary", "arbitrary"), vmem_limit_bytes=VMEM_LIMIT),
        name="proj",
    )(x3, ln1, win, qag, wuq, kvag, wuk, wuv, cos_t, sin_t, gq, gk)


def _scan_stride(t_len):
    s = -(-t_len // SUBLANES)
    while s % SUBLANES == 0:
        s += 1
    if s % 2:
        s += 1
        if s % SUBLANES == 0:
            s += 2
    return s


def _chunk_len(stride, cap=48):
    for c in range(min(cap, stride), 0, -1):
        if stride % c == 0:
            return c
    return 1


def _rnn_kernel(xr_ref, xg_ref, xrm_ref, cw_ref, cb_ref, wg_ref, ba_ref, bi_ref, lam_ref, g_ref,
                o_ref, t1, af, bf, ab, bb, *, seq, stride, rc, tm_out):
    n_slab = xr_ref.shape[1]
    t_len = N_META_TOK + seq
    pre = SUBLANES
    rows = t1.shape[1]
    n_chunk = stride // rc
    crow = rc * SUBLANES

    for s in range(n_slab):
        t1[s, 0:pre, :] = jnp.zeros((pre, LANES), F32)
        t1[s, pre:pre + N_META_TOK, :] = xrm_ref[0, s]
        t1[s, pre + N_META_TOK:pre + t_len, :] = xr_ref[0, s]
        t1[s, pre + t_len:rows, :] = jnp.zeros((rows - pre - t_len, LANES), F32)

    lam = lam_ref[...]
    nl = -lam
    softplus = jnp.maximum(nl, 0.0) + jnp.log(1.0 + jnp.exp(-jnp.abs(nl)))
    cn = LRU_SCALE * softplus
    c2 = -cn * LOG2E
    cw = cw_ref[...]
    cb = cb_ref[...]
    ba = ba_ref[...]
    bi = bi_ref[...]

    ridx = lax.broadcasted_iota(jnp.int32, (crow, LANES), 0)
    t_local = (ridx & (SUBLANES - 1)) * stride + (ridx >> 3)

    def gate_chunk(c, carry):
        r0 = c * rc
        row0 = pl.multiple_of(c * crow, SUBLANES)
        valid = (t_local + r0) < t_len
        for s in range(n_slab):
            ls = slice(s * LANES, (s + 1) * LANES)
            taps = [t1[s, pl.ds(r0 + pre - CONV_LEFT + j, SUBLANES, stride=stride), :]
                    for j in range(rc + CONV_TAPS - 1)]
            xc_rows = []
            for rr in range(rc):
                acc = taps[rr] * cw[0:1, ls]
                for j in range(1, CONV_TAPS):
                    acc = acc + taps[rr + j] * cw[j:j + 1, ls]
                xc_rows.append(acc + cb[:, ls])
            xc = jnp.concatenate(xc_rows, axis=0)
            z = jnp.dot(xc.astype(BF16), wg_ref[s], preferred_element_type=F32)
            for d, (a_ref, b_ref) in enumerate(((af, bf), (ab, bb))):
                za = z[:, (2 * d) * LANES:(2 * d + 1) * LANES] + ba[d:d + 1, ls]
                zi = z[:, (2 * d + 1) * LANES:(2 * d + 2) * LANES] + bi[d:d + 1, ls]
                r = 0.5 * jnp.tanh(0.5 * za) + 0.5
                ig = 0.5 * jnp.tanh(0.5 * zi) + 0.5
                a = jnp.exp2(r * c2[d:d + 1, ls])
                one_m_a2 = jnp.tanh(r * cn[d:d + 1, ls]) * (1.0 + a * a)
                bv = jnp.sqrt(jnp.maximum(one_m_a2, 0.0)) * (ig * xc)
                if d == 1:
                    bv = jnp.where(valid, bv, 0.0)
                a_ref[s, pl.ds(row0, crow), :] = a
                b_ref[s, pl.ds(row0, crow), :] = bv
        return carry

    lax.fori_loop(0, n_chunk, gate_chunk, 0)

    zero = jnp.zeros((SUBLANES, LANES), F32)
    one = jnp.ones((SUBLANES, LANES), F32)

    def scan_step(r, carry):
        hf, pf, hb, pb = carry
        rf = pl.multiple_of(r * SUBLANES, SUBLANES)
        rb = pl.multiple_of((stride - 1 - r) * SUBLANES, SUBLANES)
        nhf, npf, nhb, npb = [], [], [], []
        for s in range(n_slab):
            a = af[s, pl.ds(rf, SUBLANES), :]
            h = a * hf[s] + bf[s, pl.ds(rf, SUBLANES), :]
            pr = a * pf[s]
            bf[s, pl.ds(rf, SUBLANES), :] = h
            af[s, pl.ds(rf, SUBLANES), :] = pr
            nhf.append(h)
            npf.append(pr)
            a = ab[s, pl.ds(rb, SUBLANES), :]
            h = a * hb[s] + bb[s, pl.ds(rb, SUBLANES), :]
            pr = a * pb[s]
            bb[s, pl.ds(rb, SUBLANES), :] = h
            ab[s, pl.ds(rb, SUBLANES), :] = pr
            nhb.append(h)
            npb.append(pr)
        return tuple(nhf), tuple(npf), tuple(nhb), tuple(npb)

    init = ((zero,) * n_slab, (one,) * n_slab, (zero,) * n_slab, (one,) * n_slab)
    hf, pf, hb, pb = lax.fori_loop(0, stride, scan_step, init, unroll=2)

    sub = lax.broadcasted_iota(jnp.int32, (SUBLANES, LANES), 0)
    cf, cbk = [], []
    for s in range(n_slab):
        c = zero
        for _ in range(SUBLANES - 1):
            c = jnp.where(sub == 0, 0.0, pltpu.roll(pf[s] * c + hf[s], 1, axis=0))
        cf.append(c)
        c = zero
        for _ in range(SUBLANES - 1):
            c = jnp.where(sub == SUBLANES - 1, 0.0, pltpu.roll(pb[s] * c + hb[s], SUBLANES - 1, axis=0))
        cbk.append(c)

    def combine(r, carry):
        rp = pl.multiple_of(r * SUBLANES, SUBLANES)
        for s in range(n_slab):
            y = (bf[s, pl.ds(rp, SUBLANES), :] + af[s, pl.ds(rp, SUBLANES), :] * cf[s]
                 + bb[s, pl.ds(rp, SUBLANES), :] + ab[s, pl.ds(rp, SUBLANES), :] * cbk[s])
            t1[s, pl.ds(r + pre, SUBLANES, stride=stride), :] = y
        return carry

    lax.fori_loop(0, stride, combine, 0, unroll=2)

    d_rnn = n_slab * LANES
    g = g_ref[...]
    k0 = math.sqrt(2.0 / math.pi)

    def finish(j, carry):
        ro = pl.multiple_of(j * tm_out, 2 * SUBLANES)
        rt = pl.multiple_of(j * tm_out + pre + N_META_TOK, SUBLANES)
        ys = []
        ssq = jnp.zeros((tm_out, 1), F32)
        for s in range(n_slab):
            y = t1[s, pl.ds(rt, tm_out), :]
            xg = xg_ref[0, s, pl.ds(ro, tm_out), :]
            gl = 0.5 * xg * (1.0 + jnp.tanh(k0 * (xg + 0.044715 * (xg * xg * xg))))
            y = y * gl
            ssq = ssq + jnp.sum(y * y, axis=-1, keepdims=True)
            ys.append(y)
        sc = lax.rsqrt(ssq * (1.0 / d_rnn) + NORM_EPS)
        for s in range(n_slab):
            ls = slice(s * LANES, (s + 1) * LANES)
            o_ref[0, pl.ds(ro, tm_out), ls] = (ys[s] * sc * g[:, ls]).astype(BF16)
        return carry

    lax.fori_loop(0, seq // tm_out, finish, 0)


def _rnn(xr, xg, xrm, cw, cb, wg, ba, bi, lam, g):
    b, n_slab, seq, _ = xr.shape
    t_len = N_META_TOK + seq
    stride = _scan_stride(t_len)
    rc = _chunk_len(stride)
    rows_perm = stride * SUBLANES
    rows_time = SUBLANES + rows_perm + SUBLANES
    tm_out = 256 if seq % 256 == 0 else seq
    blk = lambda i: (i, 0, 0, 0)
    in_specs = [
        pl.BlockSpec((1, n_slab, seq, LANES), blk),
        pl.BlockSpec((1, n_slab, seq, LANES), blk),
        _const_spec(xrm.shape),
    ] + [_const_spec(w.shape) for w in (cw, cb, wg, ba, bi, lam, g)]
    scratch = [pltpu.VMEM((n_slab, rows_time, LANES), F32)]
    scratch += [pltpu.VMEM((n_slab, rows_perm, LANES), F32) for _ in range(4)]
    return pl.pallas_call(
        functools.partial(_rnn_kernel, seq=seq, stride=stride, rc=rc, tm_out=tm_out),
        grid=(b,),
        in_specs=in_specs,
        out_specs=pl.BlockSpec((1, seq, n_slab * LANES), lambda i: (i, 0, 0)),
        out_shape=jax.ShapeDtypeStruct((b, seq, n_slab * LANES), BF16),
        scratch_shapes=scratch,
        compiler_params=pltpu.CompilerParams(
            dimension_semantics=("arbitrary",), vmem_limit_bytes=VMEM_LIMIT),
        name="rnn",
    )(xr, xg, xrm, cw, cb, wg, ba, bi, lam, g)


def _attn_kernel(q_ref, k_ref, v_ref, km_ref, vm_ref, o_ref):
    nt = (((1,), (1,)), ((), ()))
    v = v_ref[0]
    vm = vm_ref[0]
    lane = lax.broadcasted_iota(jnp.int32, (1, LANES), 1)
    outs = []
    for hh in range(2):
        ls = slice(hh * LANES, (hh + 1) * LANES)
        q = q_ref[0, :, ls]
        s = lax.dot_general(q, k_ref[0, :, ls], nt, preferred_element_type=F32)
        sm = lax.dot_general(q, km_ref[0, :, ls], nt, preferred_element_type=F32)
        m = jnp.maximum(jnp.max(s, axis=-1, keepdims=True), jnp.max(sm, axis=-1, keepdims=True))
        p = jnp.exp2(s - m)
        pm = jnp.exp2(sm - m)
        l = jnp.sum(p, axis=-1, keepdims=True) + jnp.sum(pm, axis=-1, keepdims=True)
        o = jnp.dot(p.astype(BF16), v, preferred_element_type=F32)
        o = o + jnp.dot(pm.astype(BF16), vm, preferred_element_type=F32)
        outs.append(o * (1.0 / l))
    o_ref[0] = jnp.where(lane < HEAD_V, outs[0], outs[1])


def _attention(q, k, v, km, vm, tq):
    b, s, _ = q.shape
    n_pair = N_HEAD // 2
    return pl.pallas_call(
        _attn_kernel,
        grid=(b, n_pair, s // tq),
        in_specs=[
            pl.BlockSpec((1, tq, 2 * LANES), lambda i, p, j: (i, j, p)),
            pl.BlockSpec((1, s, 2 * LANES), lambda i, p, j: (i, 0, p)),
            pl.BlockSpec((1, s, 2 * HEAD_V), lambda i, p, j: (i, 0, p)),
            pl.BlockSpec((1, N_META_TOK, 2 * LANES), lambda i, p, j: (0, 0, p)),
            pl.BlockSpec((1, N_META_TOK, 2 * HEAD_V), lambda i, p, j: (0, 0, p)),
        ],
        out_specs=pl.BlockSpec((1, tq, 2 * HEAD_V), lambda i, p, j: (i, j, p)),
        out_shape=jax.ShapeDtypeStruct((b, s, N_HEAD * HEAD_V), F32),
        compiler_params=pltpu.CompilerParams(
            dimension_semantics=("arbitrary", "arbitrary", "arbitrary"),
            vmem_limit_bytes=VMEM_LIMIT),
        name="attn",
    )(q, k, v, km, vm)


def _out_kernel(x_ref, oa_ref, on_ref, ag_ref, woa_ref, wor_ref, ln2_ref, wg_ref, wu_ref, wd_ref,
                o_ref, *, ff_chunk):
    oa = oa_ref[...]
    oan = (oa * _rsqrt_mean(oa, oa.shape[-1]) * ag_ref[...]).astype(BF16)
    h = x_ref[...] + jnp.dot(oan, woa_ref[...], preferred_element_type=F32)
    h = h + jnp.dot(on_ref[...], wor_ref[...], preferred_element_type=F32)
    hn = (h * _rsqrt_mean(h, h.shape[-1]) * ln2_ref[...]).astype(BF16)
    d_ff = wg_ref.shape[1]
    acc = h
    for c in range(d_ff // ff_chunk):
        cs = slice(c * ff_chunk, (c + 1) * ff_chunk)
        gt = jnp.dot(hn, wg_ref[:, cs], preferred_element_type=F32)
        up = jnp.dot(hn, wu_ref[:, cs], preferred_element_type=F32)
        act = (gt * jax.nn.sigmoid(gt) * up).astype(BF16)
        acc = acc + jnp.dot(act, wd_ref[cs, :], preferred_element_type=F32)
    o_ref[...] = acc


def _out_ffn(x2, oa2, on2, ag, woa, wor, ln2, wg, wu, wd, tm):
    n, d = x2.shape
    d_ff = wg.shape[1]
    ff_chunk = d_ff
    row = lambda i: (i, 0)
    single = pl.Buffered(1)
    wspec = lambda w: pl.BlockSpec(w.shape, lambda i: (0, 0))
    return pl.pallas_call(
        functools.partial(_out_kernel, ff_chunk=ff_chunk),
        grid=(n // tm,),
        in_specs=[
            pl.BlockSpec((tm, d), row),
            pl.BlockSpec((tm, oa2.shape[1]), row),
            pl.BlockSpec((tm, on2.shape[1]), row),
            wspec(ag), wspec(woa), wspec(wor), wspec(ln2), wspec(wg), wspec(wu), wspec(wd),
        ],
        out_specs=pl.BlockSpec((tm, d), row),
        out_shape=jax.ShapeDtypeStruct((n, d), F32),
        compiler_params=pltpu.CompilerParams(
            dimension_semantics=("arbitrary",), vmem_limit_bytes=VMEM_LIMIT),
        name="out_ffn",
    )(x2, oa2, on2, ag, woa, wor, ln2, wg, wu, wd)


def _rope_tables(t_len):
    half = ROPE // 2
    freqs = 1.0 / (THETA ** (jnp.arange(half, dtype=F32) / half))
    ang = jnp.arange(t_len, dtype=F32)[:, None] * freqs[None, :]
    cos2 = jnp.concatenate([jnp.cos(ang)] * 2, axis=-1)
    sin2 = jnp.concatenate([jnp.sin(ang)] * 2, axis=-1)
    pad = LANES - HEAD_QK
    cos_t = jnp.concatenate([jnp.ones((t_len, NOPE), F32), cos2, jnp.zeros((t_len, pad), F32)], -1)
    sin_t = jnp.concatenate([jnp.zeros((t_len, NOPE), F32), sin2, jnp.zeros((t_len, pad), F32)], -1)
    return cos_t, sin_t


def _rot_half_cols(w):
    half = ROPE // 2
    return jnp.concatenate([-w[..., half:], w[..., :half]], axis=-1)


def _gain_rows(g):
    half = ROPE // 2
    pad = jnp.zeros((LANES - HEAD_QK,), F32)
    r0 = jnp.concatenate([g, pad])
    r1 = jnp.concatenate([jnp.zeros((NOPE,), F32), g[NOPE + half:], g[NOPE:NOPE + half], pad])
    return jnp.stack([r0, r1])


def _block_diag_pairs(w):
    g, n, _ = w.shape
    z = jnp.zeros((g // 2, n, n), w.dtype)
    top = jnp.concatenate([w[0::2], z], axis=2)
    bot = jnp.concatenate([z, w[1::2]], axis=2)
    return jnp.concatenate([top, bot], axis=1)


def kernel(x, meta_tokens, ln1_g, w_in, q_a_norm_g, w_uq, kv_a_norm_g, w_ukv, q_norm_g, k_norm_g,
           conv_w, conv_b, lru_wa, lru_ba, lru_wi, lru_bi, lru_lambda, attn_out_g, rnn_out_g,
           w_out, ln2_g, w_gate, w_up, w_down):
    bsz, seq, d_model = x.shape
    q_lora = q_a_norm_g.shape[-1]
    kv_lora = kv_a_norm_g.shape[-1]
    d_rnn = conv_w.shape[-1]
    d_attn = N_HEAD * HEAD_V
    t_len = N_META_TOK + seq
    l = 0

    wi = w_in[l]
    o_kr = q_lora + kv_lora
    w_kr = wi[:, o_kr:o_kr + ROPE]
    win = jnp.concatenate(
        [wi[:, :o_kr], jnp.zeros((d_model, NOPE), F32), w_kr, _rot_half_cols(w_kr),
         wi[:, o_kr + ROPE:]], axis=1).astype(BF16)
    wq = w_uq[l].reshape(q_lora, N_HEAD, HEAD_QK)
    wuq = jnp.concatenate([wq, _rot_half_cols(wq[..., NOPE:])], axis=-1)
    wuq = wuq.reshape(q_lora, N_HEAD * LANES).astype(BF16)
    wkv = w_ukv[l].reshape(kv_lora, N_HEAD, NOPE + HEAD_V)
    wuk = jnp.concatenate([wkv[..., :NOPE], jnp.zeros((kv_lora, N_HEAD, LANES - NOPE), F32)], -1)
    wuk = wuk.reshape(kv_lora, N_HEAD * LANES).astype(BF16)
    wuv = wkv[..., NOPE:].reshape(kv_lora, d_attn).astype(BF16)
    gq = _gain_rows(q_norm_g[l])
    gk = _gain_rows(k_norm_g[l])
    cos_t, sin_t = _rope_tables(t_len)
    proj_w = (ln1_g[l][None], win, q_a_norm_g[l][None], wuq, kv_a_norm_g[l][None], wuk, wuv, gq, gk)
    dims = (q_lora, kv_lora, d_rnn)

    wgate = jnp.concatenate(
        [_block_diag_pairs(w) for w in (lru_wa[l, 0], lru_wi[l, 0], lru_wa[l, 1], lru_wi[l, 1])],
        axis=2).astype(BF16)

    tm = 512 if seq % 512 == 0 else seq
    q, k, v, xr, xg = _project(x, tm, cos_t[N_META_TOK:], sin_t[N_META_TOK:], proj_w, dims)
    _, km, vm, xrm, _ = _project(meta_tokens[None].astype(x.dtype), N_META_TOK,
                                 cos_t[:N_META_TOK], sin_t[:N_META_TOK], proj_w, dims)

    o_rnn = _rnn(xr, xg, xrm, conv_w[l], conv_b[l][None], wgate, lru_ba[l], lru_bi[l],
                 lru_lambda[l], rnn_out_g[l][None])

    tq = 512 if seq % 512 == 0 else seq
    o_attn = _attention(q, k, v, km, vm, tq)

    wo = w_out[l].astype(BF16)
    n_rows = bsz * seq
    tmo = 256 if n_rows % 256 == 0 else n_rows
    out = _out_ffn(x.reshape(n_rows, d_model), o_attn.reshape(n_rows, d_attn),
                   o_rnn.reshape(n_rows, d_rnn), attn_out_g[l][None], wo[:d_attn], wo[d_attn:],
                   ln2_g[l][None], w_gate[l].astype(BF16), w_up[l].astype(BF16),
                   w_down[l].astype(BF16), tmo)
    return out.reshape(bsz, seq, d_model)
```

```python
import functools
import math

import jax
import jax.numpy as jnp
from jax import lax
from jax.experimental import pallas as pl
from jax.experimental.pallas import tpu as pltpu

F32 = jnp.float32
BF16 = jnp.bfloat16

N_META_TOK = 16
N_HEAD = 8
NOPE = 64
ROPE = 32
HEAD_QK = NOPE + ROPE
HEAD_V = 64
LRU_SCALE = 8.0
THETA = 10000.0
NORM_EPS = 1e-6
CONV_TAPS = 4
CONV_LEFT = 2

LANES = 128
SUBLANES = 8
VMEM_LIMIT = 56 * 1024 * 1024

LOG2E = 1.4426950408889634


def _rsqrt_mean(x, n):
    return lax.rsqrt(jnp.sum(x * x, axis=-1, keepdims=True) * (1.0 / n) + NORM_EPS)


def _proj_kernel(x_ref, ln1_ref, win_ref, qag_ref, wuq_ref, kvag_ref, wuk_ref, wuv_ref,
                 cos_ref, sin_ref, gq_ref, gk_ref,
                 q_out, k_out, v_out, xr_out, xg_out, *, q_lora, kv_lora, d_rnn):
    x = x_ref[0]
    d_model = x.shape[-1]
    hn = (x * _rsqrt_mean(x, d_model) * ln1_ref[...]).astype(BF16)
    p = jnp.dot(hn, win_ref[...], preferred_element_type=F32)
    o_kv = q_lora
    o_kr = o_kv + kv_lora
    o_xr = o_kr + LANES
    o_xg = o_xr + d_rnn
    c_q = p[:, :o_kv]
    c_kv = p[:, o_kv:o_kr]
    kr = p[:, o_kr:o_xr]
    n_slab = d_rnn // LANES
    for s in range(n_slab):
        xr_out[0, s] = p[:, o_xr + s * LANES:o_xr + (s + 1) * LANES]
        xg_out[0, s] = p[:, o_xg + s * LANES:o_xg + (s + 1) * LANES]

    lane = lax.broadcasted_iota(jnp.int32, (1, LANES), 1)
    qk_mask = (lane < HEAD_QK).astype(F32)
    cos_t = cos_ref[...]
    sin_t = sin_ref[...]
    inv_hd = 1.0 / HEAD_QK

    cqn = (c_q * _rsqrt_mean(c_q, q_lora) * qag_ref[...]).astype(BF16)
    q = jnp.dot(cqn, wuq_ref[...], preferred_element_type=F32)
    cq = cos_t * gq_ref[0:1, :]
    sq = sin_t * gq_ref[1:2, :]
    q_scale = (HEAD_QK ** -0.5) * LOG2E
    for h in range(N_HEAD):
        qh = q[:, h * LANES:(h + 1) * LANES]
        ss = jnp.sum(qh * qh * qk_mask, axis=-1, keepdims=True)
        sc = lax.rsqrt(ss * inv_hd + NORM_EPS) * q_scale
        qr = pltpu.roll(qh, LANES - ROPE, axis=1)
        q_out[0, :, h * LANES:(h + 1) * LANES] = ((qh * cq + qr * sq) * sc).astype(BF16)

    ckn = (c_kv * _rsqrt_mean(c_kv, kv_lora) * kvag_ref[...]).astype(BF16)
    kn = jnp.dot(ckn, wuk_ref[...], preferred_element_type=F32)
    v_out[0] = jnp.dot(ckn, wuv_ref[...], preferred_element_type=F32).T.astype(BF16)
    ck = cos_t * gk_ref[0:1, :]
    sk = sin_t * gk_ref[1:2, :]
    ss_r = jnp.sum(kr * kr * qk_mask, axis=-1, keepdims=True)
    kr_rot = kr * ck + pltpu.roll(kr, LANES - ROPE, axis=1) * sk
    for h in range(N_HEAD):
        kh = kn[:, h * LANES:(h + 1) * LANES]
        ss = jnp.sum(kh * kh, axis=-1, keepdims=True) + ss_r
        sc = lax.rsqrt(ss * inv_hd + NORM_EPS)
        k_out[0, :, h * LANES:(h + 1) * LANES] = ((kh * ck + kr_rot) * sc).astype(BF16)


def _const_spec(shape):
    nd = len(shape)
    return pl.BlockSpec(shape, lambda *_: (0,) * nd)


def _project(x3, tm, cos_t, sin_t, wts, dims):
    q_lora, kv_lora, d_rnn = dims
    b, s, d = x3.shape
    n_slab = d_rnn // LANES
    nt = s // tm
    ln1, win, qag, wuq, kvag, wuk, wuv, gq, gk = wts
    row3 = lambda i, j: (i, j, 0)
    row4 = lambda i, j: (i, 0, j, 0)
    tab = lambda i, j: (j, 0)
    in_specs = [pl.BlockSpec((1, tm, d), row3)]
    in_specs += [_const_spec(w.shape) for w in (ln1, win, qag, wuq, kvag, wuk, wuv)]
    in_specs += [pl.BlockSpec((tm, LANES), tab), pl.BlockSpec((tm, LANES), tab)]
    in_specs += [_const_spec(gq.shape), _const_spec(gk.shape)]
    out_shape = (
        jax.ShapeDtypeStruct((b, s, N_HEAD * LANES), BF16),
        jax.ShapeDtypeStruct((b, s, N_HEAD * LANES), BF16),
        jax.ShapeDtypeStruct((b, N_HEAD * HEAD_V, s), BF16),
        jax.ShapeDtypeStruct((b, n_slab, s, LANES), F32),
        jax.ShapeDtypeStruct((b, n_slab, s, LANES), F32),
    )
    out_specs = (
        pl.BlockSpec((1, tm, N_HEAD * LANES), row3),
        pl.BlockSpec((1, tm, N_HEAD * LANES), row3),
        pl.BlockSpec((1, N_HEAD * HEAD_V, tm), lambda i, j: (i, 0, j)),
        pl.BlockSpec((1, n_slab, tm, LANES), row4),
        pl.BlockSpec((1, n_slab, tm, LANES), row4),
    )
    return pl.pallas_call(
        functools.partial(_proj_kernel, q_lora=q_lora, kv_lora=kv_lora, d_rnn=d_rnn),
        grid=(b, nt),
        in_specs=in_specs,
        out_specs=out_specs,
        out_shape=out_shape,
        compiler_params=pltpu.CompilerParams(
            dimension_semantics=("arbitrary", "arbitrary"), vmem_limit_bytes=VMEM_LIMIT),
        name="proj",
    )(x3, ln1, win, qag, wuq, kvag, wuk, wuv, cos_t, sin_t, gq, gk)


def _scan_stride(t_len):
    s = -(-t_len // SUBLANES)
    while s % SUBLANES == 0:
        s += 1
    if s % 2:
        s += 1
        if s % SUBLANES == 0:
            s += 2
    return s


def _chunk_len(stride, cap=48):
    for c in range(min(cap, stride), 0, -1):
        if stride % c == 0:
            return c
    return 1


def _rnn_kernel(xr_ref, xg_ref, xrm_ref, cw_ref, cb_ref, wg_ref, ba_ref, bi_ref, lam_ref, g_ref,
                o_ref, t1, af, bf, ab, bb, *, seq, stride, rc, tm_out):
    n_slab = xr_ref.shape[1]
    t_len = N_META_TOK + seq
    pre = SUBLANES
    rows = t1.shape[1]
    n_chunk = stride // rc
    crow = rc * SUBLANES

    for s in range(n_slab):
        t1[s, 0:pre, :] = jnp.zeros((pre, LANES), F32)
        t1[s, pre:pre + N_META_TOK, :] = xrm_ref[0, s]
        t1[s, pre + N_META_TOK:pre + t_len, :] = xr_ref[0, s]
        t1[s, pre + t_len:rows, :] = jnp.zeros((rows - pre - t_len, LANES), F32)

    lam = lam_ref[...]
    nl = -lam
    softplus = jnp.maximum(nl, 0.0) + jnp.log(1.0 + jnp.exp(-jnp.abs(nl)))
    cn = LRU_SCALE * softplus
    c2 = -cn * LOG2E
    cw = cw_ref[...]
    cb = cb_ref[...]
    ba = ba_ref[...]
    bi = bi_ref[...]

    ridx = lax.broadcasted_iota(jnp.int32, (crow, LANES), 0)
    t_local = (ridx & (SUBLANES - 1)) * stride + (ridx >> 3)

    def gate_chunk(c, carry):
        r0 = c * rc
        row0 = pl.multiple_of(c * crow, SUBLANES)
        valid = (t_local + r0) < t_len
        for s in range(n_slab):
            ls = slice(s * LANES, (s + 1) * LANES)
            taps = [t1[s, pl.ds(r0 + pre - CONV_LEFT + j, SUBLANES, stride=stride), :]
                    for j in range(rc + CONV_TAPS - 1)]
            xc_rows = []
            for rr in range(rc):
                acc = taps[rr] * cw[0:1, ls]
                for j in range(1, CONV_TAPS):
                    acc = acc + taps[rr + j] * cw[j:j + 1, ls]
                xc_rows.append(acc + cb[:, ls])
            xc = jnp.concatenate(xc_rows, axis=0)
            z = jnp.dot(xc.astype(BF16), wg_ref[s], preferred_element_type=F32)
            for d, (a_ref, b_ref) in enumerate(((af, bf), (ab, bb))):
                za = z[:, (2 * d) * LANES:(2 * d + 1) * LANES] + ba[d:d + 1, ls]
                zi = z[:, (2 * d + 1) * LANES:(2 * d + 2) * LANES] + bi[d:d + 1, ls]
                r = 0.5 * jnp.tanh(0.5 * za) + 0.5
                ig = 0.5 * jnp.tanh(0.5 * zi) + 0.5
                a = jnp.exp2(r * c2[d:d + 1, ls])
                one_m_a2 = jnp.tanh(r * cn[d:d + 1, ls]) * (1.0 + a * a)
                bv = jnp.sqrt(jnp.maximum(one_m_a2, 0.0)) * (ig * xc)
                if d == 1:
                    bv = jnp.where(valid, bv, 0.0)
                a_ref[s, pl.ds(row0, crow), :] = a
                b_ref[s, pl.ds(row0, crow), :] = bv
        return carry

    lax.fori_loop(0, n_chunk, gate_chunk, 0)

    zero = jnp.zeros((SUBLANES, LANES), F32)
    one = jnp.ones((SUBLANES, LANES), F32)

    def scan_step(r, carry):
        hf, pf, hb, pb = carry
        rf = pl.multiple_of(r * SUBLANES, SUBLANES)
        rb = pl.multiple_of((stride - 1 - r) * SUBLANES, SUBLANES)
        nhf, npf, nhb, npb = [], [], [], []
        for s in range(n_slab):
            a = af[s, pl.ds(rf, SUBLANES), :]
            h = a * hf[s] + bf[s, pl.ds(rf, SUBLANES), :]
            pr = a * pf[s]
            bf[s, pl.ds(rf, SUBLANES), :] = h
            af[s, pl.ds(rf, SUBLANES), :] = pr
            nhf.append(h)
            npf.append(pr)
            a = ab[s, pl.ds(rb, SUBLANES), :]
            h = a * hb[s] + bb[s, pl.ds(rb, SUBLANES), :]
            pr = a * pb[s]
            bb[s, pl.ds(rb, SUBLANES), :] = h
            ab[s, pl.ds(rb, SUBLANES), :] = pr
            nhb.append(h)
            npb.append(pr)
        return tuple(nhf), tuple(npf), tuple(nhb), tuple(npb)

    init = ((zero,) * n_slab, (one,) * n_slab, (zero,) * n_slab, (one,) * n_slab)
    hf, pf, hb, pb = lax.fori_loop(0, stride, scan_step, init, unroll=2)

    sub = lax.broadcasted_iota(jnp.int32, (SUBLANES, LANES), 0)
    cf, cbk = [], []
    for s in range(n_slab):
        c = zero
        for _ in range(SUBLANES - 1):
            c = jnp.where(sub == 0, 0.0, pltpu.roll(pf[s] * c + hf[s], 1, axis=0))
        cf.append(c)
        c = zero
        for _ in range(SUBLANES - 1):
            c = jnp.where(sub == SUBLANES - 1, 0.0, pltpu.roll(pb[s] * c + hb[s], SUBLANES - 1, axis=0))
        cbk.append(c)

    def combine(r, carry):
        rp = pl.multiple_of(r * SUBLANES, SUBLANES)
        for s in range(n_slab):
            y = (bf[s, pl.ds(rp, SUBLANES), :] + af[s, pl.ds(rp, SUBLANES), :] * cf[s]
                 + bb[s, pl.ds(rp, SUBLANES), :] + ab[s, pl.ds(rp, SUBLANES), :] * cbk[s])
            t1[s, pl.ds(r + pre, SUBLANES, stride=stride), :] = y
        return carry

    lax.fori_loop(0, stride, combine, 0, unroll=2)

    d_rnn = n_slab * LANES
    g = g_ref[...]
    k0 = math.sqrt(2.0 / math.pi)

    def finish(j, carry):
        ro = pl.multiple_of(j * tm_out, 2 * SUBLANES)
        rt = pl.multiple_of(j * tm_out + pre + N_META_TOK, SUBLANES)
        ys = []
        ssq = jnp.zeros((tm_out, 1), F32)
        for s in range(n_slab):
            y = t1[s, pl.ds(rt, tm_out), :]
            xg = xg_ref[0, s, pl.ds(ro, tm_out), :]
            gl = 0.5 * xg * (1.0 + jnp.tanh(k0 * (xg + 0.044715 * (xg * xg * xg))))
            y = y * gl
            ssq = ssq + jnp.sum(y * y, axis=-1, keepdims=True)
            ys.append(y)
        sc = lax.rsqrt(ssq * (1.0 / d_rnn) + NORM_EPS)
        for s in range(n_slab):
            ls = slice(s * LANES, (s + 1) * LANES)
            o_ref[0, pl.ds(ro, tm_out), ls] = (ys[s] * sc * g[:, ls]).astype(BF16)
        return carry

    lax.fori_loop(0, seq // tm_out, finish, 0)


def _rnn(xr, xg, xrm, cw, cb, wg, ba, bi, lam, g):
    b, n_slab, seq, _ = xr.shape
    t_len = N_META_TOK + seq
    stride = _scan_stride(t_len)
    rc = _chunk_len(stride)
    rows_perm = stride * SUBLANES
    rows_time = SUBLANES + rows_perm + SUBLANES
    tm_out = 256 if seq % 256 == 0 else seq
    blk = lambda i: (i, 0, 0, 0)
    in_specs = [
        pl.BlockSpec((1, n_slab, seq, LANES), blk),
        pl.BlockSpec((1, n_slab, seq, LANES), blk),
        _const_spec(xrm.shape),
    ] + [_const_spec(w.shape) for w in (cw, cb, wg, ba, bi, lam, g)]
    scratch = [pltpu.VMEM((n_slab, rows_time, LANES), F32)]
    scratch += [pltpu.VMEM((n_slab, rows_perm, LANES), F32) for _ in range(4)]
    return pl.pallas_call(
        functools.partial(_rnn_kernel, seq=seq, stride=stride, rc=rc, tm_out=tm_out),
        grid=(b,),
        in_specs=in_specs,
        out_specs=pl.BlockSpec((1, seq, n_slab * LANES), lambda i: (i, 0, 0)),
        out_shape=jax.ShapeDtypeStruct((b, seq, n_slab * LANES), BF16),
        scratch_shapes=scratch,
        compiler_params=pltpu.CompilerParams(
            dimension_semantics=("arbitrary",), vmem_limit_bytes=VMEM_LIMIT),
        name="rnn",
    )(xr, xg, xrm, cw, cb, wg, ba, bi, lam, g)


def _attn_kernel(q_ref, k_ref, vt_ref, km_ref, vtm_ref, o_ref, kfull):
    nt = (((1,), (1,)), ((), ()))
    n_key = k_ref.shape[1]

    @pl.when(pl.program_id(2) == 0)
    def _():
        kfull[0:n_key, :] = k_ref[0]
        kfull[n_key:n_key + N_META_TOK, :] = km_ref[0]

    outs = []
    for hh in range(2):
        ls = slice(hh * LANES, (hh + 1) * LANES)
        vs = slice(hh * HEAD_V, (hh + 1) * HEAD_V)
        st = lax.dot_general(kfull[:, ls], q_ref[0, :, ls], nt, preferred_element_type=F32)
        m = jnp.max(st, axis=0, keepdims=True)
        p = jnp.exp2(st - m)
        l = jnp.sum(p, axis=0, keepdims=True)
        pb = p.astype(BF16)
        ot = jnp.dot(vt_ref[0, vs, :], pb[:n_key], preferred_element_type=F32)
        ot = ot + jnp.dot(vtm_ref[0, vs, :], pb[n_key:], preferred_element_type=F32)
        outs.append(ot * (1.0 / l))
    o_ref[0] = jnp.concatenate(outs, axis=0).T


def _attention(q, k, vt, km, vtm, tq):
    b, s, _ = q.shape
    n_pair = N_HEAD // 2
    return pl.pallas_call(
        _attn_kernel,
        grid=(b, n_pair, s // tq),
        in_specs=[
            pl.BlockSpec((1, tq, 2 * LANES), lambda i, p, j: (i, j, p)),
            pl.BlockSpec((1, s, 2 * LANES), lambda i, p, j: (i, 0, p)),
            pl.BlockSpec((1, 2 * HEAD_V, s), lambda i, p, j: (i, p, 0)),
            pl.BlockSpec((1, N_META_TOK, 2 * LANES), lambda i, p, j: (0, 0, p)),
            pl.BlockSpec((1, 2 * HEAD_V, N_META_TOK), lambda i, p, j: (0, p, 0)),
        ],
        out_specs=pl.BlockSpec((1, tq, 2 * HEAD_V), lambda i, p, j: (i, j, p)),
        out_shape=jax.ShapeDtypeStruct((b, s, N_HEAD * HEAD_V), F32),
        scratch_shapes=[pltpu.VMEM((s + N_META_TOK, 2 * LANES), BF16)],
        compiler_params=pltpu.CompilerParams(
            dimension_semantics=("arbitrary", "arbitrary", "arbitrary"),
            vmem_limit_bytes=VMEM_LIMIT),
        name="attn",
    )(q, k, vt, km, vtm)


def _out_kernel(x_ref, oa_ref, on_ref, ag_ref, woa_ref, wor_ref, ln2_ref, wg_ref, wu_ref, wd_ref,
                o_ref, *, ff_chunk):
    oa = oa_ref[...]
    oan = (oa * _rsqrt_mean(oa, oa.shape[-1]) * ag_ref[...]).astype(BF16)
    h = x_ref[...] + jnp.dot(oan, woa_ref[...], preferred_element_type=F32)
    h = h + jnp.dot(on_ref[...], wor_ref[...], preferred_element_type=F32)
    hn = (h * _rsqrt_mean(h, h.shape[-1]) * ln2_ref[...]).astype(BF16)
    d_ff = wg_ref.shape[1]
    acc = h
    for c in range(d_ff // ff_chunk):
        cs = slice(c * ff_chunk, (c + 1) * ff_chunk)
        gt = jnp.dot(hn, wg_ref[:, cs], preferred_element_type=F32)
        up = jnp.dot(hn, wu_ref[:, cs], preferred_element_type=F32)
        act = (gt * jax.nn.sigmoid(gt) * up).astype(BF16)
        acc = acc + jnp.dot(act, wd_ref[cs, :], preferred_element_type=F32)
    o_ref[...] = acc


def _out_ffn(x2, oa2, on2, ag, woa, wor, ln2, wg, wu, wd, tm):
    n, d = x2.shape
    d_ff = wg.shape[1]
    ff_chunk = d_ff
    row = lambda i: (i, 0)
    single = pl.Buffered(1)
    wspec = lambda w: pl.BlockSpec(w.shape, lambda i: (0, 0))
    return pl.pallas_call(
        functools.partial(_out_kernel, ff_chunk=ff_chunk),
        grid=(n // tm,),
        in_specs=[
            pl.BlockSpec((tm, d), row),
            pl.BlockSpec((tm, oa2.shape[1]), row),
            pl.BlockSpec((tm, on2.shape[1]), row),
            wspec(ag), wspec(woa), wspec(wor), wspec(ln2), wspec(wg), wspec(wu), wspec(wd),
        ],
        out_specs=pl.BlockSpec((tm, d), row),
        out_shape=jax.ShapeDtypeStruct((n, d), F32),
        compiler_params=pltpu.CompilerParams(
            dimension_semantics=("arbitrary",), vmem_limit_bytes=VMEM_LIMIT),
        name="out_ffn",
    )(x2, oa2, on2, ag, woa, wor, ln2, wg, wu, wd)


def _rope_tables(t_len):
    half = ROPE // 2
    freqs = 1.0 / (THETA ** (jnp.arange(half, dtype=F32) / half))
    ang = jnp.arange(t_len, dtype=F32)[:, None] * freqs[None, :]
    cos2 = jnp.concatenate([jnp.cos(ang)] * 2, axis=-1)
    sin2 = jnp.concatenate([jnp.sin(ang)] * 2, axis=-1)
    pad = LANES - HEAD_QK
    cos_t = jnp.concatenate([jnp.ones((t_len, NOPE), F32), cos2, jnp.zeros((t_len, pad), F32)], -1)
    sin_t = jnp.concatenate([jnp.zeros((t_len, NOPE), F32), sin2, jnp.zeros((t_len, pad), F32)], -1)
    return cos_t, sin_t


def _rot_half_cols(w):
    half = ROPE // 2
    return jnp.concatenate([-w[..., half:], w[..., :half]], axis=-1)


def _gain_rows(g):
    half = ROPE // 2
    pad = jnp.zeros((LANES - HEAD_QK,), F32)
    r0 = jnp.concatenate([g, pad])
    r1 = jnp.concatenate([jnp.zeros((NOPE,), F32), g[NOPE + half:], g[NOPE:NOPE + half], pad])
    return jnp.stack([r0, r1])


def _block_diag_pairs(w):
    g, n, _ = w.shape
    z = jnp.zeros((g // 2, n, n), w.dtype)
    top = jnp.concatenate([w[0::2], z], axis=2)
    bot = jnp.concatenate([z, w[1::2]], axis=2)
    return jnp.concatenate([top, bot], axis=1)


def kernel(x, meta_tokens, ln1_g, w_in, q_a_norm_g, w_uq, kv_a_norm_g, w_ukv, q_norm_g, k_norm_g,
           conv_w, conv_b, lru_wa, lru_ba, lru_wi, lru_bi, lru_lambda, attn_out_g, rnn_out_g,
           w_out, ln2_g, w_gate, w_up, w_down):
    bsz, seq, d_model = x.shape
    q_lora = q_a_norm_g.shape[-1]
    kv_lora = kv_a_norm_g.shape[-1]
    d_rnn = conv_w.shape[-1]
    d_attn = N_HEAD * HEAD_V
    t_len = N_META_TOK + seq
    l = 0

    wi = w_in[l]
    o_kr = q_lora + kv_lora
    w_kr = wi[:, o_kr:o_kr + ROPE]
    win = jnp.concatenate(
        [wi[:, :o_kr], jnp.zeros((d_model, NOPE), F32), w_kr, _rot_half_cols(w_kr),
         wi[:, o_kr + ROPE:]], axis=1).astype(BF16)
    wq = w_uq[l].reshape(q_lora, N_HEAD, HEAD_QK)
    wuq = jnp.concatenate([wq, _rot_half_cols(wq[..., NOPE:])], axis=-1)
    wuq = wuq.reshape(q_lora, N_HEAD * LANES).astype(BF16)
    wkv = w_ukv[l].reshape(kv_lora, N_HEAD, NOPE + HEAD_V)
    wuk = jnp.concatenate([wkv[..., :NOPE], jnp.zeros((kv_lora, N_HEAD, LANES - NOPE), F32)], -1)
    wuk = wuk.reshape(kv_lora, N_HEAD * LANES).astype(BF16)
    wuv = wkv[..., NOPE:].reshape(kv_lora, d_attn).astype(BF16)
    gq = _gain_rows(q_norm_g[l])
    gk = _gain_rows(k_norm_g[l])
    cos_t, sin_t = _rope_tables(t_len)
    proj_w = (ln1_g[l][None], win, q_a_norm_g[l][None], wuq, kv_a_norm_g[l][None], wuk, wuv, gq, gk)
    dims = (q_lora, kv_lora, d_rnn)

    wgate = jnp.concatenate(
        [_block_diag_pairs(w) for w in (lru_wa[l, 0], lru_wi[l, 0], lru_wa[l, 1], lru_wi[l, 1])],
        axis=2).astype(BF16)

    tm = 512 if seq % 512 == 0 else seq
    q, k, v, xr, xg = _project(x, tm, cos_t[N_META_TOK:], sin_t[N_META_TOK:], proj_w, dims)
    _, km, vm, xrm, _ = _project(meta_tokens[None].astype(x.dtype), N_META_TOK,
                                 cos_t[:N_META_TOK], sin_t[:N_META_TOK], proj_w, dims)

    o_rnn = _rnn(xr, xg, xrm, conv_w[l], conv_b[l][None], wgate, lru_ba[l], lru_bi[l],
                 lru_lambda[l], rnn_out_g[l][None])

    tq = 512 if seq % 512 == 0 else seq
    o_attn = _attention(q, k, v, km, vm, tq)

    wo = w_out[l].astype(BF16)
    n_rows = bsz * seq
    tmo = 256 if n_rows % 256 == 0 else n_rows
    out = _out_ffn(x.reshape(n_rows, d_model), o_attn.reshape(n_rows, d_attn),
                   o_rnn.reshape(n_rows, d_rnn), attn_out_g[l][None], wo[:d_attn], wo[d_attn:],
                   ln2_g[l][None], w_gate[l].astype(BF16), w_up[l].astype(BF16),
                   w_down[l].astype(BF16), tmo)
    return out.reshape(bsz, seq, d_model)
```

```python
import functools
import math

import jax
import jax.numpy as jnp
from jax import lax
from jax.experimental import pallas as pl
from jax.experimental.pallas import tpu as pltpu

F32 = jnp.float32
BF16 = jnp.bfloat16

N_META_TOK = 16
N_HEAD = 8
NOPE = 64
ROPE = 32
HEAD_QK = NOPE + ROPE
HEAD_V = 64
LRU_SCALE = 8.0
THETA = 10000.0
NORM_EPS = 1e-6
CONV_TAPS = 4
CONV_LEFT = 2

LANES = 128
SUBLANES = 8
VMEM_LIMIT = 56 * 1024 * 1024

LOG2E = 1.4426950408889634


def _rsqrt_mean(x, n):
    return lax.rsqrt(jnp.sum(x * x, axis=-1, keepdims=True) * (1.0 / n) + NORM_EPS)


def _proj_kernel(x_ref, ln1_ref, win_ref, qag_ref, wuq_ref, kvag_ref, wuk_ref, wuv_ref,
                 cos_ref, sin_ref, gq_ref, gk_ref,
                 q_out, k_out, v_out, xr_out, xg_out, *, q_lora, kv_lora, d_rnn):
    x = x_ref[0]
    d_model = x.shape[-1]
    hn = (x * _rsqrt_mean(x, d_model) * ln1_ref[...]).astype(BF16)
    p = jnp.dot(hn, win_ref[...], preferred_element_type=F32)
    o_kv = q_lora
    o_kr = o_kv + kv_lora
    o_xr = o_kr + LANES
    o_xg = o_xr + d_rnn
    c_q = p[:, :o_kv]
    c_kv = p[:, o_kv:o_kr]
    kr = p[:, o_kr:o_xr]
    n_slab = d_rnn // LANES
    for s in range(n_slab):
        xr_out[0, s] = p[:, o_xr + s * LANES:o_xr + (s + 1) * LANES]
        xg_out[0, s] = p[:, o_xg + s * LANES:o_xg + (s + 1) * LANES]

    lane = lax.broadcasted_iota(jnp.int32, (1, LANES), 1)
    qk_mask = (lane < HEAD_QK).astype(F32)
    cos_t = cos_ref[...]
    sin_t = sin_ref[...]
    inv_hd = 1.0 / HEAD_QK

    cqn = (c_q * _rsqrt_mean(c_q, q_lora) * qag_ref[...]).astype(BF16)
    q = jnp.dot(cqn, wuq_ref[...], preferred_element_type=F32)
    cq = cos_t * gq_ref[0:1, :]
    sq = sin_t * gq_ref[1:2, :]
    q_scale = (HEAD_QK ** -0.5) * LOG2E
    for h in range(N_HEAD):
        qh = q[:, h * LANES:(h + 1) * LANES]
        ss = jnp.sum(qh * qh * qk_mask, axis=-1, keepdims=True)
        sc = lax.rsqrt(ss * inv_hd + NORM_EPS) * q_scale
        qr = pltpu.roll(qh, LANES - ROPE, axis=1)
        q_out[0, :, h * LANES:(h + 1) * LANES] = ((qh * cq + qr * sq) * sc).astype(BF16)

    ckn = (c_kv * _rsqrt_mean(c_kv, kv_lora) * kvag_ref[...]).astype(BF16)
    kn = jnp.dot(ckn, wuk_ref[...], preferred_element_type=F32)
    v_out[0] = jnp.dot(ckn, wuv_ref[...], preferred_element_type=F32).T.astype(BF16)
    ck = cos_t * gk_ref[0:1, :]
    sk = sin_t * gk_ref[1:2, :]
    ss_r = jnp.sum(kr * kr * qk_mask, axis=-1, keepdims=True)
    kr_rot = kr * ck + pltpu.roll(kr, LANES - ROPE, axis=1) * sk
    for h in range(N_HEAD):
        kh = kn[:, h * LANES:(h + 1) * LANES]
        ss = jnp.sum(kh * kh, axis=-1, keepdims=True) + ss_r
        sc = lax.rsqrt(ss * inv_hd + NORM_EPS)
        k_out[0, :, h * LANES:(h + 1) * LANES] = ((kh * ck + kr_rot) * sc).astype(BF16)


def _const_spec(shape):
    nd = len(shape)
    return pl.BlockSpec(shape, lambda *_: (0,) * nd)


def _project(x3, tm, cos_t, sin_t, wts, dims):
    q_lora, kv_lora, d_rnn = dims
    b, s, d = x3.shape
    n_slab = d_rnn // LANES
    nt = s // tm
    ln1, win, qag, wuq, kvag, wuk, wuv, gq, gk = wts
    row3 = lambda i, j: (i, j, 0)
    row4 = lambda i, j: (i, 0, j, 0)
    tab = lambda i, j: (j, 0)
    in_specs = [pl.BlockSpec((1, tm, d), row3)]
    in_specs += [_const_spec(w.shape) for w in (ln1, win, qag, wuq, kvag, wuk, wuv)]
    in_specs += [pl.BlockSpec((tm, LANES), tab), pl.BlockSpec((tm, LANES), tab)]
    in_specs += [_const_spec(gq.shape), _const_spec(gk.shape)]
    out_shape = (
        jax.ShapeDtypeStruct((b, s, N_HEAD * LANES), BF16),
        jax.ShapeDtypeStruct((b, s, N_HEAD * LANES), BF16),
        jax.ShapeDtypeStruct((b, N_HEAD * HEAD_V, s), BF16),
        jax.ShapeDtypeStruct((b, n_slab, s, LANES), F32),
        jax.ShapeDtypeStruct((b, n_slab, s, LANES), F32),
    )
    out_specs = (
        pl.BlockSpec((1, tm, N_HEAD * LANES), row3),
        pl.BlockSpec((1, tm, N_HEAD * LANES), row3),
        pl.BlockSpec((1, N_HEAD * HEAD_V, tm), lambda i, j: (i, 0, j)),
        pl.BlockSpec((1, n_slab, tm, LANES), row4),
        pl.BlockSpec((1, n_slab, tm, LANES), row4),
    )
    return pl.pallas_call(
        functools.partial(_proj_kernel, q_lora=q_lora, kv_lora=kv_lora, d_rnn=d_rnn),
        grid=(b, nt),
        in_specs=in_specs,
        out_specs=out_specs,
        out_shape=out_shape,
        compiler_params=pltpu.CompilerParams(
            dimension_semantics=("arbitrary", "arbitrary"), vmem_limit_bytes=VMEM_LIMIT),
        name="proj",
    )(x3, ln1, win, qag, wuq, kvag, wuk, wuv, cos_t, sin_t, gq, gk)


def _scan_stride(t_len):
    s = -(-t_len // SUBLANES)
    while s % SUBLANES == 0:
        s += 1
    if s % 2:
        s += 1
        if s % SUBLANES == 0:
            s += 2
    return s


def _chunk_len(stride, cap=48):
    for c in range(min(cap, stride), 0, -1):
        if stride % c == 0:
            return c
    return 1


def _rnn_kernel(xr_ref, xg_ref, xrm_ref, cw_ref, cb_ref, wg_ref, ba_ref, bi_ref, lam_ref, g_ref,
                o_ref, t1, af, bf, ab, bb, *, seq, stride, rc, tm_out):
    n_slab = xr_ref.shape[1]
    t_len = N_META_TOK + seq
    pre = SUBLANES
    rows = t1.shape[1]
    n_chunk = stride // rc
    crow = rc * SUBLANES

    for s in range(n_slab):
        t1[s, 0:pre, :] = jnp.zeros((pre, LANES), F32)
        t1[s, pre:pre + N_META_TOK, :] = xrm_ref[0, s]
        t1[s, pre + N_META_TOK:pre + t_len, :] = xr_ref[0, s]
        t1[s, pre + t_len:rows, :] = jnp.zeros((rows - pre - t_len, LANES), F32)

    lam = lam_ref[...]
    nl = -lam
    softplus = jnp.maximum(nl, 0.0) + jnp.log(1.0 + jnp.exp(-jnp.abs(nl)))
    cn = LRU_SCALE * softplus
    c2 = -cn * LOG2E
    cw = cw_ref[...]
    cb = cb_ref[...]
    ba = ba_ref[...]
    bi = bi_ref[...]

    ridx = lax.broadcasted_iota(jnp.int32, (crow, LANES), 0)
    t_local = (ridx & (SUBLANES - 1)) * stride + (ridx >> 3)

    def gate_chunk(c, carry):
        r0 = c * rc
        row0 = pl.multiple_of(c * crow, SUBLANES)
        valid = (t_local + r0) < t_len
        for s in range(n_slab):
            ls = slice(s * LANES, (s + 1) * LANES)
            taps = [t1[s, pl.ds(r0 + pre - CONV_LEFT + j, SUBLANES, stride=stride), :]
                    for j in range(rc + CONV_TAPS - 1)]
            xc_rows = []
            for rr in range(rc):
                acc = taps[rr] * cw[0:1, ls]
                for j in range(1, CONV_TAPS):
                    acc = acc + taps[rr + j] * cw[j:j + 1, ls]
                xc_rows.append(acc + cb[:, ls])
            xc = jnp.concatenate(xc_rows, axis=0)
            z = jnp.dot(xc.astype(BF16), wg_ref[s], preferred_element_type=F32)
            for d, (a_ref, b_ref) in enumerate(((af, bf), (ab, bb))):
                za = z[:, (2 * d) * LANES:(2 * d + 1) * LANES] + ba[d:d + 1, ls]
                zi = z[:, (2 * d + 1) * LANES:(2 * d + 2) * LANES] + bi[d:d + 1, ls]
                r = 0.5 * jnp.tanh(0.5 * za) + 0.5
                ig = 0.5 * jnp.tanh(0.5 * zi) + 0.5
                a = jnp.exp2(r * c2[d:d + 1, ls])
                one_m_a2 = jnp.tanh(r * cn[d:d + 1, ls]) * (1.0 + a * a)
                bv = jnp.sqrt(jnp.maximum(one_m_a2, 0.0)) * (ig * xc)
                if d == 1:
                    bv = jnp.where(valid, bv, 0.0)
                a_ref[s, pl.ds(row0, crow), :] = a
                b_ref[s, pl.ds(row0, crow), :] = bv
        return carry

    lax.fori_loop(0, n_chunk, gate_chunk, 0)

    zero = jnp.zeros((SUBLANES, LANES), F32)
    one = jnp.ones((SUBLANES, LANES), F32)

    def scan_step(r, carry):
        hf, pf, hb, pb = carry
        rf = pl.multiple_of(r * SUBLANES, SUBLANES)
        rb = pl.multiple_of((stride - 1 - r) * SUBLANES, SUBLANES)
        nhf, npf, nhb, npb = [], [], [], []
        for s in range(n_slab):
            a = af[s, pl.ds(rf, SUBLANES), :]
            h = a * hf[s] + bf[s, pl.ds(rf, SUBLANES), :]
            pr = a * pf[s]
            bf[s, pl.ds(rf, SUBLANES), :] = h
            af[s, pl.ds(rf, SUBLANES), :] = pr
            nhf.append(h)
            npf.append(pr)
            a = ab[s, pl.ds(rb, SUBLANES), :]
            h = a * hb[s] + bb[s, pl.ds(rb, SUBLANES), :]
            pr = a * pb[s]
            bb[s, pl.ds(rb, SUBLANES), :] = h
            ab[s, pl.ds(rb, SUBLANES), :] = pr
            nhb.append(h)
            npb.append(pr)
        return tuple(nhf), tuple(npf), tuple(nhb), tuple(npb)

    init = ((zero,) * n_slab, (one,) * n_slab, (zero,) * n_slab, (one,) * n_slab)
    hf, pf, hb, pb = lax.fori_loop(0, stride, scan_step, init, unroll=2)

    sub = lax.broadcasted_iota(jnp.int32, (SUBLANES, LANES), 0)
    cf, cbk = [], []
    for s in range(n_slab):
        c = zero
        for _ in range(SUBLANES - 1):
            c = jnp.where(sub == 0, 0.0, pltpu.roll(pf[s] * c + hf[s], 1, axis=0))
        cf.append(c)
        c = zero
        for _ in range(SUBLANES - 1):
            c = jnp.where(sub == SUBLANES - 1, 0.0, pltpu.roll(pb[s] * c + hb[s], SUBLANES - 1, axis=0))
        cbk.append(c)

    def combine(r, carry):
        rp = pl.multiple_of(r * SUBLANES, SUBLANES)
        for s in range(n_slab):
            y = (bf[s, pl.ds(rp, SUBLANES), :] + af[s, pl.ds(rp, SUBLANES), :] * cf[s]
                 + bb[s, pl.ds(rp, SUBLANES), :] + ab[s, pl.ds(rp, SUBLANES), :] * cbk[s])
            t1[s, pl.ds(r + pre, SUBLANES, stride=stride), :] = y
        return carry

    lax.fori_loop(0, stride, combine, 0, unroll=2)

    d_rnn = n_slab * LANES
    g = g_ref[...]
    k0 = math.sqrt(2.0 / math.pi)

    def finish(j, carry):
        ro = pl.multiple_of(j * tm_out, 2 * SUBLANES)
        rt = pl.multiple_of(j * tm_out + pre + N_META_TOK, SUBLANES)
        ys = []
        ssq = jnp.zeros((tm_out, 1), F32)
        for s in range(n_slab):
            y = t1[s, pl.ds(rt, tm_out), :]
            xg = xg_ref[0, s, pl.ds(ro, tm_out), :]
            gl = 0.5 * xg * (1.0 + jnp.tanh(k0 * (xg + 0.044715 * (xg * xg * xg))))
            y = y * gl
            ssq = ssq + jnp.sum(y * y, axis=-1, keepdims=True)
            ys.append(y)
        sc = lax.rsqrt(ssq * (1.0 / d_rnn) + NORM_EPS)
        for s in range(n_slab):
            ls = slice(s * LANES, (s + 1) * LANES)
            o_ref[0, pl.ds(ro, tm_out), ls] = (ys[s] * sc * g[:, ls]).astype(BF16)
        return carry

    lax.fori_loop(0, seq // tm_out, finish, 0)


def _rnn(xr, xg, xrm, cw, cb, wg, ba, bi, lam, g):
    b, n_slab, seq, _ = xr.shape
    t_len = N_META_TOK + seq
    stride = _scan_stride(t_len)
    rc = _chunk_len(stride)
    rows_perm = stride * SUBLANES
    rows_time = SUBLANES + rows_perm + SUBLANES
    tm_out = 256 if seq % 256 == 0 else seq
    blk = lambda i: (i, 0, 0, 0)
    in_specs = [
        pl.BlockSpec((1, n_slab, seq, LANES), blk),
        pl.BlockSpec((1, n_slab, seq, LANES), blk),
        _const_spec(xrm.shape),
    ] + [_const_spec(w.shape) for w in (cw, cb, wg, ba, bi, lam, g)]
    scratch = [pltpu.VMEM((n_slab, rows_time, LANES), F32)]
    scratch += [pltpu.VMEM((n_slab, rows_perm, LANES), F32) for _ in range(4)]
    return pl.pallas_call(
        functools.partial(_rnn_kernel, seq=seq, stride=stride, rc=rc, tm_out=tm_out),
        grid=(b,),
        in_specs=in_specs,
        out_specs=pl.BlockSpec((1, seq, n_slab * LANES), lambda i: (i, 0, 0)),
        out_shape=jax.ShapeDtypeStruct((b, seq, n_slab * LANES), BF16),
        scratch_shapes=scratch,
        compiler_params=pltpu.CompilerParams(
            dimension_semantics=("arbitrary",), vmem_limit_bytes=VMEM_LIMIT),
        name="rnn",
    )(xr, xg, xrm, cw, cb, wg, ba, bi, lam, g)


def _attn_kernel(q_ref, k_ref, vt_ref, km_ref, vtm_ref, o_ref, kfull, st_buf, *, tq):
    nt = (((1,), (1,)), ((), ()))
    n_key = k_ref.shape[1]
    kfull[0:n_key, :] = k_ref[0]
    kfull[n_key:n_key + N_META_TOK, :] = km_ref[0]

    n_stream = 2 * (n_key // tq)

    def scores(i):
        j, hh = divmod(i, 2)
        ls = slice(hh * LANES, (hh + 1) * LANES)
        st = lax.dot_general(kfull[:, ls], q_ref[0, j * tq:(j + 1) * tq, ls], nt,
                             preferred_element_type=F32)
        st_buf[i % 2] = st
        return jnp.max(st, axis=0, keepdims=True)

    def weighted_values(i, m):
        hh = i % 2
        vs = slice(hh * HEAD_V, (hh + 1) * HEAD_V)
        p = jnp.exp2(st_buf[i % 2] - m)
        l = jnp.sum(p, axis=0, keepdims=True)
        pb = p.astype(BF16)
        ot = jnp.dot(vt_ref[0, vs, :], pb[:n_key], preferred_element_type=F32)
        ot = ot + jnp.dot(vtm_ref[0, vs, :], pb[n_key:], preferred_element_type=F32)
        return ot * (1.0 / l)

    m_next = scores(0)
    outs = []
    for i in range(n_stream):
        m_cur = m_next
        if i + 1 < n_stream:
            m_next = scores(i + 1)
        outs.append(weighted_values(i, m_cur))
        if i % 2 == 1:
            j = i // 2
            o_ref[0, j * tq:(j + 1) * tq, :] = jnp.concatenate(outs, axis=0).T
            outs = []


def _attention(q, k, vt, km, vtm, tq):
    b, s, _ = q.shape
    n_pair = N_HEAD // 2
    return pl.pallas_call(
        functools.partial(_attn_kernel, tq=tq),
        grid=(b, n_pair),
        in_specs=[
            pl.BlockSpec((1, s, 2 * LANES), lambda i, p: (i, 0, p)),
            pl.BlockSpec((1, s, 2 * LANES), lambda i, p: (i, 0, p)),
            pl.BlockSpec((1, 2 * HEAD_V, s), lambda i, p: (i, p, 0)),
            pl.BlockSpec((1, N_META_TOK, 2 * LANES), lambda i, p: (0, 0, p)),
            pl.BlockSpec((1, 2 * HEAD_V, N_META_TOK), lambda i, p: (0, p, 0)),
        ],
        out_specs=pl.BlockSpec((1, s, 2 * HEAD_V), lambda i, p: (i, 0, p)),
        out_shape=jax.ShapeDtypeStruct((b, s, N_HEAD * HEAD_V), F32),
        scratch_shapes=[pltpu.VMEM((s + N_META_TOK, 2 * LANES), BF16),
                        pltpu.VMEM((2, s + N_META_TOK, tq), F32)],
        compiler_params=pltpu.CompilerParams(
            dimension_semantics=("arbitrary", "arbitrary"), vmem_limit_bytes=VMEM_LIMIT),
        name="attn",
    )(q, k, vt, km, vtm)


def _out_kernel(x_ref, oa_ref, on_ref, ag_ref, woa_ref, wor_ref, ln2_ref, wg_ref, wu_ref, wd_ref,
                o_ref, *, ff_chunk):
    oa = oa_ref[...]
    oan = (oa * _rsqrt_mean(oa, oa.shape[-1]) * ag_ref[...]).astype(BF16)
    h = x_ref[...] + jnp.dot(oan, woa_ref[...], preferred_element_type=F32)
    h = h + jnp.dot(on_ref[...], wor_ref[...], preferred_element_type=F32)
    hn = (h * _rsqrt_mean(h, h.shape[-1]) * ln2_ref[...]).astype(BF16)
    d_ff = wg_ref.shape[1]
    acc = h
    for c in range(d_ff // ff_chunk):
        cs = slice(c * ff_chunk, (c + 1) * ff_chunk)
        gt = jnp.dot(hn, wg_ref[:, cs], preferred_element_type=F32)
        up = jnp.dot(hn, wu_ref[:, cs], preferred_element_type=F32)
        act = (gt * jax.nn.sigmoid(gt) * up).astype(BF16)
        acc = acc + jnp.dot(act, wd_ref[cs, :], preferred_element_type=F32)
    o_ref[...] = acc


def _out_ffn(x2, oa2, on2, ag, woa, wor, ln2, wg, wu, wd, tm):
    n, d = x2.shape
    d_ff = wg.shape[1]
    ff_chunk = d_ff
    row = lambda i: (i, 0)
    single = pl.Buffered(1)
    wspec = lambda w: pl.BlockSpec(w.shape, lambda i: (0, 0))
    return pl.pallas_call(
        functools.partial(_out_kernel, ff_chunk=ff_chunk),
        grid=(n // tm,),
        in_specs=[
            pl.BlockSpec((tm, d), row),
            pl.BlockSpec((tm, oa2.shape[1]), row),
            pl.BlockSpec((tm, on2.shape[1]), row),
            wspec(ag), wspec(woa), wspec(wor), wspec(ln2), wspec(wg), wspec(wu), wspec(wd),
        ],
        out_specs=pl.BlockSpec((tm, d), row),
        out_shape=jax.ShapeDtypeStruct((n, d), F32),
        compiler_params=pltpu.CompilerParams(
            dimension_semantics=("arbitrary",), vmem_limit_bytes=VMEM_LIMIT),
        name="out_ffn",
    )(x2, oa2, on2, ag, woa, wor, ln2, wg, wu, wd)


def _rope_tables(t_len):
    half = ROPE // 2
    freqs = 1.0 / (THETA ** (jnp.arange(half, dtype=F32) / half))
    ang = jnp.arange(t_len, dtype=F32)[:, None] * freqs[None, :]
    cos2 = jnp.concatenate([jnp.cos(ang)] * 2, axis=-1)
    sin2 = jnp.concatenate([jnp.sin(ang)] * 2, axis=-1)
    pad = LANES - HEAD_QK
    cos_t = jnp.concatenate([jnp.ones((t_len, NOPE), F32), cos2, jnp.zeros((t_len, pad), F32)], -1)
    sin_t = jnp.concatenate([jnp.zeros((t_len, NOPE), F32), sin2, jnp.zeros((t_len, pad), F32)], -1)
    return cos_t, sin_t


def _rot_half_cols(w):
    half = ROPE // 2
    return jnp.concatenate([-w[..., half:], w[..., :half]], axis=-1)


def _gain_rows(g):
    half = ROPE // 2
    pad = jnp.zeros((LANES - HEAD_QK,), F32)
    r0 = jnp.concatenate([g, pad])
    r1 = jnp.concatenate([jnp.zeros((NOPE,), F32), g[NOPE + half:], g[NOPE:NOPE + half], pad])
    return jnp.stack([r0, r1])


def _block_diag_pairs(w):
    g, n, _ = w.shape
    z = jnp.zeros((g // 2, n, n), w.dtype)
    top = jnp.concatenate([w[0::2], z], axis=2)
    bot = jnp.concatenate([z, w[1::2]], axis=2)
    return jnp.concatenate([top, bot], axis=1)


def kernel(x, meta_tokens, ln1_g, w_in, q_a_norm_g, w_uq, kv_a_norm_g, w_ukv, q_norm_g, k_norm_g,
           conv_w, conv_b, lru_wa, lru_ba, lru_wi, lru_bi, lru_lambda, attn_out_g, rnn_out_g,
           w_out, ln2_g, w_gate, w_up, w_down):
    bsz, seq, d_model = x.shape
    q_lora = q_a_norm_g.shape[-1]
    kv_lora = kv_a_norm_g.shape[-1]
    d_rnn = conv_w.shape[-1]
    d_attn = N_HEAD * HEAD_V
    t_len = N_META_TOK + seq
    l = 0

    wi = w_in[l]
    o_kr = q_lora + kv_lora
    w_kr = wi[:, o_kr:o_kr + ROPE]
    win = jnp.concatenate(
        [wi[:, :o_kr], jnp.zeros((d_model, NOPE), F32), w_kr, _rot_half_cols(w_kr),
         wi[:, o_kr + ROPE:]], axis=1).astype(BF16)
    wq = w_uq[l].reshape(q_lora, N_HEAD, HEAD_QK)
    wuq = jnp.concatenate([wq, _rot_half_cols(wq[..., NOPE:])], axis=-1)
    wuq = wuq.reshape(q_lora, N_HEAD * LANES).astype(BF16)
    wkv = w_ukv[l].reshape(kv_lora, N_HEAD, NOPE + HEAD_V)
    wuk = jnp.concatenate([wkv[..., :NOPE], jnp.zeros((kv_lora, N_HEAD, LANES - NOPE), F32)], -1)
    wuk = wuk.reshape(kv_lora, N_HEAD * LANES).astype(BF16)
    wuv = wkv[..., NOPE:].reshape(kv_lora, d_attn).astype(BF16)
    gq = _gain_rows(q_norm_g[l])
    gk = _gain_rows(k_norm_g[l])
    cos_t, sin_t = _rope_tables(t_len)
    proj_w = (ln1_g[l][None], win, q_a_norm_g[l][None], wuq, kv_a_norm_g[l][None], wuk, wuv, gq, gk)
    dims = (q_lora, kv_lora, d_rnn)

    wgate = jnp.concatenate(
        [_block_diag_pairs(w) for w in (lru_wa[l, 0], lru_wi[l, 0], lru_wa[l, 1], lru_wi[l, 1])],
        axis=2).astype(BF16)

    tm = 512 if seq % 512 == 0 else seq
    q, k, v, xr, xg = _project(x, tm, cos_t[N_META_TOK:], sin_t[N_META_TOK:], proj_w, dims)
    _, km, vm, xrm, _ = _project(meta_tokens[None].astype(x.dtype), N_META_TOK,
                                 cos_t[:N_META_TOK], sin_t[:N_META_TOK], proj_w, dims)

    o_rnn = _rnn(xr, xg, xrm, conv_w[l], conv_b[l][None], wgate, lru_ba[l], lru_bi[l],
                 lru_lambda[l], rnn_out_g[l][None])

    tq = 512 if seq % 512 == 0 else seq
    o_attn = _attention(q, k, v, km, vm, tq)

    wo = w_out[l].astype(BF16)
    n_rows = bsz * seq
    tmo = 256 if n_rows % 256 == 0 else n_rows
    out = _out_ffn(x.reshape(n_rows, d_model), o_attn.reshape(n_rows, d_attn),
                   o_rnn.reshape(n_rows, d_rnn), attn_out_g[l][None], wo[:d_attn], wo[d_attn:],
                   ln2_g[l][None], w_gate[l].astype(BF16), w_up[l].astype(BF16),
                   w_down[l].astype(BF16), tmo)
    return out.reshape(bsz, seq, d_model)
```

```python
import functools
import math

import jax
import jax.numpy as jnp
from jax import lax
from jax.experimental import pallas as pl
from jax.experimental.pallas import tpu as pltpu

F32 = jnp.float32
BF16 = jnp.bfloat16

N_META_TOK = 16
N_HEAD = 8
NOPE = 64
ROPE = 32
HEAD_QK = NOPE + ROPE
HEAD_V = 64
LRU_SCALE = 8.0
THETA = 10000.0
NORM_EPS = 1e-6
CONV_TAPS = 4
CONV_LEFT = 2

LANES = 128
SUBLANES = 8
VMEM_LIMIT = 56 * 1024 * 1024

LOG2E = 1.4426950408889634


def _rsqrt_mean(x, n):
    return lax.rsqrt(jnp.sum(x * x, axis=-1, keepdims=True) * (1.0 / n) + NORM_EPS)


def _proj_kernel(x_ref, ln1_ref, win_ref, qag_ref, wuq_ref, kvag_ref, wuk_ref, wuv_ref,
                 cos_ref, sin_ref, gq_ref, gk_ref, hsum_ref,
                 q_out, k_out, v_out, xr_out, xg_out, *, q_lora, kv_lora, d_rnn):
    x = x_ref[0]
    d_model = x.shape[-1]
    hn = (x * _rsqrt_mean(x, d_model) * ln1_ref[...]).astype(BF16)
    p = jnp.dot(hn, win_ref[...], preferred_element_type=F32)
    o_kv = q_lora
    o_kr = o_kv + kv_lora
    o_xr = o_kr + LANES
    o_xg = o_xr + d_rnn
    c_q = p[:, :o_kv]
    c_kv = p[:, o_kv:o_kr]
    kr = p[:, o_kr:o_xr]
    n_slab = d_rnn // LANES
    for s in range(n_slab):
        xr_out[0, s] = p[:, o_xr + s * LANES:o_xr + (s + 1) * LANES]
        xg_out[0, s] = p[:, o_xg + s * LANES:o_xg + (s + 1) * LANES]

    cos_t = cos_ref[...]
    sin_t = sin_ref[...]
    inv_hd = 1.0 / HEAD_QK
    hsum = hsum_ref[...]

    def head_sums(sq_pair):
        return jnp.dot(sq_pair.astype(BF16), hsum, preferred_element_type=F32)

    cqn = (c_q * _rsqrt_mean(c_q, q_lora) * qag_ref[...]).astype(BF16)
    q = jnp.dot(cqn, wuq_ref[...], preferred_element_type=F32)
    cq = cos_t * gq_ref[0:1, :]
    sq = sin_t * gq_ref[1:2, :]
    q_scale = (HEAD_QK ** -0.5) * LOG2E
    for pr in range(N_HEAD // 2):
        qp = q[:, 2 * pr * LANES:2 * (pr + 1) * LANES]
        sc = lax.rsqrt(head_sums(qp * qp) * inv_hd + NORM_EPS) * q_scale
        for hh in range(2):
            h = 2 * pr + hh
            qh = qp[:, hh * LANES:(hh + 1) * LANES]
            qr = pltpu.roll(qh, LANES - ROPE, axis=1)
            q_out[0, :, h * LANES:(h + 1) * LANES] = (
                (qh * cq + qr * sq) * sc[:, hh * LANES:(hh + 1) * LANES]).astype(BF16)

    ckn = (c_kv * _rsqrt_mean(c_kv, kv_lora) * kvag_ref[...]).astype(BF16)
    kn = jnp.dot(ckn, wuk_ref[...], preferred_element_type=F32)
    v_out[0] = jnp.dot(ckn, wuv_ref[...], preferred_element_type=F32).T.astype(BF16)
    ck = cos_t * gk_ref[0:1, :]
    sk = sin_t * gk_ref[1:2, :]
    kr_sq = kr * kr
    kr_sq2 = jnp.concatenate([kr_sq, kr_sq], axis=1)
    kr_rot = kr * ck + pltpu.roll(kr, LANES - ROPE, axis=1) * sk
    for pr in range(N_HEAD // 2):
        kp = kn[:, 2 * pr * LANES:2 * (pr + 1) * LANES]
        sc = lax.rsqrt(head_sums(kp * kp + kr_sq2) * inv_hd + NORM_EPS)
        for hh in range(2):
            h = 2 * pr + hh
            kh = kp[:, hh * LANES:(hh + 1) * LANES]
            k_out[0, :, h * LANES:(h + 1) * LANES] = (
                (kh * ck + kr_rot) * sc[:, hh * LANES:(hh + 1) * LANES]).astype(BF16)


def _const_spec(shape):
    nd = len(shape)
    return pl.BlockSpec(shape, lambda *_: (0,) * nd)


def _project(x3, tm, cos_t, sin_t, wts, dims):
    q_lora, kv_lora, d_rnn = dims
    b, s, d = x3.shape
    n_slab = d_rnn // LANES
    nt = s // tm
    ln1, win, qag, wuq, kvag, wuk, wuv, gq, gk, hsum = wts
    row3 = lambda i, j: (i, j, 0)
    row4 = lambda i, j: (i, 0, j, 0)
    tab = lambda i, j: (j, 0)
    in_specs = [pl.BlockSpec((1, tm, d), row3)]
    in_specs += [_const_spec(w.shape) for w in (ln1, win, qag, wuq, kvag, wuk, wuv)]
    in_specs += [pl.BlockSpec((tm, LANES), tab), pl.BlockSpec((tm, LANES), tab)]
    in_specs += [_const_spec(gq.shape), _const_spec(gk.shape), _const_spec(hsum.shape)]
    out_shape = (
        jax.ShapeDtypeStruct((b, s, N_HEAD * LANES), BF16),
        jax.ShapeDtypeStruct((b, s, N_HEAD * LANES), BF16),
        jax.ShapeDtypeStruct((b, N_HEAD * HEAD_V, s), BF16),
        jax.ShapeDtypeStruct((b, n_slab, s, LANES), F32),
        jax.ShapeDtypeStruct((b, n_slab, s, LANES), F32),
    )
    out_specs = (
        pl.BlockSpec((1, tm, N_HEAD * LANES), row3),
        pl.BlockSpec((1, tm, N_HEAD * LANES), row3),
        pl.BlockSpec((1, N_HEAD * HEAD_V, tm), lambda i, j: (i, 0, j)),
        pl.BlockSpec((1, n_slab, tm, LANES), row4),
        pl.BlockSpec((1, n_slab, tm, LANES), row4),
    )
    return pl.pallas_call(
        functools.partial(_proj_kernel, q_lora=q_lora, kv_lora=kv_lora, d_rnn=d_rnn),
        grid=(b, nt),
        in_specs=in_specs,
        out_specs=out_specs,
        out_shape=out_shape,
        compiler_params=pltpu.CompilerParams(
            dimension_semantics=("arbitrary", "arbitrary"), vmem_limit_bytes=VMEM_LIMIT),
        name="proj",
    )(x3, ln1, win, qag, wuq, kvag, wuk, wuv, cos_t, sin_t, gq, gk, hsum)


def _scan_stride(t_len):
    s = -(-t_len // SUBLANES)
    while s % SUBLANES == 0:
        s += 1
    if s % 2:
        s += 1
        if s % SUBLANES == 0:
            s += 2
    return s


def _chunk_len(stride, cap=48):
    for c in range(min(cap, stride), 0, -1):
        if stride % c == 0:
            return c
    return 1


def _rnn_kernel(xr_ref, xg_ref, xrm_ref, cw_ref, cb_ref, wg_ref, ba_ref, bi_ref, lam_ref, g_ref,
                o_ref, t1, af, bf, ab, bb, *, seq, stride, rc, tm_out):
    n_slab = xr_ref.shape[1]
    t_len = N_META_TOK + seq
    pre = SUBLANES
    rows = t1.shape[1]
    n_chunk = stride // rc
    crow = rc * SUBLANES

    for s in range(n_slab):
        t1[s, 0:pre, :] = jnp.zeros((pre, LANES), F32)
        t1[s, pre:pre + N_META_TOK, :] = xrm_ref[0, s]
        t1[s, pre + N_META_TOK:pre + t_len, :] = xr_ref[0, s]
        t1[s, pre + t_len:rows, :] = jnp.zeros((rows - pre - t_len, LANES), F32)

    lam = lam_ref[...]
    nl = -lam
    softplus = jnp.maximum(nl, 0.0) + jnp.log(1.0 + jnp.exp(-jnp.abs(nl)))
    cn = LRU_SCALE * softplus
    c2 = -cn * LOG2E
    cw = cw_ref[...]
    cb = cb_ref[...]
    ba = ba_ref[...]
    bi = bi_ref[...]

    ridx = lax.broadcasted_iota(jnp.int32, (crow, LANES), 0)
    t_local = (ridx & (SUBLANES - 1)) * stride + (ridx >> 3)

    def gate_chunk(c, carry):
        r0 = c * rc
        row0 = pl.multiple_of(c * crow, SUBLANES)
        valid = (t_local + r0) < t_len
        for s in range(n_slab):
            ls = slice(s * LANES, (s + 1) * LANES)
            taps = [t1[s, pl.ds(r0 + pre - CONV_LEFT + j, SUBLANES, stride=stride), :]
                    for j in range(rc + CONV_TAPS - 1)]
            xc_rows = []
            for rr in range(rc):
                acc = taps[rr] * cw[0:1, ls]
                for j in range(1, CONV_TAPS):
                    acc = acc + taps[rr + j] * cw[j:j + 1, ls]
                xc_rows.append(acc + cb[:, ls])
            xc = jnp.concatenate(xc_rows, axis=0)
            z = jnp.dot(xc.astype(BF16), wg_ref[s], preferred_element_type=F32)
            for d, (a_ref, b_ref) in enumerate(((af, bf), (ab, bb))):
                za = z[:, (2 * d) * LANES:(2 * d + 1) * LANES] + ba[d:d + 1, ls]
                zi = z[:, (2 * d + 1) * LANES:(2 * d + 2) * LANES] + bi[d:d + 1, ls]
                r = 0.5 * jnp.tanh(0.5 * za) + 0.5
                ig = 0.5 * jnp.tanh(0.5 * zi) + 0.5
                a = jnp.exp2(r * c2[d:d + 1, ls])
                one_m_a2 = jnp.tanh(r * cn[d:d + 1, ls]) * (1.0 + a * a)
                bv = jnp.sqrt(jnp.maximum(one_m_a2, 0.0)) * (ig * xc)
                if d == 1:
                    bv = jnp.where(valid, bv, 0.0)
                a_ref[s, pl.ds(row0, crow), :] = a
                b_ref[s, pl.ds(row0, crow), :] = bv
        return carry

    lax.fori_loop(0, n_chunk, gate_chunk, 0)

    zero = jnp.zeros((SUBLANES, LANES), F32)
    one = jnp.ones((SUBLANES, LANES), F32)

    def scan_step(r, carry):
        hf, pf, hb, pb = carry
        rf = pl.multiple_of(r * SUBLANES, SUBLANES)
        rb = pl.multiple_of((stride - 1 - r) * SUBLANES, SUBLANES)
        nhf, npf, nhb, npb = [], [], [], []
        for s in range(n_slab):
            a = af[s, pl.ds(rf, SUBLANES), :]
            h = a * hf[s] + bf[s, pl.ds(rf, SUBLANES), :]
            pr = a * pf[s]
            bf[s, pl.ds(rf, SUBLANES), :] = h
            af[s, pl.ds(rf, SUBLANES), :] = pr
            nhf.append(h)
            npf.append(pr)
            a = ab[s, pl.ds(rb, SUBLANES), :]
            h = a * hb[s] + bb[s, pl.ds(rb, SUBLANES), :]
            pr = a * pb[s]
            bb[s, pl.ds(rb, SUBLANES), :] = h
            ab[s, pl.ds(rb, SUBLANES), :] = pr
            nhb.append(h)
            npb.append(pr)
        return tuple(nhf), tuple(npf), tuple(nhb), tuple(npb)

    init = ((zero,) * n_slab, (one,) * n_slab, (zero,) * n_slab, (one,) * n_slab)
    hf, pf, hb, pb = lax.fori_loop(0, stride, scan_step, init, unroll=2)

    sub = lax.broadcasted_iota(jnp.int32, (SUBLANES, LANES), 0)
    cf, cbk = [], []
    for s in range(n_slab):
        c = zero
        for _ in range(SUBLANES - 1):
            c = jnp.where(sub == 0, 0.0, pltpu.roll(pf[s] * c + hf[s], 1, axis=0))
        cf.append(c)
        c = zero
        for _ in range(SUBLANES - 1):
            c = jnp.where(sub == SUBLANES - 1, 0.0, pltpu.roll(pb[s] * c + hb[s], SUBLANES - 1, axis=0))
        cbk.append(c)

    def combine(r, carry):
        rp = pl.multiple_of(r * SUBLANES, SUBLANES)
        for s in range(n_slab):
            y = (bf[s, pl.ds(rp, SUBLANES), :] + af[s, pl.ds(rp, SUBLANES), :] * cf[s]
                 + bb[s, pl.ds(rp, SUBLANES), :] + ab[s, pl.ds(rp, SUBLANES), :] * cbk[s])
            t1[s, pl.ds(r + pre, SUBLANES, stride=stride), :] = y
        return carry

    lax.fori_loop(0, stride, combine, 0, unroll=2)

    d_rnn = n_slab * LANES
    g = g_ref[...]
    k0 = math.sqrt(2.0 / math.pi)

    def finish(j, carry):
        ro = pl.multiple_of(j * tm_out, 2 * SUBLANES)
        rt = pl.multiple_of(j * tm_out + pre + N_META_TOK, SUBLANES)
        ys = []
        ssq = jnp.zeros((tm_out, 1), F32)
        for s in range(n_slab):
            y = t1[s, pl.ds(rt, tm_out), :]
            xg = xg_ref[0, s, pl.ds(ro, tm_out), :]
            gl = 0.5 * xg * (1.0 + jnp.tanh(k0 * (xg + 0.044715 * (xg * xg * xg))))
            y = y * gl
            ssq = ssq + jnp.sum(y * y, axis=-1, keepdims=True)
            ys.append(y)
        sc = lax.rsqrt(ssq * (1.0 / d_rnn) + NORM_EPS)
        for s in range(n_slab):
            ls = slice(s * LANES, (s + 1) * LANES)
            o_ref[0, pl.ds(ro, tm_out), ls] = (ys[s] * sc * g[:, ls]).astype(BF16)
        return carry

    lax.fori_loop(0, seq // tm_out, finish, 0)


def _rnn(xr, xg, xrm, cw, cb, wg, ba, bi, lam, g):
    b, n_slab, seq, _ = xr.shape
    t_len = N_META_TOK + seq
    stride = _scan_stride(t_len)
    rc = _chunk_len(stride)
    rows_perm = stride * SUBLANES
    rows_time = SUBLANES + rows_perm + SUBLANES
    tm_out = 256 if seq % 256 == 0 else seq
    blk = lambda i: (i, 0, 0, 0)
    in_specs = [
        pl.BlockSpec((1, n_slab, seq, LANES), blk),
        pl.BlockSpec((1, n_slab, seq, LANES), blk),
        _const_spec(xrm.shape),
    ] + [_const_spec(w.shape) for w in (cw, cb, wg, ba, bi, lam, g)]
    scratch = [pltpu.VMEM((n_slab, rows_time, LANES), F32)]
    scratch += [pltpu.VMEM((n_slab, rows_perm, LANES), F32) for _ in range(4)]
    return pl.pallas_call(
        functools.partial(_rnn_kernel, seq=seq, stride=stride, rc=rc, tm_out=tm_out),
        grid=(b,),
        in_specs=in_specs,
        out_specs=pl.BlockSpec((1, seq, n_slab * LANES), lambda i: (i, 0, 0)),
        out_shape=jax.ShapeDtypeStruct((b, seq, n_slab * LANES), BF16),
        scratch_shapes=scratch,
        compiler_params=pltpu.CompilerParams(
            dimension_semantics=("arbitrary",), vmem_limit_bytes=VMEM_LIMIT),
        name="rnn",
    )(xr, xg, xrm, cw, cb, wg, ba, bi, lam, g)


def _attn_kernel(q_ref, k_ref, vt_ref, km_ref, vtm_ref, o_ref, kfull, st_buf, *, tq):
    nt = (((1,), (1,)), ((), ()))
    n_key = k_ref.shape[1]
    kfull[0:n_key, :] = k_ref[0]
    kfull[n_key:n_key + N_META_TOK, :] = km_ref[0]

    n_stream = 2 * (n_key // tq)

    def scores(i):
        j, hh = divmod(i, 2)
        ls = slice(hh * LANES, (hh + 1) * LANES)
        st = lax.dot_general(kfull[:, ls], q_ref[0, j * tq:(j + 1) * tq, ls], nt,
                             preferred_element_type=F32)
        st_buf[i % 2] = st
        return jnp.max(st, axis=0, keepdims=True)

    def weighted_values(i, m):
        hh = i % 2
        vs = slice(hh * HEAD_V, (hh + 1) * HEAD_V)
        p = jnp.exp2(st_buf[i % 2] - m)
        l = jnp.sum(p, axis=0, keepdims=True)
        pb = p.astype(BF16)
        ot = jnp.dot(vt_ref[0, vs, :], pb[:n_key], preferred_element_type=F32)
        ot = ot + jnp.dot(vtm_ref[0, vs, :], pb[n_key:], preferred_element_type=F32)
        return ot * (1.0 / l)

    m_next = scores(0)
    outs = []
    for i in range(n_stream):
        m_cur = m_next
        if i + 1 < n_stream:
            m_next = scores(i + 1)
        outs.append(weighted_values(i, m_cur))
        if i % 2 == 1:
            j = i // 2
            o_ref[0, j * tq:(j + 1) * tq, :] = jnp.concatenate(outs, axis=0).T
            outs = []


def _attention(q, k, vt, km, vtm, tq):
    b, s, _ = q.shape
    n_pair = N_HEAD // 2
    return pl.pallas_call(
        functools.partial(_attn_kernel, tq=tq),
        grid=(b, n_pair),
        in_specs=[
            pl.BlockSpec((1, s, 2 * LANES), lambda i, p: (i, 0, p)),
            pl.BlockSpec((1, s, 2 * LANES), lambda i, p: (i, 0, p)),
            pl.BlockSpec((1, 2 * HEAD_V, s), lambda i, p: (i, p, 0)),
            pl.BlockSpec((1, N_META_TOK, 2 * LANES), lambda i, p: (0, 0, p)),
            pl.BlockSpec((1, 2 * HEAD_V, N_META_TOK), lambda i, p: (0, p, 0)),
        ],
        out_specs=pl.BlockSpec((1, s, 2 * HEAD_V), lambda i, p: (i, 0, p)),
        out_shape=jax.ShapeDtypeStruct((b, s, N_HEAD * HEAD_V), F32),
        scratch_shapes=[pltpu.VMEM((s + N_META_TOK, 2 * LANES), BF16),
                        pltpu.VMEM((2, s + N_META_TOK, tq), F32)],
        compiler_params=pltpu.CompilerParams(
            dimension_semantics=("arbitrary", "arbitrary"), vmem_limit_bytes=VMEM_LIMIT),
        name="attn",
    )(q, k, vt, km, vtm)


def _out_kernel(x_ref, oa_ref, on_ref, ag_ref, woa_ref, wor_ref, ln2_ref, wg_ref, wu_ref, wd_ref,
                o_ref, *, ff_chunk):
    oa = oa_ref[...]
    oan = (oa * _rsqrt_mean(oa, oa.shape[-1]) * ag_ref[...]).astype(BF16)
    h = x_ref[...] + jnp.dot(oan, woa_ref[...], preferred_element_type=F32)
    h = h + jnp.dot(on_ref[...], wor_ref[...], preferred_element_type=F32)
    hn = (h * _rsqrt_mean(h, h.shape[-1]) * ln2_ref[...]).astype(BF16)
    d_ff = wg_ref.shape[1]
    acc = h
    for c in range(d_ff // ff_chunk):
        cs = slice(c * ff_chunk, (c + 1) * ff_chunk)
        gt = jnp.dot(hn, wg_ref[:, cs], preferred_element_type=F32)
        up = jnp.dot(hn, wu_ref[:, cs], preferred_element_type=F32)
        act = (gt * jax.nn.sigmoid(gt) * up).astype(BF16)
        acc = acc + jnp.dot(act, wd_ref[cs, :], preferred_element_type=F32)
    o_ref[...] = acc


def _out_ffn(x2, oa2, on2, ag, woa, wor, ln2, wg, wu, wd, tm):
    n, d = x2.shape
    d_ff = wg.shape[1]
    ff_chunk = d_ff
    row = lambda i: (i, 0)
    single = pl.Buffered(1)
    wspec = lambda w: pl.BlockSpec(w.shape, lambda i: (0, 0))
    return pl.pallas_call(
        functools.partial(_out_kernel, ff_chunk=ff_chunk),
        grid=(n // tm,),
        in_specs=[
            pl.BlockSpec((tm, d), row),
            pl.BlockSpec((tm, oa2.shape[1]), row),
            pl.BlockSpec((tm, on2.shape[1]), row),
            wspec(ag), wspec(woa), wspec(wor), wspec(ln2), wspec(wg), wspec(wu), wspec(wd),
        ],
        out_specs=pl.BlockSpec((tm, d), row),
        out_shape=jax.ShapeDtypeStruct((n, d), F32),
        compiler_params=pltpu.CompilerParams(
            dimension_semantics=("arbitrary",), vmem_limit_bytes=VMEM_LIMIT),
        name="out_ffn",
    )(x2, oa2, on2, ag, woa, wor, ln2, wg, wu, wd)


def _rope_tables(t_len):
    half = ROPE // 2
    freqs = 1.0 / (THETA ** (jnp.arange(half, dtype=F32) / half))
    ang = jnp.arange(t_len, dtype=F32)[:, None] * freqs[None, :]
    cos2 = jnp.concatenate([jnp.cos(ang)] * 2, axis=-1)
    sin2 = jnp.concatenate([jnp.sin(ang)] * 2, axis=-1)
    pad = LANES - HEAD_QK
    cos_t = jnp.concatenate([jnp.ones((t_len, NOPE), F32), cos2, jnp.zeros((t_len, pad), F32)], -1)
    sin_t = jnp.concatenate([jnp.zeros((t_len, NOPE), F32), sin2, jnp.zeros((t_len, pad), F32)], -1)
    return cos_t, sin_t


def _rot_half_cols(w):
    half = ROPE // 2
    return jnp.concatenate([-w[..., half:], w[..., :half]], axis=-1)


def _gain_rows(g):
    half = ROPE // 2
    pad = jnp.zeros((LANES - HEAD_QK,), F32)
    r0 = jnp.concatenate([g, pad])
    r1 = jnp.concatenate([jnp.zeros((NOPE,), F32), g[NOPE + half:], g[NOPE:NOPE + half], pad])
    return jnp.stack([r0, r1])


def _block_diag_pairs(w):
    g, n, _ = w.shape
    z = jnp.zeros((g // 2, n, n), w.dtype)
    top = jnp.concatenate([w[0::2], z], axis=2)
    bot = jnp.concatenate([z, w[1::2]], axis=2)
    return jnp.concatenate([top, bot], axis=1)


def kernel(x, meta_tokens, ln1_g, w_in, q_a_norm_g, w_uq, kv_a_norm_g, w_ukv, q_norm_g, k_norm_g,
           conv_w, conv_b, lru_wa, lru_ba, lru_wi, lru_bi, lru_lambda, attn_out_g, rnn_out_g,
           w_out, ln2_g, w_gate, w_up, w_down):
    bsz, seq, d_model = x.shape
    q_lora = q_a_norm_g.shape[-1]
    kv_lora = kv_a_norm_g.shape[-1]
    d_rnn = conv_w.shape[-1]
    d_attn = N_HEAD * HEAD_V
    t_len = N_META_TOK + seq
    l = 0

    wi = w_in[l]
    o_kr = q_lora + kv_lora
    w_kr = wi[:, o_kr:o_kr + ROPE]
    win = jnp.concatenate(
        [wi[:, :o_kr], jnp.zeros((d_model, NOPE), F32), w_kr, _rot_half_cols(w_kr),
         wi[:, o_kr + ROPE:]], axis=1).astype(BF16)
    wq = w_uq[l].reshape(q_lora, N_HEAD, HEAD_QK)
    wuq = jnp.concatenate([wq, _rot_half_cols(wq[..., NOPE:])], axis=-1)
    wuq = wuq.reshape(q_lora, N_HEAD * LANES).astype(BF16)
    wkv = w_ukv[l].reshape(kv_lora, N_HEAD, NOPE + HEAD_V)
    wuk = jnp.concatenate([wkv[..., :NOPE], jnp.zeros((kv_lora, N_HEAD, LANES - NOPE), F32)], -1)
    wuk = wuk.reshape(kv_lora, N_HEAD * LANES).astype(BF16)
    wuv = wkv[..., NOPE:].reshape(kv_lora, d_attn).astype(BF16)
    gq = _gain_rows(q_norm_g[l])
    gk = _gain_rows(k_norm_g[l])
    cos_t, sin_t = _rope_tables(t_len)
    lane2 = jnp.arange(2 * LANES)
    hsum = ((lane2[:, None] // LANES == lane2[None, :] // LANES)
            & (lane2[:, None] % LANES < HEAD_QK)).astype(BF16)
    proj_w = (ln1_g[l][None], win, q_a_norm_g[l][None], wuq, kv_a_norm_g[l][None], wuk, wuv, gq, gk, hsum)
    dims = (q_lora, kv_lora, d_rnn)

    wgate = jnp.concatenate(
        [_block_diag_pairs(w) for w in (lru_wa[l, 0], lru_wi[l, 0], lru_wa[l, 1], lru_wi[l, 1])],
        axis=2).astype(BF16)

    tm = 512 if seq % 512 == 0 else seq
    q, k, v, xr, xg = _project(x, tm, cos_t[N_META_TOK:], sin_t[N_META_TOK:], proj_w, dims)
    _, km, vm, xrm, _ = _project(meta_tokens[None].astype(x.dtype), N_META_TOK,
                                 cos_t[:N_META_TOK], sin_t[:N_META_TOK], proj_w, dims)

    o_rnn = _rnn(xr, xg, xrm, conv_w[l], conv_b[l][None], wgate, lru_ba[l], lru_bi[l],
                 lru_lambda[l], rnn_out_g[l][None])

    tq = 512 if seq % 512 == 0 else seq
    o_attn = _attention(q, k, v, km, vm, tq)

    wo = w_out[l].astype(BF16)
    n_rows = bsz * seq
    tmo = 256 if n_rows % 256 == 0 else n_rows
    out = _out_ffn(x.reshape(n_rows, d_model), o_attn.reshape(n_rows, d_attn),
                   o_rnn.reshape(n_rows, d_rnn), attn_out_g[l][None], wo[:d_attn], wo[d_attn:],
                   ln2_g[l][None], w_gate[l].astype(BF16), w_up[l].astype(BF16),
                   w_down[l].astype(BF16), tmo)
    return out.reshape(bsz, seq, d_model)
```

```python
import functools
import math

import jax
import jax.numpy as jnp
from jax import lax
from jax.experimental import pallas as pl
from jax.experimental.pallas import tpu as pltpu

F32 = jnp.float32
BF16 = jnp.bfloat16

N_META_TOK = 16
N_HEAD = 8
NOPE = 64
ROPE = 32
HEAD_QK = NOPE + ROPE
HEAD_V = 64
VT_ROWS = HEAD_V + 16
LRU_SCALE = 8.0
THETA = 10000.0
NORM_EPS = 1e-6
CONV_TAPS = 4
CONV_LEFT = 2

LANES = 128
SUBLANES = 8
VMEM_LIMIT = 56 * 1024 * 1024
KEY_TILE = 256

LOG2E = 1.4426950408889634


def _rsqrt_mean(x, n):
    return lax.rsqrt(jnp.sum(x * x, axis=-1, keepdims=True) * (1.0 / n) + NORM_EPS)


def _proj_kernel(x_ref, ln1_ref, win_ref, qag_ref, wuq_ref, kvag_ref, wuk_ref, wuv_ref,
                 cos_ref, sin_ref, gq_ref, gk_ref, hsum_ref,
                 q_out, k_out, v_out, xr_out, xg_out, *, q_lora, kv_lora, d_rnn):
    x = x_ref[0]
    d_model = x.shape[-1]
    hn = (x * _rsqrt_mean(x, d_model) * ln1_ref[...]).astype(BF16)
    p = jnp.dot(hn, win_ref[...], preferred_element_type=F32)
    o_kv = q_lora
    o_kr = o_kv + kv_lora
    o_xr = o_kr + LANES
    o_xg = o_xr + d_rnn
    c_q = p[:, :o_kv]
    c_kv = p[:, o_kv:o_kr]
    kr = p[:, o_kr:o_xr]
    n_slab = d_rnn // LANES
    for s in range(n_slab):
        xr_out[0, s] = p[:, o_xr + s * LANES:o_xr + (s + 1) * LANES]
        xg_out[0, s] = p[:, o_xg + s * LANES:o_xg + (s + 1) * LANES]

    cos_t = cos_ref[...]
    sin_t = sin_ref[...]
    inv_hd = 1.0 / HEAD_QK
    hsum = hsum_ref[...]

    def head_sums(sq_pair):
        return jnp.dot(sq_pair.astype(BF16), hsum, preferred_element_type=F32)

    cqn = (c_q * _rsqrt_mean(c_q, q_lora) * qag_ref[...]).astype(BF16)
    q = jnp.dot(cqn, wuq_ref[...], preferred_element_type=F32)
    cq = cos_t * gq_ref[0:1, :]
    sq = sin_t * gq_ref[1:2, :]
    q_scale = (HEAD_QK ** -0.5) * LOG2E
    for pr in range(N_HEAD // 2):
        qp = q[:, 2 * pr * LANES:2 * (pr + 1) * LANES]
        sc = lax.rsqrt(head_sums(qp * qp) * inv_hd + NORM_EPS) * q_scale
        for hh in range(2):
            h = 2 * pr + hh
            qh = qp[:, hh * LANES:(hh + 1) * LANES]
            qr = pltpu.roll(qh, LANES - ROPE, axis=1)
            q_out[0, :, h * LANES:(h + 1) * LANES] = (
                (qh * cq + qr * sq) * sc[:, hh * LANES:(hh + 1) * LANES]).astype(BF16)

    ckn = (c_kv * _rsqrt_mean(c_kv, kv_lora) * kvag_ref[...]).astype(BF16)
    kn = jnp.dot(ckn, wuk_ref[...], preferred_element_type=F32)
    vt = jnp.dot(ckn, wuv_ref[...], preferred_element_type=F32).T
    tm = vt.shape[1]
    ones_rows = (lax.broadcasted_iota(jnp.int32, (VT_ROWS - HEAD_V, tm), 0) == 0).astype(BF16)
    for h in range(N_HEAD):
        v_out[0, h * VT_ROWS:h * VT_ROWS + HEAD_V, :] = vt[h * HEAD_V:(h + 1) * HEAD_V].astype(BF16)
        v_out[0, h * VT_ROWS + HEAD_V:(h + 1) * VT_ROWS, :] = ones_rows
    ck = cos_t * gk_ref[0:1, :]
    sk = sin_t * gk_ref[1:2, :]
    kr_sq = kr * kr
    kr_sq2 = jnp.concatenate([kr_sq, kr_sq], axis=1)
    kr_rot = kr * ck + pltpu.roll(kr, LANES - ROPE, axis=1) * sk
    for pr in range(N_HEAD // 2):
        kp = kn[:, 2 * pr * LANES:2 * (pr + 1) * LANES]
        sc = lax.rsqrt(head_sums(kp * kp + kr_sq2) * inv_hd + NORM_EPS)
        for hh in range(2):
            h = 2 * pr + hh
            kh = kp[:, hh * LANES:(hh + 1) * LANES]
            k_out[0, :, h * LANES:(h + 1) * LANES] = (
                (kh * ck + kr_rot) * sc[:, hh * LANES:(hh + 1) * LANES]).astype(BF16)


def _const_spec(shape):
    nd = len(shape)
    return pl.BlockSpec(shape, lambda *_: (0,) * nd)


def _project(x3, tm, cos_t, sin_t, wts, dims):
    q_lora, kv_lora, d_rnn = dims
    b, s, d = x3.shape
    n_slab = d_rnn // LANES
    nt = s // tm
    ln1, win, qag, wuq, kvag, wuk, wuv, gq, gk, hsum = wts
    row3 = lambda i, j: (i, j, 0)
    row4 = lambda i, j: (i, 0, j, 0)
    tab = lambda i, j: (j, 0)
    in_specs = [pl.BlockSpec((1, tm, d), row3)]
    in_specs += [_const_spec(w.shape) for w in (ln1, win, qag, wuq, kvag, wuk, wuv)]
    in_specs += [pl.BlockSpec((tm, LANES), tab), pl.BlockSpec((tm, LANES), tab)]
    in_specs += [_const_spec(gq.shape), _const_spec(gk.shape), _const_spec(hsum.shape)]
    out_shape = (
        jax.ShapeDtypeStruct((b, s, N_HEAD * LANES), BF16),
        jax.ShapeDtypeStruct((b, s, N_HEAD * LANES), BF16),
        jax.ShapeDtypeStruct((b, N_HEAD * VT_ROWS, s), BF16),
        jax.ShapeDtypeStruct((b, n_slab, s, LANES), F32),
        jax.ShapeDtypeStruct((b, n_slab, s, LANES), F32),
    )
    out_specs = (
        pl.BlockSpec((1, tm, N_HEAD * LANES), row3),
        pl.BlockSpec((1, tm, N_HEAD * LANES), row3),
        pl.BlockSpec((1, N_HEAD * VT_ROWS, tm), lambda i, j: (i, 0, j)),
        pl.BlockSpec((1, n_slab, tm, LANES), row4),
        pl.BlockSpec((1, n_slab, tm, LANES), row4),
    )
    return pl.pallas_call(
        functools.partial(_proj_kernel, q_lora=q_lora, kv_lora=kv_lora, d_rnn=d_rnn),
        grid=(b, nt),
        in_specs=in_specs,
        out_specs=out_specs,
        out_shape=out_shape,
        compiler_params=pltpu.CompilerParams(
            dimension_semantics=("arbitrary", "arbitrary"), vmem_limit_bytes=VMEM_LIMIT),
        name="proj",
    )(x3, ln1, win, qag, wuq, kvag, wuk, wuv, cos_t, sin_t, gq, gk, hsum)


def _scan_stride(t_len):
    s = -(-t_len // SUBLANES)
    while s % SUBLANES == 0:
        s += 1
    if s % 2:
        s += 1
        if s % SUBLANES == 0:
            s += 2
    return s


def _chunk_len(stride, cap=48):
    for c in range(min(cap, stride), 0, -1):
        if stride % c == 0:
            return c
    return 1


def _rnn_kernel(xr_ref, xg_ref, xrm_ref, cw_ref, cb_ref, wg_ref, ba_ref, bi_ref, lam_ref, g_ref,
                o_ref, t1, af, bf, ab, bb, *, seq, stride, rc, tm_out):
    n_slab = xr_ref.shape[1]
    t_len = N_META_TOK + seq
    pre = SUBLANES
    rows = t1.shape[1]
    n_chunk = stride // rc
    crow = rc * SUBLANES

    for s in range(n_slab):
        t1[s, 0:pre, :] = jnp.zeros((pre, LANES), F32)
        t1[s, pre:pre + N_META_TOK, :] = xrm_ref[0, s]
        t1[s, pre + N_META_TOK:pre + t_len, :] = xr_ref[0, s]
        t1[s, pre + t_len:rows, :] = jnp.zeros((rows - pre - t_len, LANES), F32)

    lam = lam_ref[...]
    nl = -lam
    softplus = jnp.maximum(nl, 0.0) + jnp.log(1.0 + jnp.exp(-jnp.abs(nl)))
    cn = LRU_SCALE * softplus
    c2 = -cn * LOG2E
    cw = cw_ref[...]
    cb = cb_ref[...]
    ba = ba_ref[...]
    bi = bi_ref[...]

    ridx = lax.broadcasted_iota(jnp.int32, (crow, LANES), 0)
    t_local = (ridx & (SUBLANES - 1)) * stride + (ridx >> 3)

    def gate_chunk(c, carry):
        r0 = c * rc
        row0 = pl.multiple_of(c * crow, SUBLANES)
        valid = (t_local + r0) < t_len
        for s in range(n_slab):
            ls = slice(s * LANES, (s + 1) * LANES)
            taps = [t1[s, pl.ds(r0 + pre - CONV_LEFT + j, SUBLANES, stride=stride), :]
                    for j in range(rc + CONV_TAPS - 1)]
            xc_rows = []
            for rr in range(rc):
                acc = taps[rr] * cw[0:1, ls]
                for j in range(1, CONV_TAPS):
                    acc = acc + taps[rr + j] * cw[j:j + 1, ls]
                xc_rows.append(acc + cb[:, ls])
            xc = jnp.concatenate(xc_rows, axis=0)
            z = jnp.dot(xc.astype(BF16), wg_ref[s], preferred_element_type=F32)
            for d, (a_ref, b_ref) in enumerate(((af, bf), (ab, bb))):
                za = z[:, (2 * d) * LANES:(2 * d + 1) * LANES] + ba[d:d + 1, ls]
                zi = z[:, (2 * d + 1) * LANES:(2 * d + 2) * LANES] + bi[d:d + 1, ls]
                r = 0.5 * jnp.tanh(0.5 * za) + 0.5
                ig = 0.5 * jnp.tanh(0.5 * zi) + 0.5
                a = jnp.exp2(r * c2[d:d + 1, ls])
                one_m_a2 = jnp.tanh(r * cn[d:d + 1, ls]) * (1.0 + a * a)
                bv = jnp.sqrt(jnp.maximum(one_m_a2, 0.0)) * (ig * xc)
                if d == 1:
                    bv = jnp.where(valid, bv, 0.0)
                a_ref[s, pl.ds(row0, crow), :] = a
                b_ref[s, pl.ds(row0, crow), :] = bv
        return carry

    lax.fori_loop(0, n_chunk, gate_chunk, 0)

    zero = jnp.zeros((SUBLANES, LANES), F32)
    one = jnp.ones((SUBLANES, LANES), F32)

    def scan_step(r, carry):
        hf, pf, hb, pb = carry
        rf = pl.multiple_of(r * SUBLANES, SUBLANES)
        rb = pl.multiple_of((stride - 1 - r) * SUBLANES, SUBLANES)
        nhf, npf, nhb, npb = [], [], [], []
        for s in range(n_slab):
            a = af[s, pl.ds(rf, SUBLANES), :]
            h = a * hf[s] + bf[s, pl.ds(rf, SUBLANES), :]
            pr = a * pf[s]
            bf[s, pl.ds(rf, SUBLANES), :] = h
            af[s, pl.ds(rf, SUBLANES), :] = pr
            nhf.append(h)
            npf.append(pr)
            a = ab[s, pl.ds(rb, SUBLANES), :]
            h = a * hb[s] + bb[s, pl.ds(rb, SUBLANES), :]
            pr = a * pb[s]
            bb[s, pl.ds(rb, SUBLANES), :] = h
            ab[s, pl.ds(rb, SUBLANES), :] = pr
            nhb.append(h)
            npb.append(pr)
        return tuple(nhf), tuple(npf), tuple(nhb), tuple(npb)

    init = ((zero,) * n_slab, (one,) * n_slab, (zero,) * n_slab, (one,) * n_slab)
    hf, pf, hb, pb = lax.fori_loop(0, stride, scan_step, init, unroll=2)

    sub = lax.broadcasted_iota(jnp.int32, (SUBLANES, LANES), 0)
    cf, cbk = [], []
    for s in range(n_slab):
        c = zero
        for _ in range(SUBLANES - 1):
            c = jnp.where(sub == 0, 0.0, pltpu.roll(pf[s] * c + hf[s], 1, axis=0))
        cf.append(c)
        c = zero
        for _ in range(SUBLANES - 1):
            c = jnp.where(sub == SUBLANES - 1, 0.0, pltpu.roll(pb[s] * c + hb[s], SUBLANES - 1, axis=0))
        cbk.append(c)

    def combine(r, carry):
        rp = pl.multiple_of(r * SUBLANES, SUBLANES)
        for s in range(n_slab):
            y = (bf[s, pl.ds(rp, SUBLANES), :] + af[s, pl.ds(rp, SUBLANES), :] * cf[s]
                 + bb[s, pl.ds(rp, SUBLANES), :] + ab[s, pl.ds(rp, SUBLANES), :] * cbk[s])
            t1[s, pl.ds(r + pre, SUBLANES, stride=stride), :] = y
        return carry

    lax.fori_loop(0, stride, combine, 0, unroll=2)

    d_rnn = n_slab * LANES
    g = g_ref[...]
    k0 = math.sqrt(2.0 / math.pi)

    def finish(j, carry):
        ro = pl.multiple_of(j * tm_out, 2 * SUBLANES)
        rt = pl.multiple_of(j * tm_out + pre + N_META_TOK, SUBLANES)
        ys = []
        ssq = jnp.zeros((tm_out, 1), F32)
        for s in range(n_slab):
            y = t1[s, pl.ds(rt, tm_out), :]
            xg = xg_ref[0, s, pl.ds(ro, tm_out), :]
            gl = 0.5 * xg * (1.0 + jnp.tanh(k0 * (xg + 0.044715 * (xg * xg * xg))))
            y = y * gl
            ssq = ssq + jnp.sum(y * y, axis=-1, keepdims=True)
            ys.append(y)
        sc = lax.rsqrt(ssq * (1.0 / d_rnn) + NORM_EPS)
        for s in range(n_slab):
            ls = slice(s * LANES, (s + 1) * LANES)
            o_ref[0, pl.ds(ro, tm_out), ls] = (ys[s] * sc * g[:, ls]).astype(BF16)
        return carry

    lax.fori_loop(0, seq // tm_out, finish, 0)


def _rnn(xr, xg, xrm, cw, cb, wg, ba, bi, lam, g):
    b, n_slab, seq, _ = xr.shape
    t_len = N_META_TOK + seq
    stride = _scan_stride(t_len)
    rc = _chunk_len(stride)
    rows_perm = stride * SUBLANES
    rows_time = SUBLANES + rows_perm + SUBLANES
    tm_out = 256 if seq % 256 == 0 else seq
    blk = lambda i: (i, 0, 0, 0)
    in_specs = [
        pl.BlockSpec((1, n_slab, seq, LANES), blk),
        pl.BlockSpec((1, n_slab, seq, LANES), blk),
        _const_spec(xrm.shape),
    ] + [_const_spec(w.shape) for w in (cw, cb, wg, ba, bi, lam, g)]
    scratch = [pltpu.VMEM((n_slab, rows_time, LANES), F32)]
    scratch += [pltpu.VMEM((n_slab, rows_perm, LANES), F32) for _ in range(4)]
    return pl.pallas_call(
        functools.partial(_rnn_kernel, seq=seq, stride=stride, rc=rc, tm_out=tm_out),
        grid=(b,),
        in_specs=in_specs,
        out_specs=pl.BlockSpec((1, seq, n_slab * LANES), lambda i: (i, 0, 0)),
        out_shape=jax.ShapeDtypeStruct((b, seq, n_slab * LANES), BF16),
        scratch_shapes=scratch,
        compiler_params=pltpu.CompilerParams(
            dimension_semantics=("arbitrary",), vmem_limit_bytes=VMEM_LIMIT),
        name="rnn",
    )(xr, xg, xrm, cw, cb, wg, ba, bi, lam, g)


def _attn_kernel(q_ref, k_ref, vt_ref, km_ref, vtm_ref, o_ref, kfull, *, tq):
    nt = (((1,), (1,)), ((), ()))
    n_key = k_ref.shape[1]
    kfull[0:n_key, :] = k_ref[0]
    kfull[n_key:n_key + N_META_TOK, :] = km_ref[0]

    n_stream = 2 * (n_key // tq)
    n_tile = n_key // KEY_TILE
    tiles = [(t * KEY_TILE, (t + 1) * KEY_TILE) for t in range(n_tile)]
    tiles[-1] = (tiles[-1][0], n_key + N_META_TOK)
    steps = [(i, t) for i in range(n_stream) for t in range(n_tile)]

    def score_tile(i, t):
        j, hh = divmod(i, 2)
        k0, k1 = tiles[t]
        ls = slice(hh * LANES, (hh + 1) * LANES)
        return lax.dot_general(kfull[k0:k1, ls], q_ref[0, j * tq:(j + 1) * tq, ls], nt,
                               preferred_element_type=F32)

    outs = []
    st_next = score_tile(*steps[0])
    m, acc = None, None
    for n, (i, t) in enumerate(steps):
        st = st_next
        if n + 1 < len(steps):
            st_next = score_tile(*steps[n + 1])
        k0, k1 = tiles[t]
        vs = slice((i % 2) * VT_ROWS, (i % 2 + 1) * VT_ROWS)
        st3 = st.reshape((k1 - k0) // SUBLANES, SUBLANES, tq)
        tmax = jnp.max(jnp.max(st3, axis=0), axis=0, keepdims=True)
        m_new = tmax if m is None else jnp.maximum(m, tmax)
        mb = jnp.broadcast_to(m_new, (SUBLANES, tq))
        pb = jnp.exp2(st3 - mb[None]).reshape(k1 - k0, tq).astype(BF16)
        part = jnp.dot(vt_ref[0, vs, k0:k0 + KEY_TILE], pb[:KEY_TILE], preferred_element_type=F32)
        if k1 - k0 > KEY_TILE:
            part = part + jnp.dot(vtm_ref[0, vs, :], pb[KEY_TILE:], preferred_element_type=F32)
        acc = part if m is None else acc * jnp.exp2(m - m_new) + part
        m = m_new
        if t == n_tile - 1:
            outs.append(acc[:HEAD_V] * (1.0 / acc[HEAD_V:HEAD_V + 1]))
            m, acc = None, None
            if i % 2 == 1:
                j = i // 2
                o_ref[0, j * tq:(j + 1) * tq, :] = jnp.concatenate(outs, axis=0).T
                outs = []


def _attention(q, k, vt, km, vtm, tq):
    b, s, _ = q.shape
    n_pair = N_HEAD // 2
    return pl.pallas_call(
        functools.partial(_attn_kernel, tq=tq),
        grid=(b, n_pair),
        in_specs=[
            pl.BlockSpec((1, s, 2 * LANES), lambda i, p: (i, 0, p)),
            pl.BlockSpec((1, s, 2 * LANES), lambda i, p: (i, 0, p)),
            pl.BlockSpec((1, 2 * VT_ROWS, s), lambda i, p: (i, p, 0)),
            pl.BlockSpec((1, N_META_TOK, 2 * LANES), lambda i, p: (0, 0, p)),
            pl.BlockSpec((1, 2 * VT_ROWS, N_META_TOK), lambda i, p: (0, p, 0)),
        ],
        out_specs=pl.BlockSpec((1, s, 2 * HEAD_V), lambda i, p: (i, 0, p)),
        out_shape=jax.ShapeDtypeStruct((b, s, N_HEAD * HEAD_V), F32),
        scratch_shapes=[pltpu.VMEM((s + N_META_TOK, 2 * LANES), BF16)],
        compiler_params=pltpu.CompilerParams(
            dimension_semantics=("arbitrary", "arbitrary"), vmem_limit_bytes=VMEM_LIMIT),
        name="attn",
    )(q, k, vt, km, vtm)


def _out_kernel(x_ref, oa_ref, on_ref, ag_ref, woa_ref, wor_ref, ln2_ref, wg_ref, wu_ref, wd_ref,
                o_ref, *, ff_chunk):
    oa = oa_ref[...]
    oan = (oa * _rsqrt_mean(oa, oa.shape[-1]) * ag_ref[...]).astype(BF16)
    h = x_ref[...] + jnp.dot(oan, woa_ref[...], preferred_element_type=F32)
    h = h + jnp.dot(on_ref[...], wor_ref[...], preferred_element_type=F32)
    hn = (h * _rsqrt_mean(h, h.shape[-1]) * ln2_ref[...]).astype(BF16)
    d_ff = wg_ref.shape[1]
    acc = h
    for c in range(d_ff // ff_chunk):
        cs = slice(c * ff_chunk, (c + 1) * ff_chunk)
        gt = jnp.dot(hn, wg_ref[:, cs], preferred_element_type=F32)
        up = jnp.dot(hn, wu_ref[:, cs], preferred_element_type=F32)
        act = (gt * jax.nn.sigmoid(gt) * up).astype(BF16)
        acc = acc + jnp.dot(act, wd_ref[cs, :], preferred_element_type=F32)
    o_ref[...] = acc


def _out_ffn(x2, oa2, on2, ag, woa, wor, ln2, wg, wu, wd, tm):
    n, d = x2.shape
    d_ff = wg.shape[1]
    ff_chunk = d_ff
    row = lambda i: (i, 0)
    single = pl.Buffered(1)
    wspec = lambda w: pl.BlockSpec(w.shape, lambda i: (0, 0))
    return pl.pallas_call(
        functools.partial(_out_kernel, ff_chunk=ff_chunk),
        grid=(n // tm,),
        in_specs=[
            pl.BlockSpec((tm, d), row),
            pl.BlockSpec((tm, oa2.shape[1]), row),
            pl.BlockSpec((tm, on2.shape[1]), row),
            wspec(ag), wspec(woa), wspec(wor), wspec(ln2), wspec(wg), wspec(wu), wspec(wd),
        ],
        out_specs=pl.BlockSpec((tm, d), row),
        out_shape=jax.ShapeDtypeStruct((n, d), F32),
        compiler_params=pltpu.CompilerParams(
            dimension_semantics=("arbitrary",), vmem_limit_bytes=VMEM_LIMIT),
        name="out_ffn",
    )(x2, oa2, on2, ag, woa, wor, ln2, wg, wu, wd)


def _rope_tables(t_len):
    half = ROPE // 2
    freqs = 1.0 / (THETA ** (jnp.arange(half, dtype=F32) / half))
    ang = jnp.arange(t_len, dtype=F32)[:, None] * freqs[None, :]
    cos2 = jnp.concatenate([jnp.cos(ang)] * 2, axis=-1)
    sin2 = jnp.concatenate([jnp.sin(ang)] * 2, axis=-1)
    pad = LANES - HEAD_QK
    cos_t = jnp.concatenate([jnp.ones((t_len, NOPE), F32), cos2, jnp.zeros((t_len, pad), F32)], -1)
    sin_t = jnp.concatenate([jnp.zeros((t_len, NOPE), F32), sin2, jnp.zeros((t_len, pad), F32)], -1)
    return cos_t, sin_t


def _rot_half_cols(w):
    half = ROPE // 2
    return jnp.concatenate([-w[..., half:], w[..., :half]], axis=-1)


def _gain_rows(g):
    half = ROPE // 2
    pad = jnp.zeros((LANES - HEAD_QK,), F32)
    r0 = jnp.concatenate([g, pad])
    r1 = jnp.concatenate([jnp.zeros((NOPE,), F32), g[NOPE + half:], g[NOPE:NOPE + half], pad])
    return jnp.stack([r0, r1])


def _block_diag_pairs(w):
    g, n, _ = w.shape
    z = jnp.zeros((g // 2, n, n), w.dtype)
    top = jnp.concatenate([w[0::2], z], axis=2)
    bot = jnp.concatenate([z, w[1::2]], axis=2)
    return jnp.concatenate([top, bot], axis=1)


def kernel(x, meta_tokens, ln1_g, w_in, q_a_norm_g, w_uq, kv_a_norm_g, w_ukv, q_norm_g, k_norm_g,
           conv_w, conv_b, lru_wa, lru_ba, lru_wi, lru_bi, lru_lambda, attn_out_g, rnn_out_g,
           w_out, ln2_g, w_gate, w_up, w_down):
    bsz, seq, d_model = x.shape
    q_lora = q_a_norm_g.shape[-1]
    kv_lora = kv_a_norm_g.shape[-1]
    d_rnn = conv_w.shape[-1]
    d_attn = N_HEAD * HEAD_V
    t_len = N_META_TOK + seq
    l = 0

    wi = w_in[l]
    o_kr = q_lora + kv_lora
    w_kr = wi[:, o_kr:o_kr + ROPE]
    win = jnp.concatenate(
        [wi[:, :o_kr], jnp.zeros((d_model, NOPE), F32), w_kr, _rot_half_cols(w_kr),
         wi[:, o_kr + ROPE:]], axis=1).astype(BF16)
    wq = w_uq[l].reshape(q_lora, N_HEAD, HEAD_QK)
    wuq = jnp.concatenate([wq, _rot_half_cols(wq[..., NOPE:])], axis=-1)
    wuq = wuq.reshape(q_lora, N_HEAD * LANES).astype(BF16)
    wkv = w_ukv[l].reshape(kv_lora, N_HEAD, NOPE + HEAD_V)
    wuk = jnp.concatenate([wkv[..., :NOPE], jnp.zeros((kv_lora, N_HEAD, LANES - NOPE), F32)], -1)
    wuk = wuk.reshape(kv_lora, N_HEAD * LANES).astype(BF16)
    wuv = wkv[..., NOPE:].reshape(kv_lora, d_attn).astype(BF16)
    gq = _gain_rows(q_norm_g[l])
    gk = _gain_rows(k_norm_g[l])
    cos_t, sin_t = _rope_tables(t_len)
    lane2 = jnp.arange(2 * LANES)
    hsum = ((lane2[:, None] // LANES == lane2[None, :] // LANES)
            & (lane2[:, None] % LANES < HEAD_QK)).astype(BF16)
    proj_w = (ln1_g[l][None], win, q_a_norm_g[l][None], wuq, kv_a_norm_g[l][None], wuk, wuv, gq, gk, hsum)
    dims = (q_lora, kv_lora, d_rnn)

    wgate = jnp.concatenate(
        [_block_diag_pairs(w) for w in (lru_wa[l, 0], lru_wi[l, 0], lru_wa[l, 1], lru_wi[l, 1])],
        axis=2).astype(BF16)

    tm = 512 if seq % 512 == 0 else seq
    q, k, v, xr, xg = _project(x, tm, cos_t[N_META_TOK:], sin_t[N_META_TOK:], proj_w, dims)
    _, km, vm, xrm, _ = _project(meta_tokens[None].astype(x.dtype), N_META_TOK,
                                 cos_t[:N_META_TOK], sin_t[:N_META_TOK], proj_w, dims)

    o_rnn = _rnn(xr, xg, xrm, conv_w[l], conv_b[l][None], wgate, lru_ba[l], lru_bi[l],
                 lru_lambda[l], rnn_out_g[l][None])

    tq = 1024 if seq % 1024 == 0 else seq
    o_attn = _attention(q, k, v, km, vm, tq)

    wo = w_out[l].astype(BF16)
    n_rows = bsz * seq
    tmo = 256 if n_rows % 256 == 0 else n_rows
    out = _out_ffn(x.reshape(n_rows, d_model), o_attn.reshape(n_rows, d_attn),
                   o_rnn.reshape(n_rows, d_rnn), attn_out_g[l][None], wo[:d_attn], wo[d_attn:],
                   ln2_g[l][None], w_gate[l].astype(BF16), w_up[l].astype(BF16),
                   w_down[l].astype(BF16), tmo)
    return out.reshape(bsz, seq, d_model)
```

```python
import functools
import math

import jax
import jax.numpy as jnp
from jax import lax
from jax.experimental import pallas as pl
from jax.experimental.pallas import tpu as pltpu

F32 = jnp.float32
BF16 = jnp.bfloat16

N_META_TOK = 16
N_HEAD = 8
NOPE = 64
ROPE = 32
HEAD_QK = NOPE + ROPE
HEAD_V = 64
VT_ROWS = HEAD_V + 16
LRU_SCALE = 8.0
THETA = 10000.0
NORM_EPS = 1e-6
CONV_TAPS = 4
CONV_LEFT = 2

LANES = 128
SUBLANES = 8
VMEM_LIMIT = 56 * 1024 * 1024
KEY_TILE = 256
SCORE_LOOKAHEAD = 2

LOG2E = 1.4426950408889634


def _rsqrt_mean(x, n):
    return lax.rsqrt(jnp.sum(x * x, axis=-1, keepdims=True) * (1.0 / n) + NORM_EPS)


def _proj_kernel(x_ref, ln1_ref, win_ref, qag_ref, wuq_ref, kvag_ref, wuk_ref, wuv_ref,
                 cos_ref, sin_ref, gq_ref, gk_ref, hsum_ref,
                 q_out, k_out, v_out, xr_out, xg_out, *, q_lora, kv_lora, d_rnn):
    x = x_ref[0]
    d_model = x.shape[-1]
    hn = (x * _rsqrt_mean(x, d_model) * ln1_ref[...]).astype(BF16)
    p = jnp.dot(hn, win_ref[...], preferred_element_type=F32)
    o_kv = q_lora
    o_kr = o_kv + kv_lora
    o_xr = o_kr + LANES
    o_xg = o_xr + d_rnn
    c_q = p[:, :o_kv]
    c_kv = p[:, o_kv:o_kr]
    kr = p[:, o_kr:o_xr]
    n_slab = d_rnn // LANES
    for s in range(n_slab):
        xr_out[0, s] = p[:, o_xr + s * LANES:o_xr + (s + 1) * LANES]
        xg_out[0, s] = p[:, o_xg + s * LANES:o_xg + (s + 1) * LANES]

    cos_t = cos_ref[...]
    sin_t = sin_ref[...]
    inv_hd = 1.0 / HEAD_QK
    hsum = hsum_ref[...]

    def head_sums(sq_pair):
        return jnp.dot(sq_pair.astype(BF16), hsum, preferred_element_type=F32)

    cqn = (c_q * _rsqrt_mean(c_q, q_lora) * qag_ref[...]).astype(BF16)
    q = jnp.dot(cqn, wuq_ref[...], preferred_element_type=F32)
    cq = cos_t * gq_ref[0:1, :]
    sq = sin_t * gq_ref[1:2, :]
    q_scale = (HEAD_QK ** -0.5) * LOG2E
    for pr in range(N_HEAD // 2):
        qp = q[:, 2 * pr * LANES:2 * (pr + 1) * LANES]
        sc = lax.rsqrt(head_sums(qp * qp) * inv_hd + NORM_EPS) * q_scale
        for hh in range(2):
            h = 2 * pr + hh
            qh = qp[:, hh * LANES:(hh + 1) * LANES]
            qr = pltpu.roll(qh, LANES - ROPE, axis=1)
            q_out[0, :, h * LANES:(h + 1) * LANES] = (
                (qh * cq + qr * sq) * sc[:, hh * LANES:(hh + 1) * LANES]).astype(BF16)

    ckn = (c_kv * _rsqrt_mean(c_kv, kv_lora) * kvag_ref[...]).astype(BF16)
    kn = jnp.dot(ckn, wuk_ref[...], preferred_element_type=F32)
    vt = jnp.dot(ckn, wuv_ref[...], preferred_element_type=F32).T
    tm = vt.shape[1]
    ones_rows = (lax.broadcasted_iota(jnp.int32, (VT_ROWS - HEAD_V, tm), 0) == 0).astype(BF16)
    for h in range(N_HEAD):
        v_out[0, h * VT_ROWS:h * VT_ROWS + HEAD_V, :] = vt[h * HEAD_V:(h + 1) * HEAD_V].astype(BF16)
        v_out[0, h * VT_ROWS + HEAD_V:(h + 1) * VT_ROWS, :] = ones_rows
    ck = cos_t * gk_ref[0:1, :]
    sk = sin_t * gk_ref[1:2, :]
    kr_sq = kr * kr
    kr_sq2 = jnp.concatenate([kr_sq, kr_sq], axis=1)
    kr_rot = kr * ck + pltpu.roll(kr, LANES - ROPE, axis=1) * sk
    for pr in range(N_HEAD // 2):
        kp = kn[:, 2 * pr * LANES:2 * (pr + 1) * LANES]
        sc = lax.rsqrt(head_sums(kp * kp + kr_sq2) * inv_hd + NORM_EPS)
        for hh in range(2):
            h = 2 * pr + hh
            kh = kp[:, hh * LANES:(hh + 1) * LANES]
            k_out[0, :, h * LANES:(h + 1) * LANES] = (
                (kh * ck + kr_rot) * sc[:, hh * LANES:(hh + 1) * LANES]).astype(BF16)


def _const_spec(shape):
    nd = len(shape)
    return pl.BlockSpec(shape, lambda *_: (0,) * nd)


def _project(x3, tm, cos_t, sin_t, wts, dims):
    q_lora, kv_lora, d_rnn = dims
    b, s, d = x3.shape
    n_slab = d_rnn // LANES
    nt = s // tm
    ln1, win, qag, wuq, kvag, wuk, wuv, gq, gk, hsum = wts
    row3 = lambda i, j: (i, j, 0)
    row4 = lambda i, j: (i, 0, j, 0)
    tab = lambda i, j: (j, 0)
    in_specs = [pl.BlockSpec((1, tm, d), row3)]
    in_specs += [_const_spec(w.shape) for w in (ln1, win, qag, wuq, kvag, wuk, wuv)]
    in_specs += [pl.BlockSpec((tm, LANES), tab), pl.BlockSpec((tm, LANES), tab)]
    in_specs += [_const_spec(gq.shape), _const_spec(gk.shape), _const_spec(hsum.shape)]
    out_shape = (
        jax.ShapeDtypeStruct((b, s, N_HEAD * LANES), BF16),
        jax.ShapeDtypeStruct((b, s, N_HEAD * LANES), BF16),
        jax.ShapeDtypeStruct((b, N_HEAD * VT_ROWS, s), BF16),
        jax.ShapeDtypeStruct((b, n_slab, s, LANES), F32),
        jax.ShapeDtypeStruct((b, n_slab, s, LANES), F32),
    )
    out_specs = (
        pl.BlockSpec((1, tm, N_HEAD * LANES), row3),
        pl.BlockSpec((1, tm, N_HEAD * LANES), row3),
        pl.BlockSpec((1, N_HEAD * VT_ROWS, tm), lambda i, j: (i, 0, j)),
        pl.BlockSpec((1, n_slab, tm, LANES), row4),
        pl.BlockSpec((1, n_slab, tm, LANES), row4),
    )
    return pl.pallas_call(
        functools.partial(_proj_kernel, q_lora=q_lora, kv_lora=kv_lora, d_rnn=d_rnn),
        grid=(b, nt),
        in_specs=in_specs,
        out_specs=out_specs,
        out_shape=out_shape,
        compiler_params=pltpu.CompilerParams(
            dimension_semantics=("arbitrary", "arbitrary"), vmem_limit_bytes=VMEM_LIMIT),
        name="proj",
    )(x3, ln1, win, qag, wuq, kvag, wuk, wuv, cos_t, sin_t, gq, gk, hsum)


def _scan_stride(t_len):
    s = -(-t_len // SUBLANES)
    while s % SUBLANES == 0:
        s += 1
    if s % 2:
        s += 1
        if s % SUBLANES == 0:
            s += 2
    return s


def _chunk_len(stride, cap=48):
    for c in range(min(cap, stride), 0, -1):
        if stride % c == 0:
            return c
    return 1


def _rnn_kernel(xr_ref, xg_ref, xrm_ref, cw_ref, cb_ref, wg_ref, ba_ref, bi_ref, lam_ref, g_ref,
                o_ref, t1, af, bf, ab, bb, *, seq, stride, rc, tm_out):
    n_slab = xr_ref.shape[1]
    t_len = N_META_TOK + seq
    pre = SUBLANES
    rows = t1.shape[1]
    n_chunk = stride // rc
    crow = rc * SUBLANES

    for s in range(n_slab):
        t1[s, 0:pre, :] = jnp.zeros((pre, LANES), F32)
        t1[s, pre:pre + N_META_TOK, :] = xrm_ref[0, s]
        t1[s, pre + N_META_TOK:pre + t_len, :] = xr_ref[0, s]
        t1[s, pre + t_len:rows, :] = jnp.zeros((rows - pre - t_len, LANES), F32)

    lam = lam_ref[...]
    nl = -lam
    softplus = jnp.maximum(nl, 0.0) + jnp.log(1.0 + jnp.exp(-jnp.abs(nl)))
    hcn = (0.5 * LRU_SCALE) * softplus
    cw = cw_ref[...]
    cb = cb_ref[...]
    ba = 0.5 * ba_ref[...]
    bi = 0.5 * bi_ref[...]

    padded = stride * SUBLANES > t_len
    if padded:
        ridx = lax.broadcasted_iota(jnp.int32, (crow, LANES), 0)
        t_local = (ridx & (SUBLANES - 1)) * stride + (ridx >> 3)

    def gate_chunk(c, carry):
        r0 = c * rc
        row0 = pl.multiple_of(c * crow, SUBLANES)
        for s in range(n_slab):
            ls = slice(s * LANES, (s + 1) * LANES)
            taps = [t1[s, pl.ds(r0 + pre - CONV_LEFT + j, SUBLANES, stride=stride), :]
                    for j in range(rc + CONV_TAPS - 1)]
            xc_rows = []
            for rr in range(rc):
                acc = taps[rr] * cw[0:1, ls]
                for j in range(1, CONV_TAPS):
                    acc = acc + taps[rr + j] * cw[j:j + 1, ls]
                xc_rows.append(acc + cb[:, ls])
            xc = jnp.concatenate(xc_rows, axis=0)
            hx = 0.5 * xc
            z = jnp.dot(xc.astype(BF16), wg_ref[s], preferred_element_type=F32)
            for d, (a_ref, b_ref) in enumerate(((af, bf), (ab, bb))):
                ta = jnp.tanh(z[:, (2 * d) * LANES:(2 * d + 1) * LANES] + ba[d:d + 1, ls])
                ti = jnp.tanh(z[:, (2 * d + 1) * LANES:(2 * d + 2) * LANES] + bi[d:d + 1, ls])
                hc = hcn[d:d + 1, ls]
                nla = ta * hc + hc
                a = jnp.exp2(nla * (-LOG2E))
                one_m_a2 = jnp.tanh(nla) * (1.0 + a * a)
                root = jnp.where(one_m_a2 > 0.0, one_m_a2 * lax.rsqrt(one_m_a2), 0.0)
                bv = root * (ti * hx + hx)
                if d == 1 and padded:
                    bv = jnp.where((t_local + r0) < t_len, bv, 0.0)
                a_ref[s, pl.ds(row0, crow), :] = a
                b_ref[s, pl.ds(row0, crow), :] = bv
        return carry

    lax.fori_loop(0, n_chunk, gate_chunk, 0)

    zero = jnp.zeros((SUBLANES, LANES), F32)
    one = jnp.ones((SUBLANES, LANES), F32)

    def scan_step(r, carry):
        hf, pf, hb, pb = carry
        rf = pl.multiple_of(r * SUBLANES, SUBLANES)
        rb = pl.multiple_of((stride - 1 - r) * SUBLANES, SUBLANES)
        nhf, npf, nhb, npb = [], [], [], []
        for s in range(n_slab):
            a = af[s, pl.ds(rf, SUBLANES), :]
            h = a * hf[s] + bf[s, pl.ds(rf, SUBLANES), :]
            pr = a * pf[s]
            bf[s, pl.ds(rf, SUBLANES), :] = h
            af[s, pl.ds(rf, SUBLANES), :] = pr
            nhf.append(h)
            npf.append(pr)
            a = ab[s, pl.ds(rb, SUBLANES), :]
            h = a * hb[s] + bb[s, pl.ds(rb, SUBLANES), :]
            pr = a * pb[s]
            bb[s, pl.ds(rb, SUBLANES), :] = h
            ab[s, pl.ds(rb, SUBLANES), :] = pr
            nhb.append(h)
            npb.append(pr)
        return tuple(nhf), tuple(npf), tuple(nhb), tuple(npb)

    init = ((zero,) * n_slab, (one,) * n_slab, (zero,) * n_slab, (one,) * n_slab)
    hf, pf, hb, pb = lax.fori_loop(0, stride, scan_step, init, unroll=2)

    sub = lax.broadcasted_iota(jnp.int32, (SUBLANES, LANES), 0)
    cf, cbk = [], []
    for s in range(n_slab):
        c = zero
        for _ in range(SUBLANES - 1):
            c = jnp.where(sub == 0, 0.0, pltpu.roll(pf[s] * c + hf[s], 1, axis=0))
        cf.append(c)
        c = zero
        for _ in range(SUBLANES - 1):
            c = jnp.where(sub == SUBLANES - 1, 0.0, pltpu.roll(pb[s] * c + hb[s], SUBLANES - 1, axis=0))
        cbk.append(c)

    def combine(r, carry):
        rp = pl.multiple_of(r * SUBLANES, SUBLANES)
        for s in range(n_slab):
            y = (bf[s, pl.ds(rp, SUBLANES), :] + af[s, pl.ds(rp, SUBLANES), :] * cf[s]
                 + bb[s, pl.ds(rp, SUBLANES), :] + ab[s, pl.ds(rp, SUBLANES), :] * cbk[s])
            t1[s, pl.ds(r + pre, SUBLANES, stride=stride), :] = y
        return carry

    lax.fori_loop(0, stride, combine, 0, unroll=2)

    d_rnn = n_slab * LANES
    g = g_ref[...]
    k0 = math.sqrt(2.0 / math.pi)

    def finish(j, carry):
        ro = pl.multiple_of(j * tm_out, 2 * SUBLANES)
        rt = pl.multiple_of(j * tm_out + pre + N_META_TOK, SUBLANES)
        ys = []
        ssq = jnp.zeros((tm_out, 1), F32)
        for s in range(n_slab):
            y = t1[s, pl.ds(rt, tm_out), :]
            xg = xg_ref[0, s, pl.ds(ro, tm_out), :]
            yh = y * (0.5 * xg)
            y = yh + yh * jnp.tanh(xg * (k0 + (k0 * 0.044715) * (xg * xg)))
            ssq = ssq + jnp.sum(y * y, axis=-1, keepdims=True)
            ys.append(y)
        sc = lax.rsqrt(ssq * (1.0 / d_rnn) + NORM_EPS)
        for s in range(n_slab):
            ls = slice(s * LANES, (s + 1) * LANES)
            o_ref[0, pl.ds(ro, tm_out), ls] = (ys[s] * sc * g[:, ls]).astype(BF16)
        return carry

    lax.fori_loop(0, seq // tm_out, finish, 0)


def _rnn(xr, xg, xrm, cw, cb, wg, ba, bi, lam, g):
    b, n_slab, seq, _ = xr.shape
    t_len = N_META_TOK + seq
    stride = _scan_stride(t_len)
    rc = _chunk_len(stride)
    rows_perm = stride * SUBLANES
    rows_time = SUBLANES + rows_perm + SUBLANES
    tm_out = 256 if seq % 256 == 0 else seq
    blk = lambda i: (i, 0, 0, 0)
    in_specs = [
        pl.BlockSpec((1, n_slab, seq, LANES), blk),
        pl.BlockSpec((1, n_slab, seq, LANES), blk),
        _const_spec(xrm.shape),
    ] + [_const_spec(w.shape) for w in (cw, cb, wg, ba, bi, lam, g)]
    scratch = [pltpu.VMEM((n_slab, rows_time, LANES), F32)]
    scratch += [pltpu.VMEM((n_slab, rows_perm, LANES), F32) for _ in range(4)]
    return pl.pallas_call(
        functools.partial(_rnn_kernel, seq=seq, stride=stride, rc=rc, tm_out=tm_out),
        grid=(b,),
        in_specs=in_specs,
        out_specs=pl.BlockSpec((1, seq, n_slab * LANES), lambda i: (i, 0, 0)),
        out_shape=jax.ShapeDtypeStruct((b, seq, n_slab * LANES), BF16),
        scratch_shapes=scratch,
        compiler_params=pltpu.CompilerParams(
            dimension_semantics=("arbitrary",), vmem_limit_bytes=VMEM_LIMIT),
        name="rnn",
    )(xr, xg, xrm, cw, cb, wg, ba, bi, lam, g)


def _attn_kernel(q_ref, k_ref, vt_ref, km_ref, vtm_ref, o_ref, kfull, *, tq):
    nt = (((1,), (1,)), ((), ()))
    n_key = k_ref.shape[1]
    kfull[0:n_key, :] = k_ref[0]
    kfull[n_key:n_key + N_META_TOK, :] = km_ref[0]

    n_stream = 2 * (n_key // tq)
    n_tile = n_key // KEY_TILE
    tiles = [(t * KEY_TILE, (t + 1) * KEY_TILE) for t in range(n_tile)]
    tiles[-1] = (tiles[-1][0], n_key + N_META_TOK)
    steps = [(i, t) for i in range(n_stream) for t in range(n_tile)]

    def score_tile(i, t):
        j, hh = divmod(i, 2)
        k0, k1 = tiles[t]
        ls = slice(hh * LANES, (hh + 1) * LANES)
        return lax.dot_general(kfull[k0:k1, ls], q_ref[0, j * tq:(j + 1) * tq, ls], nt,
                               preferred_element_type=F32)

    outs = []
    pending = [score_tile(*steps[n]) for n in range(SCORE_LOOKAHEAD)]
    m, acc = None, None
    for n, (i, t) in enumerate(steps):
        st = pending.pop(0)
        if n + SCORE_LOOKAHEAD < len(steps):
            pending.append(score_tile(*steps[n + SCORE_LOOKAHEAD]))
        k0, k1 = tiles[t]
        vs = slice((i % 2) * VT_ROWS, (i % 2 + 1) * VT_ROWS)
        st3 = st.reshape((k1 - k0) // SUBLANES, SUBLANES, tq)
        tmax = jnp.max(jnp.max(st3, axis=0), axis=0, keepdims=True)
        m_new = tmax if m is None else jnp.maximum(m, tmax)
        mb = jnp.broadcast_to(m_new, (SUBLANES, tq))
        pb = jnp.exp2(st3 - mb[None]).reshape(k1 - k0, tq).astype(BF16)
        part = jnp.dot(vt_ref[0, vs, k0:k0 + KEY_TILE], pb[:KEY_TILE], preferred_element_type=F32)
        if k1 - k0 > KEY_TILE:
            part = part + jnp.dot(vtm_ref[0, vs, :], pb[KEY_TILE:], preferred_element_type=F32)
        acc = part if m is None else acc * jnp.exp2(m - m_new) + part
        m = m_new
        if t == n_tile - 1:
            outs.append(acc[:HEAD_V] * (1.0 / acc[HEAD_V:HEAD_V + 1]))
            m, acc = None, None
            if i % 2 == 1:
                j = i // 2
                o_ref[0, j * tq:(j + 1) * tq, :] = jnp.concatenate(outs, axis=0).T
                outs = []


def _attention(q, k, vt, km, vtm, tq):
    b, s, _ = q.shape
    n_pair = N_HEAD // 2
    return pl.pallas_call(
        functools.partial(_attn_kernel, tq=tq),
        grid=(b, n_pair),
        in_specs=[
            pl.BlockSpec((1, s, 2 * LANES), lambda i, p: (i, 0, p)),
            pl.BlockSpec((1, s, 2 * LANES), lambda i, p: (i, 0, p)),
            pl.BlockSpec((1, 2 * VT_ROWS, s), lambda i, p: (i, p, 0)),
            pl.BlockSpec((1, N_META_TOK, 2 * LANES), lambda i, p: (0, 0, p)),
            pl.BlockSpec((1, 2 * VT_ROWS, N_META_TOK), lambda i, p: (0, p, 0)),
        ],
        out_specs=pl.BlockSpec((1, s, 2 * HEAD_V), lambda i, p: (i, 0, p)),
        out_shape=jax.ShapeDtypeStruct((b, s, N_HEAD * HEAD_V), F32),
        scratch_shapes=[pltpu.VMEM((s + N_META_TOK, 2 * LANES), BF16)],
        compiler_params=pltpu.CompilerParams(
            dimension_semantics=("arbitrary", "arbitrary"), vmem_limit_bytes=VMEM_LIMIT),
        name="attn",
    )(q, k, vt, km, vtm)


def _out_kernel(x_ref, oa_ref, on_ref, ag_ref, woa_ref, wor_ref, ln2_ref, wg_ref, wu_ref, wd_ref,
                o_ref, *, ff_chunk):
    oa = oa_ref[...]
    oan = (oa * _rsqrt_mean(oa, oa.shape[-1]) * ag_ref[...]).astype(BF16)
    h = x_ref[...] + jnp.dot(oan, woa_ref[...], preferred_element_type=F32)
    h = h + jnp.dot(on_ref[...], wor_ref[...], preferred_element_type=F32)
    hn = (h * _rsqrt_mean(h, h.shape[-1]) * ln2_ref[...]).astype(BF16)
    d_ff = wg_ref.shape[1]
    acc = h
    for c in range(d_ff // ff_chunk):
        cs = slice(c * ff_chunk, (c + 1) * ff_chunk)
        gt = jnp.dot(hn, wg_ref[:, cs], preferred_element_type=F32)
        up = jnp.dot(hn, wu_ref[:, cs], preferred_element_type=F32)
        act = (gt * jax.nn.sigmoid(gt) * up).astype(BF16)
        acc = acc + jnp.dot(act, wd_ref[cs, :], preferred_element_type=F32)
    o_ref[...] = acc


def _out_ffn(x2, oa2, on2, ag, woa, wor, ln2, wg, wu, wd, tm):
    n, d = x2.shape
    d_ff = wg.shape[1]
    ff_chunk = d_ff
    row = lambda i: (i, 0)
    single = pl.Buffered(1)
    wspec = lambda w: pl.BlockSpec(w.shape, lambda i: (0, 0))
    return pl.pallas_call(
        functools.partial(_out_kernel, ff_chunk=ff_chunk),
        grid=(n // tm,),
        in_specs=[
            pl.BlockSpec((tm, d), row),
            pl.BlockSpec((tm, oa2.shape[1]), row),
            pl.BlockSpec((tm, on2.shape[1]), row),
            wspec(ag), wspec(woa), wspec(wor), wspec(ln2), wspec(wg), wspec(wu), wspec(wd),
        ],
        out_specs=pl.BlockSpec((tm, d), row),
        out_shape=jax.ShapeDtypeStruct((n, d), F32),
        compiler_params=pltpu.CompilerParams(
            dimension_semantics=("arbitrary",), vmem_limit_bytes=VMEM_LIMIT),
        name="out_ffn",
    )(x2, oa2, on2, ag, woa, wor, ln2, wg, wu, wd)


def _rope_tables(t_len):
    half = ROPE // 2
    freqs = 1.0 / (THETA ** (jnp.arange(half, dtype=F32) / half))
    ang = jnp.arange(t_len, dtype=F32)[:, None] * freqs[None, :]
    cos2 = jnp.concatenate([jnp.cos(ang)] * 2, axis=-1)
    sin2 = jnp.concatenate([jnp.sin(ang)] * 2, axis=-1)
    pad = LANES - HEAD_QK
    cos_t = jnp.concatenate([jnp.ones((t_len, NOPE), F32), cos2, jnp.zeros((t_len, pad), F32)], -1)
    sin_t = jnp.concatenate([jnp.zeros((t_len, NOPE), F32), sin2, jnp.zeros((t_len, pad), F32)], -1)
    return cos_t, sin_t


def _rot_half_cols(w):
    half = ROPE // 2
    return jnp.concatenate([-w[..., half:], w[..., :half]], axis=-1)


def _gain_rows(g):
    half = ROPE // 2
    pad = jnp.zeros((LANES - HEAD_QK,), F32)
    r0 = jnp.concatenate([g, pad])
    r1 = jnp.concatenate([jnp.zeros((NOPE,), F32), g[NOPE + half:], g[NOPE:NOPE + half], pad])
    return jnp.stack([r0, r1])


def _block_diag_pairs(w):
    g, n, _ = w.shape
    z = jnp.zeros((g // 2, n, n), w.dtype)
    top = jnp.concatenate([w[0::2], z], axis=2)
    bot = jnp.concatenate([z, w[1::2]], axis=2)
    return jnp.concatenate([top, bot], axis=1)


def kernel(x, meta_tokens, ln1_g, w_in, q_a_norm_g, w_uq, kv_a_norm_g, w_ukv, q_norm_g, k_norm_g,
           conv_w, conv_b, lru_wa, lru_ba, lru_wi, lru_bi, lru_lambda, attn_out_g, rnn_out_g,
           w_out, ln2_g, w_gate, w_up, w_down):
    bsz, seq, d_model = x.shape
    q_lora = q_a_norm_g.shape[-1]
    kv_lora = kv_a_norm_g.shape[-1]
    d_rnn = conv_w.shape[-1]
    d_attn = N_HEAD * HEAD_V
    t_len = N_META_TOK + seq
    l = 0

    wi = w_in[l]
    o_kr = q_lora + kv_lora
    w_kr = wi[:, o_kr:o_kr + ROPE]
    win = jnp.concatenate(
        [wi[:, :o_kr], jnp.zeros((d_model, NOPE), F32), w_kr, _rot_half_cols(w_kr),
         wi[:, o_kr + ROPE:]], axis=1).astype(BF16)
    wq = w_uq[l].reshape(q_lora, N_HEAD, HEAD_QK)
    wuq = jnp.concatenate([wq, _rot_half_cols(wq[..., NOPE:])], axis=-1)
    wuq = wuq.reshape(q_lora, N_HEAD * LANES).astype(BF16)
    wkv = w_ukv[l].reshape(kv_lora, N_HEAD, NOPE + HEAD_V)
    wuk = jnp.concatenate([wkv[..., :NOPE], jnp.zeros((kv_lora, N_HEAD, LANES - NOPE), F32)], -1)
    wuk = wuk.reshape(kv_lora, N_HEAD * LANES).astype(BF16)
    wuv = wkv[..., NOPE:].reshape(kv_lora, d_attn).astype(BF16)
    gq = _gain_rows(q_norm_g[l])
    gk = _gain_rows(k_norm_g[l])
    cos_t, sin_t = _rope_tables(t_len)
    lane2 = jnp.arange(2 * LANES)
    hsum = ((lane2[:, None] // LANES == lane2[None, :] // LANES)
            & (lane2[:, None] % LANES < HEAD_QK)).astype(BF16)
    proj_w = (ln1_g[l][None], win, q_a_norm_g[l][None], wuq, kv_a_norm_g[l][None], wuk, wuv, gq, gk, hsum)
    dims = (q_lora, kv_lora, d_rnn)

    wgate = (0.5 * jnp.concatenate(
        [_block_diag_pairs(w) for w in (lru_wa[l, 0], lru_wi[l, 0], lru_wa[l, 1], lru_wi[l, 1])],
        axis=2)).astype(BF16)

    tm = 512 if seq % 512 == 0 else seq
    q, k, v, xr, xg = _project(x, tm, cos_t[N_META_TOK:], sin_t[N_META_TOK:], proj_w, dims)
    _, km, vm, xrm, _ = _project(meta_tokens[None].astype(x.dtype), N_META_TOK,
                                 cos_t[:N_META_TOK], sin_t[:N_META_TOK], proj_w, dims)

    o_rnn = _rnn(xr, xg, xrm, conv_w[l], conv_b[l][None], wgate, lru_ba[l], lru_bi[l],
                 lru_lambda[l], rnn_out_g[l][None])

    tq = 1024 if seq % 1024 == 0 else seq
    o_attn = _attention(q, k, v, km, vm, tq)

    wo = w_out[l].astype(BF16)
    n_rows = bsz * seq
    tmo = 256 if n_rows % 256 == 0 else n_rows
    out = _out_ffn(x.reshape(n_rows, d_model), o_attn.reshape(n_rows, d_attn),
                   o_rnn.reshape(n_rows, d_rnn), attn_out_g[l][None], wo[:d_attn], wo[d_attn:],
                   ln2_g[l][None], w_gate[l].astype(BF16), w_up[l].astype(BF16),
                   w_down[l].astype(BF16), tmo)
    return out.reshape(bsz, seq, d_model)
```

```python
import functools
import math

import jax
import jax.numpy as jnp
from jax import lax
from jax.experimental import pallas as pl
from jax.experimental.pallas import tpu as pltpu

F32 = jnp.float32
BF16 = jnp.bfloat16

N_META_TOK = 16
N_HEAD = 8
NOPE = 64
ROPE = 32
HEAD_QK = NOPE + ROPE
HEAD_V = 64
VT_ROWS = HEAD_V + 16
LRU_SCALE = 8.0
THETA = 10000.0
NORM_EPS = 1e-6
CONV_TAPS = 4
CONV_LEFT = 2

LANES = 128
SUBLANES = 8
VMEM_LIMIT = 56 * 1024 * 1024
KEY_TILE = 512
SCORE_LOOKAHEAD = 2

LOG2E = 1.4426950408889634


def _rsqrt_mean(x, n):
    return lax.rsqrt(jnp.sum(x * x, axis=-1, keepdims=True) * (1.0 / n) + NORM_EPS)


def _proj_kernel(x_ref, ln1_ref, win_ref, qag_ref, wuq_ref, kvag_ref, wuk_ref, wuv_ref,
                 cos_ref, sin_ref, gq_ref, gk_ref, hsum_ref,
                 q_out, k_out, v_out, xr_out, xg_out, *, q_lora, kv_lora, d_rnn):
    x = x_ref[0]
    d_model = x.shape[-1]
    hn = (x * _rsqrt_mean(x, d_model) * ln1_ref[...]).astype(BF16)
    p = jnp.dot(hn, win_ref[...], preferred_element_type=F32)
    o_kv = q_lora
    o_kr = o_kv + kv_lora
    o_xr = o_kr + LANES
    o_xg = o_xr + d_rnn
    c_q = p[:, :o_kv]
    c_kv = p[:, o_kv:o_kr]
    kr = p[:, o_kr:o_xr]
    n_slab = d_rnn // LANES
    for s in range(n_slab):
        xr_out[0, s] = p[:, o_xr + s * LANES:o_xr + (s + 1) * LANES]
        xg_out[0, s] = p[:, o_xg + s * LANES:o_xg + (s + 1) * LANES]

    cos_t = cos_ref[...]
    sin_t = sin_ref[...]
    inv_hd = 1.0 / HEAD_QK
    hsum = hsum_ref[...]

    def head_sums(sq_pair):
        return jnp.dot(sq_pair.astype(BF16), hsum, preferred_element_type=F32)

    cqn = (c_q * _rsqrt_mean(c_q, q_lora) * qag_ref[...]).astype(BF16)
    q = jnp.dot(cqn, wuq_ref[...], preferred_element_type=F32)
    cq = cos_t * gq_ref[0:1, :]
    sq = sin_t * gq_ref[1:2, :]
    q_scale = (HEAD_QK ** -0.5) * LOG2E
    for pr in range(N_HEAD // 2):
        qp = q[:, 2 * pr * LANES:2 * (pr + 1) * LANES]
        sc = lax.rsqrt(head_sums(qp * qp) * inv_hd + NORM_EPS) * q_scale
        for hh in range(2):
            h = 2 * pr + hh
            qh = qp[:, hh * LANES:(hh + 1) * LANES]
            qr = pltpu.roll(qh, LANES - ROPE, axis=1)
            q_out[0, :, h * LANES:(h + 1) * LANES] = (
                (qh * cq + qr * sq) * sc[:, hh * LANES:(hh + 1) * LANES]).astype(BF16)

    ckn = (c_kv * _rsqrt_mean(c_kv, kv_lora) * kvag_ref[...]).astype(BF16)
    kn = jnp.dot(ckn, wuk_ref[...], preferred_element_type=F32)
    vt = jnp.dot(ckn, wuv_ref[...], preferred_element_type=F32).T
    tm = vt.shape[1]
    ones_rows = (lax.broadcasted_iota(jnp.int32, (VT_ROWS - HEAD_V, tm), 0) == 0).astype(BF16)
    for h in range(N_HEAD):
        v_out[0, h * VT_ROWS:h * VT_ROWS + HEAD_V, :] = vt[h * HEAD_V:(h + 1) * HEAD_V].astype(BF16)
        v_out[0, h * VT_ROWS + HEAD_V:(h + 1) * VT_ROWS, :] = ones_rows
    ck = cos_t * gk_ref[0:1, :]
    sk = sin_t * gk_ref[1:2, :]
    kr_sq = kr * kr
    kr_sq2 = jnp.concatenate([kr_sq, kr_sq], axis=1)
    kr_rot = kr * ck + pltpu.roll(kr, LANES - ROPE, axis=1) * sk
    for pr in range(N_HEAD // 2):
        kp = kn[:, 2 * pr * LANES:2 * (pr + 1) * LANES]
        sc = lax.rsqrt(head_sums(kp * kp + kr_sq2) * inv_hd + NORM_EPS)
        for hh in range(2):
            h = 2 * pr + hh
            kh = kp[:, hh * LANES:(hh + 1) * LANES]
            k_out[0, :, h * LANES:(h + 1) * LANES] = (
                (kh * ck + kr_rot) * sc[:, hh * LANES:(hh + 1) * LANES]).astype(BF16)


def _const_spec(shape):
    nd = len(shape)
    return pl.BlockSpec(shape, lambda *_: (0,) * nd)


def _project(x3, tm, cos_t, sin_t, wts, dims):
    q_lora, kv_lora, d_rnn = dims
    b, s, d = x3.shape
    n_slab = d_rnn // LANES
    nt = s // tm
    ln1, win, qag, wuq, kvag, wuk, wuv, gq, gk, hsum = wts
    row3 = lambda i, j: (i, j, 0)
    row4 = lambda i, j: (i, 0, j, 0)
    tab = lambda i, j: (j, 0)
    in_specs = [pl.BlockSpec((1, tm, d), row3)]
    in_specs += [_const_spec(w.shape) for w in (ln1, win, qag, wuq, kvag, wuk, wuv)]
    in_specs += [pl.BlockSpec((tm, LANES), tab), pl.BlockSpec((tm, LANES), tab)]
    in_specs += [_const_spec(gq.shape), _const_spec(gk.shape), _const_spec(hsum.shape)]
    out_shape = (
        jax.ShapeDtypeStruct((b, s, N_HEAD * LANES), BF16),
        jax.ShapeDtypeStruct((b, s, N_HEAD * LANES), BF16),
        jax.ShapeDtypeStruct((b, N_HEAD * VT_ROWS, s), BF16),
        jax.ShapeDtypeStruct((b, n_slab, s, LANES), F32),
        jax.ShapeDtypeStruct((b, n_slab, s, LANES), F32),
    )
    out_specs = (
        pl.BlockSpec((1, tm, N_HEAD * LANES), row3),
        pl.BlockSpec((1, tm, N_HEAD * LANES), row3),
        pl.BlockSpec((1, N_HEAD * VT_ROWS, tm), lambda i, j: (i, 0, j)),
        pl.BlockSpec((1, n_slab, tm, LANES), row4),
        pl.BlockSpec((1, n_slab, tm, LANES), row4),
    )
    return pl.pallas_call(
        functools.partial(_proj_kernel, q_lora=q_lora, kv_lora=kv_lora, d_rnn=d_rnn),
        grid=(b, nt),
        in_specs=in_specs,
        out_specs=out_specs,
        out_shape=out_shape,
        compiler_params=pltpu.CompilerParams(
            dimension_semantics=("arbitrary", "arbitrary"), vmem_limit_bytes=VMEM_LIMIT),
        name="proj",
    )(x3, ln1, win, qag, wuq, kvag, wuk, wuv, cos_t, sin_t, gq, gk, hsum)


def _scan_stride(t_len):
    s = -(-t_len // SUBLANES)
    while s % SUBLANES == 0:
        s += 1
    if s % 2:
        s += 1
        if s % SUBLANES == 0:
            s += 2
    return s


def _chunk_len(stride, cap=48):
    for c in range(min(cap, stride), 0, -1):
        if stride % c == 0:
            return c
    return 1


def _rnn_kernel(xr_ref, xg_ref, xrm_ref, cw_ref, cb_ref, wg_ref, ba_ref, bi_ref, lam_ref, g_ref,
                o_ref, t1, af, bf, ab, bb, *, seq, stride, rc, tm_out):
    n_slab = xr_ref.shape[1]
    t_len = N_META_TOK + seq
    pre = SUBLANES
    rows = t1.shape[1]
    n_chunk = stride // rc
    crow = rc * SUBLANES

    for s in range(n_slab):
        t1[s, 0:pre, :] = jnp.zeros((pre, LANES), F32)
        t1[s, pre:pre + N_META_TOK, :] = xrm_ref[0, s]
        t1[s, pre + N_META_TOK:pre + t_len, :] = xr_ref[0, s]
        t1[s, pre + t_len:rows, :] = jnp.zeros((rows - pre - t_len, LANES), F32)

    lam = lam_ref[...]
    nl = -lam
    softplus = jnp.maximum(nl, 0.0) + jnp.log(1.0 + jnp.exp(-jnp.abs(nl)))
    hcn = (0.5 * LRU_SCALE) * softplus
    cw = cw_ref[...]
    cb = cb_ref[...]
    ba = 0.5 * ba_ref[...]
    bi = 0.5 * bi_ref[...]

    padded = stride * SUBLANES > t_len
    if padded:
        ridx = lax.broadcasted_iota(jnp.int32, (crow, LANES), 0)
        t_local = (ridx & (SUBLANES - 1)) * stride + (ridx >> 3)

    def gate_chunk(c, carry):
        r0 = c * rc
        row0 = pl.multiple_of(c * crow, SUBLANES)
        for s in range(n_slab):
            ls = slice(s * LANES, (s + 1) * LANES)
            taps = [t1[s, pl.ds(r0 + pre - CONV_LEFT + j, SUBLANES, stride=stride), :]
                    for j in range(rc + CONV_TAPS - 1)]
            xc_rows = []
            for rr in range(rc):
                acc = taps[rr] * cw[0:1, ls]
                for j in range(1, CONV_TAPS):
                    acc = acc + taps[rr + j] * cw[j:j + 1, ls]
                xc_rows.append(acc + cb[:, ls])
            xc = jnp.concatenate(xc_rows, axis=0)
            hx = 0.5 * xc
            z = jnp.dot(xc.astype(BF16), wg_ref[s], preferred_element_type=F32)
            for d, (a_ref, b_ref) in enumerate(((af, bf), (ab, bb))):
                ta = jnp.tanh(z[:, (2 * d) * LANES:(2 * d + 1) * LANES] + ba[d:d + 1, ls])
                ti = jnp.tanh(z[:, (2 * d + 1) * LANES:(2 * d + 2) * LANES] + bi[d:d + 1, ls])
                hc = hcn[d:d + 1, ls]
                nla = ta * hc + hc
                a = jnp.exp2(nla * (-LOG2E))
                one_m_a2 = jnp.tanh(nla) * (1.0 + a * a)
                root = jnp.where(one_m_a2 > 0.0, one_m_a2 * lax.rsqrt(one_m_a2), 0.0)
                bv = root * (ti * hx + hx)
                if d == 1 and padded:
                    bv = jnp.where((t_local + r0) < t_len, bv, 0.0)
                a_ref[s, pl.ds(row0, crow), :] = a
                b_ref[s, pl.ds(row0, crow), :] = bv
        return carry

    lax.fori_loop(0, n_chunk, gate_chunk, 0)

    zero = jnp.zeros((SUBLANES, LANES), F32)
    one = jnp.ones((SUBLANES, LANES), F32)

    def scan_step(r, carry):
        hf, pf, hb, pb = carry
        rf = pl.multiple_of(r * SUBLANES, SUBLANES)
        rb = pl.multiple_of((stride - 1 - r) * SUBLANES, SUBLANES)
        nhf, npf, nhb, npb = [], [], [], []
        for s in range(n_slab):
            a = af[s, pl.ds(rf, SUBLANES), :]
            h = a * hf[s] + bf[s, pl.ds(rf, SUBLANES), :]
            pr = a * pf[s]
            bf[s, pl.ds(rf, SUBLANES), :] = h
            af[s, pl.ds(rf, SUBLANES), :] = pr
            nhf.append(h)
            npf.append(pr)
            a = ab[s, pl.ds(rb, SUBLANES), :]
            h = a * hb[s] + bb[s, pl.ds(rb, SUBLANES), :]
            pr = a * pb[s]
            bb[s, pl.ds(rb, SUBLANES), :] = h
            ab[s, pl.ds(rb, SUBLANES), :] = pr
            nhb.append(h)
            npb.append(pr)
        return tuple(nhf), tuple(npf), tuple(nhb), tuple(npb)

    init = ((zero,) * n_slab, (one,) * n_slab, (zero,) * n_slab, (one,) * n_slab)
    hf, pf, hb, pb = lax.fori_loop(0, stride, scan_step, init, unroll=2)

    sub = lax.broadcasted_iota(jnp.int32, (SUBLANES, LANES), 0)
    cf, cbk = [], []
    for s in range(n_slab):
        c = zero
        for _ in range(SUBLANES - 1):
            c = jnp.where(sub == 0, 0.0, pltpu.roll(pf[s] * c + hf[s], 1, axis=0))
        cf.append(c)
        c = zero
        for _ in range(SUBLANES - 1):
            c = jnp.where(sub == SUBLANES - 1, 0.0, pltpu.roll(pb[s] * c + hb[s], SUBLANES - 1, axis=0))
        cbk.append(c)

    def combine(r, carry):
        rp = pl.multiple_of(r * SUBLANES, SUBLANES)
        for s in range(n_slab):
            y = (bf[s, pl.ds(rp, SUBLANES), :] + af[s, pl.ds(rp, SUBLANES), :] * cf[s]
                 + bb[s, pl.ds(rp, SUBLANES), :] + ab[s, pl.ds(rp, SUBLANES), :] * cbk[s])
            t1[s, pl.ds(r + pre, SUBLANES, stride=stride), :] = y
        return carry

    lax.fori_loop(0, stride, combine, 0, unroll=2)

    d_rnn = n_slab * LANES
    g = g_ref[...]
    k0 = math.sqrt(2.0 / math.pi)

    def finish(j, carry):
        ro = pl.multiple_of(j * tm_out, 2 * SUBLANES)
        rt = pl.multiple_of(j * tm_out + pre + N_META_TOK, SUBLANES)
        ys = []
        ssq = jnp.zeros((tm_out, 1), F32)
        for s in range(n_slab):
            y = t1[s, pl.ds(rt, tm_out), :]
            xg = xg_ref[0, s, pl.ds(ro, tm_out), :]
            yh = y * (0.5 * xg)
            y = yh + yh * jnp.tanh(xg * (k0 + (k0 * 0.044715) * (xg * xg)))
            ssq = ssq + jnp.sum(y * y, axis=-1, keepdims=True)
            ys.append(y)
        sc = lax.rsqrt(ssq * (1.0 / d_rnn) + NORM_EPS)
        for s in range(n_slab):
            ls = slice(s * LANES, (s + 1) * LANES)
            o_ref[0, pl.ds(ro, tm_out), ls] = (ys[s] * sc * g[:, ls]).astype(BF16)
        return carry

    lax.fori_loop(0, seq // tm_out, finish, 0)


def _rnn(xr, xg, xrm, cw, cb, wg, ba, bi, lam, g):
    b, n_slab, seq, _ = xr.shape
    t_len = N_META_TOK + seq
    stride = _scan_stride(t_len)
    rc = _chunk_len(stride)
    rows_perm = stride * SUBLANES
    rows_time = SUBLANES + rows_perm + SUBLANES
    tm_out = 256 if seq % 256 == 0 else seq
    blk = lambda i: (i, 0, 0, 0)
    in_specs = [
        pl.BlockSpec((1, n_slab, seq, LANES), blk),
        pl.BlockSpec((1, n_slab, seq, LANES), blk),
        _const_spec(xrm.shape),
    ] + [_const_spec(w.shape) for w in (cw, cb, wg, ba, bi, lam, g)]
    scratch = [pltpu.VMEM((n_slab, rows_time, LANES), F32)]
    scratch += [pltpu.VMEM((n_slab, rows_perm, LANES), F32) for _ in range(4)]
    return pl.pallas_call(
        functools.partial(_rnn_kernel, seq=seq, stride=stride, rc=rc, tm_out=tm_out),
        grid=(b,),
        in_specs=in_specs,
        out_specs=pl.BlockSpec((1, seq, n_slab * LANES), lambda i: (i, 0, 0)),
        out_shape=jax.ShapeDtypeStruct((b, seq, n_slab * LANES), BF16),
        scratch_shapes=scratch,
        compiler_params=pltpu.CompilerParams(
            dimension_semantics=("arbitrary",), vmem_limit_bytes=VMEM_LIMIT),
        name="rnn",
    )(xr, xg, xrm, cw, cb, wg, ba, bi, lam, g)


def _attn_kernel(q_ref, k_ref, vt_ref, km_ref, vtm_ref, o_ref, kfull, *, tq):
    nt = (((1,), (1,)), ((), ()))
    n_key = k_ref.shape[1]
    kfull[0:n_key, :] = k_ref[0]
    kfull[n_key:n_key + N_META_TOK, :] = km_ref[0]

    n_stream = 2 * (n_key // tq)
    n_tile = n_key // KEY_TILE
    tiles = [(t * KEY_TILE, (t + 1) * KEY_TILE) for t in range(n_tile)]
    tiles[-1] = (tiles[-1][0], n_key + N_META_TOK)
    steps = [(i, t) for i in range(n_stream) for t in range(n_tile)]

    def score_tile(i, t):
        j, hh = divmod(i, 2)
        k0, k1 = tiles[t]
        ls = slice(hh * LANES, (hh + 1) * LANES)
        return lax.dot_general(kfull[k0:k1, ls], q_ref[0, j * tq:(j + 1) * tq, ls], nt,
                               preferred_element_type=F32)

    outs = []
    pending = [score_tile(*steps[n]) for n in range(SCORE_LOOKAHEAD)]
    m, acc = None, None
    for n, (i, t) in enumerate(steps):
        st = pending.pop(0)
        if n + SCORE_LOOKAHEAD < len(steps):
            pending.append(score_tile(*steps[n + SCORE_LOOKAHEAD]))
        k0, k1 = tiles[t]
        vs = slice((i % 2) * VT_ROWS, (i % 2 + 1) * VT_ROWS)
        st3 = st.reshape((k1 - k0) // SUBLANES, SUBLANES, tq)
        tmax = jnp.max(jnp.max(st3, axis=0), axis=0, keepdims=True)
        m_new = tmax if m is None else jnp.maximum(m, tmax)
        mb = jnp.broadcast_to(m_new, (SUBLANES, tq))
        pb = jnp.exp2(st3 - mb[None]).reshape(k1 - k0, tq).astype(BF16)
        part = jnp.dot(vt_ref[0, vs, k0:k0 + KEY_TILE], pb[:KEY_TILE], preferred_element_type=F32)
        if k1 - k0 > KEY_TILE:
            part = part + jnp.dot(vtm_ref[0, vs, :], pb[KEY_TILE:], preferred_element_type=F32)
        acc = part if m is None else acc * jnp.exp2(m - m_new) + part
        m = m_new
        if t == n_tile - 1:
            outs.append(acc[:HEAD_V] * (1.0 / acc[HEAD_V:HEAD_V + 1]))
            m, acc = None, None
            if i % 2 == 1:
                j = i // 2
                o_ref[0, j * tq:(j + 1) * tq, :] = jnp.concatenate(outs, axis=0).T
                outs = []


def _attention(q, k, vt, km, vtm, tq):
    b, s, _ = q.shape
    n_pair = N_HEAD // 2
    return pl.pallas_call(
        functools.partial(_attn_kernel, tq=tq),
        grid=(b, n_pair),
        in_specs=[
            pl.BlockSpec((1, s, 2 * LANES), lambda i, p: (i, 0, p)),
            pl.BlockSpec((1, s, 2 * LANES), lambda i, p: (i, 0, p)),
            pl.BlockSpec((1, 2 * VT_ROWS, s), lambda i, p: (i, p, 0)),
            pl.BlockSpec((1, N_META_TOK, 2 * LANES), lambda i, p: (0, 0, p)),
            pl.BlockSpec((1, 2 * VT_ROWS, N_META_TOK), lambda i, p: (0, p, 0)),
        ],
        out_specs=pl.BlockSpec((1, s, 2 * HEAD_V), lambda i, p: (i, 0, p)),
        out_shape=jax.ShapeDtypeStruct((b, s, N_HEAD * HEAD_V), F32),
        scratch_shapes=[pltpu.VMEM((s + N_META_TOK, 2 * LANES), BF16)],
        compiler_params=pltpu.CompilerParams(
            dimension_semantics=("arbitrary", "arbitrary"), vmem_limit_bytes=VMEM_LIMIT),
        name="attn",
    )(q, k, vt, km, vtm)


def _out_kernel(x_ref, oa_ref, on_ref, ag_ref, woa_ref, wor_ref, ln2_ref, wg_ref, wu_ref, wd_ref,
                o_ref, *, ff_chunk):
    oa = oa_ref[...]
    oan = (oa * _rsqrt_mean(oa, oa.shape[-1]) * ag_ref[...]).astype(BF16)
    h = x_ref[...] + jnp.dot(oan, woa_ref[...], preferred_element_type=F32)
    h = h + jnp.dot(on_ref[...], wor_ref[...], preferred_element_type=F32)
    hn = (h * _rsqrt_mean(h, h.shape[-1]) * ln2_ref[...]).astype(BF16)
    d_ff = wg_ref.shape[1]
    acc = h
    for c in range(d_ff // ff_chunk):
        cs = slice(c * ff_chunk, (c + 1) * ff_chunk)
        gt = jnp.dot(hn, wg_ref[:, cs], preferred_element_type=F32)
        up = jnp.dot(hn, wu_ref[:, cs], preferred_element_type=F32)
        act = (gt * jax.nn.sigmoid(gt) * up).astype(BF16)
        acc = acc + jnp.dot(act, wd_ref[cs, :], preferred_element_type=F32)
    o_ref[...] = acc


def _out_ffn(x2, oa2, on2, ag, woa, wor, ln2, wg, wu, wd, tm):
    n, d = x2.shape
    d_ff = wg.shape[1]
    ff_chunk = d_ff
    row = lambda i: (i, 0)
    single = pl.Buffered(1)
    wspec = lambda w: pl.BlockSpec(w.shape, lambda i: (0, 0))
    return pl.pallas_call(
        functools.partial(_out_kernel, ff_chunk=ff_chunk),
        grid=(n // tm,),
        in_specs=[
            pl.BlockSpec((tm, d), row),
            pl.BlockSpec((tm, oa2.shape[1]), row),
            pl.BlockSpec((tm, on2.shape[1]), row),
            wspec(ag), wspec(woa), wspec(wor), wspec(ln2), wspec(wg), wspec(wu), wspec(wd),
        ],
        out_specs=pl.BlockSpec((tm, d), row),
        out_shape=jax.ShapeDtypeStruct((n, d), F32),
        compiler_params=pltpu.CompilerParams(
            dimension_semantics=("arbitrary",), vmem_limit_bytes=VMEM_LIMIT),
        name="out_ffn",
    )(x2, oa2, on2, ag, woa, wor, ln2, wg, wu, wd)


def _rope_tables(t_len):
    half = ROPE // 2
    freqs = 1.0 / (THETA ** (jnp.arange(half, dtype=F32) / half))
    ang = jnp.arange(t_len, dtype=F32)[:, None] * freqs[None, :]
    cos2 = jnp.concatenate([jnp.cos(ang)] * 2, axis=-1)
    sin2 = jnp.concatenate([jnp.sin(ang)] * 2, axis=-1)
    pad = LANES - HEAD_QK
    cos_t = jnp.concatenate([jnp.ones((t_len, NOPE), F32), cos2, jnp.zeros((t_len, pad), F32)], -1)
    sin_t = jnp.concatenate([jnp.zeros((t_len, NOPE), F32), sin2, jnp.zeros((t_len, pad), F32)], -1)
    return cos_t, sin_t


def _rot_half_cols(w):
    half = ROPE // 2
    return jnp.concatenate([-w[..., half:], w[..., :half]], axis=-1)


def _gain_rows(g):
    half = ROPE // 2
    pad = jnp.zeros((LANES - HEAD_QK,), F32)
    r0 = jnp.concatenate([g, pad])
    r1 = jnp.concatenate([jnp.zeros((NOPE,), F32), g[NOPE + half:], g[NOPE:NOPE + half], pad])
    return jnp.stack([r0, r1])


def _block_diag_pairs(w):
    g, n, _ = w.shape
    z = jnp.zeros((g // 2, n, n), w.dtype)
    top = jnp.concatenate([w[0::2], z], axis=2)
    bot = jnp.concatenate([z, w[1::2]], axis=2)
    return jnp.concatenate([top, bot], axis=1)


def kernel(x, meta_tokens, ln1_g, w_in, q_a_norm_g, w_uq, kv_a_norm_g, w_ukv, q_norm_g, k_norm_g,
           conv_w, conv_b, lru_wa, lru_ba, lru_wi, lru_bi, lru_lambda, attn_out_g, rnn_out_g,
           w_out, ln2_g, w_gate, w_up, w_down):
    bsz, seq, d_model = x.shape
    q_lora = q_a_norm_g.shape[-1]
    kv_lora = kv_a_norm_g.shape[-1]
    d_rnn = conv_w.shape[-1]
    d_attn = N_HEAD * HEAD_V
    t_len = N_META_TOK + seq
    l = 0

    wi = w_in[l]
    o_kr = q_lora + kv_lora
    w_kr = wi[:, o_kr:o_kr + ROPE]
    win = jnp.concatenate(
        [wi[:, :o_kr], jnp.zeros((d_model, NOPE), F32), w_kr, _rot_half_cols(w_kr),
         wi[:, o_kr + ROPE:]], axis=1).astype(BF16)
    wq = w_uq[l].reshape(q_lora, N_HEAD, HEAD_QK)
    wuq = jnp.concatenate([wq, _rot_half_cols(wq[..., NOPE:])], axis=-1)
    wuq = wuq.reshape(q_lora, N_HEAD * LANES).astype(BF16)
    wkv = w_ukv[l].reshape(kv_lora, N_HEAD, NOPE + HEAD_V)
    wuk = jnp.concatenate([wkv[..., :NOPE], jnp.zeros((kv_lora, N_HEAD, LANES - NOPE), F32)], -1)
    wuk = wuk.reshape(kv_lora, N_HEAD * LANES).astype(BF16)
    wuv = wkv[..., NOPE:].reshape(kv_lora, d_attn).astype(BF16)
    gq = _gain_rows(q_norm_g[l])
    gk = _gain_rows(k_norm_g[l])
    cos_t, sin_t = _rope_tables(t_len)
    lane2 = jnp.arange(2 * LANES)
    hsum = ((lane2[:, None] // LANES == lane2[None, :] // LANES)
            & (lane2[:, None] % LANES < HEAD_QK)).astype(BF16)
    proj_w = (ln1_g[l][None], win, q_a_norm_g[l][None], wuq, kv_a_norm_g[l][None], wuk, wuv, gq, gk, hsum)
    dims = (q_lora, kv_lora, d_rnn)

    wgate = (0.5 * jnp.concatenate(
        [_block_diag_pairs(w) for w in (lru_wa[l, 0], lru_wi[l, 0], lru_wa[l, 1], lru_wi[l, 1])],
        axis=2)).astype(BF16)

    tm = 512 if seq % 512 == 0 else seq
    q, k, v, xr, xg = _project(x, tm, cos_t[N_META_TOK:], sin_t[N_META_TOK:], proj_w, dims)
    _, km, vm, xrm, _ = _project(meta_tokens[None].astype(x.dtype), N_META_TOK,
                                 cos_t[:N_META_TOK], sin_t[:N_META_TOK], proj_w, dims)

    o_rnn = _rnn(xr, xg, xrm, conv_w[l], conv_b[l][None], wgate, lru_ba[l], lru_bi[l],
                 lru_lambda[l], rnn_out_g[l][None])

    tq = 512 if seq % 512 == 0 else seq
    o_attn = _attention(q, k, v, km, vm, tq)

    wo = w_out[l].astype(BF16)
    n_rows = bsz * seq
    tmo = 256 if n_rows % 256 == 0 else n_rows
    out = _out_ffn(x.reshape(n_rows, d_model), o_attn.reshape(n_rows, d_attn),
                   o_rnn.reshape(n_rows, d_rnn), attn_out_g[l][None], wo[:d_attn], wo[d_attn:],
                   ln2_g[l][None], w_gate[l].astype(BF16), w_up[l].astype(BF16),
                   w_down[l].astype(BF16), tmo)
    return out.reshape(bsz, seq, d_model)
```

```python
import functools
import math

import jax
import jax.numpy as jnp
from jax import lax
from jax.experimental import pallas as pl
from jax.experimental.pallas import tpu as pltpu

F32 = jnp.float32
BF16 = jnp.bfloat16

N_META_TOK = 16
N_HEAD = 8
NOPE = 64
ROPE = 32
HEAD_QK = NOPE + ROPE
HEAD_V = 64
VT_ROWS = HEAD_V + 16
LRU_SCALE = 8.0
THETA = 10000.0
NORM_EPS = 1e-6
CONV_TAPS = 4
CONV_LEFT = 2

LANES = 128
SUBLANES = 8
VMEM_LIMIT = 56 * 1024 * 1024
KEY_TILE = 256
SCORE_LOOKAHEAD = 2

LOG2E = 1.4426950408889634


def _rsqrt_mean(x, n):
    return lax.rsqrt(jnp.sum(x * x, axis=-1, keepdims=True) * (1.0 / n) + NORM_EPS)


def _proj_kernel(x_ref, ln1_ref, win_ref, qag_ref, wuq_ref, kvag_ref, wuk_ref, wuv_ref,
                 cos_ref, sin_ref, gq_ref, gk_ref, hsum_ref,
                 q_out, k_out, v_out, xr_out, xg_out, *, q_lora, kv_lora, d_rnn):
    x = x_ref[0]
    d_model = x.shape[-1]
    hn = (x * _rsqrt_mean(x, d_model) * ln1_ref[...]).astype(BF16)
    p = jnp.dot(hn, win_ref[...], preferred_element_type=F32)
    o_kv = q_lora
    o_kr = o_kv + kv_lora
    o_xr = o_kr + LANES
    o_xg = o_xr + d_rnn
    c_q = p[:, :o_kv]
    c_kv = p[:, o_kv:o_kr]
    kr = p[:, o_kr:o_xr]
    n_slab = d_rnn // LANES
    for s in range(n_slab):
        xr_out[0, s] = p[:, o_xr + s * LANES:o_xr + (s + 1) * LANES]
        xg_out[0, s] = p[:, o_xg + s * LANES:o_xg + (s + 1) * LANES]

    cos_t = cos_ref[...]
    sin_t = sin_ref[...]
    inv_hd = 1.0 / HEAD_QK
    hsum = hsum_ref[...]

    def head_sums(sq_pair):
        return jnp.dot(sq_pair.astype(BF16), hsum, preferred_element_type=F32)

    cqn = (c_q * _rsqrt_mean(c_q, q_lora) * qag_ref[...]).astype(BF16)
    q = jnp.dot(cqn, wuq_ref[...], preferred_element_type=F32)
    cq = cos_t * gq_ref[0:1, :]
    sq = sin_t * gq_ref[1:2, :]
    q_scale = (HEAD_QK ** -0.5) * LOG2E
    for pr in range(N_HEAD // 2):
        qp = q[:, 2 * pr * LANES:2 * (pr + 1) * LANES]
        sc = lax.rsqrt(head_sums(qp * qp) * inv_hd + NORM_EPS) * q_scale
        for hh in range(2):
            h = 2 * pr + hh
            qh = qp[:, hh * LANES:(hh + 1) * LANES]
            qr = pltpu.roll(qh, LANES - ROPE, axis=1)
            qf = (qh * cq + qr * sq) * sc[:, hh * LANES:(hh + 1) * LANES]
            q_out[0, h * LANES:(h + 1) * LANES, :] = qf.T.astype(BF16)

    ckn = (c_kv * _rsqrt_mean(c_kv, kv_lora) * kvag_ref[...]).astype(BF16)
    kn = jnp.dot(ckn, wuk_ref[...], preferred_element_type=F32)
    vt = jnp.dot(ckn, wuv_ref[...], preferred_element_type=F32).T
    tm = vt.shape[1]
    ones_rows = (lax.broadcasted_iota(jnp.int32, (VT_ROWS - HEAD_V, tm), 0) == 0).astype(BF16)
    for h in range(N_HEAD):
        v_out[0, h * VT_ROWS:h * VT_ROWS + HEAD_V, :] = vt[h * HEAD_V:(h + 1) * HEAD_V].astype(BF16)
        v_out[0, h * VT_ROWS + HEAD_V:(h + 1) * VT_ROWS, :] = ones_rows
    ck = cos_t * gk_ref[0:1, :]
    sk = sin_t * gk_ref[1:2, :]
    kr_sq = kr * kr
    kr_sq2 = jnp.concatenate([kr_sq, kr_sq], axis=1)
    kr_rot = kr * ck + pltpu.roll(kr, LANES - ROPE, axis=1) * sk
    for pr in range(N_HEAD // 2):
        kp = kn[:, 2 * pr * LANES:2 * (pr + 1) * LANES]
        sc = lax.rsqrt(head_sums(kp * kp + kr_sq2) * inv_hd + NORM_EPS)
        for hh in range(2):
            h = 2 * pr + hh
            kh = kp[:, hh * LANES:(hh + 1) * LANES]
            k_out[0, :, h * LANES:(h + 1) * LANES] = (
                (kh * ck + kr_rot) * sc[:, hh * LANES:(hh + 1) * LANES]).astype(BF16)


def _const_spec(shape):
    nd = len(shape)
    return pl.BlockSpec(shape, lambda *_: (0,) * nd)


def _project(x3, tm, cos_t, sin_t, wts, dims):
    q_lora, kv_lora, d_rnn = dims
    b, s, d = x3.shape
    n_slab = d_rnn // LANES
    nt = s // tm
    ln1, win, qag, wuq, kvag, wuk, wuv, gq, gk, hsum = wts
    row3 = lambda i, j: (i, j, 0)
    row4 = lambda i, j: (i, 0, j, 0)
    tab = lambda i, j: (j, 0)
    in_specs = [pl.BlockSpec((1, tm, d), row3)]
    in_specs += [_const_spec(w.shape) for w in (ln1, win, qag, wuq, kvag, wuk, wuv)]
    in_specs += [pl.BlockSpec((tm, LANES), tab), pl.BlockSpec((tm, LANES), tab)]
    in_specs += [_const_spec(gq.shape), _const_spec(gk.shape), _const_spec(hsum.shape)]
    col3 = lambda i, j: (i, 0, j)
    out_shape = (
        jax.ShapeDtypeStruct((b, N_HEAD * LANES, s), BF16),
        jax.ShapeDtypeStruct((b, s, N_HEAD * LANES), BF16),
        jax.ShapeDtypeStruct((b, N_HEAD * VT_ROWS, s), BF16),
        jax.ShapeDtypeStruct((b, n_slab, s, LANES), F32),
        jax.ShapeDtypeStruct((b, n_slab, s, LANES), F32),
    )
    out_specs = (
        pl.BlockSpec((1, N_HEAD * LANES, tm), col3),
        pl.BlockSpec((1, tm, N_HEAD * LANES), row3),
        pl.BlockSpec((1, N_HEAD * VT_ROWS, tm), col3),
        pl.BlockSpec((1, n_slab, tm, LANES), row4),
        pl.BlockSpec((1, n_slab, tm, LANES), row4),
    )
    return pl.pallas_call(
        functools.partial(_proj_kernel, q_lora=q_lora, kv_lora=kv_lora, d_rnn=d_rnn),
        grid=(b, nt),
        in_specs=in_specs,
        out_specs=out_specs,
        out_shape=out_shape,
        compiler_params=pltpu.CompilerParams(
            dimension_semantics=("arbitrary", "arbitrary"), vmem_limit_bytes=VMEM_LIMIT),
        name="proj",
    )(x3, ln1, win, qag, wuq, kvag, wuk, wuv, cos_t, sin_t, gq, gk, hsum)


def _scan_stride(t_len):
    s = -(-t_len // SUBLANES)
    while s % SUBLANES == 0:
        s += 1
    if s % 2:
        s += 1
        if s % SUBLANES == 0:
            s += 2
    return s


def _chunk_len(stride, cap=48):
    for c in range(min(cap, stride), 0, -1):
        if stride % c == 0:
            return c
    return 1


def _rnn_kernel(xr_ref, xg_ref, xrm_ref, cw_ref, cb_ref, wg_ref, ba_ref, bi_ref, lam_ref, g_ref,
                o_ref, t1, af, bf, ab, bb, *, seq, stride, rc, tm_out):
    n_slab = xr_ref.shape[1]
    t_len = N_META_TOK + seq
    pre = SUBLANES
    rows = t1.shape[1]
    n_chunk = stride // rc
    crow = rc * SUBLANES

    for s in range(n_slab):
        t1[s, 0:pre, :] = jnp.zeros((pre, LANES), F32)
        t1[s, pre:pre + N_META_TOK, :] = xrm_ref[0, s]
        t1[s, pre + N_META_TOK:pre + t_len, :] = xr_ref[0, s]
        t1[s, pre + t_len:rows, :] = jnp.zeros((rows - pre - t_len, LANES), F32)

    lam = lam_ref[...]
    nl = -lam
    softplus = jnp.maximum(nl, 0.0) + jnp.log(1.0 + jnp.exp(-jnp.abs(nl)))
    hcn = (0.5 * LRU_SCALE) * softplus
    cw = cw_ref[...]
    cb = cb_ref[...]
    ba = 0.5 * ba_ref[...]
    bi = 0.5 * bi_ref[...]

    padded = stride * SUBLANES > t_len
    if padded:
        ridx = lax.broadcasted_iota(jnp.int32, (crow, LANES), 0)
        t_local = (ridx & (SUBLANES - 1)) * stride + (ridx >> 3)

    def gate_chunk(c, carry):
        r0 = c * rc
        row0 = pl.multiple_of(c * crow, SUBLANES)
        for s in range(n_slab):
            ls = slice(s * LANES, (s + 1) * LANES)
            taps = [t1[s, pl.ds(r0 + pre - CONV_LEFT + j, SUBLANES, stride=stride), :]
                    for j in range(rc + CONV_TAPS - 1)]
            xc_rows = []
            for rr in range(rc):
                acc = taps[rr] * cw[0:1, ls]
                for j in range(1, CONV_TAPS):
                    acc = acc + taps[rr + j] * cw[j:j + 1, ls]
                xc_rows.append(acc + cb[:, ls])
            xc = jnp.concatenate(xc_rows, axis=0)
            hx = 0.5 * xc
            z = jnp.dot(xc.astype(BF16), wg_ref[s], preferred_element_type=F32)
            for d, (a_ref, b_ref) in enumerate(((af, bf), (ab, bb))):
                ta = jnp.tanh(z[:, (2 * d) * LANES:(2 * d + 1) * LANES] + ba[d:d + 1, ls])
                ti = jnp.tanh(z[:, (2 * d + 1) * LANES:(2 * d + 2) * LANES] + bi[d:d + 1, ls])
                hc = hcn[d:d + 1, ls]
                nla = ta * hc + hc
                a = jnp.exp2(nla * (-LOG2E))
                one_m_a2 = jnp.tanh(nla) * (1.0 + a * a)
                root = jnp.where(one_m_a2 > 0.0, one_m_a2 * lax.rsqrt(one_m_a2), 0.0)
                bv = root * (ti * hx + hx)
                if d == 1 and padded:
                    bv = jnp.where((t_local + r0) < t_len, bv, 0.0)
                a_ref[s, pl.ds(row0, crow), :] = a
                b_ref[s, pl.ds(row0, crow), :] = bv
        return carry

    lax.fori_loop(0, n_chunk, gate_chunk, 0)

    zero = jnp.zeros((SUBLANES, LANES), F32)
    one = jnp.ones((SUBLANES, LANES), F32)

    def scan_step(r, carry):
        hf, pf, hb, pb = carry
        rf = pl.multiple_of(r * SUBLANES, SUBLANES)
        rb = pl.multiple_of((stride - 1 - r) * SUBLANES, SUBLANES)
        nhf, npf, nhb, npb = [], [], [], []
        for s in range(n_slab):
            a = af[s, pl.ds(rf, SUBLANES), :]
            h = a * hf[s] + bf[s, pl.ds(rf, SUBLANES), :]
            pr = a * pf[s]
            bf[s, pl.ds(rf, SUBLANES), :] = h
            af[s, pl.ds(rf, SUBLANES), :] = pr
            nhf.append(h)
            npf.append(pr)
            a = ab[s, pl.ds(rb, SUBLANES), :]
            h = a * hb[s] + bb[s, pl.ds(rb, SUBLANES), :]
            pr = a * pb[s]
            bb[s, pl.ds(rb, SUBLANES), :] = h
            ab[s, pl.ds(rb, SUBLANES), :] = pr
            nhb.append(h)
            npb.append(pr)
        return tuple(nhf), tuple(npf), tuple(nhb), tuple(npb)

    init = ((zero,) * n_slab, (one,) * n_slab, (zero,) * n_slab, (one,) * n_slab)
    hf, pf, hb, pb = lax.fori_loop(0, stride, scan_step, init, unroll=2)

    sub = lax.broadcasted_iota(jnp.int32, (SUBLANES, LANES), 0)
    cf, cbk = [], []
    for s in range(n_slab):
        c = zero
        for _ in range(SUBLANES - 1):
            c = jnp.where(sub == 0, 0.0, pltpu.roll(pf[s] * c + hf[s], 1, axis=0))
        cf.append(c)
        c = zero
        for _ in range(SUBLANES - 1):
            c = jnp.where(sub == SUBLANES - 1, 0.0, pltpu.roll(pb[s] * c + hb[s], SUBLANES - 1, axis=0))
        cbk.append(c)

    def combine(r, carry):
        rp = pl.multiple_of(r * SUBLANES, SUBLANES)
        for s in range(n_slab):
            y = (bf[s, pl.ds(rp, SUBLANES), :] + af[s, pl.ds(rp, SUBLANES), :] * cf[s]
                 + bb[s, pl.ds(rp, SUBLANES), :] + ab[s, pl.ds(rp, SUBLANES), :] * cbk[s])
            t1[s, pl.ds(r + pre, SUBLANES, stride=stride), :] = y
        return carry

    lax.fori_loop(0, stride, combine, 0, unroll=2)

    d_rnn = n_slab * LANES
    g = g_ref[...]
    k0 = math.sqrt(2.0 / math.pi)

    def finish(j, carry):
        ro = pl.multiple_of(j * tm_out, 2 * SUBLANES)
        rt = pl.multiple_of(j * tm_out + pre + N_META_TOK, SUBLANES)
        ys = []
        ssq = jnp.zeros((tm_out, 1), F32)
        for s in range(n_slab):
            y = t1[s, pl.ds(rt, tm_out), :]
            xg = xg_ref[0, s, pl.ds(ro, tm_out), :]
            yh = y * (0.5 * xg)
            y = yh + yh * jnp.tanh(xg * (k0 + (k0 * 0.044715) * (xg * xg)))
            ssq = ssq + jnp.sum(y * y, axis=-1, keepdims=True)
            ys.append(y)
        sc = lax.rsqrt(ssq * (1.0 / d_rnn) + NORM_EPS)
        for s in range(n_slab):
            ls = slice(s * LANES, (s + 1) * LANES)
            o_ref[0, pl.ds(ro, tm_out), ls] = (ys[s] * sc * g[:, ls]).astype(BF16)
        return carry

    lax.fori_loop(0, seq // tm_out, finish, 0)


def _rnn(xr, xg, xrm, cw, cb, wg, ba, bi, lam, g):
    b, n_slab, seq, _ = xr.shape
    t_len = N_META_TOK + seq
    stride = _scan_stride(t_len)
    rc = _chunk_len(stride)
    rows_perm = stride * SUBLANES
    rows_time = SUBLANES + rows_perm + SUBLANES
    tm_out = 256 if seq % 256 == 0 else seq
    blk = lambda i: (i, 0, 0, 0)
    in_specs = [
        pl.BlockSpec((1, n_slab, seq, LANES), blk),
        pl.BlockSpec((1, n_slab, seq, LANES), blk),
        _const_spec(xrm.shape),
    ] + [_const_spec(w.shape) for w in (cw, cb, wg, ba, bi, lam, g)]
    scratch = [pltpu.VMEM((n_slab, rows_time, LANES), F32)]
    scratch += [pltpu.VMEM((n_slab, rows_perm, LANES), F32) for _ in range(4)]
    return pl.pallas_call(
        functools.partial(_rnn_kernel, seq=seq, stride=stride, rc=rc, tm_out=tm_out),
        grid=(b,),
        in_specs=in_specs,
        out_specs=pl.BlockSpec((1, seq, n_slab * LANES), lambda i: (i, 0, 0)),
        out_shape=jax.ShapeDtypeStruct((b, seq, n_slab * LANES), BF16),
        scratch_shapes=scratch,
        compiler_params=pltpu.CompilerParams(
            dimension_semantics=("arbitrary",), vmem_limit_bytes=VMEM_LIMIT),
        name="rnn",
    )(xr, xg, xrm, cw, cb, wg, ba, bi, lam, g)


def _attn_kernel(qt_ref, k_ref, vt_ref, km_ref, vtm_ref, o_ref, kfull, *, tq):
    n_key = k_ref.shape[1]
    kfull[0:n_key, :] = k_ref[0]
    kfull[n_key:n_key + N_META_TOK, :] = km_ref[0]

    n_stream = 2 * (n_key // tq)
    n_tile = n_key // KEY_TILE
    tiles = [(t * KEY_TILE, (t + 1) * KEY_TILE) for t in range(n_tile)]
    tiles[-1] = (tiles[-1][0], n_key + N_META_TOK)
    steps = [(i, t) for i in range(n_stream) for t in range(n_tile)]

    def score_tile(i, t):
        j, hh = divmod(i, 2)
        k0, k1 = tiles[t]
        ls = slice(hh * LANES, (hh + 1) * LANES)
        return jnp.dot(kfull[k0:k1, ls], qt_ref[0, ls, j * tq:(j + 1) * tq],
                       preferred_element_type=F32)

    outs = []
    pending = [score_tile(*steps[n]) for n in range(SCORE_LOOKAHEAD)]
    m, acc = None, None
    for n, (i, t) in enumerate(steps):
        st = pending.pop(0)
        if n + SCORE_LOOKAHEAD < len(steps):
            pending.append(score_tile(*steps[n + SCORE_LOOKAHEAD]))
        k0, k1 = tiles[t]
        vs = slice((i % 2) * VT_ROWS, (i % 2 + 1) * VT_ROWS)
        st3 = st.reshape((k1 - k0) // SUBLANES, SUBLANES, tq)
        tmax = jnp.max(jnp.max(st3, axis=0), axis=0, keepdims=True)
        m_new = tmax if m is None else jnp.maximum(m, tmax)
        mb = jnp.broadcast_to(m_new, (SUBLANES, tq))
        pb = jnp.exp2(st3 - mb[None]).reshape(k1 - k0, tq).astype(BF16)
        part = jnp.dot(vt_ref[0, vs, k0:k0 + KEY_TILE], pb[:KEY_TILE], preferred_element_type=F32)
        if k1 - k0 > KEY_TILE:
            part = part + jnp.dot(vtm_ref[0, vs, :], pb[KEY_TILE:], preferred_element_type=F32)
        acc = part if m is None else acc * jnp.exp2(m - m_new) + part
        m = m_new
        if t == n_tile - 1:
            outs.append(acc[:HEAD_V] * (1.0 / acc[HEAD_V:HEAD_V + 1]))
            m, acc = None, None
            if i % 2 == 1:
                j = i // 2
                o_ref[0, j * tq:(j + 1) * tq, :] = jnp.concatenate(outs, axis=0).T
                outs = []


def _attention(qt, k, vt, km, vtm, tq):
    b, s, _ = k.shape
    n_pair = N_HEAD // 2
    return pl.pallas_call(
        functools.partial(_attn_kernel, tq=tq),
        grid=(b, n_pair),
        in_specs=[
            pl.BlockSpec((1, 2 * LANES, s), lambda i, p: (i, p, 0)),
            pl.BlockSpec((1, s, 2 * LANES), lambda i, p: (i, 0, p)),
            pl.BlockSpec((1, 2 * VT_ROWS, s), lambda i, p: (i, p, 0)),
            pl.BlockSpec((1, N_META_TOK, 2 * LANES), lambda i, p: (0, 0, p)),
            pl.BlockSpec((1, 2 * VT_ROWS, N_META_TOK), lambda i, p: (0, p, 0)),
        ],
        out_specs=pl.BlockSpec((1, s, 2 * HEAD_V), lambda i, p: (i, 0, p)),
        out_shape=jax.ShapeDtypeStruct((b, s, N_HEAD * HEAD_V), F32),
        scratch_shapes=[pltpu.VMEM((s + N_META_TOK, 2 * LANES), BF16)],
        compiler_params=pltpu.CompilerParams(
            dimension_semantics=("arbitrary", "arbitrary"), vmem_limit_bytes=VMEM_LIMIT),
        name="attn",
    )(qt, k, vt, km, vtm)


def _out_kernel(x_ref, oa_ref, on_ref, ag_ref, woa_ref, wor_ref, ln2_ref, wg_ref, wu_ref, wd_ref,
                o_ref, *, ff_chunk):
    oa = oa_ref[...]
    oan = (oa * _rsqrt_mean(oa, oa.shape[-1]) * ag_ref[...]).astype(BF16)
    h = x_ref[...] + jnp.dot(oan, woa_ref[...], preferred_element_type=F32)
    h = h + jnp.dot(on_ref[...], wor_ref[...], preferred_element_type=F32)
    hn = (h * _rsqrt_mean(h, h.shape[-1]) * ln2_ref[...]).astype(BF16)
    d_ff = wg_ref.shape[1]
    acc = h
    for c in range(d_ff // ff_chunk):
        cs = slice(c * ff_chunk, (c + 1) * ff_chunk)
        gt = jnp.dot(hn, wg_ref[:, cs], preferred_element_type=F32)
        up = jnp.dot(hn, wu_ref[:, cs], preferred_element_type=F32)
        act = (gt * jax.nn.sigmoid(gt) * up).astype(BF16)
        acc = acc + jnp.dot(act, wd_ref[cs, :], preferred_element_type=F32)
    o_ref[...] = acc


def _out_ffn(x2, oa2, on2, ag, woa, wor, ln2, wg, wu, wd, tm):
    n, d = x2.shape
    d_ff = wg.shape[1]
    ff_chunk = d_ff
    row = lambda i: (i, 0)
    single = pl.Buffered(1)
    wspec = lambda w: pl.BlockSpec(w.shape, lambda i: (0, 0))
    return pl.pallas_call(
        functools.partial(_out_kernel, ff_chunk=ff_chunk),
        grid=(n // tm,),
        in_specs=[
            pl.BlockSpec((tm, d), row),
            pl.BlockSpec((tm, oa2.shape[1]), row),
            pl.BlockSpec((tm, on2.shape[1]), row),
            wspec(ag), wspec(woa), wspec(wor), wspec(ln2), wspec(wg), wspec(wu), wspec(wd),
        ],
        out_specs=pl.BlockSpec((tm, d), row),
        out_shape=jax.ShapeDtypeStruct((n, d), F32),
        compiler_params=pltpu.CompilerParams(
            dimension_semantics=("arbitrary",), vmem_limit_bytes=VMEM_LIMIT),
        name="out_ffn",
    )(x2, oa2, on2, ag, woa, wor, ln2, wg, wu, wd)


def _rope_tables(t_len):
    half = ROPE // 2
    freqs = 1.0 / (THETA ** (jnp.arange(half, dtype=F32) / half))
    ang = jnp.arange(t_len, dtype=F32)[:, None] * freqs[None, :]
    cos2 = jnp.concatenate([jnp.cos(ang)] * 2, axis=-1)
    sin2 = jnp.concatenate([jnp.sin(ang)] * 2, axis=-1)
    pad = LANES - HEAD_QK
    cos_t = jnp.concatenate([jnp.ones((t_len, NOPE), F32), cos2, jnp.zeros((t_len, pad), F32)], -1)
    sin_t = jnp.concatenate([jnp.zeros((t_len, NOPE), F32), sin2, jnp.zeros((t_len, pad), F32)], -1)
    return cos_t, sin_t


def _rot_half_cols(w):
    half = ROPE // 2
    return jnp.concatenate([-w[..., half:], w[..., :half]], axis=-1)


def _gain_rows(g):
    half = ROPE // 2
    pad = jnp.zeros((LANES - HEAD_QK,), F32)
    r0 = jnp.concatenate([g, pad])
    r1 = jnp.concatenate([jnp.zeros((NOPE,), F32), g[NOPE + half:], g[NOPE:NOPE + half], pad])
    return jnp.stack([r0, r1])


def _block_diag_pairs(w):
    g, n, _ = w.shape
    z = jnp.zeros((g // 2, n, n), w.dtype)
    top = jnp.concatenate([w[0::2], z], axis=2)
    bot = jnp.concatenate([z, w[1::2]], axis=2)
    return jnp.concatenate([top, bot], axis=1)


def kernel(x, meta_tokens, ln1_g, w_in, q_a_norm_g, w_uq, kv_a_norm_g, w_ukv, q_norm_g, k_norm_g,
           conv_w, conv_b, lru_wa, lru_ba, lru_wi, lru_bi, lru_lambda, attn_out_g, rnn_out_g,
           w_out, ln2_g, w_gate, w_up, w_down):
    bsz, seq, d_model = x.shape
    q_lora = q_a_norm_g.shape[-1]
    kv_lora = kv_a_norm_g.shape[-1]
    d_rnn = conv_w.shape[-1]
    d_attn = N_HEAD * HEAD_V
    t_len = N_META_TOK + seq
    l = 0

    wi = w_in[l]
    o_kr = q_lora + kv_lora
    w_kr = wi[:, o_kr:o_kr + ROPE]
    win = jnp.concatenate(
        [wi[:, :o_kr], jnp.zeros((d_model, NOPE), F32), w_kr, _rot_half_cols(w_kr),
         wi[:, o_kr + ROPE:]], axis=1).astype(BF16)
    wq = w_uq[l].reshape(q_lora, N_HEAD, HEAD_QK)
    wuq = jnp.concatenate([wq, _rot_half_cols(wq[..., NOPE:])], axis=-1)
    wuq = wuq.reshape(q_lora, N_HEAD * LANES).astype(BF16)
    wkv = w_ukv[l].reshape(kv_lora, N_HEAD, NOPE + HEAD_V)
    wuk = jnp.concatenate([wkv[..., :NOPE], jnp.zeros((kv_lora, N_HEAD, LANES - NOPE), F32)], -1)
    wuk = wuk.reshape(kv_lora, N_HEAD * LANES).astype(BF16)
    wuv = wkv[..., NOPE:].reshape(kv_lora, d_attn).astype(BF16)
    gq = _gain_rows(q_norm_g[l])
    gk = _gain_rows(k_norm_g[l])
    cos_t, sin_t = _rope_tables(t_len)
    lane2 = jnp.arange(2 * LANES)
    hsum = ((lane2[:, None] // LANES == lane2[None, :] // LANES)
            & (lane2[:, None] % LANES < HEAD_QK)).astype(BF16)
    proj_w = (ln1_g[l][None], win, q_a_norm_g[l][None], wuq, kv_a_norm_g[l][None], wuk, wuv, gq, gk, hsum)
    dims = (q_lora, kv_lora, d_rnn)

    wgate = (0.5 * jnp.concatenate(
        [_block_diag_pairs(w) for w in (lru_wa[l, 0], lru_wi[l, 0], lru_wa[l, 1], lru_wi[l, 1])],
        axis=2)).astype(BF16)

    tm = 512 if seq % 512 == 0 else seq
    q, k, v, xr, xg = _project(x, tm, cos_t[N_META_TOK:], sin_t[N_META_TOK:], proj_w, dims)
    _, km, vm, xrm, _ = _project(meta_tokens[None].astype(x.dtype), N_META_TOK,
                                 cos_t[:N_META_TOK], sin_t[:N_META_TOK], proj_w, dims)

    o_rnn = _rnn(xr, xg, xrm, conv_w[l], conv_b[l][None], wgate, lru_ba[l], lru_bi[l],
                 lru_lambda[l], rnn_out_g[l][None])

    tq = 1024 if seq % 1024 == 0 else seq
    o_attn = _attention(q, k, v, km, vm, tq)

    wo = w_out[l].astype(BF16)
    n_rows = bsz * seq
    tmo = 256 if n_rows % 256 == 0 else n_rows
    out = _out_ffn(x.reshape(n_rows, d_model), o_attn.reshape(n_rows, d_attn),
                   o_rnn.reshape(n_rows, d_rnn), attn_out_g[l][None], wo[:d_attn], wo[d_attn:],
                   ln2_g[l][None], w_gate[l].astype(BF16), w_up[l].astype(BF16),
                   w_down[l].astype(BF16), tmo)
    return out.reshape(bsz, seq, d_model)
```

```python
import functools
import math

import numpy as np

import jax
import jax.numpy as jnp
from jax import lax
from jax.experimental import pallas as pl
from jax.experimental.pallas import tpu as pltpu

F32 = jnp.float32
BF16 = jnp.bfloat16

N_META_TOK = 16
N_HEAD = 8
NOPE = 64
ROPE = 32
HEAD_QK = NOPE + ROPE
HEAD_V = 64
VT_ROWS = HEAD_V + 16
LRU_SCALE = 8.0
THETA = 10000.0
NORM_EPS = 1e-6
CONV_TAPS = 4
CONV_LEFT = 2

LANES = 128
SUBLANES = 8
VMEM_LIMIT = 56 * 1024 * 1024
KEY_TILE = 256
SCAN_UNROLL = 6
SCORE_LOOKAHEAD = 2

LOG2E = 1.4426950408889634


def _rsqrt_mean(x, n):
    return lax.rsqrt(jnp.sum(x * x, axis=-1, keepdims=True) * (1.0 / n) + NORM_EPS)


def _proj_kernel(x_ref, ln1_ref, win_ref, qag_ref, wuq_ref, kvag_ref, wuk_ref, wuv_ref,
                 cos_ref, sin_ref, gq_ref, gk_ref, hsum_ref,
                 q_out, k_out, v_out, xr_out, xg_out, *, q_lora, kv_lora, d_rnn):
    x = x_ref[0]
    d_model = x.shape[-1]
    hn = (x * _rsqrt_mean(x, d_model) * ln1_ref[...]).astype(BF16)
    o_kv = q_lora
    o_kr = o_kv + kv_lora
    o_xr = o_kr + LANES
    p = jnp.dot(hn, win_ref[:, :o_xr], preferred_element_type=F32)
    c_q = p[:, :o_kv]
    c_kv = p[:, o_kv:o_kr]
    kr = p[:, o_kr:o_xr]
    n_slab = d_rnn // LANES

    def recurrent_inputs(c):
        px = jnp.dot(hn, win_ref[:, o_xr + 2 * c * LANES:o_xr + 2 * (c + 1) * LANES],
                     preferred_element_type=F32)
        for u in range(2):
            slab = 2 * c + u
            dst = xr_out if slab < n_slab else xg_out
            dst[0, slab % n_slab] = px[:, u * LANES:(u + 1) * LANES]

    cos_t = cos_ref[...]
    sin_t = sin_ref[...]
    inv_hd = 1.0 / HEAD_QK
    hsum = hsum_ref[...]

    def head_sums(sq_pair):
        return jnp.dot(sq_pair.astype(BF16), hsum, preferred_element_type=F32)

    cqn = (c_q * _rsqrt_mean(c_q, q_lora) * qag_ref[...]).astype(BF16)
    q = jnp.dot(cqn, wuq_ref[...], preferred_element_type=F32)
    cq = cos_t * gq_ref[0:1, :]
    sq = sin_t * gq_ref[1:2, :]
    q_scale = (HEAD_QK ** -0.5) * LOG2E

    ckn = (c_kv * _rsqrt_mean(c_kv, kv_lora) * kvag_ref[...]).astype(BF16)
    kn = jnp.dot(ckn, wuk_ref[...], preferred_element_type=F32)
    vt = jnp.dot(ckn, wuv_ref[...], preferred_element_type=F32).T
    tm = vt.shape[1]
    ones_rows = (lax.broadcasted_iota(jnp.int32, (VT_ROWS - HEAD_V, tm), 0) == 0).astype(BF16)
    for h in range(N_HEAD):
        v_out[0, h * VT_ROWS:h * VT_ROWS + HEAD_V, :] = vt[h * HEAD_V:(h + 1) * HEAD_V].astype(BF16)
        v_out[0, h * VT_ROWS + HEAD_V:(h + 1) * VT_ROWS, :] = ones_rows
    ck = cos_t * gk_ref[0:1, :]
    sk = sin_t * gk_ref[1:2, :]
    kr_sq = kr * kr
    kr_sq2 = jnp.concatenate([kr_sq, kr_sq], axis=1)
    kr_rot = kr * ck + pltpu.roll(kr, LANES - ROPE, axis=1) * sk
    n_pair = N_HEAD // 2
    n_rec = n_slab
    for pr in range(n_pair):
        for c in range(pr * n_rec // n_pair, (pr + 1) * n_rec // n_pair):
            recurrent_inputs(c)
        qp = q[:, 2 * pr * LANES:2 * (pr + 1) * LANES]
        sc = lax.rsqrt(head_sums(qp * qp) * inv_hd + NORM_EPS) * q_scale
        for hh in range(2):
            h = 2 * pr + hh
            qh = qp[:, hh * LANES:(hh + 1) * LANES]
            qr = pltpu.roll(qh, LANES - ROPE, axis=1)
            qf = (qh * cq + qr * sq) * sc[:, hh * LANES:(hh + 1) * LANES]
            q_out[0, h * LANES:(h + 1) * LANES, :] = qf.T.astype(BF16)
        kp = kn[:, 2 * pr * LANES:2 * (pr + 1) * LANES]
        sc = lax.rsqrt(head_sums(kp * kp + kr_sq2) * inv_hd + NORM_EPS)
        for hh in range(2):
            h = 2 * pr + hh
            kh = kp[:, hh * LANES:(hh + 1) * LANES]
            k_out[0, :, h * LANES:(h + 1) * LANES] = (
                (kh * ck + kr_rot) * sc[:, hh * LANES:(hh + 1) * LANES]).astype(BF16)


def _const_spec(shape):
    nd = len(shape)
    return pl.BlockSpec(shape, lambda *_: (0,) * nd)


def _project(x3, tm, cos_t, sin_t, wts, dims):
    q_lora, kv_lora, d_rnn = dims
    b, s, d = x3.shape
    n_slab = d_rnn // LANES
    nt = s // tm
    ln1, win, qag, wuq, kvag, wuk, wuv, gq, gk, hsum = wts
    row3 = lambda i, j: (i, j, 0)
    row4 = lambda i, j: (i, 0, j, 0)
    tab = lambda i, j: (j, 0)
    in_specs = [pl.BlockSpec((1, tm, d), row3)]
    in_specs += [_const_spec(w.shape) for w in (ln1, win, qag, wuq, kvag, wuk, wuv)]
    in_specs += [pl.BlockSpec((tm, LANES), tab), pl.BlockSpec((tm, LANES), tab)]
    in_specs += [_const_spec(gq.shape), _const_spec(gk.shape), _const_spec(hsum.shape)]
    col3 = lambda i, j: (i, 0, j)
    out_shape = (
        jax.ShapeDtypeStruct((b, N_HEAD * LANES, s), BF16),
        jax.ShapeDtypeStruct((b, s, N_HEAD * LANES), BF16),
        jax.ShapeDtypeStruct((b, N_HEAD * VT_ROWS, s), BF16),
        jax.ShapeDtypeStruct((b, n_slab, s, LANES), F32),
        jax.ShapeDtypeStruct((b, n_slab, s, LANES), F32),
    )
    out_specs = (
        pl.BlockSpec((1, N_HEAD * LANES, tm), col3),
        pl.BlockSpec((1, tm, N_HEAD * LANES), row3),
        pl.BlockSpec((1, N_HEAD * VT_ROWS, tm), col3),
        pl.BlockSpec((1, n_slab, tm, LANES), row4),
        pl.BlockSpec((1, n_slab, tm, LANES), row4),
    )
    return pl.pallas_call(
        functools.partial(_proj_kernel, q_lora=q_lora, kv_lora=kv_lora, d_rnn=d_rnn),
        grid=(b, nt),
        in_specs=in_specs,
        out_specs=out_specs,
        out_shape=out_shape,
        compiler_params=pltpu.CompilerParams(
            dimension_semantics=("arbitrary", "arbitrary"), vmem_limit_bytes=VMEM_LIMIT),
        name="proj",
    )(x3, ln1, win, qag, wuq, kvag, wuk, wuv, cos_t, sin_t, gq, gk, hsum)


def _scan_stride(t_len):
    s = -(-t_len // SUBLANES)
    while s % SUBLANES == 0:
        s += 1
    if s % 2:
        s += 1
        if s % SUBLANES == 0:
            s += 2
    return s


def _chunk_len(stride, cap=48):
    for c in range(min(cap, stride), 0, -1):
        if stride % c == 0:
            return c
    return 1


def _rnn_kernel(xr_ref, xg_ref, xrm_ref, cw_ref, cb_ref, wg_ref, lam_ref, g_ref,
                o_ref, t1, af, bf, ab, bb, *, seq, stride, rc, tm_out):
    n_slab = xr_ref.shape[1]
    t_len = N_META_TOK + seq
    pre = SUBLANES
    rows = t1.shape[1]
    n_chunk = stride // rc
    crow = rc * SUBLANES

    for s in range(n_slab):
        t1[s, 0:pre, :] = jnp.zeros((pre, LANES), F32)
        t1[s, pre:pre + N_META_TOK, :] = xrm_ref[0, s]
        t1[s, pre + N_META_TOK:pre + t_len, :] = xr_ref[0, s]
        t1[s, pre + t_len:rows, :] = jnp.zeros((rows - pre - t_len, LANES), F32)

    lam = lam_ref[...]
    nl = -lam
    softplus = jnp.maximum(nl, 0.0) + jnp.log(1.0 + jnp.exp(-jnp.abs(nl)))
    hcn = (0.5 * LRU_SCALE) * softplus
    cw = cw_ref[...]
    cb = cb_ref[...]
    ones_k = jnp.ones((crow, LANES), BF16)

    padded = stride * SUBLANES > t_len
    if padded:
        ridx = lax.broadcasted_iota(jnp.int32, (crow, LANES), 0)
        t_local = (ridx & (SUBLANES - 1)) * stride + (ridx >> 3)

    def gate_chunk(c, carry):
        r0 = c * rc
        row0 = pl.multiple_of(c * crow, SUBLANES)
        for s in range(n_slab):
            ls = slice(s * LANES, (s + 1) * LANES)
            taps = [t1[s, pl.ds(r0 + pre - CONV_LEFT + j, SUBLANES, stride=stride), :]
                    for j in range(rc + CONV_TAPS - 1)]
            xc_rows = []
            for rr in range(rc):
                acc = taps[rr] * cw[0:1, ls]
                for j in range(1, CONV_TAPS):
                    acc = acc + taps[rr + j] * cw[j:j + 1, ls]
                xc_rows.append(acc + cb[:, ls])
            xc = jnp.concatenate(xc_rows, axis=0)
            hx = 0.5 * xc
            z = jnp.dot(jnp.concatenate([xc.astype(BF16), ones_k], axis=1), wg_ref[s],
                        preferred_element_type=F32)
            for d, (a_ref, b_ref) in enumerate(((af, bf), (ab, bb))):
                ta = jnp.tanh(z[:, (2 * d) * LANES:(2 * d + 1) * LANES])
                ti = jnp.tanh(z[:, (2 * d + 1) * LANES:(2 * d + 2) * LANES])
                hc = hcn[d:d + 1, ls]
                nla = ta * hc + hc
                a = jnp.exp2(nla * (-LOG2E))
                one_m_a2 = jnp.tanh(nla) * (1.0 + a * a)
                root = jnp.where(one_m_a2 > 0.0, one_m_a2 * lax.rsqrt(one_m_a2), 0.0)
                bv = root * (ti * hx + hx)
                if d == 1 and padded:
                    bv = jnp.where((t_local + r0) < t_len, bv, 0.0)
                a_ref[s, pl.ds(row0, crow), :] = a
                b_ref[s, pl.ds(row0, crow), :] = bv
        return carry

    lax.fori_loop(0, n_chunk, gate_chunk, 0)

    zero = jnp.zeros((SUBLANES, LANES), F32)
    one = jnp.ones((SUBLANES, LANES), F32)

    def scan_step(r, carry):
        hf, pf, hb, pb = carry
        rf = pl.multiple_of(r * SUBLANES, SUBLANES)
        rb = pl.multiple_of((stride - 1 - r) * SUBLANES, SUBLANES)
        nhf, npf, nhb, npb = [], [], [], []
        for s in range(n_slab):
            a = af[s, pl.ds(rf, SUBLANES), :]
            h = a * hf[s] + bf[s, pl.ds(rf, SUBLANES), :]
            pr = a * pf[s]
            bf[s, pl.ds(rf, SUBLANES), :] = h
            af[s, pl.ds(rf, SUBLANES), :] = pr
            nhf.append(h)
            npf.append(pr)
            a = ab[s, pl.ds(rb, SUBLANES), :]
            h = a * hb[s] + bb[s, pl.ds(rb, SUBLANES), :]
            pr = a * pb[s]
            bb[s, pl.ds(rb, SUBLANES), :] = h
            ab[s, pl.ds(rb, SUBLANES), :] = pr
            nhb.append(h)
            npb.append(pr)
        return tuple(nhf), tuple(npf), tuple(nhb), tuple(npb)

    init = ((zero,) * n_slab, (one,) * n_slab, (zero,) * n_slab, (one,) * n_slab)
    hf, pf, hb, pb = lax.fori_loop(0, stride, scan_step, init, unroll=SCAN_UNROLL)

    sub = lax.broadcasted_iota(jnp.int32, (SUBLANES, LANES), 0)
    cf, cbk = [], []
    for s in range(n_slab):
        c = zero
        for _ in range(SUBLANES - 1):
            c = jnp.where(sub == 0, 0.0, pltpu.roll(pf[s] * c + hf[s], 1, axis=0))
        cf.append(c)
        c = zero
        for _ in range(SUBLANES - 1):
            c = jnp.where(sub == SUBLANES - 1, 0.0, pltpu.roll(pb[s] * c + hb[s], SUBLANES - 1, axis=0))
        cbk.append(c)

    def combine(r, carry):
        rp = pl.multiple_of(r * SUBLANES, SUBLANES)
        for s in range(n_slab):
            y = (bf[s, pl.ds(rp, SUBLANES), :] + af[s, pl.ds(rp, SUBLANES), :] * cf[s]
                 + bb[s, pl.ds(rp, SUBLANES), :] + ab[s, pl.ds(rp, SUBLANES), :] * cbk[s])
            t1[s, pl.ds(r + pre, SUBLANES, stride=stride), :] = y
        return carry

    lax.fori_loop(0, stride, combine, 0, unroll=SCAN_UNROLL)

    d_rnn = n_slab * LANES
    g = g_ref[...]
    k0 = math.sqrt(2.0 / math.pi)

    def finish(j, carry):
        ro = pl.multiple_of(j * tm_out, 2 * SUBLANES)
        rt = pl.multiple_of(j * tm_out + pre + N_META_TOK, SUBLANES)
        ys = []
        ssq = jnp.zeros((tm_out, 1), F32)
        for s in range(n_slab):
            y = t1[s, pl.ds(rt, tm_out), :]
            xg = xg_ref[0, s, pl.ds(ro, tm_out), :]
            yh = y * (0.5 * xg)
            y = yh + yh * jnp.tanh(xg * (k0 + (k0 * 0.044715) * (xg * xg)))
            ssq = ssq + jnp.sum(y * y, axis=-1, keepdims=True)
            ys.append(y)
        sc = lax.rsqrt(ssq * (1.0 / d_rnn) + NORM_EPS)
        for s in range(n_slab):
            ls = slice(s * LANES, (s + 1) * LANES)
            o_ref[0, pl.ds(ro, tm_out), ls] = (ys[s] * sc * g[:, ls]).astype(BF16)
        return carry

    lax.fori_loop(0, seq // tm_out, finish, 0)


def _rnn(xr, xg, xrm, cw, cb, wg, lam, g):
    b, n_slab, seq, _ = xr.shape
    t_len = N_META_TOK + seq
    stride = _scan_stride(t_len)
    rc = _chunk_len(stride)
    rows_perm = stride * SUBLANES
    rows_time = SUBLANES + rows_perm + SUBLANES
    tm_out = 256 if seq % 256 == 0 else seq
    blk = lambda i: (i, 0, 0, 0)
    in_specs = [
        pl.BlockSpec((1, n_slab, seq, LANES), blk),
        pl.BlockSpec((1, n_slab, seq, LANES), blk),
        _const_spec(xrm.shape),
    ] + [_const_spec(w.shape) for w in (cw, cb, wg, lam, g)]
    scratch = [pltpu.VMEM((n_slab, rows_time, LANES), F32)]
    scratch += [pltpu.VMEM((n_slab, rows_perm, LANES), F32) for _ in range(4)]
    return pl.pallas_call(
        functools.partial(_rnn_kernel, seq=seq, stride=stride, rc=rc, tm_out=tm_out),
        grid=(b,),
        in_specs=in_specs,
        out_specs=pl.BlockSpec((1, seq, n_slab * LANES), lambda i: (i, 0, 0)),
        out_shape=jax.ShapeDtypeStruct((b, seq, n_slab * LANES), BF16),
        scratch_shapes=scratch,
        compiler_params=pltpu.CompilerParams(
            dimension_semantics=("arbitrary",), vmem_limit_bytes=VMEM_LIMIT),
        name="rnn",
    )(xr, xg, xrm, cw, cb, wg, lam, g)


def _attn_kernel(qt_ref, k_ref, vt_ref, km_ref, vtm_ref, o_ref, kfull, *, tq):
    n_key = k_ref.shape[1]
    kfull[0:n_key, :] = k_ref[0]
    kfull[n_key:n_key + N_META_TOK, :] = km_ref[0]

    n_stream = 2 * (n_key // tq)
    n_tile = n_key // KEY_TILE
    tiles = [(t * KEY_TILE, (t + 1) * KEY_TILE) for t in range(n_tile)]
    tiles[-1] = (tiles[-1][0], n_key + N_META_TOK)
    steps = [(i, t) for i in range(n_stream) for t in range(n_tile)]

    def score_tile(i, t):
        j, hh = divmod(i, 2)
        k0, k1 = tiles[t]
        ls = slice(hh * LANES, (hh + 1) * LANES)
        return jnp.dot(kfull[k0:k1, ls], qt_ref[0, ls, j * tq:(j + 1) * tq],
                       preferred_element_type=F32)

    outs = []
    pending = [score_tile(*steps[n]) for n in range(SCORE_LOOKAHEAD)]
    m, acc = None, None
    for n, (i, t) in enumerate(steps):
        st = pending.pop(0)
        if n + SCORE_LOOKAHEAD < len(steps):
            pending.append(score_tile(*steps[n + SCORE_LOOKAHEAD]))
        k0, k1 = tiles[t]
        vs = slice((i % 2) * VT_ROWS, (i % 2 + 1) * VT_ROWS)
        st3 = st.reshape((k1 - k0) // SUBLANES, SUBLANES, tq)
        tmax = jnp.max(jnp.max(st3, axis=0), axis=0, keepdims=True)
        m_new = tmax if m is None else jnp.maximum(m, tmax)
        mb = jnp.broadcast_to(m_new, (SUBLANES, tq))
        pb = jnp.exp2(st3 - mb[None]).reshape(k1 - k0, tq).astype(BF16)
        part = jnp.dot(vt_ref[0, vs, k0:k0 + KEY_TILE], pb[:KEY_TILE], preferred_element_type=F32)
        if k1 - k0 > KEY_TILE:
            part = part + jnp.dot(vtm_ref[0, vs, :], pb[KEY_TILE:], preferred_element_type=F32)
        acc = part if m is None else acc * jnp.exp2(m - m_new) + part
        m = m_new
        if t == n_tile - 1:
            outs.append(acc[:HEAD_V] * (1.0 / acc[HEAD_V:HEAD_V + 1]))
            m, acc = None, None
            if i % 2 == 1:
                j = i // 2
                o_ref[0, j * tq:(j + 1) * tq, :] = jnp.concatenate(outs, axis=0).T
                outs = []


def _attention(qt, k, vt, km, vtm, tq):
    b, s, _ = k.shape
    n_pair = N_HEAD // 2
    return pl.pallas_call(
        functools.partial(_attn_kernel, tq=tq),
        grid=(b, n_pair),
        in_specs=[
            pl.BlockSpec((1, 2 * LANES, s), lambda i, p: (i, p, 0)),
            pl.BlockSpec((1, s, 2 * LANES), lambda i, p: (i, 0, p)),
            pl.BlockSpec((1, 2 * VT_ROWS, s), lambda i, p: (i, p, 0)),
            pl.BlockSpec((1, N_META_TOK, 2 * LANES), lambda i, p: (0, 0, p)),
            pl.BlockSpec((1, 2 * VT_ROWS, N_META_TOK), lambda i, p: (0, p, 0)),
        ],
        out_specs=pl.BlockSpec((1, s, 2 * HEAD_V), lambda i, p: (i, 0, p)),
        out_shape=jax.ShapeDtypeStruct((b, s, N_HEAD * HEAD_V), F32),
        scratch_shapes=[pltpu.VMEM((s + N_META_TOK, 2 * LANES), BF16)],
        compiler_params=pltpu.CompilerParams(
            dimension_semantics=("arbitrary", "arbitrary"), vmem_limit_bytes=VMEM_LIMIT),
        name="attn",
    )(qt, k, vt, km, vtm)


def _out_kernel(x_ref, oa_ref, on_ref, ag_ref, woa_ref, wor_ref, ln2_ref, wg_ref, wu_ref, wd_ref,
                o_ref, *, ff_chunk):
    oa = oa_ref[...]
    oan = (oa * _rsqrt_mean(oa, oa.shape[-1]) * ag_ref[...]).astype(BF16)
    h = x_ref[...] + jnp.dot(oan, woa_ref[...], preferred_element_type=F32)
    h = h + jnp.dot(on_ref[...], wor_ref[...], preferred_element_type=F32)
    hn = (h * _rsqrt_mean(h, h.shape[-1]) * ln2_ref[...]).astype(BF16)
    d_ff = wg_ref.shape[1]
    acc = h
    for c in range(d_ff // ff_chunk):
        cs = slice(c * ff_chunk, (c + 1) * ff_chunk)
        gt = jnp.dot(hn, wg_ref[:, cs], preferred_element_type=F32)
        up = jnp.dot(hn, wu_ref[:, cs], preferred_element_type=F32)
        act = (gt * jax.nn.sigmoid(gt) * up).astype(BF16)
        acc = acc + jnp.dot(act, wd_ref[cs, :], preferred_element_type=F32)
    o_ref[...] = acc


def _out_ffn(x2, oa2, on2, ag, woa, wor, ln2, wg, wu, wd, tm):
    n, d = x2.shape
    d_ff = wg.shape[1]
    ff_chunk = d_ff
    row = lambda i: (i, 0)
    single = pl.Buffered(1)
    wspec = lambda w: pl.BlockSpec(w.shape, lambda i: (0, 0), pipeline_mode=single)
    return pl.pallas_call(
        functools.partial(_out_kernel, ff_chunk=ff_chunk),
        grid=(n // tm,),
        in_specs=[
            pl.BlockSpec((tm, d), row),
            pl.BlockSpec((tm, oa2.shape[1]), row),
            pl.BlockSpec((tm, on2.shape[1]), row),
            wspec(ag), wspec(woa), wspec(wor), wspec(ln2), wspec(wg), wspec(wu), wspec(wd),
        ],
        out_specs=pl.BlockSpec((tm, d), row),
        out_shape=jax.ShapeDtypeStruct((n, d), F32),
        compiler_params=pltpu.CompilerParams(
            dimension_semantics=("arbitrary",), vmem_limit_bytes=VMEM_LIMIT),
        name="out_ffn",
    )(x2, oa2, on2, ag, woa, wor, ln2, wg, wu, wd)


def _rope_tables(t_len):
    half = ROPE // 2
    freqs = 1.0 / (THETA ** (np.arange(half, dtype=np.float64) / half))
    ang = np.arange(t_len, dtype=np.float64)[:, None] * freqs[None, :]
    pad = LANES - HEAD_QK
    cos_t = np.concatenate([np.ones((t_len, NOPE)), np.cos(ang), np.cos(ang), np.zeros((t_len, pad))], -1)
    sin_t = np.concatenate([np.zeros((t_len, NOPE)), np.sin(ang), np.sin(ang), np.zeros((t_len, pad))], -1)
    return jnp.asarray(cos_t, F32), jnp.asarray(sin_t, F32)


def _rot_half_cols(w):
    half = ROPE // 2
    return jnp.concatenate([-w[..., half:], w[..., :half]], axis=-1)


def _gain_rows(g):
    half = ROPE // 2
    pad = jnp.zeros((LANES - HEAD_QK,), F32)
    r0 = jnp.concatenate([g, pad])
    r1 = jnp.concatenate([jnp.zeros((NOPE,), F32), g[NOPE + half:], g[NOPE:NOPE + half], pad])
    return jnp.stack([r0, r1])


def _block_diag_pairs(w):
    g, n, _ = w.shape
    z = jnp.zeros((g // 2, n, n), w.dtype)
    top = jnp.concatenate([w[0::2], z], axis=2)
    bot = jnp.concatenate([z, w[1::2]], axis=2)
    return jnp.concatenate([top, bot], axis=1)


def kernel(x, meta_tokens, ln1_g, w_in, q_a_norm_g, w_uq, kv_a_norm_g, w_ukv, q_norm_g, k_norm_g,
           conv_w, conv_b, lru_wa, lru_ba, lru_wi, lru_bi, lru_lambda, attn_out_g, rnn_out_g,
           w_out, ln2_g, w_gate, w_up, w_down):
    bsz, seq, d_model = x.shape
    q_lora = q_a_norm_g.shape[-1]
    kv_lora = kv_a_norm_g.shape[-1]
    d_rnn = conv_w.shape[-1]
    d_attn = N_HEAD * HEAD_V
    t_len = N_META_TOK + seq
    l = 0

    wi = w_in[l]
    o_kr = q_lora + kv_lora
    w_kr = wi[:, o_kr:o_kr + ROPE]
    win = jnp.concatenate(
        [wi[:, :o_kr], jnp.zeros((d_model, NOPE), F32), w_kr, _rot_half_cols(w_kr),
         wi[:, o_kr + ROPE:]], axis=1).astype(BF16)
    wq = w_uq[l].reshape(q_lora, N_HEAD, HEAD_QK)
    wuq = jnp.concatenate([wq, _rot_half_cols(wq[..., NOPE:])], axis=-1)
    wuq = wuq.reshape(q_lora, N_HEAD * LANES).astype(BF16)
    wkv = w_ukv[l].reshape(kv_lora, N_HEAD, NOPE + HEAD_V)
    wuk = jnp.concatenate([wkv[..., :NOPE], jnp.zeros((kv_lora, N_HEAD, LANES - NOPE), F32)], -1)
    wuk = wuk.reshape(kv_lora, N_HEAD * LANES).astype(BF16)
    wuv = wkv[..., NOPE:].reshape(kv_lora, d_attn).astype(BF16)
    gq = _gain_rows(q_norm_g[l])
    gk = _gain_rows(k_norm_g[l])
    cos_t, sin_t = _rope_tables(t_len)
    lane2 = np.arange(2 * LANES)
    hsum = jnp.asarray((lane2[:, None] // LANES == lane2[None, :] // LANES)
                       & (lane2[:, None] % LANES < HEAD_QK), BF16)
    proj_w = (ln1_g[l][None], win, q_a_norm_g[l][None], wuq, kv_a_norm_g[l][None], wuk, wuv, gq, gk, hsum)
    dims = (q_lora, kv_lora, d_rnn)

    wgate = (0.5 * jnp.concatenate(
        [_block_diag_pairs(w) for w in (lru_wa[l, 0], lru_wi[l, 0], lru_wa[l, 1], lru_wi[l, 1])],
        axis=2)).astype(BF16)
    n_slab = d_rnn // LANES
    gate_b = 0.5 * jnp.stack([lru_ba[l, 0], lru_bi[l, 0], lru_ba[l, 1], lru_bi[l, 1]])
    gate_b = gate_b.reshape(4, n_slab, LANES).transpose(1, 0, 2).reshape(n_slab, 1, 4 * LANES)
    b_hi = gate_b.astype(BF16)
    b_lo = (gate_b - b_hi.astype(F32)).astype(BF16)
    wgate = jnp.concatenate(
        [wgate, b_hi, b_lo, jnp.zeros((n_slab, LANES - 2, 4 * LANES), BF16)], axis=1)

    tm = 512 if seq % 512 == 0 else seq
    q, k, v, xr, xg = _project(x, tm, cos_t[N_META_TOK:], sin_t[N_META_TOK:], proj_w, dims)
    _, km, vm, xrm, _ = _project(meta_tokens[None].astype(x.dtype), N_META_TOK,
                                 cos_t[:N_META_TOK], sin_t[:N_META_TOK], proj_w, dims)

    o_rnn = _rnn(xr, xg, xrm, conv_w[l], conv_b[l][None], wgate, lru_lambda[l], rnn_out_g[l][None])

    tq = 1024 if seq % 1024 == 0 else seq
    o_attn = _attention(q, k, v, km, vm, tq)

    wo = w_out[l].astype(BF16)
    n_rows = bsz * seq
    tmo = 512 if n_rows % 512 == 0 else n_rows
    out = _out_ffn(x.reshape(n_rows, d_model), o_attn.reshape(n_rows, d_attn),
                   o_rnn.reshape(n_rows, d_rnn), attn_out_g[l][None], wo[:d_attn], wo[d_attn:],
                   ln2_g[l][None], w_gate[l].astype(BF16), w_up[l].astype(BF16),
                   w_down[l].astype(BF16), tmo)
    return out.reshape(bsz, seq, d_model)
```

```python
import functools
import math

import numpy as np

import jax
import jax.numpy as jnp
from jax import lax
from jax.experimental import pallas as pl
from jax.experimental.pallas import tpu as pltpu

F32 = jnp.float32
BF16 = jnp.bfloat16

N_META_TOK = 16
N_HEAD = 8
NOPE = 64
ROPE = 32
HEAD_QK = NOPE + ROPE
HEAD_V = 64
LRU_SCALE = 8.0
THETA = 10000.0
NORM_EPS = 1e-6
CONV_TAPS = 4
CONV_LEFT = 2

LANES = 128
SUBLANES = 8
VMEM_LIMIT = 56 * 1024 * 1024
KEY_TILE = 512
SCAN_UNROLL = 6
SCORE_LOOKAHEAD = 2

LOG2E = 1.4426950408889634


def _rsqrt_mean(x, n):
    return lax.rsqrt(jnp.sum(x * x, axis=-1, keepdims=True) * (1.0 / n) + NORM_EPS)


def _proj_kernel(x_ref, ln1_ref, win_ref, qag_ref, wuq_ref, kvag_ref, wuk_ref, wuv_ref,
                 cos_ref, sin_ref, gq_ref, gk_ref, hsum_ref,
                 q_out, k_out, v_out, xr_out, xg_out, *, q_lora, kv_lora, d_rnn):
    x = x_ref[0]
    d_model = x.shape[-1]
    hn = (x * _rsqrt_mean(x, d_model) * ln1_ref[...]).astype(BF16)
    o_kv = q_lora
    o_kr = o_kv + kv_lora
    o_xr = o_kr + LANES
    p = jnp.dot(hn, win_ref[:, :o_xr], preferred_element_type=F32)
    c_q = p[:, :o_kv]
    c_kv = p[:, o_kv:o_kr]
    kr = p[:, o_kr:o_xr]
    n_slab = d_rnn // LANES

    def recurrent_inputs(c):
        px = jnp.dot(hn, win_ref[:, o_xr + 2 * c * LANES:o_xr + 2 * (c + 1) * LANES],
                     preferred_element_type=F32)
        for u in range(2):
            slab = 2 * c + u
            dst = xr_out if slab < n_slab else xg_out
            dst[0, slab % n_slab] = px[:, u * LANES:(u + 1) * LANES]

    cos_t = cos_ref[...]
    sin_t = sin_ref[...]
    inv_hd = 1.0 / HEAD_QK
    hsum = hsum_ref[...]

    def head_sums(sq_pair):
        return jnp.dot(sq_pair.astype(BF16), hsum, preferred_element_type=F32)

    cqn = (c_q * _rsqrt_mean(c_q, q_lora) * qag_ref[...]).astype(BF16)
    q = jnp.dot(cqn, wuq_ref[...], preferred_element_type=F32)
    cq = cos_t * gq_ref[0:1, :]
    sq = sin_t * gq_ref[1:2, :]
    q_scale = (HEAD_QK ** -0.5) * LOG2E

    ckn = (c_kv * _rsqrt_mean(c_kv, kv_lora) * kvag_ref[...]).astype(BF16)
    kn = jnp.dot(ckn, wuk_ref[...], preferred_element_type=F32)
    v_out[0] = jnp.dot(ckn, wuv_ref[...], preferred_element_type=F32).T.astype(BF16)
    ck = cos_t * gk_ref[0:1, :]
    sk = sin_t * gk_ref[1:2, :]
    kr_sq = kr * kr
    kr_sq2 = jnp.concatenate([kr_sq, kr_sq], axis=1)
    kr_rot = kr * ck + pltpu.roll(kr, LANES - ROPE, axis=1) * sk
    n_pair = N_HEAD // 2
    n_rec = n_slab
    for pr in range(n_pair):
        for c in range(pr * n_rec // n_pair, (pr + 1) * n_rec // n_pair):
            recurrent_inputs(c)
        qp = q[:, 2 * pr * LANES:2 * (pr + 1) * LANES]
        sc = lax.rsqrt(head_sums(qp * qp) * inv_hd + NORM_EPS) * q_scale
        for hh in range(2):
            h = 2 * pr + hh
            qh = qp[:, hh * LANES:(hh + 1) * LANES]
            qr = pltpu.roll(qh, LANES - ROPE, axis=1)
            qf = (qh * cq + qr * sq) * sc[:, hh * LANES:(hh + 1) * LANES]
            q_out[0, h * LANES:(h + 1) * LANES, :] = qf.T.astype(BF16)
        kp = kn[:, 2 * pr * LANES:2 * (pr + 1) * LANES]
        sc = lax.rsqrt(head_sums(kp * kp + kr_sq2) * inv_hd + NORM_EPS)
        for hh in range(2):
            h = 2 * pr + hh
            kh = kp[:, hh * LANES:(hh + 1) * LANES]
            k_out[0, :, h * LANES:(h + 1) * LANES] = (
                (kh * ck + kr_rot) * sc[:, hh * LANES:(hh + 1) * LANES]).astype(BF16)


def _const_spec(shape):
    nd = len(shape)
    return pl.BlockSpec(shape, lambda *_: (0,) * nd)


def _project(x3, tm, cos_t, sin_t, wts, dims):
    q_lora, kv_lora, d_rnn = dims
    b, s, d = x3.shape
    n_slab = d_rnn // LANES
    nt = s // tm
    ln1, win, qag, wuq, kvag, wuk, wuv, gq, gk, hsum = wts
    row3 = lambda i, j: (i, j, 0)
    row4 = lambda i, j: (i, 0, j, 0)
    tab = lambda i, j: (j, 0)
    in_specs = [pl.BlockSpec((1, tm, d), row3)]
    in_specs += [_const_spec(w.shape) for w in (ln1, win, qag, wuq, kvag, wuk, wuv)]
    in_specs += [pl.BlockSpec((tm, LANES), tab), pl.BlockSpec((tm, LANES), tab)]
    in_specs += [_const_spec(gq.shape), _const_spec(gk.shape), _const_spec(hsum.shape)]
    col3 = lambda i, j: (i, 0, j)
    out_shape = (
        jax.ShapeDtypeStruct((b, N_HEAD * LANES, s), BF16),
        jax.ShapeDtypeStruct((b, s, N_HEAD * LANES), BF16),
        jax.ShapeDtypeStruct((b, N_HEAD * HEAD_V, s), BF16),
        jax.ShapeDtypeStruct((b, n_slab, s, LANES), F32),
        jax.ShapeDtypeStruct((b, n_slab, s, LANES), F32),
    )
    out_specs = (
        pl.BlockSpec((1, N_HEAD * LANES, tm), col3),
        pl.BlockSpec((1, tm, N_HEAD * LANES), row3),
        pl.BlockSpec((1, N_HEAD * HEAD_V, tm), col3),
        pl.BlockSpec((1, n_slab, tm, LANES), row4),
        pl.BlockSpec((1, n_slab, tm, LANES), row4),
    )
    return pl.pallas_call(
        functools.partial(_proj_kernel, q_lora=q_lora, kv_lora=kv_lora, d_rnn=d_rnn),
        grid=(b, nt),
        in_specs=in_specs,
        out_specs=out_specs,
        out_shape=out_shape,
        compiler_params=pltpu.CompilerParams(
            dimension_semantics=("arbitrary", "arbitrary"), vmem_limit_bytes=VMEM_LIMIT),
        name="proj",
    )(x3, ln1, win, qag, wuq, kvag, wuk, wuv, cos_t, sin_t, gq, gk, hsum)


def _scan_stride(t_len):
    s = -(-t_len // SUBLANES)
    while s % SUBLANES == 0:
        s += 1
    if s % 2:
        s += 1
        if s % SUBLANES == 0:
            s += 2
    return s


def _chunk_len(stride, cap=48):
    for c in range(min(cap, stride), 0, -1):
        if stride % c == 0:
            return c
    return 1


def _rnn_kernel(xr_ref, xg_ref, xrm_ref, cw_ref, cb_ref, wg_ref, lam_ref, g_ref,
                o_ref, t1, af, bf, ab, bb, *, seq, stride, rc, tm_out):
    n_slab = xr_ref.shape[1]
    t_len = N_META_TOK + seq
    pre = SUBLANES
    rows = t1.shape[1]
    n_chunk = stride // rc
    crow = rc * SUBLANES

    for s in range(n_slab):
        t1[s, 0:pre, :] = jnp.zeros((pre, LANES), F32)
        t1[s, pre:pre + N_META_TOK, :] = xrm_ref[0, s]
        t1[s, pre + N_META_TOK:pre + t_len, :] = xr_ref[0, s]
        t1[s, pre + t_len:rows, :] = jnp.zeros((rows - pre - t_len, LANES), F32)

    lam = lam_ref[...]
    nl = -lam
    softplus = jnp.maximum(nl, 0.0) + jnp.log(1.0 + jnp.exp(-jnp.abs(nl)))
    hcn = (0.5 * LRU_SCALE) * softplus
    cw = cw_ref[...]
    cb = cb_ref[...]
    ones_k = jnp.ones((crow, LANES), BF16)

    padded = stride * SUBLANES > t_len
    if padded:
        ridx = lax.broadcasted_iota(jnp.int32, (crow, LANES), 0)
        t_local = (ridx & (SUBLANES - 1)) * stride + (ridx >> 3)

    def gate_chunk(c, carry):
        r0 = c * rc
        row0 = pl.multiple_of(c * crow, SUBLANES)
        for s in range(n_slab):
            ls = slice(s * LANES, (s + 1) * LANES)
            taps = [t1[s, pl.ds(r0 + pre - CONV_LEFT + j, SUBLANES, stride=stride), :]
                    for j in range(rc + CONV_TAPS - 1)]
            xc_rows = []
            for rr in range(rc):
                acc = taps[rr] * cw[0:1, ls]
                for j in range(1, CONV_TAPS):
                    acc = acc + taps[rr + j] * cw[j:j + 1, ls]
                xc_rows.append(acc + cb[:, ls])
            xc = jnp.concatenate(xc_rows, axis=0)
            hx = 0.5 * xc
            z = jnp.dot(jnp.concatenate([xc.astype(BF16), ones_k], axis=1), wg_ref[s],
                        preferred_element_type=F32)
            for d, (a_ref, b_ref) in enumerate(((af, bf), (ab, bb))):
                ta = jnp.tanh(z[:, (2 * d) * LANES:(2 * d + 1) * LANES])
                ti = jnp.tanh(z[:, (2 * d + 1) * LANES:(2 * d + 2) * LANES])
                hc = hcn[d:d + 1, ls]
                nla = ta * hc + hc
                a = jnp.exp2(nla * (-LOG2E))
                one_m_a2 = jnp.tanh(nla) * (1.0 + a * a)
                root = jnp.where(one_m_a2 > 0.0, one_m_a2 * lax.rsqrt(one_m_a2), 0.0)
                bv = root * (ti * hx + hx)
                if d == 1 and padded:
                    bv = jnp.where((t_local + r0) < t_len, bv, 0.0)
                a_ref[s, pl.ds(row0, crow), :] = a
                b_ref[s, pl.ds(row0, crow), :] = bv
        return carry

    lax.fori_loop(0, n_chunk, gate_chunk, 0)

    zero = jnp.zeros((SUBLANES, LANES), F32)
    one = jnp.ones((SUBLANES, LANES), F32)

    def scan_step(r, carry):
        hf, pf, hb, pb = carry
        rf = pl.multiple_of(r * SUBLANES, SUBLANES)
        rb = pl.multiple_of((stride - 1 - r) * SUBLANES, SUBLANES)
        nhf, npf, nhb, npb = [], [], [], []
        for s in range(n_slab):
            a = af[s, pl.ds(rf, SUBLANES), :]
            h = a * hf[s] + bf[s, pl.ds(rf, SUBLANES), :]
            pr = a * pf[s]
            bf[s, pl.ds(rf, SUBLANES), :] = h
            af[s, pl.ds(rf, SUBLANES), :] = pr
            nhf.append(h)
            npf.append(pr)
            a = ab[s, pl.ds(rb, SUBLANES), :]
            h = a * hb[s] + bb[s, pl.ds(rb, SUBLANES), :]
            pr = a * pb[s]
            bb[s, pl.ds(rb, SUBLANES), :] = h
            ab[s, pl.ds(rb, SUBLANES), :] = pr
            nhb.append(h)
            npb.append(pr)
        return tuple(nhf), tuple(npf), tuple(nhb), tuple(npb)

    init = ((zero,) * n_slab, (one,) * n_slab, (zero,) * n_slab, (one,) * n_slab)
    hf, pf, hb, pb = lax.fori_loop(0, stride, scan_step, init, unroll=SCAN_UNROLL)

    sub = lax.broadcasted_iota(jnp.int32, (SUBLANES, LANES), 0)
    cf, cbk = [], []
    for s in range(n_slab):
        c = zero
        for _ in range(SUBLANES - 1):
            c = jnp.where(sub == 0, 0.0, pltpu.roll(pf[s] * c + hf[s], 1, axis=0))
        cf.append(c)
        c = zero
        for _ in range(SUBLANES - 1):
            c = jnp.where(sub == SUBLANES - 1, 0.0, pltpu.roll(pb[s] * c + hb[s], SUBLANES - 1, axis=0))
        cbk.append(c)

    def combine(r, carry):
        rp = pl.multiple_of(r * SUBLANES, SUBLANES)
        for s in range(n_slab):
            y = (bf[s, pl.ds(rp, SUBLANES), :] + af[s, pl.ds(rp, SUBLANES), :] * cf[s]
                 + bb[s, pl.ds(rp, SUBLANES), :] + ab[s, pl.ds(rp, SUBLANES), :] * cbk[s])
            t1[s, pl.ds(r + pre, SUBLANES, stride=stride), :] = y
        return carry

    lax.fori_loop(0, stride, combine, 0, unroll=SCAN_UNROLL)

    d_rnn = n_slab * LANES
    g = g_ref[...]
    k0 = math.sqrt(2.0 / math.pi)

    def finish(j, carry):
        ro = pl.multiple_of(j * tm_out, 2 * SUBLANES)
        rt = pl.multiple_of(j * tm_out + pre + N_META_TOK, SUBLANES)
        ys = []
        ssq = jnp.zeros((tm_out, 1), F32)
        for s in range(n_slab):
            y = t1[s, pl.ds(rt, tm_out), :]
            xg = xg_ref[0, s, pl.ds(ro, tm_out), :]
            yh = y * (0.5 * xg)
            y = yh + yh * jnp.tanh(xg * (k0 + (k0 * 0.044715) * (xg * xg)))
            ssq = ssq + jnp.sum(y * y, axis=-1, keepdims=True)
            ys.append(y)
        sc = lax.rsqrt(ssq * (1.0 / d_rnn) + NORM_EPS)
        for s in range(n_slab):
            ls = slice(s * LANES, (s + 1) * LANES)
            o_ref[0, pl.ds(ro, tm_out), ls] = (ys[s] * sc * g[:, ls]).astype(BF16)
        return carry

    lax.fori_loop(0, seq // tm_out, finish, 0)


def _rnn(xr, xg, xrm, cw, cb, wg, lam, g):
    b, n_slab, seq, _ = xr.shape
    t_len = N_META_TOK + seq
    stride = _scan_stride(t_len)
    rc = _chunk_len(stride)
    rows_perm = stride * SUBLANES
    rows_time = SUBLANES + rows_perm + SUBLANES
    tm_out = 256 if seq % 256 == 0 else seq
    blk = lambda i: (i, 0, 0, 0)
    in_specs = [
        pl.BlockSpec((1, n_slab, seq, LANES), blk),
        pl.BlockSpec((1, n_slab, seq, LANES), blk),
        _const_spec(xrm.shape),
    ] + [_const_spec(w.shape) for w in (cw, cb, wg, lam, g)]
    scratch = [pltpu.VMEM((n_slab, rows_time, LANES), F32)]
    scratch += [pltpu.VMEM((n_slab, rows_perm, LANES), F32) for _ in range(4)]
    return pl.pallas_call(
        functools.partial(_rnn_kernel, seq=seq, stride=stride, rc=rc, tm_out=tm_out),
        grid=(b,),
        in_specs=in_specs,
        out_specs=pl.BlockSpec((1, seq, n_slab * LANES), lambda i: (i, 0, 0)),
        out_shape=jax.ShapeDtypeStruct((b, seq, n_slab * LANES), BF16),
        scratch_shapes=scratch,
        compiler_params=pltpu.CompilerParams(
            dimension_semantics=("arbitrary",), vmem_limit_bytes=VMEM_LIMIT),
        name="rnn",
    )(xr, xg, xrm, cw, cb, wg, lam, g)


def _attn_kernel(qt_ref, k_ref, vt_ref, km_ref, vtm_ref, o_ref, *, tq):
    n_key = k_ref.shape[1]

    n_stream = 2 * (n_key // tq)
    n_tile = n_key // KEY_TILE
    tiles = [(t * KEY_TILE, (t + 1) * KEY_TILE) for t in range(n_tile)]
    tiles[-1] = (tiles[-1][0], n_key + N_META_TOK)
    steps = [(i, t) for i in range(n_stream) for t in range(n_tile)]

    def score_tile(i, t):
        j, hh = divmod(i, 2)
        k0, k1 = tiles[t]
        ls = slice(hh * LANES, (hh + 1) * LANES)
        keys = k_ref[0, k0:min(k1, n_key), ls]
        if k1 > n_key:
            keys = jnp.concatenate([keys, km_ref[0, :, ls]], axis=0)
        return jnp.dot(keys, qt_ref[0, ls, j * tq:(j + 1) * tq],
                       preferred_element_type=F32)

    outs = []
    pending = [score_tile(*steps[n]) for n in range(SCORE_LOOKAHEAD)]
    m, acc, l8 = None, None, None
    for n, (i, t) in enumerate(steps):
        st = pending.pop(0)
        if n + SCORE_LOOKAHEAD < len(steps):
            pending.append(score_tile(*steps[n + SCORE_LOOKAHEAD]))
        k0, k1 = tiles[t]
        vs = slice((i % 2) * HEAD_V, (i % 2 + 1) * HEAD_V)
        st3 = st.reshape((k1 - k0) // SUBLANES, SUBLANES, tq)
        tmax = jnp.max(jnp.max(st3, axis=0), axis=0, keepdims=True)
        m_new = tmax if m is None else jnp.maximum(m, tmax)
        mb = jnp.broadcast_to(m_new, (SUBLANES, tq))
        p3 = jnp.exp2(st3 - mb[None])
        ps = jnp.sum(p3, axis=0)
        pb = p3.reshape(k1 - k0, tq).astype(BF16)
        part = jnp.dot(vt_ref[0, vs, k0:k0 + KEY_TILE], pb[:KEY_TILE], preferred_element_type=F32)
        if k1 - k0 > KEY_TILE:
            part = part + jnp.dot(vtm_ref[0, vs, :], pb[KEY_TILE:], preferred_element_type=F32)
        if m is None:
            acc, l8 = part, ps
        else:
            alpha = jnp.exp2(m - m_new)
            acc, l8 = acc * alpha + part, l8 * alpha + ps
        m = m_new
        if t == n_tile - 1:
            outs.append(acc * (1.0 / jnp.sum(l8, axis=0, keepdims=True)))
            m, acc, l8 = None, None, None
            if i % 2 == 1:
                j = i // 2
                o_ref[0, j * tq:(j + 1) * tq, :] = jnp.concatenate(outs, axis=0).T
                outs = []


def _attention(qt, k, vt, km, vtm, tq):
    b, s, _ = k.shape
    n_pair = N_HEAD // 2
    return pl.pallas_call(
        functools.partial(_attn_kernel, tq=tq),
        grid=(b, n_pair),
        in_specs=[
            pl.BlockSpec((1, 2 * LANES, s), lambda i, p: (i, p, 0)),
            pl.BlockSpec((1, s, 2 * LANES), lambda i, p: (i, 0, p)),
            pl.BlockSpec((1, 2 * HEAD_V, s), lambda i, p: (i, p, 0)),
            pl.BlockSpec((1, N_META_TOK, 2 * LANES), lambda i, p: (0, 0, p)),
            pl.BlockSpec((1, 2 * HEAD_V, N_META_TOK), lambda i, p: (0, p, 0)),
        ],
        out_specs=pl.BlockSpec((1, s, 2 * HEAD_V), lambda i, p: (i, 0, p)),
        out_shape=jax.ShapeDtypeStruct((b, s, N_HEAD * HEAD_V), F32),
        compiler_params=pltpu.CompilerParams(
            dimension_semantics=("arbitrary", "arbitrary"), vmem_limit_bytes=VMEM_LIMIT),
        name="attn",
    )(qt, k, vt, km, vtm)


def _out_kernel(x_ref, oa_ref, on_ref, ag_ref, woa_ref, wor_ref, ln2_ref, wg_ref, wu_ref, wd_ref,
                o_ref, *, ff_chunk):
    oa = oa_ref[...]
    oan = (oa * _rsqrt_mean(oa, oa.shape[-1]) * ag_ref[...]).astype(BF16)
    h = x_ref[...] + jnp.dot(oan, woa_ref[...], preferred_element_type=F32)
    h = h + jnp.dot(on_ref[...], wor_ref[...], preferred_element_type=F32)
    hn = (h * _rsqrt_mean(h, h.shape[-1]) * ln2_ref[...]).astype(BF16)
    d_ff = wg_ref.shape[1]
    acc = h
    for c in range(d_ff // ff_chunk):
        cs = slice(c * ff_chunk, (c + 1) * ff_chunk)
        gt = jnp.dot(hn, wg_ref[:, cs], preferred_element_type=F32)
        up = jnp.dot(hn, wu_ref[:, cs], preferred_element_type=F32)
        act = (gt * jax.nn.sigmoid(gt) * up).astype(BF16)
        acc = acc + jnp.dot(act, wd_ref[cs, :], preferred_element_type=F32)
    o_ref[...] = acc


def _out_ffn(x2, oa2, on2, ag, woa, wor, ln2, wg, wu, wd, tm):
    n, d = x2.shape
    d_ff = wg.shape[1]
    ff_chunk = d_ff
    row = lambda i: (i, 0)
    single = pl.Buffered(1)
    wspec = lambda w: pl.BlockSpec(w.shape, lambda i: (0, 0), pipeline_mode=single)
    return pl.pallas_call(
        functools.partial(_out_kernel, ff_chunk=ff_chunk),
        grid=(n // tm,),
        in_specs=[
            pl.BlockSpec((tm, d), row),
            pl.BlockSpec((tm, oa2.shape[1]), row),
            pl.BlockSpec((tm, on2.shape[1]), row),
            wspec(ag), wspec(woa), wspec(wor), wspec(ln2), wspec(wg), wspec(wu), wspec(wd),
        ],
        out_specs=pl.BlockSpec((tm, d), row),
        out_shape=jax.ShapeDtypeStruct((n, d), F32),
        compiler_params=pltpu.CompilerParams(
            dimension_semantics=("arbitrary",), vmem_limit_bytes=VMEM_LIMIT),
        name="out_ffn",
    )(x2, oa2, on2, ag, woa, wor, ln2, wg, wu, wd)


def _rope_tables(t_len):
    half = ROPE // 2
    freqs = 1.0 / (THETA ** (np.arange(half, dtype=np.float64) / half))
    ang = np.arange(t_len, dtype=np.float64)[:, None] * freqs[None, :]
    pad = LANES - HEAD_QK
    cos_t = np.concatenate([np.ones((t_len, NOPE)), np.cos(ang), np.cos(ang), np.zeros((t_len, pad))], -1)
    sin_t = np.concatenate([np.zeros((t_len, NOPE)), np.sin(ang), np.sin(ang), np.zeros((t_len, pad))], -1)
    return jnp.asarray(cos_t, F32), jnp.asarray(sin_t, F32)


def _rot_half_cols(w):
    half = ROPE // 2
    return jnp.concatenate([-w[..., half:], w[..., :half]], axis=-1)


def _gain_rows(g):
    half = ROPE // 2
    pad = jnp.zeros((LANES - HEAD_QK,), F32)
    r0 = jnp.concatenate([g, pad])
    r1 = jnp.concatenate([jnp.zeros((NOPE,), F32), g[NOPE + half:], g[NOPE:NOPE + half], pad])
    return jnp.stack([r0, r1])


def _block_diag_pairs(w):
    g, n, _ = w.shape
    z = jnp.zeros((g // 2, n, n), w.dtype)
    top = jnp.concatenate([w[0::2], z], axis=2)
    bot = jnp.concatenate([z, w[1::2]], axis=2)
    return jnp.concatenate([top, bot], axis=1)


def kernel(x, meta_tokens, ln1_g, w_in, q_a_norm_g, w_uq, kv_a_norm_g, w_ukv, q_norm_g, k_norm_g,
           conv_w, conv_b, lru_wa, lru_ba, lru_wi, lru_bi, lru_lambda, attn_out_g, rnn_out_g,
           w_out, ln2_g, w_gate, w_up, w_down):
    bsz, seq, d_model = x.shape
    q_lora = q_a_norm_g.shape[-1]
    kv_lora = kv_a_norm_g.shape[-1]
    d_rnn = conv_w.shape[-1]
    d_attn = N_HEAD * HEAD_V
    t_len = N_META_TOK + seq
    l = 0

    wi = w_in[l]
    o_kr = q_lora + kv_lora
    w_kr = wi[:, o_kr:o_kr + ROPE]
    win = jnp.concatenate(
        [wi[:, :o_kr], jnp.zeros((d_model, NOPE), F32), w_kr, _rot_half_cols(w_kr),
         wi[:, o_kr + ROPE:]], axis=1).astype(BF16)
    wq = w_uq[l].reshape(q_lora, N_HEAD, HEAD_QK)
    wuq = jnp.concatenate([wq, _rot_half_cols(wq[..., NOPE:])], axis=-1)
    wuq = wuq.reshape(q_lora, N_HEAD * LANES).astype(BF16)
    wkv = w_ukv[l].reshape(kv_lora, N_HEAD, NOPE + HEAD_V)
    wuk = jnp.concatenate([wkv[..., :NOPE], jnp.zeros((kv_lora, N_HEAD, LANES - NOPE), F32)], -1)
    wuk = wuk.reshape(kv_lora, N_HEAD * LANES).astype(BF16)
    wuv = wkv[..., NOPE:].reshape(kv_lora, d_attn).astype(BF16)
    gq = _gain_rows(q_norm_g[l])
    gk = _gain_rows(k_norm_g[l])
    cos_t, sin_t = _rope_tables(t_len)
    lane2 = np.arange(2 * LANES)
    hsum = jnp.asarray((lane2[:, None] // LANES == lane2[None, :] // LANES)
                       & (lane2[:, None] % LANES < HEAD_QK), BF16)
    proj_w = (ln1_g[l][None], win, q_a_norm_g[l][None], wuq, kv_a_norm_g[l][None], wuk, wuv, gq, gk, hsum)
    dims = (q_lora, kv_lora, d_rnn)

    wgate = (0.5 * jnp.concatenate(
        [_block_diag_pairs(w) for w in (lru_wa[l, 0], lru_wi[l, 0], lru_wa[l, 1], lru_wi[l, 1])],
        axis=2)).astype(BF16)
    n_slab = d_rnn // LANES
    gate_b = 0.5 * jnp.stack([lru_ba[l, 0], lru_bi[l, 0], lru_ba[l, 1], lru_bi[l, 1]])
    gate_b = gate_b.reshape(4, n_slab, LANES).transpose(1, 0, 2).reshape(n_slab, 1, 4 * LANES)
    b_hi = gate_b.astype(BF16)
    b_lo = (gate_b - b_hi.astype(F32)).astype(BF16)
    wgate = jnp.concatenate(
        [wgate, b_hi, b_lo, jnp.zeros((n_slab, LANES - 2, 4 * LANES), BF16)], axis=1)

    tm = 512 if seq % 512 == 0 else seq
    q, k, v, xr, xg = _project(x, tm, cos_t[N_META_TOK:], sin_t[N_META_TOK:], proj_w, dims)
    _, km, vm, xrm, _ = _project(meta_tokens[None].astype(x.dtype), N_META_TOK,
                                 cos_t[:N_META_TOK], sin_t[:N_META_TOK], proj_w, dims)

    o_rnn = _rnn(xr, xg, xrm, conv_w[l], conv_b[l][None], wgate, lru_lambda[l], rnn_out_g[l][None])

    tq = 1024 if seq % 1024 == 0 else seq
    o_attn = _attention(q, k, v, km, vm, tq)

    wo = w_out[l].astype(BF16)
    n_rows = bsz * seq
    tmo = 512 if n_rows % 512 == 0 else n_rows
    out = _out_ffn(x.reshape(n_rows, d_model), o_attn.reshape(n_rows, d_attn),
                   o_rnn.reshape(n_rows, d_rnn), attn_out_g[l][None], wo[:d_attn], wo[d_attn:],
                   ln2_g[l][None], w_gate[l].astype(BF16), w_up[l].astype(BF16),
                   w_down[l].astype(BF16), tmo)
    return out.reshape(bsz, seq, d_model)
```

```python
import functools
import math

import numpy as np

import jax
import jax.numpy as jnp
from jax import lax
from jax.experimental import pallas as pl
from jax.experimental.pallas import tpu as pltpu

F32 = jnp.float32
BF16 = jnp.bfloat16

N_META_TOK = 16
N_HEAD = 8
NOPE = 64
ROPE = 32
HEAD_QK = NOPE + ROPE
HEAD_V = 64
VT_ROWS = HEAD_V + 16
LRU_SCALE = 8.0
THETA = 10000.0
NORM_EPS = 1e-6
CONV_TAPS = 4
CONV_LEFT = 2

LANES = 128
SUBLANES = 8
VMEM_LIMIT = 56 * 1024 * 1024
KEY_TILE = 256
SCAN_UNROLL = 6
SCORE_LOOKAHEAD = 2

LOG2E = 1.4426950408889634


def _rsqrt_mean(x, n):
    return lax.rsqrt(jnp.sum(x * x, axis=-1, keepdims=True) * (1.0 / n) + NORM_EPS)


def _proj_kernel(x_ref, ln1_ref, win_ref, qag_ref, wuq_ref, kvag_ref, wuk_ref, wuv_ref,
                 cos_ref, sin_ref, gq_ref, gk_ref, hsum_ref,
                 q_out, k_out, v_out, xr_out, xg_out, *, q_lora, kv_lora, d_rnn):
    x = x_ref[0]
    d_model = x.shape[-1]
    hn = (x * _rsqrt_mean(x, d_model) * ln1_ref[...]).astype(BF16)
    o_kv = q_lora
    o_kr = o_kv + kv_lora
    o_xr = o_kr + LANES
    p = jnp.dot(hn, win_ref[:, :o_xr], preferred_element_type=F32)
    c_q = p[:, :o_kv]
    c_kv = p[:, o_kv:o_kr]
    kr = p[:, o_kr:o_xr]
    n_slab = d_rnn // LANES

    def recurrent_inputs(c):
        px = jnp.dot(hn, win_ref[:, o_xr + 2 * c * LANES:o_xr + 2 * (c + 1) * LANES],
                     preferred_element_type=F32)
        for u in range(2):
            slab = 2 * c + u
            dst = xr_out if slab < n_slab else xg_out
            dst[0, slab % n_slab] = px[:, u * LANES:(u + 1) * LANES]

    cos_t = cos_ref[...]
    sin_t = sin_ref[...]
    inv_hd = 1.0 / HEAD_QK
    hsum = hsum_ref[...]

    def head_sums(sq_pair):
        return jnp.dot(sq_pair.astype(BF16), hsum, preferred_element_type=F32)

    cqn = (c_q * _rsqrt_mean(c_q, q_lora) * qag_ref[...]).astype(BF16)
    q = jnp.dot(cqn, wuq_ref[...], preferred_element_type=F32)
    cq = cos_t * gq_ref[0:1, :]
    sq = sin_t * gq_ref[1:2, :]
    q_scale = (HEAD_QK ** -0.5) * LOG2E

    ckn = (c_kv * _rsqrt_mean(c_kv, kv_lora) * kvag_ref[...]).astype(BF16)
    kn = jnp.dot(ckn, wuk_ref[...], preferred_element_type=F32)
    vt = jnp.dot(ckn, wuv_ref[...], preferred_element_type=F32).T
    tm = vt.shape[1]
    ones_rows = (lax.broadcasted_iota(jnp.int32, (VT_ROWS - HEAD_V, tm), 0) == 0).astype(BF16)
    for h in range(N_HEAD):
        v_out[0, h * VT_ROWS:h * VT_ROWS + HEAD_V, :] = vt[h * HEAD_V:(h + 1) * HEAD_V].astype(BF16)
        v_out[0, h * VT_ROWS + HEAD_V:(h + 1) * VT_ROWS, :] = ones_rows
    ck = cos_t * gk_ref[0:1, :]
    sk = sin_t * gk_ref[1:2, :]
    kr_sq = kr * kr
    kr_sq2 = jnp.concatenate([kr_sq, kr_sq], axis=1)
    kr_rot = kr * ck + pltpu.roll(kr, LANES - ROPE, axis=1) * sk
    n_pair = N_HEAD // 2
    n_rec = n_slab
    for pr in range(n_pair):
        for c in range(pr * n_rec // n_pair, (pr + 1) * n_rec // n_pair):
            recurrent_inputs(c)
        qp = q[:, 2 * pr * LANES:2 * (pr + 1) * LANES]
        sc = lax.rsqrt(head_sums(qp * qp) * inv_hd + NORM_EPS) * q_scale
        for hh in range(2):
            h = 2 * pr + hh
            qh = qp[:, hh * LANES:(hh + 1) * LANES]
            qr = pltpu.roll(qh, LANES - ROPE, axis=1)
            qf = (qh * cq + qr * sq) * sc[:, hh * LANES:(hh + 1) * LANES]
            q_out[0, h * LANES:(h + 1) * LANES, :] = qf.T.astype(BF16)
        kp = kn[:, 2 * pr * LANES:2 * (pr + 1) * LANES]
        sc = lax.rsqrt(head_sums(kp * kp + kr_sq2) * inv_hd + NORM_EPS)
        for hh in range(2):
            h = 2 * pr + hh
            kh = kp[:, hh * LANES:(hh + 1) * LANES]
            k_out[0, :, h * LANES:(h + 1) * LANES] = (
                (kh * ck + kr_rot) * sc[:, hh * LANES:(hh + 1) * LANES]).astype(BF16)


def _const_spec(shape):
    nd = len(shape)
    return pl.BlockSpec(shape, lambda *_: (0,) * nd)


def _project(x3, tm, cos_t, sin_t, wts, dims):
    q_lora, kv_lora, d_rnn = dims
    b, s, d = x3.shape
    n_slab = d_rnn // LANES
    nt = s // tm
    ln1, win, qag, wuq, kvag, wuk, wuv, gq, gk, hsum = wts
    row3 = lambda i, j: (i, j, 0)
    row4 = lambda i, j: (i, 0, j, 0)
    tab = lambda i, j: (j, 0)
    in_specs = [pl.BlockSpec((1, tm, d), row3)]
    in_specs += [_const_spec(w.shape) for w in (ln1, win, qag, wuq, kvag, wuk, wuv)]
    in_specs += [pl.BlockSpec((tm, LANES), tab), pl.BlockSpec((tm, LANES), tab)]
    in_specs += [_const_spec(gq.shape), _const_spec(gk.shape), _const_spec(hsum.shape)]
    col3 = lambda i, j: (i, 0, j)
    out_shape = (
        jax.ShapeDtypeStruct((b, N_HEAD * LANES, s), BF16),
        jax.ShapeDtypeStruct((b, s, N_HEAD * LANES), BF16),
        jax.ShapeDtypeStruct((b, N_HEAD * VT_ROWS, s), BF16),
        jax.ShapeDtypeStruct((b, n_slab, s, LANES), F32),
        jax.ShapeDtypeStruct((b, n_slab, s, LANES), F32),
    )
    out_specs = (
        pl.BlockSpec((1, N_HEAD * LANES, tm), col3),
        pl.BlockSpec((1, tm, N_HEAD * LANES), row3),
        pl.BlockSpec((1, N_HEAD * VT_ROWS, tm), col3),
        pl.BlockSpec((1, n_slab, tm, LANES), row4),
        pl.BlockSpec((1, n_slab, tm, LANES), row4),
    )
    return pl.pallas_call(
        functools.partial(_proj_kernel, q_lora=q_lora, kv_lora=kv_lora, d_rnn=d_rnn),
        grid=(b, nt),
        in_specs=in_specs,
        out_specs=out_specs,
        out_shape=out_shape,
        compiler_params=pltpu.CompilerParams(
            dimension_semantics=("arbitrary", "arbitrary"), vmem_limit_bytes=VMEM_LIMIT),
        name="proj",
    )(x3, ln1, win, qag, wuq, kvag, wuk, wuv, cos_t, sin_t, gq, gk, hsum)


def _scan_stride(t_len):
    s = -(-t_len // SUBLANES)
    while s % SUBLANES == 0:
        s += 1
    if s % 2:
        s += 1
        if s % SUBLANES == 0:
            s += 2
    return s


def _chunk_len(stride, cap=258):
    for c in range(min(cap, stride), 0, -1):
        if stride % c == 0:
            return c
    return 1


def _rnn_kernel(xr_ref, xg_ref, xrm_ref, cw_ref, cb_ref, wg_ref, lam_ref, g_ref,
                o_ref, t1, af, bf, ab, bb, *, seq, stride, rc, tm_out):
    n_slab = xr_ref.shape[1]
    t_len = N_META_TOK + seq
    pre = SUBLANES
    rows = t1.shape[1]
    n_chunk = stride // rc
    crow = rc * SUBLANES

    for s in range(n_slab):
        t1[s, 0:pre, :] = jnp.zeros((pre, LANES), F32)
        t1[s, pre:pre + N_META_TOK, :] = xrm_ref[0, s]
        t1[s, pre + N_META_TOK:pre + t_len, :] = xr_ref[0, s]
        t1[s, pre + t_len:rows, :] = jnp.zeros((rows - pre - t_len, LANES), F32)

    lam = lam_ref[...]
    nl = -lam
    softplus = jnp.maximum(nl, 0.0) + jnp.log(1.0 + jnp.exp(-jnp.abs(nl)))
    hcn = (0.5 * LRU_SCALE) * softplus
    cw = cw_ref[...]
    cb = cb_ref[...]
    ones_k = jnp.ones((crow, LANES), BF16)

    padded = stride * SUBLANES > t_len
    if padded:
        ridx = lax.broadcasted_iota(jnp.int32, (crow, LANES), 0)
        t_local = (ridx & (SUBLANES - 1)) * stride + (ridx >> 3)

    def gate_chunk(c, carry):
        r0 = c * rc
        row0 = pl.multiple_of(c * crow, SUBLANES)
        for s in range(n_slab):
            ls = slice(s * LANES, (s + 1) * LANES)
            taps = [t1[s, pl.ds(r0 + pre - CONV_LEFT + j, SUBLANES, stride=stride), :]
                    for j in range(rc + CONV_TAPS - 1)]
            xc_rows = []
            for rr in range(rc):
                acc = taps[rr] * cw[0:1, ls]
                for j in range(1, CONV_TAPS):
                    acc = acc + taps[rr + j] * cw[j:j + 1, ls]
                xc_rows.append(acc + cb[:, ls])
            xc = jnp.concatenate(xc_rows, axis=0)
            hx = 0.5 * xc
            z = jnp.dot(jnp.concatenate([xc.astype(BF16), ones_k], axis=1), wg_ref[s],
                        preferred_element_type=F32)
            for d, (a_ref, b_ref) in enumerate(((af, bf), (ab, bb))):
                ta = jnp.tanh(z[:, (2 * d) * LANES:(2 * d + 1) * LANES])
                ti = jnp.tanh(z[:, (2 * d + 1) * LANES:(2 * d + 2) * LANES])
                hc = hcn[d:d + 1, ls]
                nla = ta * hc + hc
                a = jnp.exp2(nla * (-LOG2E))
                one_m_a2 = jnp.tanh(nla) * (1.0 + a * a)
                root = jnp.where(one_m_a2 > 0.0, one_m_a2 * lax.rsqrt(one_m_a2), 0.0)
                bv = root * (ti * hx + hx)
                if d == 1 and padded:
                    bv = jnp.where((t_local + r0) < t_len, bv, 0.0)
                a_ref[s, pl.ds(row0, crow), :] = a
                b_ref[s, pl.ds(row0, crow), :] = bv
        return carry

    lax.fori_loop(0, n_chunk, gate_chunk, 0)

    zero = jnp.zeros((SUBLANES, LANES), F32)
    one = jnp.ones((SUBLANES, LANES), F32)

    def scan_step(r, carry):
        hf, pf, hb, pb = carry
        rf = pl.multiple_of(r * SUBLANES, SUBLANES)
        rb = pl.multiple_of((stride - 1 - r) * SUBLANES, SUBLANES)
        nhf, npf, nhb, npb = [], [], [], []
        for s in range(n_slab):
            a = af[s, pl.ds(rf, SUBLANES), :]
            h = a * hf[s] + bf[s, pl.ds(rf, SUBLANES), :]
            pr = a * pf[s]
            bf[s, pl.ds(rf, SUBLANES), :] = h
            af[s, pl.ds(rf, SUBLANES), :] = pr
            nhf.append(h)
            npf.append(pr)
            a = ab[s, pl.ds(rb, SUBLANES), :]
            h = a * hb[s] + bb[s, pl.ds(rb, SUBLANES), :]
            pr = a * pb[s]
            bb[s, pl.ds(rb, SUBLANES), :] = h
            ab[s, pl.ds(rb, SUBLANES), :] = pr
            nhb.append(h)
            npb.append(pr)
        return tuple(nhf), tuple(npf), tuple(nhb), tuple(npb)

    init = ((zero,) * n_slab, (one,) * n_slab, (zero,) * n_slab, (one,) * n_slab)
    hf, pf, hb, pb = lax.fori_loop(0, stride, scan_step, init, unroll=SCAN_UNROLL)

    sub = lax.broadcasted_iota(jnp.int32, (SUBLANES, LANES), 0)
    cf, cbk = [], []
    for s in range(n_slab):
        c = zero
        for _ in range(SUBLANES - 1):
            c = jnp.where(sub == 0, 0.0, pltpu.roll(pf[s] * c + hf[s], 1, axis=0))
        cf.append(c)
        c = zero
        for _ in range(SUBLANES - 1):
            c = jnp.where(sub == SUBLANES - 1, 0.0, pltpu.roll(pb[s] * c + hb[s], SUBLANES - 1, axis=0))
        cbk.append(c)

    def combine(r, carry):
        rp = pl.multiple_of(r * SUBLANES, SUBLANES)
        for s in range(n_slab):
            y = (bf[s, pl.ds(rp, SUBLANES), :] + af[s, pl.ds(rp, SUBLANES), :] * cf[s]
                 + bb[s, pl.ds(rp, SUBLANES), :] + ab[s, pl.ds(rp, SUBLANES), :] * cbk[s])
            t1[s, pl.ds(r + pre, SUBLANES, stride=stride), :] = y
        return carry

    lax.fori_loop(0, stride, combine, 0, unroll=SCAN_UNROLL)

    d_rnn = n_slab * LANES
    g = g_ref[...]
    k0 = math.sqrt(2.0 / math.pi)

    def finish(j, carry):
        ro = pl.multiple_of(j * tm_out, 2 * SUBLANES)
        rt = pl.multiple_of(j * tm_out + pre + N_META_TOK, SUBLANES)
        ys = []
        ssq = jnp.zeros((tm_out, 1), F32)
        for s in range(n_slab):
            y = t1[s, pl.ds(rt, tm_out), :]
            xg = xg_ref[0, s, pl.ds(ro, tm_out), :]
            yh = y * (0.5 * xg)
            y = yh + yh * jnp.tanh(xg * (k0 + (k0 * 0.044715) * (xg * xg)))
            ssq = ssq + jnp.sum(y * y, axis=-1, keepdims=True)
            ys.append(y)
        sc = lax.rsqrt(ssq * (1.0 / d_rnn) + NORM_EPS)
        for s in range(n_slab):
            ls = slice(s * LANES, (s + 1) * LANES)
            o_ref[0, pl.ds(ro, tm_out), ls] = (ys[s] * sc * g[:, ls]).astype(BF16)
        return carry

    lax.fori_loop(0, seq // tm_out, finish, 0)


def _rnn(xr, xg, xrm, cw, cb, wg, lam, g):
    b, n_slab, seq, _ = xr.shape
    t_len = N_META_TOK + seq
    stride = _scan_stride(t_len)
    rc = _chunk_len(stride)
    rows_perm = stride * SUBLANES
    rows_time = SUBLANES + rows_perm + SUBLANES
    tm_out = 512 if seq % 512 == 0 else seq
    blk = lambda i: (i, 0, 0, 0)
    in_specs = [
        pl.BlockSpec((1, n_slab, seq, LANES), blk),
        pl.BlockSpec((1, n_slab, seq, LANES), blk),
        _const_spec(xrm.shape),
    ] + [_const_spec(w.shape) for w in (cw, cb, wg, lam, g)]
    scratch = [pltpu.VMEM((n_slab, rows_time, LANES), F32)]
    scratch += [pltpu.VMEM((n_slab, rows_perm, LANES), F32) for _ in range(4)]
    return pl.pallas_call(
        functools.partial(_rnn_kernel, seq=seq, stride=stride, rc=rc, tm_out=tm_out),
        grid=(b,),
        in_specs=in_specs,
        out_specs=pl.BlockSpec((1, seq, n_slab * LANES), lambda i: (i, 0, 0)),
        out_shape=jax.ShapeDtypeStruct((b, seq, n_slab * LANES), BF16),
        scratch_shapes=scratch,
        compiler_params=pltpu.CompilerParams(
            dimension_semantics=("arbitrary",), vmem_limit_bytes=VMEM_LIMIT),
        name="rnn",
    )(xr, xg, xrm, cw, cb, wg, lam, g)


def _attn_kernel(qt_ref, k_ref, vt_ref, km_ref, vtm_ref, o_ref, *, tq):
    n_key = k_ref.shape[1]

    n_stream = 2 * (n_key // tq)
    n_tile = n_key // KEY_TILE
    tiles = [(t * KEY_TILE, (t + 1) * KEY_TILE) for t in range(n_tile)]
    tiles[-1] = (tiles[-1][0], n_key + N_META_TOK)
    steps = [(i, t) for i in range(n_stream) for t in range(n_tile)]

    def score_tile(i, t):
        j, hh = divmod(i, 2)
        k0, k1 = tiles[t]
        ls = slice(hh * LANES, (hh + 1) * LANES)
        keys = k_ref[0, k0:min(k1, n_key), ls]
        if k1 > n_key:
            keys = jnp.concatenate([keys, km_ref[0, :, ls]], axis=0)
        return jnp.dot(keys, qt_ref[0, ls, j * tq:(j + 1) * tq],
                       preferred_element_type=F32)

    outs = []
    pending = [score_tile(*steps[n]) for n in range(SCORE_LOOKAHEAD)]
    m, acc = None, None
    for n, (i, t) in enumerate(steps):
        st = pending.pop(0)
        if n + SCORE_LOOKAHEAD < len(steps):
            pending.append(score_tile(*steps[n + SCORE_LOOKAHEAD]))
        k0, k1 = tiles[t]
        vs = slice((i % 2) * VT_ROWS, (i % 2 + 1) * VT_ROWS)
        st3 = st.reshape((k1 - k0) // SUBLANES, SUBLANES, tq)
        tmax = jnp.max(jnp.max(st3, axis=0), axis=0, keepdims=True)
        m_new = tmax if m is None else jnp.maximum(m, tmax)
        mb = jnp.broadcast_to(m_new, (SUBLANES, tq))
        pb = jnp.exp2(st3 - mb[None]).reshape(k1 - k0, tq).astype(BF16)
        part = jnp.dot(vt_ref[0, vs, k0:k0 + KEY_TILE], pb[:KEY_TILE], preferred_element_type=F32)
        if k1 - k0 > KEY_TILE:
            part = part + jnp.dot(vtm_ref[0, vs, :], pb[KEY_TILE:], preferred_element_type=F32)
        acc = part if m is None else acc * jnp.exp2(m - m_new) + part
        m = m_new
        if t == n_tile - 1:
            outs.append(acc[:HEAD_V] * (1.0 / acc[HEAD_V:HEAD_V + 1]))
            m, acc = None, None
            if i % 2 == 1:
                j = i // 2
                o_ref[0, j * tq:(j + 1) * tq, :] = jnp.concatenate(outs, axis=0).T
                outs = []


def _attention(qt, k, vt, km, vtm, tq):
    b, s, _ = k.shape
    n_pair = N_HEAD // 2
    return pl.pallas_call(
        functools.partial(_attn_kernel, tq=tq),
        grid=(b, n_pair),
        in_specs=[
            pl.BlockSpec((1, 2 * LANES, s), lambda i, p: (i, p, 0)),
            pl.BlockSpec((1, s, 2 * LANES), lambda i, p: (i, 0, p)),
            pl.BlockSpec((1, 2 * VT_ROWS, s), lambda i, p: (i, p, 0)),
            pl.BlockSpec((1, N_META_TOK, 2 * LANES), lambda i, p: (0, 0, p)),
            pl.BlockSpec((1, 2 * VT_ROWS, N_META_TOK), lambda i, p: (0, p, 0)),
        ],
        out_specs=pl.BlockSpec((1, s, 2 * HEAD_V), lambda i, p: (i, 0, p)),
        out_shape=jax.ShapeDtypeStruct((b, s, N_HEAD * HEAD_V), F32),
        compiler_params=pltpu.CompilerParams(
            dimension_semantics=("arbitrary", "arbitrary"), vmem_limit_bytes=VMEM_LIMIT),
        name="attn",
    )(qt, k, vt, km, vtm)


def _out_kernel(x_ref, oa_ref, on_ref, ag_ref, woa_ref, wor_ref, ln2_ref, wg_ref, wu_ref, wd_ref,
                o_ref, *, ff_chunk):
    oa = oa_ref[...]
    oan = (oa * _rsqrt_mean(oa, oa.shape[-1]) * ag_ref[...]).astype(BF16)
    h = x_ref[...] + jnp.dot(oan, woa_ref[...], preferred_element_type=F32)
    h = h + jnp.dot(on_ref[...], wor_ref[...], preferred_element_type=F32)
    hn = (h * _rsqrt_mean(h, h.shape[-1]) * ln2_ref[...]).astype(BF16)
    d_ff = wg_ref.shape[1]
    acc = h
    for c in range(d_ff // ff_chunk):
        cs = slice(c * ff_chunk, (c + 1) * ff_chunk)
        gt = jnp.dot(hn, wg_ref[:, cs], preferred_element_type=F32)
        up = jnp.dot(hn, wu_ref[:, cs], preferred_element_type=F32)
        act = (gt * jax.nn.sigmoid(gt) * up).astype(BF16)
        acc = acc + jnp.dot(act, wd_ref[cs, :], preferred_element_type=F32)
    o_ref[...] = acc


def _out_ffn(x2, oa2, on2, ag, woa, wor, ln2, wg, wu, wd, tm):
    n, d = x2.shape
    d_ff = wg.shape[1]
    ff_chunk = d_ff
    row = lambda i: (i, 0)
    single = pl.Buffered(1)
    wspec = lambda w: pl.BlockSpec(w.shape, lambda i: (0, 0), pipeline_mode=single)
    return pl.pallas_call(
        functools.partial(_out_kernel, ff_chunk=ff_chunk),
        grid=(n // tm,),
        in_specs=[
            pl.BlockSpec((tm, d), row),
            pl.BlockSpec((tm, oa2.shape[1]), row),
            pl.BlockSpec((tm, on2.shape[1]), row),
            wspec(ag), wspec(woa), wspec(wor), wspec(ln2), wspec(wg), wspec(wu), wspec(wd),
        ],
        out_specs=pl.BlockSpec((tm, d), row),
        out_shape=jax.ShapeDtypeStruct((n, d), F32),
        compiler_params=pltpu.CompilerParams(
            dimension_semantics=("arbitrary",), vmem_limit_bytes=VMEM_LIMIT),
        name="out_ffn",
    )(x2, oa2, on2, ag, woa, wor, ln2, wg, wu, wd)


def _rope_tables(t_len):
    half = ROPE // 2
    freqs = 1.0 / (THETA ** (np.arange(half, dtype=np.float64) / half))
    ang = np.arange(t_len, dtype=np.float64)[:, None] * freqs[None, :]
    pad = LANES - HEAD_QK
    cos_t = np.concatenate([np.ones((t_len, NOPE)), np.cos(ang), np.cos(ang), np.zeros((t_len, pad))], -1)
    sin_t = np.concatenate([np.zeros((t_len, NOPE)), np.sin(ang), np.sin(ang), np.zeros((t_len, pad))], -1)
    return jnp.asarray(cos_t, F32), jnp.asarray(sin_t, F32)


def _rot_half_cols(w):
    half = ROPE // 2
    return jnp.concatenate([-w[..., half:], w[..., :half]], axis=-1)


def _gain_rows(g):
    half = ROPE // 2
    pad = jnp.zeros((LANES - HEAD_QK,), F32)
    r0 = jnp.concatenate([g, pad])
    r1 = jnp.concatenate([jnp.zeros((NOPE,), F32), g[NOPE + half:], g[NOPE:NOPE + half], pad])
    return jnp.stack([r0, r1])


def _block_diag_pairs(w):
    g, n, _ = w.shape
    z = jnp.zeros((g // 2, n, n), w.dtype)
    top = jnp.concatenate([w[0::2], z], axis=2)
    bot = jnp.concatenate([z, w[1::2]], axis=2)
    return jnp.concatenate([top, bot], axis=1)


def kernel(x, meta_tokens, ln1_g, w_in, q_a_norm_g, w_uq, kv_a_norm_g, w_ukv, q_norm_g, k_norm_g,
           conv_w, conv_b, lru_wa, lru_ba, lru_wi, lru_bi, lru_lambda, attn_out_g, rnn_out_g,
           w_out, ln2_g, w_gate, w_up, w_down):
    bsz, seq, d_model = x.shape
    q_lora = q_a_norm_g.shape[-1]
    kv_lora = kv_a_norm_g.shape[-1]
    d_rnn = conv_w.shape[-1]
    d_attn = N_HEAD * HEAD_V
    t_len = N_META_TOK + seq
    l = 0

    wi = w_in[l]
    o_kr = q_lora + kv_lora
    w_kr = wi[:, o_kr:o_kr + ROPE]
    win = jnp.concatenate(
        [wi[:, :o_kr], jnp.zeros((d_model, NOPE), F32), w_kr, _rot_half_cols(w_kr),
         wi[:, o_kr + ROPE:]], axis=1).astype(BF16)
    wq = w_uq[l].reshape(q_lora, N_HEAD, HEAD_QK)
    wuq = jnp.concatenate([wq, _rot_half_cols(wq[..., NOPE:])], axis=-1)
    wuq = wuq.reshape(q_lora, N_HEAD * LANES).astype(BF16)
    wkv = w_ukv[l].reshape(kv_lora, N_HEAD, NOPE + HEAD_V)
    wuk = jnp.concatenate([wkv[..., :NOPE], jnp.zeros((kv_lora, N_HEAD, LANES - NOPE), F32)], -1)
    wuk = wuk.reshape(kv_lora, N_HEAD * LANES).astype(BF16)
    wuv = wkv[..., NOPE:].reshape(kv_lora, d_attn).astype(BF16)
    gq = _gain_rows(q_norm_g[l])
    gk = _gain_rows(k_norm_g[l])
    cos_t, sin_t = _rope_tables(t_len)
    lane2 = np.arange(2 * LANES)
    hsum = jnp.asarray((lane2[:, None] // LANES == lane2[None, :] // LANES)
                       & (lane2[:, None] % LANES < HEAD_QK), BF16)
    proj_w = (ln1_g[l][None], win, q_a_norm_g[l][None], wuq, kv_a_norm_g[l][None], wuk, wuv, gq, gk, hsum)
    dims = (q_lora, kv_lora, d_rnn)

    wgate = (0.5 * jnp.concatenate(
        [_block_diag_pairs(w) for w in (lru_wa[l, 0], lru_wi[l, 0], lru_wa[l, 1], lru_wi[l, 1])],
        axis=2)).astype(BF16)
    n_slab = d_rnn // LANES
    gate_b = 0.5 * jnp.stack([lru_ba[l, 0], lru_bi[l, 0], lru_ba[l, 1], lru_bi[l, 1]])
    gate_b = gate_b.reshape(4, n_slab, LANES).transpose(1, 0, 2).reshape(n_slab, 1, 4 * LANES)
    b_hi = gate_b.astype(BF16)
    b_lo = (gate_b - b_hi.astype(F32)).astype(BF16)
    wgate = jnp.concatenate(
        [wgate, b_hi, b_lo, jnp.zeros((n_slab, LANES - 2, 4 * LANES), BF16)], axis=1)

    tm = 512 if seq % 512 == 0 else seq
    q, k, v, xr, xg = _project(x, tm, cos_t[N_META_TOK:], sin_t[N_META_TOK:], proj_w, dims)
    _, km, vm, xrm, _ = _project(meta_tokens[None].astype(x.dtype), N_META_TOK,
                                 cos_t[:N_META_TOK], sin_t[:N_META_TOK], proj_w, dims)

    o_rnn = _rnn(xr, xg, xrm, conv_w[l], conv_b[l][None], wgate, lru_lambda[l], rnn_out_g[l][None])

    tq = 1024 if seq % 1024 == 0 else seq
    o_attn = _attention(q, k, v, km, vm, tq)

    wo = w_out[l].astype(BF16)
    n_rows = bsz * seq
    tmo = 512 if n_rows % 512 == 0 else n_rows
    out = _out_ffn(x.reshape(n_rows, d_model), o_attn.reshape(n_rows, d_attn),
                   o_rnn.reshape(n_rows, d_rnn), attn_out_g[l][None], wo[:d_attn], wo[d_attn:],
                   ln2_g[l][None], w_gate[l].astype(BF16), w_up[l].astype(BF16),
                   w_down[l].astype(BF16), tmo)
    return out.reshape(bsz, seq, d_model)
```

```python
import functools
import math

import numpy as np

import jax
import jax.numpy as jnp
from jax import lax
from jax.experimental import pallas as pl
from jax.experimental.pallas import tpu as pltpu

F32 = jnp.float32
BF16 = jnp.bfloat16

N_META_TOK = 16
N_HEAD = 8
NOPE = 64
ROPE = 32
HEAD_QK = NOPE + ROPE
HEAD_V = 64
VT_ROWS = HEAD_V + 16
LRU_SCALE = 8.0
THETA = 10000.0
NORM_EPS = 1e-6
CONV_TAPS = 4
CONV_LEFT = 2

LANES = 128
SUBLANES = 8
VMEM_LIMIT = 56 * 1024 * 1024
KEY_TILE = 256
SCAN_UNROLL = 6
COMBINE_UNROLL = 43
HEADS_PER_STEP = 4
SCORE_LOOKAHEAD = 2

LOG2E = 1.4426950408889634


def _rsqrt_mean(x, n):
    return lax.rsqrt(jnp.sum(x * x, axis=-1, keepdims=True) * (1.0 / n) + NORM_EPS)


def _proj_kernel(x_ref, ln1_ref, win_ref, qag_ref, wuq_ref, kvag_ref, wuk_ref, wuv_ref,
                 cos_ref, sin_ref, gq_ref, gk_ref, hsum_ref,
                 q_out, k_out, v_out, xr_out, xg_out, *, q_lora, kv_lora, d_rnn):
    x = x_ref[0]
    d_model = x.shape[-1]
    hn = (x * _rsqrt_mean(x, d_model) * ln1_ref[...]).astype(BF16)
    o_kv = q_lora
    o_kr = o_kv + kv_lora
    o_xr = o_kr + LANES
    p = jnp.dot(hn, win_ref[:, :o_xr], preferred_element_type=F32)
    c_q = p[:, :o_kv]
    c_kv = p[:, o_kv:o_kr]
    kr = p[:, o_kr:o_xr]
    n_slab = d_rnn // LANES

    def recurrent_inputs(c):
        px = jnp.dot(hn, win_ref[:, o_xr + 2 * c * LANES:o_xr + 2 * (c + 1) * LANES],
                     preferred_element_type=F32)
        for u in range(2):
            slab = 2 * c + u
            dst = xr_out if slab < n_slab else xg_out
            dst[0, slab % n_slab] = px[:, u * LANES:(u + 1) * LANES]

    cos_t = cos_ref[...]
    sin_t = sin_ref[...]
    inv_hd = 1.0 / HEAD_QK
    hsum = hsum_ref[...]

    def head_sums(sq_pair):
        return jnp.dot(sq_pair.astype(BF16), hsum, preferred_element_type=F32)

    cqn = (c_q * _rsqrt_mean(c_q, q_lora) * qag_ref[...]).astype(BF16)
    q = jnp.dot(cqn, wuq_ref[...], preferred_element_type=F32)
    cq = cos_t * gq_ref[0:1, :]
    sq = sin_t * gq_ref[1:2, :]
    q_scale = (HEAD_QK ** -0.5) * LOG2E

    ckn = (c_kv * _rsqrt_mean(c_kv, kv_lora) * kvag_ref[...]).astype(BF16)
    kn = jnp.dot(ckn, wuk_ref[...], preferred_element_type=F32)
    vt = jnp.dot(ckn, wuv_ref[...], preferred_element_type=F32).T
    tm = vt.shape[1]
    ones_rows = (lax.broadcasted_iota(jnp.int32, (VT_ROWS - HEAD_V, tm), 0) == 0).astype(BF16)
    for h in range(N_HEAD):
        v_out[0, h * VT_ROWS:h * VT_ROWS + HEAD_V, :] = vt[h * HEAD_V:(h + 1) * HEAD_V].astype(BF16)
        v_out[0, h * VT_ROWS + HEAD_V:(h + 1) * VT_ROWS, :] = ones_rows
    ck = cos_t * gk_ref[0:1, :]
    sk = sin_t * gk_ref[1:2, :]
    kr_sq = kr * kr
    kr_sq2 = jnp.concatenate([kr_sq, kr_sq], axis=1)
    kr_rot = kr * ck + pltpu.roll(kr, LANES - ROPE, axis=1) * sk
    n_pair = N_HEAD // 2
    n_rec = n_slab
    for pr in range(n_pair):
        for c in range(pr * n_rec // n_pair, (pr + 1) * n_rec // n_pair):
            recurrent_inputs(c)
        qp = q[:, 2 * pr * LANES:2 * (pr + 1) * LANES]
        sc = lax.rsqrt(head_sums(qp * qp) * inv_hd + NORM_EPS) * q_scale
        for hh in range(2):
            h = 2 * pr + hh
            qh = qp[:, hh * LANES:(hh + 1) * LANES]
            qr = pltpu.roll(qh, LANES - ROPE, axis=1)
            qf = (qh * cq + qr * sq) * sc[:, hh * LANES:(hh + 1) * LANES]
            q_out[0, h * LANES:(h + 1) * LANES, :] = qf.T.astype(BF16)
        kp = kn[:, 2 * pr * LANES:2 * (pr + 1) * LANES]
        sc = lax.rsqrt(head_sums(kp * kp + kr_sq2) * inv_hd + NORM_EPS)
        for hh in range(2):
            h = 2 * pr + hh
            kh = kp[:, hh * LANES:(hh + 1) * LANES]
            k_out[0, :, h * LANES:(h + 1) * LANES] = (
                (kh * ck + kr_rot) * sc[:, hh * LANES:(hh + 1) * LANES]).astype(BF16)


def _const_spec(shape):
    nd = len(shape)
    return pl.BlockSpec(shape, lambda *_: (0,) * nd)


def _project(x3, tm, cos_t, sin_t, wts, dims):
    q_lora, kv_lora, d_rnn = dims
    b, s, d = x3.shape
    n_slab = d_rnn // LANES
    nt = s // tm
    ln1, win, qag, wuq, kvag, wuk, wuv, gq, gk, hsum = wts
    row3 = lambda i, j: (i, j, 0)
    row4 = lambda i, j: (i, 0, j, 0)
    tab = lambda i, j: (j, 0)
    in_specs = [pl.BlockSpec((1, tm, d), row3)]
    in_specs += [_const_spec(w.shape) for w in (ln1, win, qag, wuq, kvag, wuk, wuv)]
    in_specs += [pl.BlockSpec((tm, LANES), tab), pl.BlockSpec((tm, LANES), tab)]
    in_specs += [_const_spec(gq.shape), _const_spec(gk.shape), _const_spec(hsum.shape)]
    col3 = lambda i, j: (i, 0, j)
    out_shape = (
        jax.ShapeDtypeStruct((b, N_HEAD * LANES, s), BF16),
        jax.ShapeDtypeStruct((b, s, N_HEAD * LANES), BF16),
        jax.ShapeDtypeStruct((b, N_HEAD * VT_ROWS, s), BF16),
        jax.ShapeDtypeStruct((b, n_slab, s, LANES), F32),
        jax.ShapeDtypeStruct((b, n_slab, s, LANES), F32),
    )
    out_specs = (
        pl.BlockSpec((1, N_HEAD * LANES, tm), col3),
        pl.BlockSpec((1, tm, N_HEAD * LANES), row3),
        pl.BlockSpec((1, N_HEAD * VT_ROWS, tm), col3),
        pl.BlockSpec((1, n_slab, tm, LANES), row4),
        pl.BlockSpec((1, n_slab, tm, LANES), row4),
    )
    return pl.pallas_call(
        functools.partial(_proj_kernel, q_lora=q_lora, kv_lora=kv_lora, d_rnn=d_rnn),
        grid=(b, nt),
        in_specs=in_specs,
        out_specs=out_specs,
        out_shape=out_shape,
        compiler_params=pltpu.CompilerParams(
            dimension_semantics=("arbitrary", "arbitrary"), vmem_limit_bytes=VMEM_LIMIT),
        name="proj",
    )(x3, ln1, win, qag, wuq, kvag, wuk, wuv, cos_t, sin_t, gq, gk, hsum)


def _scan_stride(t_len):
    s = -(-t_len // SUBLANES)
    while s % SUBLANES == 0:
        s += 1
    if s % 2:
        s += 1
        if s % SUBLANES == 0:
            s += 2
    return s


def _chunk_len(stride, cap=258):
    for c in range(min(cap, stride), 0, -1):
        if stride % c == 0:
            return c
    return 1


def _rnn_kernel(xr_ref, xg_ref, xrm_ref, cw_ref, cb_ref, wg_ref, lam_ref, g_ref,
                o_ref, t1, af, bf, ab, bb, *, seq, stride, rc, tm_out):
    n_slab = xr_ref.shape[1]
    t_len = N_META_TOK + seq
    pre = SUBLANES
    rows = t1.shape[1]
    n_chunk = stride // rc
    crow = rc * SUBLANES

    for s in range(n_slab):
        t1[s, 0:pre, :] = jnp.zeros((pre, LANES), F32)
        t1[s, pre:pre + N_META_TOK, :] = xrm_ref[0, s]
        t1[s, pre + N_META_TOK:pre + t_len, :] = xr_ref[0, s]
        t1[s, pre + t_len:rows, :] = jnp.zeros((rows - pre - t_len, LANES), F32)

    lam = lam_ref[...]
    nl = -lam
    softplus = jnp.maximum(nl, 0.0) + jnp.log(1.0 + jnp.exp(-jnp.abs(nl)))
    hcn = (0.5 * LRU_SCALE) * softplus
    cw = cw_ref[...]
    cb = cb_ref[...]
    ones_k = jnp.ones((crow, LANES), BF16)

    padded = stride * SUBLANES > t_len
    if padded:
        ridx = lax.broadcasted_iota(jnp.int32, (crow, LANES), 0)
        t_local = (ridx & (SUBLANES - 1)) * stride + (ridx >> 3)

    def gate_chunk(c, carry):
        r0 = c * rc
        row0 = pl.multiple_of(c * crow, SUBLANES)
        for s in range(n_slab):
            ls = slice(s * LANES, (s + 1) * LANES)
            taps = [t1[s, pl.ds(r0 + pre - CONV_LEFT + j, SUBLANES, stride=stride), :]
                    for j in range(rc + CONV_TAPS - 1)]
            xc_rows = []
            for rr in range(rc):
                acc = taps[rr] * cw[0:1, ls]
                for j in range(1, CONV_TAPS):
                    acc = acc + taps[rr + j] * cw[j:j + 1, ls]
                xc_rows.append(acc + cb[:, ls])
            xc = jnp.concatenate(xc_rows, axis=0)
            hx = 0.5 * xc
            z = jnp.dot(jnp.concatenate([xc.astype(BF16), ones_k], axis=1), wg_ref[s],
                        preferred_element_type=F32)
            for d, (a_ref, b_ref) in enumerate(((af, bf), (ab, bb))):
                ta = jnp.tanh(z[:, (2 * d) * LANES:(2 * d + 1) * LANES])
                ti = jnp.tanh(z[:, (2 * d + 1) * LANES:(2 * d + 2) * LANES])
                hc = hcn[d:d + 1, ls]
                nla = ta * hc + hc
                a = jnp.exp2(nla * (-LOG2E))
                one_m_a2 = jnp.tanh(nla) * (1.0 + a * a)
                root = jnp.where(one_m_a2 > 0.0, one_m_a2 * lax.rsqrt(one_m_a2), 0.0)
                bv = root * (ti * hx + hx)
                if d == 1 and padded:
                    bv = jnp.where((t_local + r0) < t_len, bv, 0.0)
                a_ref[s, pl.ds(row0, crow), :] = a
                b_ref[s, pl.ds(row0, crow), :] = bv
        return carry

    lax.fori_loop(0, n_chunk, gate_chunk, 0)

    zero = jnp.zeros((SUBLANES, LANES), F32)
    one = jnp.ones((SUBLANES, LANES), F32)

    def scan_step(r, carry):
        hf, pf, hb, pb = carry
        rf = pl.multiple_of(r * SUBLANES, SUBLANES)
        rb = pl.multiple_of((stride - 1 - r) * SUBLANES, SUBLANES)
        nhf, npf, nhb, npb = [], [], [], []
        for s in range(n_slab):
            a = af[s, pl.ds(rf, SUBLANES), :]
            h = a * hf[s] + bf[s, pl.ds(rf, SUBLANES), :]
            pr = a * pf[s]
            bf[s, pl.ds(rf, SUBLANES), :] = h
            af[s, pl.ds(rf, SUBLANES), :] = pr
            nhf.append(h)
            npf.append(pr)
            a = ab[s, pl.ds(rb, SUBLANES), :]
            h = a * hb[s] + bb[s, pl.ds(rb, SUBLANES), :]
            pr = a * pb[s]
            bb[s, pl.ds(rb, SUBLANES), :] = h
            ab[s, pl.ds(rb, SUBLANES), :] = pr
            nhb.append(h)
            npb.append(pr)
        return tuple(nhf), tuple(npf), tuple(nhb), tuple(npb)

    init = ((zero,) * n_slab, (one,) * n_slab, (zero,) * n_slab, (one,) * n_slab)
    hf, pf, hb, pb = lax.fori_loop(0, stride, scan_step, init, unroll=SCAN_UNROLL)

    sub = lax.broadcasted_iota(jnp.int32, (SUBLANES, LANES), 0)
    cf, cbk = [], []
    for s in range(n_slab):
        c = zero
        for _ in range(SUBLANES - 1):
            c = jnp.where(sub == 0, 0.0, pltpu.roll(pf[s] * c + hf[s], 1, axis=0))
        cf.append(c)
        c = zero
        for _ in range(SUBLANES - 1):
            c = jnp.where(sub == SUBLANES - 1, 0.0, pltpu.roll(pb[s] * c + hb[s], SUBLANES - 1, axis=0))
        cbk.append(c)

    def combine(r, carry):
        rp = pl.multiple_of(r * SUBLANES, SUBLANES)
        for s in range(n_slab):
            y = (bf[s, pl.ds(rp, SUBLANES), :] + af[s, pl.ds(rp, SUBLANES), :] * cf[s]
                 + bb[s, pl.ds(rp, SUBLANES), :] + ab[s, pl.ds(rp, SUBLANES), :] * cbk[s])
            t1[s, pl.ds(r + pre, SUBLANES, stride=stride), :] = y
        return carry

    lax.fori_loop(0, stride, combine, 0, unroll=COMBINE_UNROLL)

    d_rnn = n_slab * LANES
    g = g_ref[...]
    k0 = math.sqrt(2.0 / math.pi)

    def finish(j, carry):
        ro = pl.multiple_of(j * tm_out, 2 * SUBLANES)
        rt = pl.multiple_of(j * tm_out + pre + N_META_TOK, SUBLANES)
        ys = []
        ssq = jnp.zeros((tm_out, 1), F32)
        for s in range(n_slab):
            y = t1[s, pl.ds(rt, tm_out), :]
            xg = xg_ref[0, s, pl.ds(ro, tm_out), :]
            yh = y * (0.5 * xg)
            y = yh + yh * jnp.tanh(xg * (k0 + (k0 * 0.044715) * (xg * xg)))
            ssq = ssq + jnp.sum(y * y, axis=-1, keepdims=True)
            ys.append(y)
        sc = lax.rsqrt(ssq * (1.0 / d_rnn) + NORM_EPS)
        for s in range(n_slab):
            ls = slice(s * LANES, (s + 1) * LANES)
            o_ref[0, pl.ds(ro, tm_out), ls] = (ys[s] * sc * g[:, ls]).astype(BF16)
        return carry

    lax.fori_loop(0, seq // tm_out, finish, 0)


def _rnn(xr, xg, xrm, cw, cb, wg, lam, g):
    b, n_slab, seq, _ = xr.shape
    t_len = N_META_TOK + seq
    stride = _scan_stride(t_len)
    rc = _chunk_len(stride)
    rows_perm = stride * SUBLANES
    rows_time = SUBLANES + rows_perm + SUBLANES
    tm_out = 512 if seq % 512 == 0 else seq
    blk = lambda i: (i, 0, 0, 0)
    in_specs = [
        pl.BlockSpec((1, n_slab, seq, LANES), blk),
        pl.BlockSpec((1, n_slab, seq, LANES), blk),
        _const_spec(xrm.shape),
    ] + [_const_spec(w.shape) for w in (cw, cb, wg, lam, g)]
    scratch = [pltpu.VMEM((n_slab, rows_time, LANES), F32)]
    scratch += [pltpu.VMEM((n_slab, rows_perm, LANES), F32) for _ in range(4)]
    return pl.pallas_call(
        functools.partial(_rnn_kernel, seq=seq, stride=stride, rc=rc, tm_out=tm_out),
        grid=(b,),
        in_specs=in_specs,
        out_specs=pl.BlockSpec((1, seq, n_slab * LANES), lambda i: (i, 0, 0)),
        out_shape=jax.ShapeDtypeStruct((b, seq, n_slab * LANES), BF16),
        scratch_shapes=scratch,
        compiler_params=pltpu.CompilerParams(
            dimension_semantics=("arbitrary",), vmem_limit_bytes=VMEM_LIMIT),
        name="rnn",
    )(xr, xg, xrm, cw, cb, wg, lam, g)


def _attn_kernel(qt_ref, k_ref, vt_ref, km_ref, vtm_ref, o_ref, *, tq):
    n_key = k_ref.shape[1]

    n_stream = HEADS_PER_STEP * (n_key // tq)
    n_tile = n_key // KEY_TILE
    tiles = [(t * KEY_TILE, (t + 1) * KEY_TILE) for t in range(n_tile)]
    tiles[-1] = (tiles[-1][0], n_key + N_META_TOK)
    steps = [(i, t) for i in range(n_stream) for t in range(n_tile)]

    def score_tile(i, t):
        j, hh = divmod(i, HEADS_PER_STEP)
        k0, k1 = tiles[t]
        ls = slice(hh * LANES, (hh + 1) * LANES)
        keys = k_ref[0, k0:min(k1, n_key), ls]
        if k1 > n_key:
            keys = jnp.concatenate([keys, km_ref[0, :, ls]], axis=0)
        return jnp.dot(keys, qt_ref[0, ls, j * tq:(j + 1) * tq],
                       preferred_element_type=F32)

    outs = []
    pending = [score_tile(*steps[n]) for n in range(SCORE_LOOKAHEAD)]
    m, acc = None, None
    for n, (i, t) in enumerate(steps):
        st = pending.pop(0)
        if n + SCORE_LOOKAHEAD < len(steps):
            pending.append(score_tile(*steps[n + SCORE_LOOKAHEAD]))
        k0, k1 = tiles[t]
        hh = i % HEADS_PER_STEP
        vs = slice(hh * VT_ROWS, (hh + 1) * VT_ROWS)
        st3 = st.reshape((k1 - k0) // SUBLANES, SUBLANES, tq)
        tmax = jnp.max(jnp.max(st3, axis=0), axis=0, keepdims=True)
        m_new = tmax if m is None else jnp.maximum(m, tmax)
        mb = jnp.broadcast_to(m_new, (SUBLANES, tq))
        pb = jnp.exp2(st3 - mb[None]).reshape(k1 - k0, tq).astype(BF16)
        part = jnp.dot(vt_ref[0, vs, k0:k0 + KEY_TILE], pb[:KEY_TILE], preferred_element_type=F32)
        if k1 - k0 > KEY_TILE:
            part = part + jnp.dot(vtm_ref[0, vs, :], pb[KEY_TILE:], preferred_element_type=F32)
        acc = part if m is None else acc * jnp.exp2(m - m_new) + part
        m = m_new
        if t == n_tile - 1:
            outs.append(acc[:HEAD_V] * (1.0 / acc[HEAD_V:HEAD_V + 1]))
            m, acc = None, None
            if hh == HEADS_PER_STEP - 1:
                j = i // HEADS_PER_STEP
                o_ref[0, j * tq:(j + 1) * tq, :] = jnp.concatenate(outs, axis=0).T
                outs = []


def _attention(qt, k, vt, km, vtm, tq):
    b, s, _ = k.shape
    hps = HEADS_PER_STEP
    return pl.pallas_call(
        functools.partial(_attn_kernel, tq=tq),
        grid=(b, N_HEAD // hps),
        in_specs=[
            pl.BlockSpec((1, hps * LANES, s), lambda i, p: (i, p, 0)),
            pl.BlockSpec((1, s, hps * LANES), lambda i, p: (i, 0, p)),
            pl.BlockSpec((1, hps * VT_ROWS, s), lambda i, p: (i, p, 0)),
            pl.BlockSpec((1, N_META_TOK, hps * LANES), lambda i, p: (0, 0, p)),
            pl.BlockSpec((1, hps * VT_ROWS, N_META_TOK), lambda i, p: (0, p, 0)),
        ],
        out_specs=pl.BlockSpec((1, s, hps * HEAD_V), lambda i, p: (i, 0, p)),
        out_shape=jax.ShapeDtypeStruct((b, s, N_HEAD * HEAD_V), F32),
        compiler_params=pltpu.CompilerParams(
            dimension_semantics=("arbitrary", "arbitrary"), vmem_limit_bytes=VMEM_LIMIT),
        name="attn",
    )(qt, k, vt, km, vtm)


def _out_kernel(x_ref, oa_ref, on_ref, ag_ref, woa_ref, wor_ref, ln2_ref, wg_ref, wu_ref, wd_ref,
                o_ref, *, ff_chunk):
    oa = oa_ref[...]
    oan = (oa * _rsqrt_mean(oa, oa.shape[-1]) * ag_ref[...]).astype(BF16)
    h = x_ref[...] + jnp.dot(oan, woa_ref[...], preferred_element_type=F32)
    h = h + jnp.dot(on_ref[...], wor_ref[...], preferred_element_type=F32)
    hn = (h * _rsqrt_mean(h, h.shape[-1]) * ln2_ref[...]).astype(BF16)
    d_ff = wg_ref.shape[1]
    acc = h
    for c in range(d_ff // ff_chunk):
        cs = slice(c * ff_chunk, (c + 1) * ff_chunk)
        gt = jnp.dot(hn, wg_ref[:, cs], preferred_element_type=F32)
        up = jnp.dot(hn, wu_ref[:, cs], preferred_element_type=F32)
        act = (gt * jax.nn.sigmoid(gt) * up).astype(BF16)
        acc = acc + jnp.dot(act, wd_ref[cs, :], preferred_element_type=F32)
    o_ref[...] = acc


def _out_ffn(x2, oa2, on2, ag, woa, wor, ln2, wg, wu, wd, tm):
    n, d = x2.shape
    d_ff = wg.shape[1]
    ff_chunk = d_ff
    row = lambda i: (i, 0)
    single = pl.Buffered(1)
    wspec = lambda w: pl.BlockSpec(w.shape, lambda i: (0, 0), pipeline_mode=single)
    return pl.pallas_call(
        functools.partial(_out_kernel, ff_chunk=ff_chunk),
        grid=(n // tm,),
        in_specs=[
            pl.BlockSpec((tm, d), row),
            pl.BlockSpec((tm, oa2.shape[1]), row),
            pl.BlockSpec((tm, on2.shape[1]), row),
            wspec(ag), wspec(woa), wspec(wor), wspec(ln2), wspec(wg), wspec(wu), wspec(wd),
        ],
        out_specs=pl.BlockSpec((tm, d), row),
        out_shape=jax.ShapeDtypeStruct((n, d), F32),
        compiler_params=pltpu.CompilerParams(
            dimension_semantics=("arbitrary",), vmem_limit_bytes=VMEM_LIMIT),
        name="out_ffn",
    )(x2, oa2, on2, ag, woa, wor, ln2, wg, wu, wd)


def _rope_tables(t_len):
    half = ROPE // 2
    freqs = 1.0 / (THETA ** (np.arange(half, dtype=np.float64) / half))
    ang = np.arange(t_len, dtype=np.float64)[:, None] * freqs[None, :]
    pad = LANES - HEAD_QK
    cos_t = np.concatenate([np.ones((t_len, NOPE)), np.cos(ang), np.cos(ang), np.zeros((t_len, pad))], -1)
    sin_t = np.concatenate([np.zeros((t_len, NOPE)), np.sin(ang), np.sin(ang), np.zeros((t_len, pad))], -1)
    return jnp.asarray(cos_t, F32), jnp.asarray(sin_t, F32)


def _rot_half_cols(w):
    half = ROPE // 2
    return jnp.concatenate([-w[..., half:], w[..., :half]], axis=-1)


def _gain_rows(g):
    half = ROPE // 2
    pad = jnp.zeros((LANES - HEAD_QK,), F32)
    r0 = jnp.concatenate([g, pad])
    r1 = jnp.concatenate([jnp.zeros((NOPE,), F32), g[NOPE + half:], g[NOPE:NOPE + half], pad])
    return jnp.stack([r0, r1])


def _block_diag_pairs(w):
    g, n, _ = w.shape
    z = jnp.zeros((g // 2, n, n), w.dtype)
    top = jnp.concatenate([w[0::2], z], axis=2)
    bot = jnp.concatenate([z, w[1::2]], axis=2)
    return jnp.concatenate([top, bot], axis=1)


def kernel(x, meta_tokens, ln1_g, w_in, q_a_norm_g, w_uq, kv_a_norm_g, w_ukv, q_norm_g, k_norm_g,
           conv_w, conv_b, lru_wa, lru_ba, lru_wi, lru_bi, lru_lambda, attn_out_g, rnn_out_g,
           w_out, ln2_g, w_gate, w_up, w_down):
    bsz, seq, d_model = x.shape
    q_lora = q_a_norm_g.shape[-1]
    kv_lora = kv_a_norm_g.shape[-1]
    d_rnn = conv_w.shape[-1]
    d_attn = N_HEAD * HEAD_V
    t_len = N_META_TOK + seq
    l = 0

    wi = w_in[l]
    o_kr = q_lora + kv_lora
    w_kr = wi[:, o_kr:o_kr + ROPE]
    win = jnp.concatenate(
        [wi[:, :o_kr], jnp.zeros((d_model, NOPE), F32), w_kr, _rot_half_cols(w_kr),
         wi[:, o_kr + ROPE:]], axis=1).astype(BF16)
    wq = w_uq[l].reshape(q_lora, N_HEAD, HEAD_QK)
    wuq = jnp.concatenate([wq, _rot_half_cols(wq[..., NOPE:])], axis=-1)
    wuq = wuq.reshape(q_lora, N_HEAD * LANES).astype(BF16)
    wkv = w_ukv[l].reshape(kv_lora, N_HEAD, NOPE + HEAD_V)
    wuk = jnp.concatenate([wkv[..., :NOPE], jnp.zeros((kv_lora, N_HEAD, LANES - NOPE), F32)], -1)
    wuk = wuk.reshape(kv_lora, N_HEAD * LANES).astype(BF16)
    wuv = wkv[..., NOPE:].reshape(kv_lora, d_attn).astype(BF16)
    gq = _gain_rows(q_norm_g[l])
    gk = _gain_rows(k_norm_g[l])
    cos_t, sin_t = _rope_tables(t_len)
    lane2 = np.arange(2 * LANES)
    hsum = jnp.asarray((lane2[:, None] // LANES == lane2[None, :] // LANES)
                       & (lane2[:, None] % LANES < HEAD_QK), BF16)
    proj_w = (ln1_g[l][None], win, q_a_norm_g[l][None], wuq, kv_a_norm_g[l][None], wuk, wuv, gq, gk, hsum)
    dims = (q_lora, kv_lora, d_rnn)

    wgate = (0.5 * jnp.concatenate(
        [_block_diag_pairs(w) for w in (lru_wa[l, 0], lru_wi[l, 0], lru_wa[l, 1], lru_wi[l, 1])],
        axis=2)).astype(BF16)
    n_slab = d_rnn // LANES
    gate_b = 0.5 * jnp.stack([lru_ba[l, 0], lru_bi[l, 0], lru_ba[l, 1], lru_bi[l, 1]])
    gate_b = gate_b.reshape(4, n_slab, LANES).transpose(1, 0, 2).reshape(n_slab, 1, 4 * LANES)
    b_hi = gate_b.astype(BF16)
    b_lo = (gate_b - b_hi.astype(F32)).astype(BF16)
    wgate = jnp.concatenate(
        [wgate, b_hi, b_lo, jnp.zeros((n_slab, LANES - 2, 4 * LANES), BF16)], axis=1)

    tm = 512 if seq % 512 == 0 else seq
    q, k, v, xr, xg = _project(x, tm, cos_t[N_META_TOK:], sin_t[N_META_TOK:], proj_w, dims)
    _, km, vm, xrm, _ = _project(meta_tokens[None].astype(x.dtype), N_META_TOK,
                                 cos_t[:N_META_TOK], sin_t[:N_META_TOK], proj_w, dims)

    o_rnn = _rnn(xr, xg, xrm, conv_w[l], conv_b[l][None], wgate, lru_lambda[l], rnn_out_g[l][None])

    tq = 1024 if seq % 1024 == 0 else seq
    o_attn = _attention(q, k, v, km, vm, tq)

    wo = w_out[l].astype(BF16)
    n_rows = bsz * seq
    tmo = 512 if n_rows % 512 == 0 else n_rows
    out = _out_ffn(x.reshape(n_rows, d_model), o_attn.reshape(n_rows, d_attn),
                   o_rnn.reshape(n_rows, d_rnn), attn_out_g[l][None], wo[:d_attn], wo[d_attn:],
                   ln2_g[l][None], w_gate[l].astype(BF16), w_up[l].astype(BF16),
                   w_down[l].astype(BF16), tmo)
    return out.reshape(bsz, seq, d_model)
```

```python
import functools
import math

import numpy as np

import jax
import jax.numpy as jnp
from jax import lax
from jax.experimental import pallas as pl
from jax.experimental.pallas import tpu as pltpu

F32 = jnp.float32
BF16 = jnp.bfloat16

N_META_TOK = 16
N_HEAD = 8
NOPE = 64
ROPE = 32
HEAD_QK = NOPE + ROPE
HEAD_V = 64
VT_ROWS = HEAD_V + 16
LRU_SCALE = 8.0
THETA = 10000.0
NORM_EPS = 1e-6
CONV_TAPS = 4
CONV_LEFT = 2

LANES = 128
SUBLANES = 8
VMEM_LIMIT = 56 * 1024 * 1024
KEY_TILE = 256
SCAN_UNROLL = 6
COMBINE_UNROLL = 43
HEADS_PER_STEP = 4
SCORE_LOOKAHEAD = 2

LOG2E = 1.4426950408889634


def _rsqrt_mean(x, n):
    return lax.rsqrt(jnp.sum(x * x, axis=-1, keepdims=True) * (1.0 / n) + NORM_EPS)


def _proj_kernel(x_ref, ln1_ref, win_ref, qag_ref, wuq_ref, kvag_ref, wuk_ref, wuv_ref,
                 cos_ref, sin_ref, gq_ref, gk_ref, hsum_ref,
                 q_out, k_out, v_out, xr_out, xg_out, *, q_lora, kv_lora, d_rnn):
    x = x_ref[0]
    d_model = x.shape[-1]
    hn = (x * _rsqrt_mean(x, d_model) * ln1_ref[...]).astype(BF16)
    o_kv = q_lora
    o_kr = o_kv + kv_lora
    o_xr = o_kr + LANES
    p = jnp.dot(hn, win_ref[:, :o_xr], preferred_element_type=F32)
    c_q = p[:, :o_kv]
    c_kv = p[:, o_kv:o_kr]
    kr = p[:, o_kr:o_xr]
    n_slab = d_rnn // LANES

    def recurrent_inputs(c):
        px = jnp.dot(hn, win_ref[:, o_xr + 2 * c * LANES:o_xr + 2 * (c + 1) * LANES],
                     preferred_element_type=F32)
        for u in range(2):
            slab = 2 * c + u
            dst = xr_out if slab < n_slab else xg_out
            dst[0, slab % n_slab] = px[:, u * LANES:(u + 1) * LANES]

    cos_t = cos_ref[...]
    sin_t = sin_ref[...]
    eps_hd = HEAD_QK * NORM_EPS
    root_hd = math.sqrt(HEAD_QK)
    hsum = hsum_ref[...]

    def head_sums(sq_pair):
        return jnp.dot(sq_pair.astype(BF16), hsum, preferred_element_type=F32)

    cqn = (c_q * _rsqrt_mean(c_q, q_lora) * qag_ref[...]).astype(BF16)
    q = jnp.dot(cqn, wuq_ref[...], preferred_element_type=F32)
    q_gain = root_hd * (HEAD_QK ** -0.5) * LOG2E
    cq = cos_t * (gq_ref[0:1, :] * q_gain)
    sq = sin_t * (gq_ref[1:2, :] * q_gain)

    ckn = (c_kv * _rsqrt_mean(c_kv, kv_lora) * kvag_ref[...]).astype(BF16)
    kn = jnp.dot(ckn, wuk_ref[...], preferred_element_type=F32)
    vt = jnp.dot(ckn, wuv_ref[...], preferred_element_type=F32).astype(BF16).T
    tm = vt.shape[1]
    ones_rows = (lax.broadcasted_iota(jnp.int32, (VT_ROWS - HEAD_V, tm), 0) == 0).astype(BF16)
    for h in range(N_HEAD):
        v_out[0, h * VT_ROWS:h * VT_ROWS + HEAD_V, :] = vt[h * HEAD_V:(h + 1) * HEAD_V]
        v_out[0, h * VT_ROWS + HEAD_V:(h + 1) * VT_ROWS, :] = ones_rows
    ck = cos_t * (gk_ref[0:1, :] * root_hd)
    sk = sin_t * (gk_ref[1:2, :] * root_hd)
    kr_sq = kr * kr
    kr_sq2 = jnp.concatenate([kr_sq, kr_sq], axis=1)
    kr_rot = kr * ck + pltpu.roll(kr, LANES - ROPE, axis=1) * sk
    n_pair = N_HEAD // 2
    n_rec = n_slab
    for pr in range(n_pair):
        for c in range(pr * n_rec // n_pair, (pr + 1) * n_rec // n_pair):
            recurrent_inputs(c)
        qp = q[:, 2 * pr * LANES:2 * (pr + 1) * LANES]
        sc = lax.rsqrt(head_sums(qp * qp) + eps_hd)
        for hh in range(2):
            h = 2 * pr + hh
            qh = qp[:, hh * LANES:(hh + 1) * LANES]
            qr = pltpu.roll(qh, LANES - ROPE, axis=1)
            qf = (qh * cq + qr * sq) * sc[:, hh * LANES:(hh + 1) * LANES]
            q_out[0, h * LANES:(h + 1) * LANES, :] = qf.astype(BF16).T
        kp = kn[:, 2 * pr * LANES:2 * (pr + 1) * LANES]
        sc = lax.rsqrt(head_sums(kp * kp + kr_sq2) + eps_hd)
        for hh in range(2):
            h = 2 * pr + hh
            kh = kp[:, hh * LANES:(hh + 1) * LANES]
            k_out[0, :, h * LANES:(h + 1) * LANES] = (
                (kh * ck + kr_rot) * sc[:, hh * LANES:(hh + 1) * LANES]).astype(BF16)


def _const_spec(shape):
    nd = len(shape)
    return pl.BlockSpec(shape, lambda *_: (0,) * nd)


def _project(x3, tm, cos_t, sin_t, wts, dims):
    q_lora, kv_lora, d_rnn = dims
    b, s, d = x3.shape
    n_slab = d_rnn // LANES
    nt = s // tm
    ln1, win, qag, wuq, kvag, wuk, wuv, gq, gk, hsum = wts
    row3 = lambda i, j: (i, j, 0)
    row4 = lambda i, j: (i, 0, j, 0)
    tab = lambda i, j: (j, 0)
    in_specs = [pl.BlockSpec((1, tm, d), row3)]
    in_specs += [_const_spec(w.shape) for w in (ln1, win, qag, wuq, kvag, wuk, wuv)]
    in_specs += [pl.BlockSpec((tm, LANES), tab), pl.BlockSpec((tm, LANES), tab)]
    in_specs += [_const_spec(gq.shape), _const_spec(gk.shape), _const_spec(hsum.shape)]
    col3 = lambda i, j: (i, 0, j)
    out_shape = (
        jax.ShapeDtypeStruct((b, N_HEAD * LANES, s), BF16),
        jax.ShapeDtypeStruct((b, s, N_HEAD * LANES), BF16),
        jax.ShapeDtypeStruct((b, N_HEAD * VT_ROWS, s), BF16),
        jax.ShapeDtypeStruct((b, n_slab, s, LANES), F32),
        jax.ShapeDtypeStruct((b, n_slab, s, LANES), F32),
    )
    out_specs = (
        pl.BlockSpec((1, N_HEAD * LANES, tm), col3),
        pl.BlockSpec((1, tm, N_HEAD * LANES), row3),
        pl.BlockSpec((1, N_HEAD * VT_ROWS, tm), col3),
        pl.BlockSpec((1, n_slab, tm, LANES), row4),
        pl.BlockSpec((1, n_slab, tm, LANES), row4),
    )
    return pl.pallas_call(
        functools.partial(_proj_kernel, q_lora=q_lora, kv_lora=kv_lora, d_rnn=d_rnn),
        grid=(b, nt),
        in_specs=in_specs,
        out_specs=out_specs,
        out_shape=out_shape,
        compiler_params=pltpu.CompilerParams(
            dimension_semantics=("arbitrary", "arbitrary"), vmem_limit_bytes=VMEM_LIMIT),
        name="proj",
    )(x3, ln1, win, qag, wuq, kvag, wuk, wuv, cos_t, sin_t, gq, gk, hsum)


def _scan_stride(t_len):
    s = -(-t_len // SUBLANES)
    while s % SUBLANES == 0:
        s += 1
    if s % 2:
        s += 1
        if s % SUBLANES == 0:
            s += 2
    return s


def _chunk_len(stride, cap=258):
    for c in range(min(cap, stride), 0, -1):
        if stride % c == 0:
            return c
    return 1


def _rnn_kernel(xr_ref, xg_ref, xrm_ref, cw_ref, cb_ref, wg_ref, lam_ref, g_ref,
                o_ref, t1, af, bf, ab, bb, *, seq, stride, rc, tm_out):
    n_slab = xr_ref.shape[1]
    t_len = N_META_TOK + seq
    pre = SUBLANES
    rows = t1.shape[1]
    n_chunk = stride // rc
    crow = rc * SUBLANES

    for s in range(n_slab):
        t1[s, 0:pre, :] = jnp.zeros((pre, LANES), F32)
        t1[s, pre:pre + N_META_TOK, :] = xrm_ref[0, s]
        t1[s, pre + N_META_TOK:pre + t_len, :] = xr_ref[0, s]
        t1[s, pre + t_len:rows, :] = jnp.zeros((rows - pre - t_len, LANES), F32)

    lam = lam_ref[...]
    nl = -lam
    softplus = jnp.maximum(nl, 0.0) + jnp.log(1.0 + jnp.exp(-jnp.abs(nl)))
    hcn = (0.5 * LRU_SCALE) * softplus
    cw = cw_ref[...]
    cb = cb_ref[...]
    ones_k = jnp.ones((crow, LANES), BF16)

    padded = stride * SUBLANES > t_len
    if padded:
        ridx = lax.broadcasted_iota(jnp.int32, (crow, LANES), 0)
        t_local = (ridx & (SUBLANES - 1)) * stride + (ridx >> 3)

    def gate_chunk(c, carry):
        r0 = c * rc
        row0 = pl.multiple_of(c * crow, SUBLANES)
        for s in range(n_slab):
            ls = slice(s * LANES, (s + 1) * LANES)
            taps = [t1[s, pl.ds(r0 + pre - CONV_LEFT + j, SUBLANES, stride=stride), :]
                    for j in range(rc + CONV_TAPS - 1)]
            xc_rows = []
            for rr in range(rc):
                acc = taps[rr] * cw[0:1, ls]
                for j in range(1, CONV_TAPS):
                    acc = acc + taps[rr + j] * cw[j:j + 1, ls]
                xc_rows.append(acc + cb[:, ls])
            xc = jnp.concatenate(xc_rows, axis=0)
            hx = 0.5 * xc
            z = jnp.dot(jnp.concatenate([xc.astype(BF16), ones_k], axis=1), wg_ref[s],
                        preferred_element_type=F32)
            for d, (a_ref, b_ref) in enumerate(((af, bf), (ab, bb))):
                ta = jnp.tanh(z[:, (2 * d) * LANES:(2 * d + 1) * LANES])
                ti = jnp.tanh(z[:, (2 * d + 1) * LANES:(2 * d + 2) * LANES])
                hc = hcn[d:d + 1, ls]
                nla = ta * hc + hc
                a = jnp.exp2(nla * (-LOG2E))
                one_m_a2 = jnp.tanh(nla) * (1.0 + a * a)
                root = jnp.where(one_m_a2 > 0.0, one_m_a2 * lax.rsqrt(one_m_a2), 0.0)
                bv = root * (ti * hx + hx)
                if d == 1 and padded:
                    bv = jnp.where((t_local + r0) < t_len, bv, 0.0)
                a_ref[s, pl.ds(row0, crow), :] = a
                b_ref[s, pl.ds(row0, crow), :] = bv
        return carry

    lax.fori_loop(0, n_chunk, gate_chunk, 0)

    zero = jnp.zeros((SUBLANES, LANES), F32)
    one = jnp.ones((SUBLANES, LANES), F32)

    def scan_step(r, carry):
        hf, pf, hb, pb = carry
        rf = pl.multiple_of(r * SUBLANES, SUBLANES)
        rb = pl.multiple_of((stride - 1 - r) * SUBLANES, SUBLANES)
        nhf, npf, nhb, npb = [], [], [], []
        for s in range(n_slab):
            a = af[s, pl.ds(rf, SUBLANES), :]
            h = a * hf[s] + bf[s, pl.ds(rf, SUBLANES), :]
            pr = a * pf[s]
            bf[s, pl.ds(rf, SUBLANES), :] = h
            af[s, pl.ds(rf, SUBLANES), :] = pr
            nhf.append(h)
            npf.append(pr)
            a = ab[s, pl.ds(rb, SUBLANES), :]
            h = a * hb[s] + bb[s, pl.ds(rb, SUBLANES), :]
            pr = a * pb[s]
            bb[s, pl.ds(rb, SUBLANES), :] = h
            ab[s, pl.ds(rb, SUBLANES), :] = pr
            nhb.append(h)
            npb.append(pr)
        return tuple(nhf), tuple(npf), tuple(nhb), tuple(npb)

    init = ((zero,) * n_slab, (one,) * n_slab, (zero,) * n_slab, (one,) * n_slab)
    hf, pf, hb, pb = lax.fori_loop(0, stride, scan_step, init, unroll=SCAN_UNROLL)

    sub = lax.broadcasted_iota(jnp.int32, (SUBLANES, LANES), 0)
    cf, cbk = [], []
    for s in range(n_slab):
        c = zero
        for _ in range(SUBLANES - 1):
            c = jnp.where(sub == 0, 0.0, pltpu.roll(pf[s] * c + hf[s], 1, axis=0))
        cf.append(c)
        c = zero
        for _ in range(SUBLANES - 1):
            c = jnp.where(sub == SUBLANES - 1, 0.0, pltpu.roll(pb[s] * c + hb[s], SUBLANES - 1, axis=0))
        cbk.append(c)

    def combine(r, carry):
        rp = pl.multiple_of(r * SUBLANES, SUBLANES)
        for s in range(n_slab):
            y = (bf[s, pl.ds(rp, SUBLANES), :] + af[s, pl.ds(rp, SUBLANES), :] * cf[s]
                 + bb[s, pl.ds(rp, SUBLANES), :] + ab[s, pl.ds(rp, SUBLANES), :] * cbk[s])
            t1[s, pl.ds(r + pre, SUBLANES, stride=stride), :] = y
        return carry

    lax.fori_loop(0, stride, combine, 0, unroll=COMBINE_UNROLL)

    d_rnn = n_slab * LANES
    g = g_ref[...]
    k0 = math.sqrt(2.0 / math.pi)

    def finish(j, carry):
        ro = pl.multiple_of(j * tm_out, 2 * SUBLANES)
        rt = pl.multiple_of(j * tm_out + pre + N_META_TOK, SUBLANES)
        ys = []
        ssq = jnp.zeros((tm_out, 1), F32)
        for s in range(n_slab):
            y = t1[s, pl.ds(rt, tm_out), :]
            xg = xg_ref[0, s, pl.ds(ro, tm_out), :]
            yh = y * (0.5 * xg)
            y = yh + yh * jnp.tanh(xg * (k0 + (k0 * 0.044715) * (xg * xg)))
            ssq = ssq + jnp.sum(y * y, axis=-1, keepdims=True)
            ys.append(y)
        sc = lax.rsqrt(ssq * (1.0 / d_rnn) + NORM_EPS)
        for s in range(n_slab):
            ls = slice(s * LANES, (s + 1) * LANES)
            o_ref[0, pl.ds(ro, tm_out), ls] = (ys[s] * sc * g[:, ls]).astype(BF16)
        return carry

    lax.fori_loop(0, seq // tm_out, finish, 0)


def _rnn(xr, xg, xrm, cw, cb, wg, lam, g):
    b, n_slab, seq, _ = xr.shape
    t_len = N_META_TOK + seq
    stride = _scan_stride(t_len)
    rc = _chunk_len(stride)
    rows_perm = stride * SUBLANES
    rows_time = SUBLANES + rows_perm + SUBLANES
    tm_out = 512 if seq % 512 == 0 else seq
    blk = lambda i: (i, 0, 0, 0)
    in_specs = [
        pl.BlockSpec((1, n_slab, seq, LANES), blk),
        pl.BlockSpec((1, n_slab, seq, LANES), blk),
        _const_spec(xrm.shape),
    ] + [_const_spec(w.shape) for w in (cw, cb, wg, lam, g)]
    scratch = [pltpu.VMEM((n_slab, rows_time, LANES), F32)]
    scratch += [pltpu.VMEM((n_slab, rows_perm, LANES), F32) for _ in range(4)]
    return pl.pallas_call(
        functools.partial(_rnn_kernel, seq=seq, stride=stride, rc=rc, tm_out=tm_out),
        grid=(b,),
        in_specs=in_specs,
        out_specs=pl.BlockSpec((1, seq, n_slab * LANES), lambda i: (i, 0, 0)),
        out_shape=jax.ShapeDtypeStruct((b, seq, n_slab * LANES), BF16),
        scratch_shapes=scratch,
        compiler_params=pltpu.CompilerParams(
            dimension_semantics=("arbitrary",), vmem_limit_bytes=VMEM_LIMIT),
        name="rnn",
    )(xr, xg, xrm, cw, cb, wg, lam, g)


def _attn_kernel(qt_ref, k_ref, vt_ref, km_ref, vtm_ref, o_ref, *, tq):
    n_key = k_ref.shape[1]

    n_stream = HEADS_PER_STEP * (n_key // tq)
    n_tile = n_key // KEY_TILE
    tiles = [(t * KEY_TILE, (t + 1) * KEY_TILE) for t in range(n_tile)]
    tiles[-1] = (tiles[-1][0], n_key + N_META_TOK)
    steps = [(i, t) for i in range(n_stream) for t in range(n_tile)]

    def score_tile(i, t):
        j, hh = divmod(i, HEADS_PER_STEP)
        k0, k1 = tiles[t]
        ls = slice(hh * LANES, (hh + 1) * LANES)
        keys = k_ref[0, k0:min(k1, n_key), ls]
        if k1 > n_key:
            keys = jnp.concatenate([keys, km_ref[0, :, ls]], axis=0)
        return jnp.dot(keys, qt_ref[0, ls, j * tq:(j + 1) * tq],
                       preferred_element_type=F32)

    outs = []
    pending = [score_tile(*steps[n]) for n in range(SCORE_LOOKAHEAD)]
    m, acc = None, None
    for n, (i, t) in enumerate(steps):
        st = pending.pop(0)
        if n + SCORE_LOOKAHEAD < len(steps):
            pending.append(score_tile(*steps[n + SCORE_LOOKAHEAD]))
        k0, k1 = tiles[t]
        hh = i % HEADS_PER_STEP
        vs = slice(hh * VT_ROWS, (hh + 1) * VT_ROWS)
        st3 = st.reshape((k1 - k0) // SUBLANES, SUBLANES, tq)
        tmax = jnp.max(jnp.max(st3, axis=0), axis=0, keepdims=True)
        m_new = tmax if m is None else jnp.maximum(m, tmax)
        mb = jnp.broadcast_to(m_new, (SUBLANES, tq))
        pb = jnp.exp2(st3 - mb[None]).reshape(k1 - k0, tq).astype(BF16)
        part = jnp.dot(vt_ref[0, vs, k0:k0 + KEY_TILE], pb[:KEY_TILE], preferred_element_type=F32)
        if k1 - k0 > KEY_TILE:
            part = part + jnp.dot(vtm_ref[0, vs, :], pb[KEY_TILE:], preferred_element_type=F32)
        acc = part if m is None else acc * jnp.exp2(m - m_new) + part
        m = m_new
        if t == n_tile - 1:
            outs.append(acc[:HEAD_V] * (1.0 / acc[HEAD_V:HEAD_V + 1]))
            m, acc = None, None
            if hh == HEADS_PER_STEP - 1:
                j = i // HEADS_PER_STEP
                o_ref[0, j * tq:(j + 1) * tq, :] = jnp.concatenate(outs, axis=0).T
                outs = []


def _attention(qt, k, vt, km, vtm, tq):
    b, s, _ = k.shape
    hps = HEADS_PER_STEP
    return pl.pallas_call(
        functools.partial(_attn_kernel, tq=tq),
        grid=(b, N_HEAD // hps),
        in_specs=[
            pl.BlockSpec((1, hps * LANES, s), lambda i, p: (i, p, 0)),
            pl.BlockSpec((1, s, hps * LANES), lambda i, p: (i, 0, p)),
            pl.BlockSpec((1, hps * VT_ROWS, s), lambda i, p: (i, p, 0)),
            pl.BlockSpec((1, N_META_TOK, hps * LANES), lambda i, p: (0, 0, p)),
            pl.BlockSpec((1, hps * VT_ROWS, N_META_TOK), lambda i, p: (0, p, 0)),
        ],
        out_specs=pl.BlockSpec((1, s, hps * HEAD_V), lambda i, p: (i, 0, p)),
        out_shape=jax.ShapeDtypeStruct((b, s, N_HEAD * HEAD_V), F32),
        compiler_params=pltpu.CompilerParams(
            dimension_semantics=("arbitrary", "arbitrary"), vmem_limit_bytes=VMEM_LIMIT),
        name="attn",
    )(qt, k, vt, km, vtm)


def _out_kernel(x_ref, oa_ref, on_ref, ag_ref, woa_ref, wor_ref, ln2_ref, wg_ref, wu_ref, wd_ref,
                o_ref, *, ff_chunk):
    oa = oa_ref[...]
    oan = (oa * _rsqrt_mean(oa, oa.shape[-1]) * ag_ref[...]).astype(BF16)
    h = x_ref[...] + jnp.dot(oan, woa_ref[...], preferred_element_type=F32)
    h = h + jnp.dot(on_ref[...], wor_ref[...], preferred_element_type=F32)
    hn = (h * _rsqrt_mean(h, h.shape[-1]) * ln2_ref[...]).astype(BF16)
    d_ff = wg_ref.shape[1]
    acc = h
    for c in range(d_ff // ff_chunk):
        cs = slice(c * ff_chunk, (c + 1) * ff_chunk)
        gt = jnp.dot(hn, wg_ref[:, cs], preferred_element_type=F32)
        up = jnp.dot(hn, wu_ref[:, cs], preferred_element_type=F32)
        act = (gt * jax.nn.sigmoid(gt) * up).astype(BF16)
        acc = acc + jnp.dot(act, wd_ref[cs, :], preferred_element_type=F32)
    o_ref[...] = acc


def _out_ffn(x2, oa2, on2, ag, woa, wor, ln2, wg, wu, wd, tm):
    n, d = x2.shape
    d_ff = wg.shape[1]
    ff_chunk = d_ff
    row = lambda i: (i, 0)
    single = pl.Buffered(1)
    wspec = lambda w: pl.BlockSpec(w.shape, lambda i: (0, 0), pipeline_mode=single)
    return pl.pallas_call(
        functools.partial(_out_kernel, ff_chunk=ff_chunk),
        grid=(n // tm,),
        in_specs=[
            pl.BlockSpec((tm, d), row),
            pl.BlockSpec((tm, oa2.shape[1]), row),
            pl.BlockSpec((tm, on2.shape[1]), row),
            wspec(ag), wspec(woa), wspec(wor), wspec(ln2), wspec(wg), wspec(wu), wspec(wd),
        ],
        out_specs=pl.BlockSpec((tm, d), row),
        out_shape=jax.ShapeDtypeStruct((n, d), F32),
        compiler_params=pltpu.CompilerParams(
            dimension_semantics=("arbitrary",), vmem_limit_bytes=VMEM_LIMIT),
        name="out_ffn",
    )(x2, oa2, on2, ag, woa, wor, ln2, wg, wu, wd)


def _rope_tables(t_len):
    half = ROPE // 2
    freqs = 1.0 / (THETA ** (np.arange(half, dtype=np.float64) / half))
    ang = np.arange(t_len, dtype=np.float64)[:, None] * freqs[None, :]
    pad = LANES - HEAD_QK
    cos_t = np.concatenate([np.ones((t_len, NOPE)), np.cos(ang), np.cos(ang), np.zeros((t_len, pad))], -1)
    sin_t = np.concatenate([np.zeros((t_len, NOPE)), np.sin(ang), np.sin(ang), np.zeros((t_len, pad))], -1)
    return jnp.asarray(cos_t, F32), jnp.asarray(sin_t, F32)


def _rot_half_cols(w):
    half = ROPE // 2
    return jnp.concatenate([-w[..., half:], w[..., :half]], axis=-1)


def _gain_rows(g):
    half = ROPE // 2
    pad = jnp.zeros((LANES - HEAD_QK,), F32)
    r0 = jnp.concatenate([g, pad])
    r1 = jnp.concatenate([jnp.zeros((NOPE,), F32), g[NOPE + half:], g[NOPE:NOPE + half], pad])
    return jnp.stack([r0, r1])


def _block_diag_pairs(w):
    g, n, _ = w.shape
    z = jnp.zeros((g // 2, n, n), w.dtype)
    top = jnp.concatenate([w[0::2], z], axis=2)
    bot = jnp.concatenate([z, w[1::2]], axis=2)
    return jnp.concatenate([top, bot], axis=1)


def kernel(x, meta_tokens, ln1_g, w_in, q_a_norm_g, w_uq, kv_a_norm_g, w_ukv, q_norm_g, k_norm_g,
           conv_w, conv_b, lru_wa, lru_ba, lru_wi, lru_bi, lru_lambda, attn_out_g, rnn_out_g,
           w_out, ln2_g, w_gate, w_up, w_down):
    bsz, seq, d_model = x.shape
    q_lora = q_a_norm_g.shape[-1]
    kv_lora = kv_a_norm_g.shape[-1]
    d_rnn = conv_w.shape[-1]
    d_attn = N_HEAD * HEAD_V
    t_len = N_META_TOK + seq
    l = 0

    wi = w_in[l]
    o_kr = q_lora + kv_lora
    w_kr = wi[:, o_kr:o_kr + ROPE]
    win = jnp.concatenate(
        [wi[:, :o_kr], jnp.zeros((d_model, NOPE), F32), w_kr, _rot_half_cols(w_kr),
         wi[:, o_kr + ROPE:]], axis=1).astype(BF16)
    wq = w_uq[l].reshape(q_lora, N_HEAD, HEAD_QK)
    wuq = jnp.concatenate([wq, _rot_half_cols(wq[..., NOPE:])], axis=-1)
    wuq = wuq.reshape(q_lora, N_HEAD * LANES).astype(BF16)
    wkv = w_ukv[l].reshape(kv_lora, N_HEAD, NOPE + HEAD_V)
    wuk = jnp.concatenate([wkv[..., :NOPE], jnp.zeros((kv_lora, N_HEAD, LANES - NOPE), F32)], -1)
    wuk = wuk.reshape(kv_lora, N_HEAD * LANES).astype(BF16)
    wuv = wkv[..., NOPE:].reshape(kv_lora, d_attn).astype(BF16)
    gq = _gain_rows(q_norm_g[l])
    gk = _gain_rows(k_norm_g[l])
    cos_t, sin_t = _rope_tables(t_len)
    lane2 = np.arange(2 * LANES)
    hsum = jnp.asarray((lane2[:, None] // LANES == lane2[None, :] // LANES)
                       & (lane2[:, None] % LANES < HEAD_QK), BF16)
    proj_w = (ln1_g[l][None], win, q_a_norm_g[l][None], wuq, kv_a_norm_g[l][None], wuk, wuv, gq, gk, hsum)
    dims = (q_lora, kv_lora, d_rnn)

    wgate = (0.5 * jnp.concatenate(
        [_block_diag_pairs(w) for w in (lru_wa[l, 0], lru_wi[l, 0], lru_wa[l, 1], lru_wi[l, 1])],
        axis=2)).astype(BF16)
    n_slab = d_rnn // LANES
    gate_b = 0.5 * jnp.stack([lru_ba[l, 0], lru_bi[l, 0], lru_ba[l, 1], lru_bi[l, 1]])
    gate_b = gate_b.reshape(4, n_slab, LANES).transpose(1, 0, 2).reshape(n_slab, 1, 4 * LANES)
    b_hi = gate_b.astype(BF16)
    b_lo = (gate_b - b_hi.astype(F32)).astype(BF16)
    wgate = jnp.concatenate(
        [wgate, b_hi, b_lo, jnp.zeros((n_slab, LANES - 2, 4 * LANES), BF16)], axis=1)

    tm = 512 if seq % 512 == 0 else seq
    q, k, v, xr, xg = _project(x, tm, cos_t[N_META_TOK:], sin_t[N_META_TOK:], proj_w, dims)
    _, km, vm, xrm, _ = _project(meta_tokens[None].astype(x.dtype), N_META_TOK,
                                 cos_t[:N_META_TOK], sin_t[:N_META_TOK], proj_w, dims)

    o_rnn = _rnn(xr, xg, xrm, conv_w[l], conv_b[l][None], wgate, lru_lambda[l], rnn_out_g[l][None])

    tq = 1024 if seq % 1024 == 0 else seq
    o_attn = _attention(q, k, v, km, vm, tq)

    wo = w_out[l].astype(BF16)
    n_rows = bsz * seq
    tmo = 512 if n_rows % 512 == 0 else n_rows
    out = _out_ffn(x.reshape(n_rows, d_model), o_attn.reshape(n_rows, d_attn),
                   o_rnn.reshape(n_rows, d_rnn), attn_out_g[l][None], wo[:d_attn], wo[d_attn:],
                   ln2_g[l][None], w_gate[l].astype(BF16), w_up[l].astype(BF16),
                   w_down[l].astype(BF16), tmo)
    return out.reshape(bsz, seq, d_model)
```

```python
import functools
import math

import numpy as np

import jax
import jax.numpy as jnp
from jax import lax
from jax.experimental import pallas as pl
from jax.experimental.pallas import tpu as pltpu

F32 = jnp.float32
BF16 = jnp.bfloat16

N_META_TOK = 16
N_HEAD = 8
NOPE = 64
ROPE = 32
HEAD_QK = NOPE + ROPE
HEAD_V = 64
VT_ROWS = HEAD_V + 16
LRU_SCALE = 8.0
THETA = 10000.0
NORM_EPS = 1e-6
CONV_TAPS = 4
CONV_LEFT = 2

LANES = 128
SUBLANES = 8
VMEM_LIMIT = 56 * 1024 * 1024
KEY_TILE = 256
SCAN_UNROLL = 6
COMBINE_UNROLL = 43
HEADS_PER_STEP = 4
SCORE_LOOKAHEAD = 1

LOG2E = 1.4426950408889634


def _rsqrt_mean(x, n):
    return lax.rsqrt(jnp.sum(x * x, axis=-1, keepdims=True) * (1.0 / n) + NORM_EPS)


def _proj_kernel(x_ref, ln1_ref, win_ref, qag_ref, wuq_ref, kvag_ref, wuk_ref, wuv_ref,
                 cos_ref, sin_ref, gq_ref, gk_ref, hsum_ref,
                 q_out, k_out, v_out, xr_out, xg_out, *, q_lora, kv_lora, d_rnn):
    x = x_ref[0]
    d_model = x.shape[-1]
    hn = (x * _rsqrt_mean(x, d_model) * ln1_ref[...]).astype(BF16)
    o_kv = q_lora
    o_kr = o_kv + kv_lora
    o_xr = o_kr + LANES
    p = jnp.dot(hn, win_ref[:, :o_xr], preferred_element_type=F32)
    c_q = p[:, :o_kv]
    c_kv = p[:, o_kv:o_kr]
    kr = p[:, o_kr:o_xr]
    n_slab = d_rnn // LANES

    def recurrent_inputs(c):
        px = jnp.dot(hn, win_ref[:, o_xr + 2 * c * LANES:o_xr + 2 * (c + 1) * LANES],
                     preferred_element_type=F32)
        for u in range(2):
            slab = 2 * c + u
            dst = xr_out if slab < n_slab else xg_out
            dst[0, slab % n_slab] = px[:, u * LANES:(u + 1) * LANES]

    cos_t = cos_ref[...]
    sin_t = sin_ref[...]
    eps_hd = HEAD_QK * NORM_EPS
    root_hd = math.sqrt(HEAD_QK)
    hsum = hsum_ref[...]

    def head_sums(sq_pair):
        return jnp.dot(sq_pair.astype(BF16), hsum, preferred_element_type=F32)

    cqn = (c_q * _rsqrt_mean(c_q, q_lora) * qag_ref[...]).astype(BF16)
    q = jnp.dot(cqn, wuq_ref[...], preferred_element_type=F32)
    q_gain = root_hd * (HEAD_QK ** -0.5) * LOG2E
    cq = cos_t * (gq_ref[0:1, :] * q_gain)
    sq = sin_t * (gq_ref[1:2, :] * q_gain)

    ckn = (c_kv * _rsqrt_mean(c_kv, kv_lora) * kvag_ref[...]).astype(BF16)
    kn = jnp.dot(ckn, wuk_ref[...], preferred_element_type=F32)
    vt = jnp.dot(ckn, wuv_ref[...], preferred_element_type=F32).astype(BF16).T
    tm = vt.shape[1]
    ones_rows = (lax.broadcasted_iota(jnp.int32, (VT_ROWS - HEAD_V, tm), 0) == 0).astype(BF16)
    for h in range(N_HEAD):
        v_out[0, h * VT_ROWS:h * VT_ROWS + HEAD_V, :] = vt[h * HEAD_V:(h + 1) * HEAD_V]
        v_out[0, h * VT_ROWS + HEAD_V:(h + 1) * VT_ROWS, :] = ones_rows
    ck = cos_t * (gk_ref[0:1, :] * root_hd)
    sk = sin_t * (gk_ref[1:2, :] * root_hd)
    kr_sq = kr * kr
    kr_sq2 = jnp.concatenate([kr_sq, kr_sq], axis=1)
    kr_rot = kr * ck + pltpu.roll(kr, LANES - ROPE, axis=1) * sk
    n_pair = N_HEAD // 2
    n_rec = n_slab
    for pr in range(n_pair):
        for c in range(pr * n_rec // n_pair, (pr + 1) * n_rec // n_pair):
            recurrent_inputs(c)
        qp = q[:, 2 * pr * LANES:2 * (pr + 1) * LANES]
        sc = lax.rsqrt(head_sums(qp * qp) + eps_hd)
        for hh in range(2):
            h = 2 * pr + hh
            qh = qp[:, hh * LANES:(hh + 1) * LANES]
            qr = pltpu.roll(qh, LANES - ROPE, axis=1)
            qf = (qh * cq + qr * sq) * sc[:, hh * LANES:(hh + 1) * LANES]
            q_out[0, h * LANES:(h + 1) * LANES, :] = qf.astype(BF16).T
        kp = kn[:, 2 * pr * LANES:2 * (pr + 1) * LANES]
        sc = lax.rsqrt(head_sums(kp * kp + kr_sq2) + eps_hd)
        for hh in range(2):
            h = 2 * pr + hh
            kh = kp[:, hh * LANES:(hh + 1) * LANES]
            k_out[0, :, h * LANES:(h + 1) * LANES] = (
                (kh * ck + kr_rot) * sc[:, hh * LANES:(hh + 1) * LANES]).astype(BF16)


def _const_spec(shape):
    nd = len(shape)
    return pl.BlockSpec(shape, lambda *_: (0,) * nd)


def _project(x3, tm, cos_t, sin_t, wts, dims):
    q_lora, kv_lora, d_rnn = dims
    b, s, d = x3.shape
    n_slab = d_rnn // LANES
    nt = s // tm
    ln1, win, qag, wuq, kvag, wuk, wuv, gq, gk, hsum = wts
    row3 = lambda i, j: (i, j, 0)
    row4 = lambda i, j: (i, 0, j, 0)
    tab = lambda i, j: (j, 0)
    in_specs = [pl.BlockSpec((1, tm, d), row3)]
    in_specs += [_const_spec(w.shape) for w in (ln1, win, qag, wuq, kvag, wuk, wuv)]
    in_specs += [pl.BlockSpec((tm, LANES), tab), pl.BlockSpec((tm, LANES), tab)]
    in_specs += [_const_spec(gq.shape), _const_spec(gk.shape), _const_spec(hsum.shape)]
    col3 = lambda i, j: (i, 0, j)
    out_shape = (
        jax.ShapeDtypeStruct((b, N_HEAD * LANES, s), BF16),
        jax.ShapeDtypeStruct((b, s, N_HEAD * LANES), BF16),
        jax.ShapeDtypeStruct((b, N_HEAD * VT_ROWS, s), BF16),
        jax.ShapeDtypeStruct((b, n_slab, s, LANES), F32),
        jax.ShapeDtypeStruct((b, n_slab, s, LANES), F32),
    )
    out_specs = (
        pl.BlockSpec((1, N_HEAD * LANES, tm), col3),
        pl.BlockSpec((1, tm, N_HEAD * LANES), row3),
        pl.BlockSpec((1, N_HEAD * VT_ROWS, tm), col3),
        pl.BlockSpec((1, n_slab, tm, LANES), row4),
        pl.BlockSpec((1, n_slab, tm, LANES), row4),
    )
    return pl.pallas_call(
        functools.partial(_proj_kernel, q_lora=q_lora, kv_lora=kv_lora, d_rnn=d_rnn),
        grid=(b, nt),
        in_specs=in_specs,
        out_specs=out_specs,
        out_shape=out_shape,
        compiler_params=pltpu.CompilerParams(
            dimension_semantics=("arbitrary", "arbitrary"), vmem_limit_bytes=VMEM_LIMIT),
        name="proj",
    )(x3, ln1, win, qag, wuq, kvag, wuk, wuv, cos_t, sin_t, gq, gk, hsum)


def _scan_stride(t_len):
    s = -(-t_len // SUBLANES)
    while s % SUBLANES == 0:
        s += 1
    if s % 2:
        s += 1
        if s % SUBLANES == 0:
            s += 2
    return s


def _chunk_len(stride, cap=258):
    for c in range(min(cap, stride), 0, -1):
        if stride % c == 0:
            return c
    return 1


def _rnn_kernel(xr_ref, xg_ref, xrm_ref, cw_ref, cb_ref, wg_ref, lam_ref, g_ref,
                o_ref, t1, af, bf, ab, bb, *, seq, stride, rc, tm_out):
    n_slab = xr_ref.shape[1]
    t_len = N_META_TOK + seq
    pre = SUBLANES
    rows = t1.shape[1]
    n_chunk = stride // rc
    crow = rc * SUBLANES

    for s in range(n_slab):
        t1[s, 0:pre, :] = jnp.zeros((pre, LANES), F32)
        t1[s, pre:pre + N_META_TOK, :] = xrm_ref[0, s]
        t1[s, pre + N_META_TOK:pre + t_len, :] = xr_ref[0, s]
        t1[s, pre + t_len:rows, :] = jnp.zeros((rows - pre - t_len, LANES), F32)

    lam = lam_ref[...]
    nl = -lam
    softplus = jnp.maximum(nl, 0.0) + jnp.log(1.0 + jnp.exp(-jnp.abs(nl)))
    hcn = (0.5 * LRU_SCALE) * softplus
    cw = cw_ref[...]
    cb = cb_ref[...]
    ones_k = jnp.ones((crow, LANES), BF16)

    padded = stride * SUBLANES > t_len
    if padded:
        ridx = lax.broadcasted_iota(jnp.int32, (crow, LANES), 0)
        t_local = (ridx & (SUBLANES - 1)) * stride + (ridx >> 3)

    def gate_chunk(c, carry):
        r0 = c * rc
        row0 = pl.multiple_of(c * crow, SUBLANES)
        for s in range(n_slab):
            ls = slice(s * LANES, (s + 1) * LANES)
            taps = [t1[s, pl.ds(r0 + pre - CONV_LEFT + j, SUBLANES, stride=stride), :]
                    for j in range(rc + CONV_TAPS - 1)]
            xc_rows = []
            for rr in range(rc):
                acc = taps[rr] * cw[0:1, ls]
                for j in range(1, CONV_TAPS):
                    acc = acc + taps[rr + j] * cw[j:j + 1, ls]
                xc_rows.append(acc + cb[:, ls])
            xc = jnp.concatenate(xc_rows, axis=0)
            hx = 0.5 * xc
            z = jnp.dot(jnp.concatenate([xc.astype(BF16), ones_k], axis=1), wg_ref[s],
                        preferred_element_type=F32)
            for d, (a_ref, b_ref) in enumerate(((af, bf), (ab, bb))):
                ta = jnp.tanh(z[:, (2 * d) * LANES:(2 * d + 1) * LANES])
                ti = jnp.tanh(z[:, (2 * d + 1) * LANES:(2 * d + 2) * LANES])
                hc = hcn[d:d + 1, ls]
                nla = ta * hc + hc
                a = jnp.exp2(nla * (-LOG2E))
                one_m_a2 = jnp.tanh(nla) * (1.0 + a * a)
                root = jnp.where(one_m_a2 > 0.0, one_m_a2 * lax.rsqrt(one_m_a2), 0.0)
                bv = root * (ti * hx + hx)
                if d == 1 and padded:
                    bv = jnp.where((t_local + r0) < t_len, bv, 0.0)
                a_ref[s, pl.ds(row0, crow), :] = a
                b_ref[s, pl.ds(row0, crow), :] = bv
        return carry

    lax.fori_loop(0, n_chunk, gate_chunk, 0)

    zero = jnp.zeros((SUBLANES, LANES), F32)
    one = jnp.ones((SUBLANES, LANES), F32)

    def scan_step(r, carry):
        hf, pf, hb, pb = carry
        rf = pl.multiple_of(r * SUBLANES, SUBLANES)
        rb = pl.multiple_of((stride - 1 - r) * SUBLANES, SUBLANES)
        nhf, npf, nhb, npb = [], [], [], []
        for s in range(n_slab):
            a = af[s, pl.ds(rf, SUBLANES), :]
            h = a * hf[s] + bf[s, pl.ds(rf, SUBLANES), :]
            pr = a * pf[s]
            bf[s, pl.ds(rf, SUBLANES), :] = h
            af[s, pl.ds(rf, SUBLANES), :] = pr
            nhf.append(h)
            npf.append(pr)
            a = ab[s, pl.ds(rb, SUBLANES), :]
            h = a * hb[s] + bb[s, pl.ds(rb, SUBLANES), :]
            pr = a * pb[s]
            bb[s, pl.ds(rb, SUBLANES), :] = h
            ab[s, pl.ds(rb, SUBLANES), :] = pr
            nhb.append(h)
            npb.append(pr)
        return tuple(nhf), tuple(npf), tuple(nhb), tuple(npb)

    init = ((zero,) * n_slab, (one,) * n_slab, (zero,) * n_slab, (one,) * n_slab)
    hf, pf, hb, pb = lax.fori_loop(0, stride, scan_step, init, unroll=SCAN_UNROLL)

    sub = lax.broadcasted_iota(jnp.int32, (SUBLANES, LANES), 0)
    cf, cbk = [], []
    for s in range(n_slab):
        c = zero
        for _ in range(SUBLANES - 1):
            c = jnp.where(sub == 0, 0.0, pltpu.roll(pf[s] * c + hf[s], 1, axis=0))
        cf.append(c)
        c = zero
        for _ in range(SUBLANES - 1):
            c = jnp.where(sub == SUBLANES - 1, 0.0, pltpu.roll(pb[s] * c + hb[s], SUBLANES - 1, axis=0))
        cbk.append(c)

    def combine(r, carry):
        rp = pl.multiple_of(r * SUBLANES, SUBLANES)
        for s in range(n_slab):
            y = (bf[s, pl.ds(rp, SUBLANES), :] + af[s, pl.ds(rp, SUBLANES), :] * cf[s]
                 + bb[s, pl.ds(rp, SUBLANES), :] + ab[s, pl.ds(rp, SUBLANES), :] * cbk[s])
            t1[s, pl.ds(r + pre, SUBLANES, stride=stride), :] = y
        return carry

    lax.fori_loop(0, stride, combine, 0, unroll=COMBINE_UNROLL)

    d_rnn = n_slab * LANES
    g = g_ref[...]
    k0 = math.sqrt(2.0 / math.pi)

    def finish(j, carry):
        ro = pl.multiple_of(j * tm_out, 2 * SUBLANES)
        rt = pl.multiple_of(j * tm_out + pre + N_META_TOK, SUBLANES)
        ys = []
        ssq = jnp.zeros((tm_out, 1), F32)
        for s in range(n_slab):
            y = t1[s, pl.ds(rt, tm_out), :]
            xg = xg_ref[0, s, pl.ds(ro, tm_out), :]
            yh = y * (0.5 * xg)
            y = yh + yh * jnp.tanh(xg * (k0 + (k0 * 0.044715) * (xg * xg)))
            ssq = ssq + jnp.sum(y * y, axis=-1, keepdims=True)
            ys.append(y)
        sc = lax.rsqrt(ssq * (1.0 / d_rnn) + NORM_EPS)
        for s in range(n_slab):
            ls = slice(s * LANES, (s + 1) * LANES)
            o_ref[0, pl.ds(ro, tm_out), ls] = (ys[s] * sc * g[:, ls]).astype(BF16)
        return carry

    lax.fori_loop(0, seq // tm_out, finish, 0)


def _rnn(xr, xg, xrm, cw, cb, wg, lam, g):
    b, n_slab, seq, _ = xr.shape
    t_len = N_META_TOK + seq
    stride = _scan_stride(t_len)
    rc = _chunk_len(stride)
    rows_perm = stride * SUBLANES
    rows_time = SUBLANES + rows_perm + SUBLANES
    tm_out = 512 if seq % 512 == 0 else seq
    blk = lambda i: (i, 0, 0, 0)
    in_specs = [
        pl.BlockSpec((1, n_slab, seq, LANES), blk),
        pl.BlockSpec((1, n_slab, seq, LANES), blk),
        _const_spec(xrm.shape),
    ] + [_const_spec(w.shape) for w in (cw, cb, wg, lam, g)]
    scratch = [pltpu.VMEM((n_slab, rows_time, LANES), F32)]
    scratch += [pltpu.VMEM((n_slab, rows_perm, LANES), F32) for _ in range(4)]
    return pl.pallas_call(
        functools.partial(_rnn_kernel, seq=seq, stride=stride, rc=rc, tm_out=tm_out),
        grid=(b,),
        in_specs=in_specs,
        out_specs=pl.BlockSpec((1, seq, n_slab * LANES), lambda i: (i, 0, 0)),
        out_shape=jax.ShapeDtypeStruct((b, seq, n_slab * LANES), BF16),
        scratch_shapes=scratch,
        compiler_params=pltpu.CompilerParams(
            dimension_semantics=("arbitrary",), vmem_limit_bytes=VMEM_LIMIT),
        name="rnn",
    )(xr, xg, xrm, cw, cb, wg, lam, g)


def _attn_kernel(qt_ref, k_ref, vt_ref, km_ref, vtm_ref, o_ref, *, tq):
    n_key = k_ref.shape[1]

    n_stream = HEADS_PER_STEP * (n_key // tq)
    n_tile = n_key // KEY_TILE
    tiles = [(t * KEY_TILE, (t + 1) * KEY_TILE) for t in range(n_tile)]
    tiles[-1] = (tiles[-1][0], n_key + N_META_TOK)
    steps = [(i, t) for i in range(n_stream) for t in range(n_tile)]

    def score_tile(i, t):
        j, hh = divmod(i, HEADS_PER_STEP)
        k0, k1 = tiles[t]
        ls = slice(hh * LANES, (hh + 1) * LANES)
        keys = k_ref[0, k0:min(k1, n_key), ls]
        if k1 > n_key:
            keys = jnp.concatenate([keys, km_ref[0, :, ls]], axis=0)
        return jnp.dot(keys, qt_ref[0, ls, j * tq:(j + 1) * tq],
                       preferred_element_type=F32)

    outs = []
    pending = [score_tile(*steps[n]) for n in range(SCORE_LOOKAHEAD)]
    m, acc = None, None
    for n, (i, t) in enumerate(steps):
        st = pending.pop(0)
        if n + SCORE_LOOKAHEAD < len(steps):
            pending.append(score_tile(*steps[n + SCORE_LOOKAHEAD]))
        k0, k1 = tiles[t]
        hh = i % HEADS_PER_STEP
        vs = slice(hh * VT_ROWS, (hh + 1) * VT_ROWS)
        st3 = st.reshape((k1 - k0) // SUBLANES, SUBLANES, tq)
        tmax = jnp.max(jnp.max(st3, axis=0), axis=0, keepdims=True)
        m_new = tmax if m is None else jnp.maximum(m, tmax)
        mb = jnp.broadcast_to(m_new, (SUBLANES, tq))
        pb = jnp.exp2(st3 - mb[None]).reshape(k1 - k0, tq).astype(BF16)
        part = jnp.dot(vt_ref[0, vs, k0:k0 + KEY_TILE], pb[:KEY_TILE], preferred_element_type=F32)
        if k1 - k0 > KEY_TILE:
            part = part + jnp.dot(vtm_ref[0, vs, :], pb[KEY_TILE:], preferred_element_type=F32)
        acc = part if m is None else acc * jnp.exp2(m - m_new) + part
        m = m_new
        if t == n_tile - 1:
            outs.append(acc[:HEAD_V] * (1.0 / acc[HEAD_V:HEAD_V + 1]))
            m, acc = None, None
            if hh == HEADS_PER_STEP - 1:
                j = i // HEADS_PER_STEP
                o_ref[0, j * tq:(j + 1) * tq, :] = jnp.concatenate(outs, axis=0).T
                outs = []


def _attention(qt, k, vt, km, vtm, tq):
    b, s, _ = k.shape
    hps = HEADS_PER_STEP
    return pl.pallas_call(
        functools.partial(_attn_kernel, tq=tq),
        grid=(b, N_HEAD // hps),
        in_specs=[
            pl.BlockSpec((1, hps * LANES, s), lambda i, p: (i, p, 0)),
            pl.BlockSpec((1, s, hps * LANES), lambda i, p: (i, 0, p)),
            pl.BlockSpec((1, hps * VT_ROWS, s), lambda i, p: (i, p, 0)),
            pl.BlockSpec((1, N_META_TOK, hps * LANES), lambda i, p: (0, 0, p)),
            pl.BlockSpec((1, hps * VT_ROWS, N_META_TOK), lambda i, p: (0, p, 0)),
        ],
        out_specs=pl.BlockSpec((1, s, hps * HEAD_V), lambda i, p: (i, 0, p)),
        out_shape=jax.ShapeDtypeStruct((b, s, N_HEAD * HEAD_V), F32),
        compiler_params=pltpu.CompilerParams(
            dimension_semantics=("arbitrary", "arbitrary"), vmem_limit_bytes=VMEM_LIMIT),
        name="attn",
    )(qt, k, vt, km, vtm)


def _out_kernel(x_ref, oa_ref, on_ref, ag_ref, woa_ref, wor_ref, ln2_ref, wg_ref, wu_ref, wd_ref,
                o_ref, *, ff_chunk):
    oa = oa_ref[...]
    oan = (oa * _rsqrt_mean(oa, oa.shape[-1]) * ag_ref[...]).astype(BF16)
    h = x_ref[...] + jnp.dot(oan, woa_ref[...], preferred_element_type=F32)
    h = h + jnp.dot(on_ref[...], wor_ref[...], preferred_element_type=F32)
    hn = (h * _rsqrt_mean(h, h.shape[-1]) * ln2_ref[...]).astype(BF16)
    d_ff = wg_ref.shape[1]
    acc = h
    for c in range(d_ff // ff_chunk):
        cs = slice(c * ff_chunk, (c + 1) * ff_chunk)
        gt = jnp.dot(hn, wg_ref[:, cs], preferred_element_type=F32)
        up = jnp.dot(hn, wu_ref[:, cs], preferred_element_type=F32)
        act = (gt * jax.nn.sigmoid(gt) * up).astype(BF16)
        acc = acc + jnp.dot(act, wd_ref[cs, :], preferred_element_type=F32)
    o_ref[...] = acc


def _out_ffn(x2, oa2, on2, ag, woa, wor, ln2, wg, wu, wd, tm):
    n, d = x2.shape
    d_ff = wg.shape[1]
    ff_chunk = d_ff
    row = lambda i: (i, 0)
    single = pl.Buffered(1)
    wspec = lambda w: pl.BlockSpec(w.shape, lambda i: (0, 0), pipeline_mode=single)
    return pl.pallas_call(
        functools.partial(_out_kernel, ff_chunk=ff_chunk),
        grid=(n // tm,),
        in_specs=[
            pl.BlockSpec((tm, d), row),
            pl.BlockSpec((tm, oa2.shape[1]), row),
            pl.BlockSpec((tm, on2.shape[1]), row),
            wspec(ag), wspec(woa), wspec(wor), wspec(ln2), wspec(wg), wspec(wu), wspec(wd),
        ],
        out_specs=pl.BlockSpec((tm, d), row),
        out_shape=jax.ShapeDtypeStruct((n, d), F32),
        compiler_params=pltpu.CompilerParams(
            dimension_semantics=("arbitrary",), vmem_limit_bytes=VMEM_LIMIT),
        name="out_ffn",
    )(x2, oa2, on2, ag, woa, wor, ln2, wg, wu, wd)


def _rope_tables(t_len):
    half = ROPE // 2
    freqs = 1.0 / (THETA ** (np.arange(half, dtype=np.float64) / half))
    ang = np.arange(t_len, dtype=np.float64)[:, None] * freqs[None, :]
    pad = LANES - HEAD_QK
    cos_t = np.concatenate([np.ones((t_len, NOPE)), np.cos(ang), np.cos(ang), np.zeros((t_len, pad))], -1)
    sin_t = np.concatenate([np.zeros((t_len, NOPE)), np.sin(ang), np.sin(ang), np.zeros((t_len, pad))], -1)
    return jnp.asarray(cos_t, F32), jnp.asarray(sin_t, F32)


def _rot_half_cols(w):
    half = ROPE // 2
    return jnp.concatenate([-w[..., half:], w[..., :half]], axis=-1)


def _gain_rows(g):
    half = ROPE // 2
    pad = jnp.zeros((LANES - HEAD_QK,), F32)
    r0 = jnp.concatenate([g, pad])
    r1 = jnp.concatenate([jnp.zeros((NOPE,), F32), g[NOPE + half:], g[NOPE:NOPE + half], pad])
    return jnp.stack([r0, r1])


def _block_diag_pairs(w):
    g, n, _ = w.shape
    z = jnp.zeros((g // 2, n, n), w.dtype)
    top = jnp.concatenate([w[0::2], z], axis=2)
    bot = jnp.concatenate([z, w[1::2]], axis=2)
    return jnp.concatenate([top, bot], axis=1)


def kernel(x, meta_tokens, ln1_g, w_in, q_a_norm_g, w_uq, kv_a_norm_g, w_ukv, q_norm_g, k_norm_g,
           conv_w, conv_b, lru_wa, lru_ba, lru_wi, lru_bi, lru_lambda, attn_out_g, rnn_out_g,
           w_out, ln2_g, w_gate, w_up, w_down):
    bsz, seq, d_model = x.shape
    q_lora = q_a_norm_g.shape[-1]
    kv_lora = kv_a_norm_g.shape[-1]
    d_rnn = conv_w.shape[-1]
    d_attn = N_HEAD * HEAD_V
    t_len = N_META_TOK + seq
    l = 0

    wi = w_in[l]
    o_kr = q_lora + kv_lora
    w_kr = wi[:, o_kr:o_kr + ROPE]
    win = jnp.concatenate(
        [wi[:, :o_kr], jnp.zeros((d_model, NOPE), F32), w_kr, _rot_half_cols(w_kr),
         wi[:, o_kr + ROPE:]], axis=1).astype(BF16)
    wq = w_uq[l].reshape(q_lora, N_HEAD, HEAD_QK)
    wuq = jnp.concatenate([wq, _rot_half_cols(wq[..., NOPE:])], axis=-1)
    wuq = wuq.reshape(q_lora, N_HEAD * LANES).astype(BF16)
    wkv = w_ukv[l].reshape(kv_lora, N_HEAD, NOPE + HEAD_V)
    wuk = jnp.concatenate([wkv[..., :NOPE], jnp.zeros((kv_lora, N_HEAD, LANES - NOPE), F32)], -1)
    wuk = wuk.reshape(kv_lora, N_HEAD * LANES).astype(BF16)
    wuv = wkv[..., NOPE:].reshape(kv_lora, d_attn).astype(BF16)
    gq = _gain_rows(q_norm_g[l])
    gk = _gain_rows(k_norm_g[l])
    cos_t, sin_t = _rope_tables(t_len)
    lane2 = np.arange(2 * LANES)
    hsum = jnp.asarray((lane2[:, None] // LANES == lane2[None, :] // LANES)
                       & (lane2[:, None] % LANES < HEAD_QK), BF16)
    proj_w = (ln1_g[l][None], win, q_a_norm_g[l][None], wuq, kv_a_norm_g[l][None], wuk, wuv, gq, gk, hsum)
    dims = (q_lora, kv_lora, d_rnn)

    wgate = (0.5 * jnp.concatenate(
        [_block_diag_pairs(w) for w in (lru_wa[l, 0], lru_wi[l, 0], lru_wa[l, 1], lru_wi[l, 1])],
        axis=2)).astype(BF16)
    n_slab = d_rnn // LANES
    gate_b = 0.5 * jnp.stack([lru_ba[l, 0], lru_bi[l, 0], lru_ba[l, 1], lru_bi[l, 1]])
    gate_b = gate_b.reshape(4, n_slab, LANES).transpose(1, 0, 2).reshape(n_slab, 1, 4 * LANES)
    b_hi = gate_b.astype(BF16)
    b_lo = (gate_b - b_hi.astype(F32)).astype(BF16)
    wgate = jnp.concatenate(
        [wgate, b_hi, b_lo, jnp.zeros((n_slab, LANES - 2, 4 * LANES), BF16)], axis=1)

    tm = 512 if seq % 512 == 0 else seq
    q, k, v, xr, xg = _project(x, tm, cos_t[N_META_TOK:], sin_t[N_META_TOK:], proj_w, dims)
    _, km, vm, xrm, _ = _project(meta_tokens[None].astype(x.dtype), N_META_TOK,
                                 cos_t[:N_META_TOK], sin_t[:N_META_TOK], proj_w, dims)

    o_rnn = _rnn(xr, xg, xrm, conv_w[l], conv_b[l][None], wgate, lru_lambda[l], rnn_out_g[l][None])

    tq = 1024 if seq % 1024 == 0 else seq
    o_attn = _attention(q, k, v, km, vm, tq)

    wo = w_out[l].astype(BF16)
    n_rows = bsz * seq
    tmo = 512 if n_rows % 512 == 0 else n_rows
    out = _out_ffn(x.reshape(n_rows, d_model), o_attn.reshape(n_rows, d_attn),
                   o_rnn.reshape(n_rows, d_rnn), attn_out_g[l][None], wo[:d_attn], wo[d_attn:],
                   ln2_g[l][None], w_gate[l].astype(BF16), w_up[l].astype(BF16),
                   w_down[l].astype(BF16), tmo)
    return out.reshape(bsz, seq, d_model)
```

```python
import functools
import math

import numpy as np

import jax
import jax.numpy as jnp
from jax import lax
from jax.experimental import pallas as pl
from jax.experimental.pallas import tpu as pltpu

F32 = jnp.float32
BF16 = jnp.bfloat16

N_META_TOK = 16
N_HEAD = 8
NOPE = 64
ROPE = 32
HEAD_QK = NOPE + ROPE
HEAD_V = 64
VT_ROWS = HEAD_V + 16
LRU_SCALE = 8.0
THETA = 10000.0
NORM_EPS = 1e-6
CONV_TAPS = 4
CONV_LEFT = 2

LANES = 128
SUBLANES = 8
VMEM_LIMIT = 56 * 1024 * 1024
KEY_TILE = 256
SCAN_UNROLL = 6
COMBINE_UNROLL = 43
HEADS_PER_STEP = 4
SCORE_LOOKAHEAD = 0

LOG2E = 1.4426950408889634


def _rsqrt_mean(x, n):
    return lax.rsqrt(jnp.sum(x * x, axis=-1, keepdims=True) * (1.0 / n) + NORM_EPS)


def _proj_kernel(x_ref, ln1_ref, win_ref, qag_ref, wuq_ref, kvag_ref, wuk_ref, wuv_ref,
                 cos_ref, sin_ref, gq_ref, gk_ref, hsum_ref,
                 q_out, k_out, v_out, xr_out, xg_out, *, q_lora, kv_lora, d_rnn):
    x = x_ref[0]
    d_model = x.shape[-1]
    hn = (x * _rsqrt_mean(x, d_model) * ln1_ref[...]).astype(BF16)
    o_kv = q_lora
    o_kr = o_kv + kv_lora
    o_xr = o_kr + LANES
    p = jnp.dot(hn, win_ref[:, :o_xr], preferred_element_type=F32)
    c_q = p[:, :o_kv]
    c_kv = p[:, o_kv:o_kr]
    kr = p[:, o_kr:o_xr]
    n_slab = d_rnn // LANES

    def recurrent_inputs(c):
        px = jnp.dot(hn, win_ref[:, o_xr + 2 * c * LANES:o_xr + 2 * (c + 1) * LANES],
                     preferred_element_type=F32)
        for u in range(2):
            slab = 2 * c + u
            dst = xr_out if slab < n_slab else xg_out
            dst[0, slab % n_slab] = px[:, u * LANES:(u + 1) * LANES]

    cos_t = cos_ref[...]
    sin_t = sin_ref[...]
    eps_hd = HEAD_QK * NORM_EPS
    root_hd = math.sqrt(HEAD_QK)
    hsum = hsum_ref[...]

    def head_sums(sq_pair):
        return jnp.dot(sq_pair.astype(BF16), hsum, preferred_element_type=F32)

    cqn = (c_q * _rsqrt_mean(c_q, q_lora) * qag_ref[...]).astype(BF16)
    q = jnp.dot(cqn, wuq_ref[...], preferred_element_type=F32)
    q_gain = root_hd * (HEAD_QK ** -0.5) * LOG2E
    cq = cos_t * (gq_ref[0:1, :] * q_gain)
    sq = sin_t * (gq_ref[1:2, :] * q_gain)

    ckn = (c_kv * _rsqrt_mean(c_kv, kv_lora) * kvag_ref[...]).astype(BF16)
    kn = jnp.dot(ckn, wuk_ref[...], preferred_element_type=F32)
    vt = jnp.dot(ckn, wuv_ref[...], preferred_element_type=F32).astype(BF16).T
    tm = vt.shape[1]
    ones_rows = (lax.broadcasted_iota(jnp.int32, (VT_ROWS - HEAD_V, tm), 0) == 0).astype(BF16)
    for h in range(N_HEAD):
        v_out[0, h * VT_ROWS:h * VT_ROWS + HEAD_V, :] = vt[h * HEAD_V:(h + 1) * HEAD_V]
        v_out[0, h * VT_ROWS + HEAD_V:(h + 1) * VT_ROWS, :] = ones_rows
    ck = cos_t * (gk_ref[0:1, :] * root_hd)
    sk = sin_t * (gk_ref[1:2, :] * root_hd)
    kr_sq = kr * kr
    kr_sq2 = jnp.concatenate([kr_sq, kr_sq], axis=1)
    kr_rot = kr * ck + pltpu.roll(kr, LANES - ROPE, axis=1) * sk
    n_pair = N_HEAD // 2
    n_rec = n_slab
    for pr in range(n_pair):
        for c in range(pr * n_rec // n_pair, (pr + 1) * n_rec // n_pair):
            recurrent_inputs(c)
        qp = q[:, 2 * pr * LANES:2 * (pr + 1) * LANES]
        sc = lax.rsqrt(head_sums(qp * qp) + eps_hd)
        for hh in range(2):
            h = 2 * pr + hh
            qh = qp[:, hh * LANES:(hh + 1) * LANES]
            qr = pltpu.roll(qh, LANES - ROPE, axis=1)
            qf = (qh * cq + qr * sq) * sc[:, hh * LANES:(hh + 1) * LANES]
            q_out[0, h * LANES:(h + 1) * LANES, :] = qf.astype(BF16).T
        kp = kn[:, 2 * pr * LANES:2 * (pr + 1) * LANES]
        sc = lax.rsqrt(head_sums(kp * kp + kr_sq2) + eps_hd)
        for hh in range(2):
            h = 2 * pr + hh
            kh = kp[:, hh * LANES:(hh + 1) * LANES]
            k_out[0, :, h * LANES:(h + 1) * LANES] = (
                (kh * ck + kr_rot) * sc[:, hh * LANES:(hh + 1) * LANES]).astype(BF16)


def _const_spec(shape):
    nd = len(shape)
    return pl.BlockSpec(shape, lambda *_: (0,) * nd)


def _project(x3, tm, cos_t, sin_t, wts, dims):
    q_lora, kv_lora, d_rnn = dims
    b, s, d = x3.shape
    n_slab = d_rnn // LANES
    nt = s // tm
    ln1, win, qag, wuq, kvag, wuk, wuv, gq, gk, hsum = wts
    row3 = lambda i, j: (i, j, 0)
    row4 = lambda i, j: (i, 0, j, 0)
    tab = lambda i, j: (j, 0)
    in_specs = [pl.BlockSpec((1, tm, d), row3)]
    in_specs += [_const_spec(w.shape) for w in (ln1, win, qag, wuq, kvag, wuk, wuv)]
    in_specs += [pl.BlockSpec((tm, LANES), tab), pl.BlockSpec((tm, LANES), tab)]
    in_specs += [_const_spec(gq.shape), _const_spec(gk.shape), _const_spec(hsum.shape)]
    col3 = lambda i, j: (i, 0, j)
    out_shape = (
        jax.ShapeDtypeStruct((b, N_HEAD * LANES, s), BF16),
        jax.ShapeDtypeStruct((b, s, N_HEAD * LANES), BF16),
        jax.ShapeDtypeStruct((b, N_HEAD * VT_ROWS, s), BF16),
        jax.ShapeDtypeStruct((b, n_slab, s, LANES), F32),
        jax.ShapeDtypeStruct((b, n_slab, s, LANES), F32),
    )
    out_specs = (
        pl.BlockSpec((1, N_HEAD * LANES, tm), col3),
        pl.BlockSpec((1, tm, N_HEAD * LANES), row3),
        pl.BlockSpec((1, N_HEAD * VT_ROWS, tm), col3),
        pl.BlockSpec((1, n_slab, tm, LANES), row4),
        pl.BlockSpec((1, n_slab, tm, LANES), row4),
    )
    return pl.pallas_call(
        functools.partial(_proj_kernel, q_lora=q_lora, kv_lora=kv_lora, d_rnn=d_rnn),
        grid=(b, nt),
        in_specs=in_specs,
        out_specs=out_specs,
        out_shape=out_shape,
        compiler_params=pltpu.CompilerParams(
            dimension_semantics=("arbitrary", "arbitrary"), vmem_limit_bytes=VMEM_LIMIT),
        name="proj",
    )(x3, ln1, win, qag, wuq, kvag, wuk, wuv, cos_t, sin_t, gq, gk, hsum)


def _scan_stride(t_len):
    s = -(-t_len // SUBLANES)
    while s % SUBLANES == 0:
        s += 1
    if s % 2:
        s += 1
        if s % SUBLANES == 0:
            s += 2
    return s


def _chunk_len(stride, cap=258):
    for c in range(min(cap, stride), 0, -1):
        if stride % c == 0:
            return c
    return 1


def _rnn_kernel(xr_ref, xg_ref, xrm_ref, cw_ref, cb_ref, wg_ref, lam_ref, g_ref,
                o_ref, t1, af, bf, ab, bb, *, seq, stride, rc, tm_out):
    n_slab = xr_ref.shape[1]
    t_len = N_META_TOK + seq
    pre = SUBLANES
    rows = t1.shape[1]
    n_chunk = stride // rc
    crow = rc * SUBLANES

    for s in range(n_slab):
        t1[s, 0:pre, :] = jnp.zeros((pre, LANES), F32)
        t1[s, pre:pre + N_META_TOK, :] = xrm_ref[0, s]
        t1[s, pre + N_META_TOK:pre + t_len, :] = xr_ref[0, s]
        t1[s, pre + t_len:rows, :] = jnp.zeros((rows - pre - t_len, LANES), F32)

    lam = lam_ref[...]
    nl = -lam
    softplus = jnp.maximum(nl, 0.0) + jnp.log(1.0 + jnp.exp(-jnp.abs(nl)))
    hcn = (0.5 * LRU_SCALE) * softplus
    cw = cw_ref[...]
    cb = cb_ref[...]
    ones_k = jnp.ones((crow, LANES), BF16)

    padded = stride * SUBLANES > t_len
    if padded:
        ridx = lax.broadcasted_iota(jnp.int32, (crow, LANES), 0)
        t_local = (ridx & (SUBLANES - 1)) * stride + (ridx >> 3)

    def gate_chunk(c, carry):
        r0 = c * rc
        row0 = pl.multiple_of(c * crow, SUBLANES)
        for s in range(n_slab):
            ls = slice(s * LANES, (s + 1) * LANES)
            taps = [t1[s, pl.ds(r0 + pre - CONV_LEFT + j, SUBLANES, stride=stride), :]
                    for j in range(rc + CONV_TAPS - 1)]
            xc_rows = []
            for rr in range(rc):
                acc = taps[rr] * cw[0:1, ls]
                for j in range(1, CONV_TAPS):
                    acc = acc + taps[rr + j] * cw[j:j + 1, ls]
                xc_rows.append(acc + cb[:, ls])
            xc = jnp.concatenate(xc_rows, axis=0)
            hx = 0.5 * xc
            z = jnp.dot(jnp.concatenate([xc.astype(BF16), ones_k], axis=1), wg_ref[s],
                        preferred_element_type=F32)
            for d, (a_ref, b_ref) in enumerate(((af, bf), (ab, bb))):
                ta = jnp.tanh(z[:, (2 * d) * LANES:(2 * d + 1) * LANES])
                ti = jnp.tanh(z[:, (2 * d + 1) * LANES:(2 * d + 2) * LANES])
                hc = hcn[d:d + 1, ls]
                nla = ta * hc + hc
                a = jnp.exp2(nla * (-LOG2E))
                one_m_a2 = jnp.tanh(nla) * (1.0 + a * a)
                root = jnp.where(one_m_a2 > 0.0, one_m_a2 * lax.rsqrt(one_m_a2), 0.0)
                bv = root * (ti * hx + hx)
                if d == 1 and padded:
                    bv = jnp.where((t_local + r0) < t_len, bv, 0.0)
                a_ref[s, pl.ds(row0, crow), :] = a
                b_ref[s, pl.ds(row0, crow), :] = bv
        return carry

    lax.fori_loop(0, n_chunk, gate_chunk, 0)

    zero = jnp.zeros((SUBLANES, LANES), F32)
    one = jnp.ones((SUBLANES, LANES), F32)

    def scan_step(r, carry):
        hf, pf, hb, pb = carry
        rf = pl.multiple_of(r * SUBLANES, SUBLANES)
        rb = pl.multiple_of((stride - 1 - r) * SUBLANES, SUBLANES)
        nhf, npf, nhb, npb = [], [], [], []
        for s in range(n_slab):
            a = af[s, pl.ds(rf, SUBLANES), :]
            h = a * hf[s] + bf[s, pl.ds(rf, SUBLANES), :]
            pr = a * pf[s]
            bf[s, pl.ds(rf, SUBLANES), :] = h
            af[s, pl.ds(rf, SUBLANES), :] = pr
            nhf.append(h)
            npf.append(pr)
            a = ab[s, pl.ds(rb, SUBLANES), :]
            h = a * hb[s] + bb[s, pl.ds(rb, SUBLANES), :]
            pr = a * pb[s]
            bb[s, pl.ds(rb, SUBLANES), :] = h
            ab[s, pl.ds(rb, SUBLANES), :] = pr
            nhb.append(h)
            npb.append(pr)
        return tuple(nhf), tuple(npf), tuple(nhb), tuple(npb)

    init = ((zero,) * n_slab, (one,) * n_slab, (zero,) * n_slab, (one,) * n_slab)
    hf, pf, hb, pb = lax.fori_loop(0, stride, scan_step, init, unroll=SCAN_UNROLL)

    sub = lax.broadcasted_iota(jnp.int32, (SUBLANES, LANES), 0)
    cf, cbk = [], []
    for s in range(n_slab):
        c = zero
        for _ in range(SUBLANES - 1):
            c = jnp.where(sub == 0, 0.0, pltpu.roll(pf[s] * c + hf[s], 1, axis=0))
        cf.append(c)
        c = zero
        for _ in range(SUBLANES - 1):
            c = jnp.where(sub == SUBLANES - 1, 0.0, pltpu.roll(pb[s] * c + hb[s], SUBLANES - 1, axis=0))
        cbk.append(c)

    def combine(r, carry):
        rp = pl.multiple_of(r * SUBLANES, SUBLANES)
        for s in range(n_slab):
            y = (bf[s, pl.ds(rp, SUBLANES), :] + af[s, pl.ds(rp, SUBLANES), :] * cf[s]
                 + bb[s, pl.ds(rp, SUBLANES), :] + ab[s, pl.ds(rp, SUBLANES), :] * cbk[s])
            t1[s, pl.ds(r + pre, SUBLANES, stride=stride), :] = y
        return carry

    lax.fori_loop(0, stride, combine, 0, unroll=COMBINE_UNROLL)

    d_rnn = n_slab * LANES
    g = g_ref[...]
    k0 = math.sqrt(2.0 / math.pi)

    def finish(j, carry):
        ro = pl.multiple_of(j * tm_out, 2 * SUBLANES)
        rt = pl.multiple_of(j * tm_out + pre + N_META_TOK, SUBLANES)
        ys = []
        ssq = jnp.zeros((tm_out, 1), F32)
        for s in range(n_slab):
            y = t1[s, pl.ds(rt, tm_out), :]
            xg = xg_ref[0, s, pl.ds(ro, tm_out), :]
            yh = y * (0.5 * xg)
            y = yh + yh * jnp.tanh(xg * (k0 + (k0 * 0.044715) * (xg * xg)))
            ssq = ssq + jnp.sum(y * y, axis=-1, keepdims=True)
            ys.append(y)
        sc = lax.rsqrt(ssq * (1.0 / d_rnn) + NORM_EPS)
        for s in range(n_slab):
            ls = slice(s * LANES, (s + 1) * LANES)
            o_ref[0, pl.ds(ro, tm_out), ls] = (ys[s] * sc * g[:, ls]).astype(BF16)
        return carry

    lax.fori_loop(0, seq // tm_out, finish, 0)


def _rnn(xr, xg, xrm, cw, cb, wg, lam, g):
    b, n_slab, seq, _ = xr.shape
    t_len = N_META_TOK + seq
    stride = _scan_stride(t_len)
    rc = _chunk_len(stride)
    rows_perm = stride * SUBLANES
    rows_time = SUBLANES + rows_perm + SUBLANES
    tm_out = 512 if seq % 512 == 0 else seq
    blk = lambda i: (i, 0, 0, 0)
    in_specs = [
        pl.BlockSpec((1, n_slab, seq, LANES), blk),
        pl.BlockSpec((1, n_slab, seq, LANES), blk),
        _const_spec(xrm.shape),
    ] + [_const_spec(w.shape) for w in (cw, cb, wg, lam, g)]
    scratch = [pltpu.VMEM((n_slab, rows_time, LANES), F32)]
    scratch += [pltpu.VMEM((n_slab, rows_perm, LANES), F32) for _ in range(4)]
    return pl.pallas_call(
        functools.partial(_rnn_kernel, seq=seq, stride=stride, rc=rc, tm_out=tm_out),
        grid=(b,),
        in_specs=in_specs,
        out_specs=pl.BlockSpec((1, seq, n_slab * LANES), lambda i: (i, 0, 0)),
        out_shape=jax.ShapeDtypeStruct((b, seq, n_slab * LANES), BF16),
        scratch_shapes=scratch,
        compiler_params=pltpu.CompilerParams(
            dimension_semantics=("arbitrary",), vmem_limit_bytes=VMEM_LIMIT),
        name="rnn",
    )(xr, xg, xrm, cw, cb, wg, lam, g)


def _attn_kernel(qt_ref, k_ref, vt_ref, km_ref, vtm_ref, o_ref, *, tq):
    n_key = k_ref.shape[1]

    n_stream = HEADS_PER_STEP * (n_key // tq)
    n_tile = n_key // KEY_TILE
    tiles = [(t * KEY_TILE, (t + 1) * KEY_TILE) for t in range(n_tile)]
    tiles[-1] = (tiles[-1][0], n_key + N_META_TOK)
    steps = [(i, t) for i in range(n_stream) for t in range(n_tile)]

    def score_tile(i, t):
        j, hh = divmod(i, HEADS_PER_STEP)
        k0, k1 = tiles[t]
        ls = slice(hh * LANES, (hh + 1) * LANES)
        keys = k_ref[0, k0:min(k1, n_key), ls]
        if k1 > n_key:
            keys = jnp.concatenate([keys, km_ref[0, :, ls]], axis=0)
        return jnp.dot(keys, qt_ref[0, ls, j * tq:(j + 1) * tq],
                       preferred_element_type=F32)

    outs = []
    pending = [score_tile(*steps[n]) for n in range(SCORE_LOOKAHEAD)]
    m, acc = None, None
    for n, (i, t) in enumerate(steps):
        if n + SCORE_LOOKAHEAD < len(steps):
            pending.append(score_tile(*steps[n + SCORE_LOOKAHEAD]))
        st = pending.pop(0)
        k0, k1 = tiles[t]
        hh = i % HEADS_PER_STEP
        vs = slice(hh * VT_ROWS, (hh + 1) * VT_ROWS)
        st3 = st.reshape((k1 - k0) // SUBLANES, SUBLANES, tq)
        tmax = jnp.max(jnp.max(st3, axis=0), axis=0, keepdims=True)
        m_new = tmax if m is None else jnp.maximum(m, tmax)
        mb = jnp.broadcast_to(m_new, (SUBLANES, tq))
        pb = jnp.exp2(st3 - mb[None]).reshape(k1 - k0, tq).astype(BF16)
        part = jnp.dot(vt_ref[0, vs, k0:k0 + KEY_TILE], pb[:KEY_TILE], preferred_element_type=F32)
        if k1 - k0 > KEY_TILE:
            part = part + jnp.dot(vtm_ref[0, vs, :], pb[KEY_TILE:], preferred_element_type=F32)
        acc = part if m is None else acc * jnp.exp2(m - m_new) + part
        m = m_new
        if t == n_tile - 1:
            outs.append(acc[:HEAD_V] * (1.0 / acc[HEAD_V:HEAD_V + 1]))
            m, acc = None, None
            if hh == HEADS_PER_STEP - 1:
                j = i // HEADS_PER_STEP
                o_ref[0, j * tq:(j + 1) * tq, :] = jnp.concatenate(outs, axis=0).T
                outs = []


def _attention(qt, k, vt, km, vtm, tq):
    b, s, _ = k.shape
    hps = HEADS_PER_STEP
    return pl.pallas_call(
        functools.partial(_attn_kernel, tq=tq),
        grid=(b, N_HEAD // hps),
        in_specs=[
            pl.BlockSpec((1, hps * LANES, s), lambda i, p: (i, p, 0)),
            pl.BlockSpec((1, s, hps * LANES), lambda i, p: (i, 0, p)),
            pl.BlockSpec((1, hps * VT_ROWS, s), lambda i, p: (i, p, 0)),
            pl.BlockSpec((1, N_META_TOK, hps * LANES), lambda i, p: (0, 0, p)),
            pl.BlockSpec((1, hps * VT_ROWS, N_META_TOK), lambda i, p: (0, p, 0)),
        ],
        out_specs=pl.BlockSpec((1, s, hps * HEAD_V), lambda i, p: (i, 0, p)),
        out_shape=jax.ShapeDtypeStruct((b, s, N_HEAD * HEAD_V), F32),
        compiler_params=pltpu.CompilerParams(
            dimension_semantics=("arbitrary", "arbitrary"), vmem_limit_bytes=VMEM_LIMIT),
        name="attn",
    )(qt, k, vt, km, vtm)


def _out_kernel(x_ref, oa_ref, on_ref, ag_ref, woa_ref, wor_ref, ln2_ref, wg_ref, wu_ref, wd_ref,
                o_ref, *, ff_chunk):
    oa = oa_ref[...]
    oan = (oa * _rsqrt_mean(oa, oa.shape[-1]) * ag_ref[...]).astype(BF16)
    h = x_ref[...] + jnp.dot(oan, woa_ref[...], preferred_element_type=F32)
    h = h + jnp.dot(on_ref[...], wor_ref[...], preferred_element_type=F32)
    hn = (h * _rsqrt_mean(h, h.shape[-1]) * ln2_ref[...]).astype(BF16)
    d_ff = wg_ref.shape[1]
    acc = h
    for c in range(d_ff // ff_chunk):
        cs = slice(c * ff_chunk, (c + 1) * ff_chunk)
        gt = jnp.dot(hn, wg_ref[:, cs], preferred_element_type=F32)
        up = jnp.dot(hn, wu_ref[:, cs], preferred_element_type=F32)
        act = (gt * jax.nn.sigmoid(gt) * up).astype(BF16)
        acc = acc + jnp.dot(act, wd_ref[cs, :], preferred_element_type=F32)
    o_ref[...] = acc


def _out_ffn(x2, oa2, on2, ag, woa, wor, ln2, wg, wu, wd, tm):
    n, d = x2.shape
    d_ff = wg.shape[1]
    ff_chunk = d_ff
    row = lambda i: (i, 0)
    single = pl.Buffered(1)
    wspec = lambda w: pl.BlockSpec(w.shape, lambda i: (0, 0), pipeline_mode=single)
    return pl.pallas_call(
        functools.partial(_out_kernel, ff_chunk=ff_chunk),
        grid=(n // tm,),
        in_specs=[
            pl.BlockSpec((tm, d), row),
            pl.BlockSpec((tm, oa2.shape[1]), row),
            pl.BlockSpec((tm, on2.shape[1]), row),
            wspec(ag), wspec(woa), wspec(wor), wspec(ln2), wspec(wg), wspec(wu), wspec(wd),
        ],
        out_specs=pl.BlockSpec((tm, d), row),
        out_shape=jax.ShapeDtypeStruct((n, d), F32),
        compiler_params=pltpu.CompilerParams(
            dimension_semantics=("arbitrary",), vmem_limit_bytes=VMEM_LIMIT),
        name="out_ffn",
    )(x2, oa2, on2, ag, woa, wor, ln2, wg, wu, wd)


def _rope_tables(t_len):
    half = ROPE // 2
    freqs = 1.0 / (THETA ** (np.arange(half, dtype=np.float64) / half))
    ang = np.arange(t_len, dtype=np.float64)[:, None] * freqs[None, :]
    pad = LANES - HEAD_QK
    cos_t = np.concatenate([np.ones((t_len, NOPE)), np.cos(ang), np.cos(ang), np.zeros((t_len, pad))], -1)
    sin_t = np.concatenate([np.zeros((t_len, NOPE)), np.sin(ang), np.sin(ang), np.zeros((t_len, pad))], -1)
    return jnp.asarray(cos_t, F32), jnp.asarray(sin_t, F32)


def _rot_half_cols(w):
    half = ROPE // 2
    return jnp.concatenate([-w[..., half:], w[..., :half]], axis=-1)


def _gain_rows(g):
    half = ROPE // 2
    pad = jnp.zeros((LANES - HEAD_QK,), F32)
    r0 = jnp.concatenate([g, pad])
    r1 = jnp.concatenate([jnp.zeros((NOPE,), F32), g[NOPE + half:], g[NOPE:NOPE + half], pad])
    return jnp.stack([r0, r1])


def _block_diag_pairs(w):
    g, n, _ = w.shape
    z = jnp.zeros((g // 2, n, n), w.dtype)
    top = jnp.concatenate([w[0::2], z], axis=2)
    bot = jnp.concatenate([z, w[1::2]], axis=2)
    return jnp.concatenate([top, bot], axis=1)


def kernel(x, meta_tokens, ln1_g, w_in, q_a_norm_g, w_uq, kv_a_norm_g, w_ukv, q_norm_g, k_norm_g,
           conv_w, conv_b, lru_wa, lru_ba, lru_wi, lru_bi, lru_lambda, attn_out_g, rnn_out_g,
           w_out, ln2_g, w_gate, w_up, w_down):
    bsz, seq, d_model = x.shape
    q_lora = q_a_norm_g.shape[-1]
    kv_lora = kv_a_norm_g.shape[-1]
    d_rnn = conv_w.shape[-1]
    d_attn = N_HEAD * HEAD_V
    t_len = N_META_TOK + seq
    l = 0

    wi = w_in[l]
    o_kr = q_lora + kv_lora
    w_kr = wi[:, o_kr:o_kr + ROPE]
    win = jnp.concatenate(
        [wi[:, :o_kr], jnp.zeros((d_model, NOPE), F32), w_kr, _rot_half_cols(w_kr),
         wi[:, o_kr + ROPE:]], axis=1).astype(BF16)
    wq = w_uq[l].reshape(q_lora, N_HEAD, HEAD_QK)
    wuq = jnp.concatenate([wq, _rot_half_cols(wq[..., NOPE:])], axis=-1)
    wuq = wuq.reshape(q_lora, N_HEAD * LANES).astype(BF16)
    wkv = w_ukv[l].reshape(kv_lora, N_HEAD, NOPE + HEAD_V)
    wuk = jnp.concatenate([wkv[..., :NOPE], jnp.zeros((kv_lora, N_HEAD, LANES - NOPE), F32)], -1)
    wuk = wuk.reshape(kv_lora, N_HEAD * LANES).astype(BF16)
    wuv = wkv[..., NOPE:].reshape(kv_lora, d_attn).astype(BF16)
    gq = _gain_rows(q_norm_g[l])
    gk = _gain_rows(k_norm_g[l])
    cos_t, sin_t = _rope_tables(t_len)
    lane2 = np.arange(2 * LANES)
    hsum = jnp.asarray((lane2[:, None] // LANES == lane2[None, :] // LANES)
                       & (lane2[:, None] % LANES < HEAD_QK), BF16)
    proj_w = (ln1_g[l][None], win, q_a_norm_g[l][None], wuq, kv_a_norm_g[l][None], wuk, wuv, gq, gk, hsum)
    dims = (q_lora, kv_lora, d_rnn)

    wgate = (0.5 * jnp.concatenate(
        [_block_diag_pairs(w) for w in (lru_wa[l, 0], lru_wi[l, 0], lru_wa[l, 1], lru_wi[l, 1])],
        axis=2)).astype(BF16)
    n_slab = d_rnn // LANES
    gate_b = 0.5 * jnp.stack([lru_ba[l, 0], lru_bi[l, 0], lru_ba[l, 1], lru_bi[l, 1]])
    gate_b = gate_b.reshape(4, n_slab, LANES).transpose(1, 0, 2).reshape(n_slab, 1, 4 * LANES)
    b_hi = gate_b.astype(BF16)
    b_lo = (gate_b - b_hi.astype(F32)).astype(BF16)
    wgate = jnp.concatenate(
        [wgate, b_hi, b_lo, jnp.zeros((n_slab, LANES - 2, 4 * LANES), BF16)], axis=1)

    tm = 512 if seq % 512 == 0 else seq
    q, k, v, xr, xg = _project(x, tm, cos_t[N_META_TOK:], sin_t[N_META_TOK:], proj_w, dims)
    _, km, vm, xrm, _ = _project(meta_tokens[None].astype(x.dtype), N_META_TOK,
                                 cos_t[:N_META_TOK], sin_t[:N_META_TOK], proj_w, dims)

    o_rnn = _rnn(xr, xg, xrm, conv_w[l], conv_b[l][None], wgate, lru_lambda[l], rnn_out_g[l][None])

    tq = 1024 if seq % 1024 == 0 else seq
    o_attn = _attention(q, k, v, km, vm, tq)

    wo = w_out[l].astype(BF16)
    n_rows = bsz * seq
    tmo = 512 if n_rows % 512 == 0 else n_rows
    out = _out_ffn(x.reshape(n_rows, d_model), o_attn.reshape(n_rows, d_attn),
                   o_rnn.reshape(n_rows, d_rnn), attn_out_g[l][None], wo[:d_attn], wo[d_attn:],
                   ln2_g[l][None], w_gate[l].astype(BF16), w_up[l].astype(BF16),
                   w_down[l].astype(BF16), tmo)
    return out.reshape(bsz, seq, d_model)
```

```python
import functools
import math

import numpy as np

import jax
import jax.numpy as jnp
from jax import lax
from jax.experimental import pallas as pl
from jax.experimental.pallas import tpu as pltpu

F32 = jnp.float32
BF16 = jnp.bfloat16

N_META_TOK = 16
N_HEAD = 8
NOPE = 64
ROPE = 32
HEAD_QK = NOPE + ROPE
HEAD_V = 64
VT_ROWS = HEAD_V + 16
LRU_SCALE = 8.0
THETA = 10000.0
NORM_EPS = 1e-6
CONV_TAPS = 4
CONV_LEFT = 2

LANES = 128
SUBLANES = 8
VMEM_LIMIT = 56 * 1024 * 1024
KEY_TILE = 256
SCAN_UNROLL = 6
COMBINE_UNROLL = 43
HEADS_PER_STEP = 4
SCORE_LOOKAHEAD = 1

LOG2E = 1.4426950408889634


def _rsqrt_mean(x, n):
    return lax.rsqrt(jnp.sum(x * x, axis=-1, keepdims=True) * (1.0 / n) + NORM_EPS)


def _proj_kernel(x_ref, ln1_ref, win_ref, qag_ref, wuq_ref, kvag_ref, wuk_ref, wuv_ref,
                 cos_ref, sin_ref, gq_ref, gk_ref, hsum_ref,
                 q_out, k_out, v_out, xr_out, xg_out, *, q_lora, kv_lora, d_rnn):
    x = x_ref[0]
    d_model = x.shape[-1]
    hn = (x * _rsqrt_mean(x, d_model) * ln1_ref[...]).astype(BF16)
    o_kv = q_lora
    o_kr = o_kv + kv_lora
    o_xr = o_kr + LANES
    p = jnp.dot(hn, win_ref[:, :o_xr], preferred_element_type=F32)
    c_q = p[:, :o_kv]
    c_kv = p[:, o_kv:o_kr]
    kr = p[:, o_kr:o_xr]
    n_slab = d_rnn // LANES

    def recurrent_inputs(c):
        px = jnp.dot(hn, win_ref[:, o_xr + 2 * c * LANES:o_xr + 2 * (c + 1) * LANES],
                     preferred_element_type=F32)
        for u in range(2):
            slab = 2 * c + u
            dst = xr_out if slab < n_slab else xg_out
            dst[0, slab % n_slab] = px[:, u * LANES:(u + 1) * LANES]

    cos_t = cos_ref[...]
    sin_t = sin_ref[...]
    eps_hd = HEAD_QK * NORM_EPS
    root_hd = math.sqrt(HEAD_QK)
    hsum = hsum_ref[...]

    def head_sums(sq_pair):
        return jnp.dot(sq_pair.astype(BF16), hsum, preferred_element_type=F32)

    cqn = (c_q * _rsqrt_mean(c_q, q_lora) * qag_ref[...]).astype(BF16)
    q = jnp.dot(cqn, wuq_ref[...], preferred_element_type=F32)
    q_gain = root_hd * (HEAD_QK ** -0.5) * LOG2E
    cq = cos_t * (gq_ref[0:1, :] * q_gain)
    sq = sin_t * (gq_ref[1:2, :] * q_gain)

    ckn = (c_kv * _rsqrt_mean(c_kv, kv_lora) * kvag_ref[...]).astype(BF16)
    kn = jnp.dot(ckn, wuk_ref[...], preferred_element_type=F32)
    vt = jnp.dot(ckn, wuv_ref[...], preferred_element_type=F32).astype(BF16).T
    tm = vt.shape[1]
    ones_rows = (lax.broadcasted_iota(jnp.int32, (VT_ROWS - HEAD_V, tm), 0) == 0).astype(BF16)
    for h in range(N_HEAD):
        v_out[0, h * VT_ROWS:h * VT_ROWS + HEAD_V, :] = vt[h * HEAD_V:(h + 1) * HEAD_V]
        v_out[0, h * VT_ROWS + HEAD_V:(h + 1) * VT_ROWS, :] = ones_rows
    ck = cos_t * (gk_ref[0:1, :] * root_hd)
    sk = sin_t * (gk_ref[1:2, :] * root_hd)
    kr_sq = kr * kr
    kr_sq2 = jnp.concatenate([kr_sq, kr_sq], axis=1)
    kr_rot = kr * ck + pltpu.roll(kr, LANES - ROPE, axis=1) * sk
    n_pair = N_HEAD // 2
    n_rec = n_slab
    for pr in range(n_pair):
        for c in range(pr * n_rec // n_pair, (pr + 1) * n_rec // n_pair):
            recurrent_inputs(c)
        qp = q[:, 2 * pr * LANES:2 * (pr + 1) * LANES]
        sc = lax.rsqrt(head_sums(qp * qp) + eps_hd)
        for hh in range(2):
            h = 2 * pr + hh
            qh = qp[:, hh * LANES:(hh + 1) * LANES]
            qr = pltpu.roll(qh, LANES - ROPE, axis=1)
            qf = (qh * cq + qr * sq) * sc[:, hh * LANES:(hh + 1) * LANES]
            q_out[0, h * LANES:(h + 1) * LANES, :] = qf.astype(BF16).T
        kp = kn[:, 2 * pr * LANES:2 * (pr + 1) * LANES]
        sc = lax.rsqrt(head_sums(kp * kp + kr_sq2) + eps_hd)
        for hh in range(2):
            h = 2 * pr + hh
            kh = kp[:, hh * LANES:(hh + 1) * LANES]
            k_out[0, :, h * LANES:(h + 1) * LANES] = (
                (kh * ck + kr_rot) * sc[:, hh * LANES:(hh + 1) * LANES]).astype(BF16)


def _const_spec(shape):
    nd = len(shape)
    return pl.BlockSpec(shape, lambda *_: (0,) * nd)


def _project(x3, tm, cos_t, sin_t, wts, dims):
    q_lora, kv_lora, d_rnn = dims
    b, s, d = x3.shape
    n_slab = d_rnn // LANES
    nt = s // tm
    ln1, win, qag, wuq, kvag, wuk, wuv, gq, gk, hsum = wts
    row3 = lambda i, j: (i, j, 0)
    row4 = lambda i, j: (i, 0, j, 0)
    tab = lambda i, j: (j, 0)
    in_specs = [pl.BlockSpec((1, tm, d), row3)]
    in_specs += [_const_spec(w.shape) for w in (ln1, win, qag, wuq, kvag, wuk, wuv)]
    in_specs += [pl.BlockSpec((tm, LANES), tab), pl.BlockSpec((tm, LANES), tab)]
    in_specs += [_const_spec(gq.shape), _const_spec(gk.shape), _const_spec(hsum.shape)]
    col3 = lambda i, j: (i, 0, j)
    out_shape = (
        jax.ShapeDtypeStruct((b, N_HEAD * LANES, s), BF16),
        jax.ShapeDtypeStruct((b, s, N_HEAD * LANES), BF16),
        jax.ShapeDtypeStruct((b, N_HEAD * VT_ROWS, s), BF16),
        jax.ShapeDtypeStruct((b, n_slab, s, LANES), F32),
        jax.ShapeDtypeStruct((b, n_slab, s, LANES), F32),
    )
    out_specs = (
        pl.BlockSpec((1, N_HEAD * LANES, tm), col3),
        pl.BlockSpec((1, tm, N_HEAD * LANES), row3),
        pl.BlockSpec((1, N_HEAD * VT_ROWS, tm), col3),
        pl.BlockSpec((1, n_slab, tm, LANES), row4),
        pl.BlockSpec((1, n_slab, tm, LANES), row4),
    )
    return pl.pallas_call(
        functools.partial(_proj_kernel, q_lora=q_lora, kv_lora=kv_lora, d_rnn=d_rnn),
        grid=(b, nt),
        in_specs=in_specs,
        out_specs=out_specs,
        out_shape=out_shape,
        compiler_params=pltpu.CompilerParams(
            dimension_semantics=("arbitrary", "arbitrary"), vmem_limit_bytes=VMEM_LIMIT),
        name="proj",
    )(x3, ln1, win, qag, wuq, kvag, wuk, wuv, cos_t, sin_t, gq, gk, hsum)


def _scan_stride(t_len):
    s = -(-t_len // SUBLANES)
    while s % SUBLANES == 0:
        s += 1
    if s % 2:
        s += 1
        if s % SUBLANES == 0:
            s += 2
    return s


def _chunk_len(stride, cap=258):
    for c in range(min(cap, stride), 0, -1):
        if stride % c == 0:
            return c
    return 1


def _rnn_kernel(xr_ref, xg_ref, xrm_ref, cw_ref, cb_ref, wg_ref, lam_ref, g_ref,
                o_ref, t1, af, bf, ab, bb, *, seq, stride, rc, tm_out):
    n_slab = xr_ref.shape[1]
    t_len = N_META_TOK + seq
    pre = SUBLANES
    rows = t1.shape[1]
    n_chunk = stride // rc
    crow = rc * SUBLANES

    for s in range(n_slab):
        t1[s, 0:pre, :] = jnp.zeros((pre, LANES), F32)
        t1[s, pre:pre + N_META_TOK, :] = xrm_ref[0, s]
        t1[s, pre + N_META_TOK:pre + t_len, :] = xr_ref[0, s]
        t1[s, pre + t_len:rows, :] = jnp.zeros((rows - pre - t_len, LANES), F32)

    lam = lam_ref[...]
    nl = -lam
    softplus = jnp.maximum(nl, 0.0) + jnp.log(1.0 + jnp.exp(-jnp.abs(nl)))
    hcn = (0.5 * LRU_SCALE) * softplus
    cw = cw_ref[...]
    cb = cb_ref[...]
    ones_k = jnp.ones((crow, LANES), BF16)

    padded = stride * SUBLANES > t_len
    if padded:
        ridx = lax.broadcasted_iota(jnp.int32, (crow, LANES), 0)
        t_local = (ridx & (SUBLANES - 1)) * stride + (ridx >> 3)

    def gate_chunk(c, carry):
        r0 = c * rc
        row0 = pl.multiple_of(c * crow, SUBLANES)
        for s in range(n_slab):
            ls = slice(s * LANES, (s + 1) * LANES)
            taps = [t1[s, pl.ds(r0 + pre - CONV_LEFT + j, SUBLANES, stride=stride), :]
                    for j in range(rc + CONV_TAPS - 1)]
            xc_rows = []
            for rr in range(rc):
                acc = taps[rr] * cw[0:1, ls]
                for j in range(1, CONV_TAPS):
                    acc = acc + taps[rr + j] * cw[j:j + 1, ls]
                xc_rows.append(acc + cb[:, ls])
            xc = jnp.concatenate(xc_rows, axis=0)
            hx = 0.5 * xc
            z = jnp.dot(jnp.concatenate([xc.astype(BF16), ones_k], axis=1), wg_ref[s],
                        preferred_element_type=F32)
            for d, (a_ref, b_ref) in enumerate(((af, bf), (ab, bb))):
                ta = jnp.tanh(z[:, (2 * d) * LANES:(2 * d + 1) * LANES])
                ti = jnp.tanh(z[:, (2 * d + 1) * LANES:(2 * d + 2) * LANES])
                hc = hcn[d:d + 1, ls]
                nla = ta * hc + hc
                a = jnp.exp2(nla * (-LOG2E))
                one_m_a2 = jnp.tanh(nla) * (1.0 + a * a)
                root = jnp.where(one_m_a2 > 0.0, one_m_a2 * lax.rsqrt(one_m_a2), 0.0)
                bv = root * (ti * hx + hx)
                if d == 1 and padded:
                    bv = jnp.where((t_local + r0) < t_len, bv, 0.0)
                a_ref[s, pl.ds(row0, crow), :] = a
                b_ref[s, pl.ds(row0, crow), :] = bv
        return carry

    lax.fori_loop(0, n_chunk, gate_chunk, 0)

    zero = jnp.zeros((SUBLANES, LANES), F32)
    one = jnp.ones((SUBLANES, LANES), F32)

    def scan_step(r, carry):
        hf, pf, hb, pb = carry
        rf = pl.multiple_of(r * SUBLANES, SUBLANES)
        rb = pl.multiple_of((stride - 1 - r) * SUBLANES, SUBLANES)
        nhf, npf, nhb, npb = [], [], [], []
        for s in range(n_slab):
            a = af[s, pl.ds(rf, SUBLANES), :]
            h = a * hf[s] + bf[s, pl.ds(rf, SUBLANES), :]
            pr = a * pf[s]
            bf[s, pl.ds(rf, SUBLANES), :] = h
            af[s, pl.ds(rf, SUBLANES), :] = pr
            nhf.append(h)
            npf.append(pr)
            a = ab[s, pl.ds(rb, SUBLANES), :]
            h = a * hb[s] + bb[s, pl.ds(rb, SUBLANES), :]
            pr = a * pb[s]
            bb[s, pl.ds(rb, SUBLANES), :] = h
            ab[s, pl.ds(rb, SUBLANES), :] = pr
            nhb.append(h)
            npb.append(pr)
        return tuple(nhf), tuple(npf), tuple(nhb), tuple(npb)

    init = ((zero,) * n_slab, (one,) * n_slab, (zero,) * n_slab, (one,) * n_slab)
    hf, pf, hb, pb = lax.fori_loop(0, stride, scan_step, init, unroll=SCAN_UNROLL)

    sub = lax.broadcasted_iota(jnp.int32, (SUBLANES, LANES), 0)
    cf, cbk = [], []
    for s in range(n_slab):
        c = zero
        for _ in range(SUBLANES - 1):
            c = jnp.where(sub == 0, 0.0, pltpu.roll(pf[s] * c + hf[s], 1, axis=0))
        cf.append(c)
        c = zero
        for _ in range(SUBLANES - 1):
            c = jnp.where(sub == SUBLANES - 1, 0.0, pltpu.roll(pb[s] * c + hb[s], SUBLANES - 1, axis=0))
        cbk.append(c)

    def combine(r, carry):
        rp = pl.multiple_of(r * SUBLANES, SUBLANES)
        for s in range(n_slab):
            y = (bf[s, pl.ds(rp, SUBLANES), :] + af[s, pl.ds(rp, SUBLANES), :] * cf[s]
                 + bb[s, pl.ds(rp, SUBLANES), :] + ab[s, pl.ds(rp, SUBLANES), :] * cbk[s])
            t1[s, pl.ds(r + pre, SUBLANES, stride=stride), :] = y
        return carry

    lax.fori_loop(0, stride, combine, 0, unroll=COMBINE_UNROLL)

    d_rnn = n_slab * LANES
    g = g_ref[...]
    k0 = math.sqrt(2.0 / math.pi)

    def finish(j, carry):
        ro = pl.multiple_of(j * tm_out, 2 * SUBLANES)
        rt = pl.multiple_of(j * tm_out + pre + N_META_TOK, SUBLANES)
        ys = []
        ssq = jnp.zeros((tm_out, 1), F32)
        for s in range(n_slab):
            y = t1[s, pl.ds(rt, tm_out), :]
            xg = xg_ref[0, s, pl.ds(ro, tm_out), :]
            yh = y * (0.5 * xg)
            y = yh + yh * jnp.tanh(xg * (k0 + (k0 * 0.044715) * (xg * xg)))
            ssq = ssq + jnp.sum(y * y, axis=-1, keepdims=True)
            ys.append(y)
        sc = lax.rsqrt(ssq * (1.0 / d_rnn) + NORM_EPS)
        for s in range(n_slab):
            ls = slice(s * LANES, (s + 1) * LANES)
            o_ref[0, pl.ds(ro, tm_out), ls] = (ys[s] * sc * g[:, ls]).astype(BF16)
        return carry

    lax.fori_loop(0, seq // tm_out, finish, 0)


def _rnn(xr, xg, xrm, cw, cb, wg, lam, g):
    b, n_slab, seq, _ = xr.shape
    t_len = N_META_TOK + seq
    stride = _scan_stride(t_len)
    rc = _chunk_len(stride)
    rows_perm = stride * SUBLANES
    rows_time = SUBLANES + rows_perm + SUBLANES
    tm_out = 512 if seq % 512 == 0 else seq
    blk = lambda i: (i, 0, 0, 0)
    in_specs = [
        pl.BlockSpec((1, n_slab, seq, LANES), blk),
        pl.BlockSpec((1, n_slab, seq, LANES), blk),
        _const_spec(xrm.shape),
    ] + [_const_spec(w.shape) for w in (cw, cb, wg, lam, g)]
    scratch = [pltpu.VMEM((n_slab, rows_time, LANES), F32)]
    scratch += [pltpu.VMEM((n_slab, rows_perm, LANES), F32) for _ in range(4)]
    return pl.pallas_call(
        functools.partial(_rnn_kernel, seq=seq, stride=stride, rc=rc, tm_out=tm_out),
        grid=(b,),
        in_specs=in_specs,
        out_specs=pl.BlockSpec((1, seq, n_slab * LANES), lambda i: (i, 0, 0)),
        out_shape=jax.ShapeDtypeStruct((b, seq, n_slab * LANES), BF16),
        scratch_shapes=scratch,
        compiler_params=pltpu.CompilerParams(
            dimension_semantics=("arbitrary",), vmem_limit_bytes=VMEM_LIMIT),
        name="rnn",
    )(xr, xg, xrm, cw, cb, wg, lam, g)


def _attn_kernel(qt_ref, k_ref, vt_ref, km_ref, vtm_ref, o_ref, *, tq):
    n_key = k_ref.shape[1]

    n_stream = HEADS_PER_STEP * (n_key // tq)
    n_tile = n_key // KEY_TILE
    tiles = [(t * KEY_TILE, (t + 1) * KEY_TILE) for t in range(n_tile)]
    tiles[-1] = (tiles[-1][0], n_key + N_META_TOK)
    steps = [(i, t) for i in range(n_stream) for t in range(n_tile)]

    def score_tile(i, t):
        j, hh = divmod(i, HEADS_PER_STEP)
        k0, k1 = tiles[t]
        ls = slice(hh * LANES, (hh + 1) * LANES)
        keys = k_ref[0, k0:min(k1, n_key), ls]
        if k1 > n_key:
            keys = jnp.concatenate([keys, km_ref[0, :, ls]], axis=0)
        return jnp.dot(keys, qt_ref[0, ls, j * tq:(j + 1) * tq],
                       preferred_element_type=F32)

    outs = []
    pending = [score_tile(*steps[n]) for n in range(SCORE_LOOKAHEAD)]
    m, acc = None, None
    for n, (i, t) in enumerate(steps):
        if n + SCORE_LOOKAHEAD < len(steps):
            pending.append(score_tile(*steps[n + SCORE_LOOKAHEAD]))
        st = pending.pop(0)
        k0, k1 = tiles[t]
        hh = i % HEADS_PER_STEP
        vs = slice(hh * VT_ROWS, (hh + 1) * VT_ROWS)
        st3 = st.reshape((k1 - k0) // SUBLANES, SUBLANES, tq)
        tmax = jnp.max(jnp.max(st3, axis=0), axis=0, keepdims=True)
        m_new = tmax if m is None else jnp.maximum(m, tmax)
        mb = jnp.broadcast_to(m_new, (SUBLANES, tq))
        pb = jnp.exp2(st3 - mb[None]).reshape(k1 - k0, tq).astype(BF16)
        part = jnp.dot(vt_ref[0, vs, k0:k0 + KEY_TILE], pb[:KEY_TILE], preferred_element_type=F32)
        if k1 - k0 > KEY_TILE:
            part = part + jnp.dot(vtm_ref[0, vs, :], pb[KEY_TILE:], preferred_element_type=F32)
        acc = part if m is None else acc * jnp.exp2(m - m_new) + part
        m = m_new
        if t == n_tile - 1:
            outs.append(acc[:HEAD_V] * (1.0 / acc[HEAD_V:HEAD_V + 1]))
            m, acc = None, None
            if hh == HEADS_PER_STEP - 1:
                j = i // HEADS_PER_STEP
                o_ref[0, j * tq:(j + 1) * tq, :] = jnp.concatenate(outs, axis=0).T
                outs = []


def _attention(qt, k, vt, km, vtm, tq):
    b, s, _ = k.shape
    hps = HEADS_PER_STEP
    return pl.pallas_call(
        functools.partial(_attn_kernel, tq=tq),
        grid=(b, N_HEAD // hps),
        in_specs=[
            pl.BlockSpec((1, hps * LANES, s), lambda i, p: (i, p, 0)),
            pl.BlockSpec((1, s, hps * LANES), lambda i, p: (i, 0, p)),
            pl.BlockSpec((1, hps * VT_ROWS, s), lambda i, p: (i, p, 0)),
            pl.BlockSpec((1, N_META_TOK, hps * LANES), lambda i, p: (0, 0, p)),
            pl.BlockSpec((1, hps * VT_ROWS, N_META_TOK), lambda i, p: (0, p, 0)),
        ],
        out_specs=pl.BlockSpec((1, s, hps * HEAD_V), lambda i, p: (i, 0, p)),
        out_shape=jax.ShapeDtypeStruct((b, s, N_HEAD * HEAD_V), F32),
        compiler_params=pltpu.CompilerParams(
            dimension_semantics=("arbitrary", "arbitrary"), vmem_limit_bytes=VMEM_LIMIT),
        name="attn",
    )(qt, k, vt, km, vtm)


def _out_kernel(x_ref, oa_ref, on_ref, ag_ref, woa_ref, wor_ref, ln2_ref, wg_ref, wu_ref, wd_ref,
                o_ref, *, ff_chunk):
    oa = oa_ref[...]
    oan = (oa * _rsqrt_mean(oa, oa.shape[-1]) * ag_ref[...]).astype(BF16)
    h = x_ref[...] + jnp.dot(oan, woa_ref[...], preferred_element_type=F32)
    h = h + jnp.dot(on_ref[...], wor_ref[...], preferred_element_type=F32)
    hn = (h * _rsqrt_mean(h, h.shape[-1]) * ln2_ref[...]).astype(BF16)
    d_ff = wg_ref.shape[1]
    acc = h
    for c in range(d_ff // ff_chunk):
        cs = slice(c * ff_chunk, (c + 1) * ff_chunk)
        gt = jnp.dot(hn, wg_ref[:, cs], preferred_element_type=F32)
        up = jnp.dot(hn, wu_ref[:, cs], preferred_element_type=F32)
        act = (gt * jax.nn.sigmoid(gt) * up).astype(BF16)
        acc = acc + jnp.dot(act, wd_ref[cs, :], preferred_element_type=F32)
    o_ref[...] = acc


def _out_ffn(x2, oa2, on2, ag, woa, wor, ln2, wg, wu, wd, tm):
    n, d = x2.shape
    d_ff = wg.shape[1]
    ff_chunk = d_ff
    row = lambda i: (i, 0)
    single = pl.Buffered(1)
    wspec = lambda w: pl.BlockSpec(w.shape, lambda i: (0, 0), pipeline_mode=single)
    return pl.pallas_call(
        functools.partial(_out_kernel, ff_chunk=ff_chunk),
        grid=(n // tm,),
        in_specs=[
            pl.BlockSpec((tm, d), row),
            pl.BlockSpec((tm, oa2.shape[1]), row),
            pl.BlockSpec((tm, on2.shape[1]), row),
            wspec(ag), wspec(woa), wspec(wor), wspec(ln2), wspec(wg), wspec(wu), wspec(wd),
        ],
        out_specs=pl.BlockSpec((tm, d), row),
        out_shape=jax.ShapeDtypeStruct((n, d), F32),
        compiler_params=pltpu.CompilerParams(
            dimension_semantics=("arbitrary",), vmem_limit_bytes=VMEM_LIMIT),
        name="out_ffn",
    )(x2, oa2, on2, ag, woa, wor, ln2, wg, wu, wd)


def _rope_tables(t_len):
    half = ROPE // 2
    freqs = 1.0 / (THETA ** (np.arange(half, dtype=np.float64) / half))
    ang = np.arange(t_len, dtype=np.float64)[:, None] * freqs[None, :]
    pad = LANES - HEAD_QK
    cos_t = np.concatenate([np.ones((t_len, NOPE)), np.cos(ang), np.cos(ang), np.zeros((t_len, pad))], -1)
    sin_t = np.concatenate([np.zeros((t_len, NOPE)), np.sin(ang), np.sin(ang), np.zeros((t_len, pad))], -1)
    return jnp.asarray(cos_t, F32), jnp.asarray(sin_t, F32)


def _rot_half_cols(w):
    half = ROPE // 2
    return jnp.concatenate([-w[..., half:], w[..., :half]], axis=-1)


def _gain_rows(g):
    half = ROPE // 2
    pad = jnp.zeros((LANES - HEAD_QK,), F32)
    r0 = jnp.concatenate([g, pad])
    r1 = jnp.concatenate([jnp.zeros((NOPE,), F32), g[NOPE + half:], g[NOPE:NOPE + half], pad])
    return jnp.stack([r0, r1])


def _block_diag_pairs(w):
    g, n, _ = w.shape
    z = jnp.zeros((g // 2, n, n), w.dtype)
    top = jnp.concatenate([w[0::2], z], axis=2)
    bot = jnp.concatenate([z, w[1::2]], axis=2)
    return jnp.concatenate([top, bot], axis=1)


def kernel(x, meta_tokens, ln1_g, w_in, q_a_norm_g, w_uq, kv_a_norm_g, w_ukv, q_norm_g, k_norm_g,
           conv_w, conv_b, lru_wa, lru_ba, lru_wi, lru_bi, lru_lambda, attn_out_g, rnn_out_g,
           w_out, ln2_g, w_gate, w_up, w_down):
    bsz, seq, d_model = x.shape
    q_lora = q_a_norm_g.shape[-1]
    kv_lora = kv_a_norm_g.shape[-1]
    d_rnn = conv_w.shape[-1]
    d_attn = N_HEAD * HEAD_V
    t_len = N_META_TOK + seq
    l = 0

    wi = w_in[l]
    o_kr = q_lora + kv_lora
    w_kr = wi[:, o_kr:o_kr + ROPE]
    win = jnp.concatenate(
        [wi[:, :o_kr], jnp.zeros((d_model, NOPE), F32), w_kr, _rot_half_cols(w_kr),
         wi[:, o_kr + ROPE:]], axis=1).astype(BF16)
    wq = w_uq[l].reshape(q_lora, N_HEAD, HEAD_QK)
    wuq = jnp.concatenate([wq, _rot_half_cols(wq[..., NOPE:])], axis=-1)
    wuq = wuq.reshape(q_lora, N_HEAD * LANES).astype(BF16)
    wkv = w_ukv[l].reshape(kv_lora, N_HEAD, NOPE + HEAD_V)
    wuk = jnp.concatenate([wkv[..., :NOPE], jnp.zeros((kv_lora, N_HEAD, LANES - NOPE), F32)], -1)
    wuk = wuk.reshape(kv_lora, N_HEAD * LANES).astype(BF16)
    wuv = wkv[..., NOPE:].reshape(kv_lora, d_attn).astype(BF16)
    gq = _gain_rows(q_norm_g[l])
    gk = _gain_rows(k_norm_g[l])
    cos_t, sin_t = _rope_tables(t_len)
    lane2 = np.arange(2 * LANES)
    hsum = jnp.asarray((lane2[:, None] // LANES == lane2[None, :] // LANES)
                       & (lane2[:, None] % LANES < HEAD_QK), BF16)
    proj_w = (ln1_g[l][None], win, q_a_norm_g[l][None], wuq, kv_a_norm_g[l][None], wuk, wuv, gq, gk, hsum)
    dims = (q_lora, kv_lora, d_rnn)

    wgate = (0.5 * jnp.concatenate(
        [_block_diag_pairs(w) for w in (lru_wa[l, 0], lru_wi[l, 0], lru_wa[l, 1], lru_wi[l, 1])],
        axis=2)).astype(BF16)
    n_slab = d_rnn // LANES
    gate_b = 0.5 * jnp.stack([lru_ba[l, 0], lru_bi[l, 0], lru_ba[l, 1], lru_bi[l, 1]])
    gate_b = gate_b.reshape(4, n_slab, LANES).transpose(1, 0, 2).reshape(n_slab, 1, 4 * LANES)
    b_hi = gate_b.astype(BF16)
    b_lo = (gate_b - b_hi.astype(F32)).astype(BF16)
    wgate = jnp.concatenate(
        [wgate, b_hi, b_lo, jnp.zeros((n_slab, LANES - 2, 4 * LANES), BF16)], axis=1)

    tm = 512 if seq % 512 == 0 else seq
    q, k, v, xr, xg = _project(x, tm, cos_t[N_META_TOK:], sin_t[N_META_TOK:], proj_w, dims)
    _, km, vm, xrm, _ = _project(meta_tokens[None].astype(x.dtype), N_META_TOK,
                                 cos_t[:N_META_TOK], sin_t[:N_META_TOK], proj_w, dims)

    o_rnn = _rnn(xr, xg, xrm, conv_w[l], conv_b[l][None], wgate, lru_lambda[l], rnn_out_g[l][None])

    tq = 512 if seq % 512 == 0 else seq
    o_attn = _attention(q, k, v, km, vm, tq)

    wo = w_out[l].astype(BF16)
    n_rows = bsz * seq
    tmo = 512 if n_rows % 512 == 0 else n_rows
    out = _out_ffn(x.reshape(n_rows, d_model), o_attn.reshape(n_rows, d_attn),
                   o_rnn.reshape(n_rows, d_rnn), attn_out_g[l][None], wo[:d_attn], wo[d_attn:],
                   ln2_g[l][None], w_gate[l].astype(BF16), w_up[l].astype(BF16),
                   w_down[l].astype(BF16), tmo)
    return out.reshape(bsz, seq, d_model)
```

```python
import functools
import math

import numpy as np

import jax
import jax.numpy as jnp
from jax import lax
from jax.experimental import pallas as pl
from jax.experimental.pallas import tpu as pltpu

F32 = jnp.float32
BF16 = jnp.bfloat16

N_META_TOK = 16
N_HEAD = 8
NOPE = 64
ROPE = 32
HEAD_QK = NOPE + ROPE
HEAD_V = 64
VT_ROWS = HEAD_V + 16
LRU_SCALE = 8.0
THETA = 10000.0
NORM_EPS = 1e-6
CONV_TAPS = 4
CONV_LEFT = 2

LANES = 128
SUBLANES = 8
VMEM_LIMIT = 56 * 1024 * 1024
KEY_TILE = 256
SCAN_UNROLL = 6
COMBINE_UNROLL = 43
HEADS_PER_STEP = 4
SCORE_LOOKAHEAD = 1

LOG2E = 1.4426950408889634


def _rsqrt_mean(x, n):
    return lax.rsqrt(jnp.sum(x * x, axis=-1, keepdims=True) * (1.0 / n) + NORM_EPS)


def _proj_kernel(x_ref, ln1_ref, win_ref, qag_ref, wuq_ref, kvag_ref, wuk_ref, wuv_ref,
                 cos_ref, sin_ref, gq_ref, gk_ref, hsum_ref,
                 q_out, k_out, v_out, xr_out, xg_out, *, q_lora, kv_lora, d_rnn):
    x = x_ref[0]
    d_model = x.shape[-1]
    hn = (x * _rsqrt_mean(x, d_model) * ln1_ref[...]).astype(BF16)
    o_kv = q_lora
    o_kr = o_kv + kv_lora
    o_xr = o_kr + LANES
    p = jnp.dot(hn, win_ref[:, :o_xr], preferred_element_type=F32)
    c_q = p[:, :o_kv]
    c_kv = p[:, o_kv:o_kr]
    kr = p[:, o_kr:o_xr]
    n_slab = d_rnn // LANES

    def recurrent_inputs(c):
        px = jnp.dot(hn, win_ref[:, o_xr + 2 * c * LANES:o_xr + 2 * (c + 1) * LANES],
                     preferred_element_type=F32)
        for u in range(2):
            slab = 2 * c + u
            dst = xr_out if slab < n_slab else xg_out
            dst[0, slab % n_slab] = px[:, u * LANES:(u + 1) * LANES]

    cos_t = cos_ref[...]
    sin_t = sin_ref[...]
    eps_hd = HEAD_QK * NORM_EPS
    root_hd = math.sqrt(HEAD_QK)
    hsum = hsum_ref[...]

    def head_sums(sq_pair):
        return jnp.dot(sq_pair.astype(BF16), hsum, preferred_element_type=F32)

    cqn = (c_q * _rsqrt_mean(c_q, q_lora) * qag_ref[...]).astype(BF16)
    q = jnp.dot(cqn, wuq_ref[...], preferred_element_type=F32)
    q_gain = root_hd * (HEAD_QK ** -0.5) * LOG2E
    cq = cos_t * (gq_ref[0:1, :] * q_gain)
    sq = sin_t * (gq_ref[1:2, :] * q_gain)

    ckn = (c_kv * _rsqrt_mean(c_kv, kv_lora) * kvag_ref[...]).astype(BF16)
    kn = jnp.dot(ckn, wuk_ref[...], preferred_element_type=F32)
    vt = jnp.dot(ckn, wuv_ref[...], preferred_element_type=F32).astype(BF16).T
    tm = vt.shape[1]
    ones_rows = (lax.broadcasted_iota(jnp.int32, (VT_ROWS - HEAD_V, tm), 0) == 0).astype(BF16)
    for h in range(N_HEAD):
        v_out[0, h * VT_ROWS:h * VT_ROWS + HEAD_V, :] = vt[h * HEAD_V:(h + 1) * HEAD_V]
        v_out[0, h * VT_ROWS + HEAD_V:(h + 1) * VT_ROWS, :] = ones_rows
    ck = cos_t * (gk_ref[0:1, :] * root_hd)
    sk = sin_t * (gk_ref[1:2, :] * root_hd)
    kr_sq = kr * kr
    kr_sq2 = jnp.concatenate([kr_sq, kr_sq], axis=1)
    kr_rot = kr * ck + pltpu.roll(kr, LANES - ROPE, axis=1) * sk
    n_pair = N_HEAD // 2
    n_rec = n_slab
    for pr in range(n_pair):
        for c in range(pr * n_rec // n_pair, (pr + 1) * n_rec // n_pair):
            recurrent_inputs(c)
        qp = q[:, 2 * pr * LANES:2 * (pr + 1) * LANES]
        sc = lax.rsqrt(head_sums(qp * qp) + eps_hd)
        for hh in range(2):
            h = 2 * pr + hh
            qh = qp[:, hh * LANES:(hh + 1) * LANES]
            qr = pltpu.roll(qh, LANES - ROPE, axis=1)
            qf = (qh * cq + qr * sq) * sc[:, hh * LANES:(hh + 1) * LANES]
            q_out[0, h * LANES:(h + 1) * LANES, :] = qf.astype(BF16).T
        kp = kn[:, 2 * pr * LANES:2 * (pr + 1) * LANES]
        sc = lax.rsqrt(head_sums(kp * kp + kr_sq2) + eps_hd)
        for hh in range(2):
            h = 2 * pr + hh
            kh = kp[:, hh * LANES:(hh + 1) * LANES]
            k_out[0, :, h * LANES:(h + 1) * LANES] = (
                (kh * ck + kr_rot) * sc[:, hh * LANES:(hh + 1) * LANES]).astype(BF16)


def _const_spec(shape):
    nd = len(shape)
    return pl.BlockSpec(shape, lambda *_: (0,) * nd)


def _project(x3, tm, cos_t, sin_t, wts, dims):
    q_lora, kv_lora, d_rnn = dims
    b, s, d = x3.shape
    n_slab = d_rnn // LANES
    nt = s // tm
    ln1, win, qag, wuq, kvag, wuk, wuv, gq, gk, hsum = wts
    row3 = lambda i, j: (i, j, 0)
    row4 = lambda i, j: (i, 0, j, 0)
    tab = lambda i, j: (j, 0)
    in_specs = [pl.BlockSpec((1, tm, d), row3)]
    in_specs += [_const_spec(w.shape) for w in (ln1, win, qag, wuq, kvag, wuk, wuv)]
    in_specs += [pl.BlockSpec((tm, LANES), tab), pl.BlockSpec((tm, LANES), tab)]
    in_specs += [_const_spec(gq.shape), _const_spec(gk.shape), _const_spec(hsum.shape)]
    col3 = lambda i, j: (i, 0, j)
    out_shape = (
        jax.ShapeDtypeStruct((b, N_HEAD * LANES, s), BF16),
        jax.ShapeDtypeStruct((b, s, N_HEAD * LANES), BF16),
        jax.ShapeDtypeStruct((b, N_HEAD * VT_ROWS, s), BF16),
        jax.ShapeDtypeStruct((b, n_slab, s, LANES), F32),
        jax.ShapeDtypeStruct((b, n_slab, s, LANES), F32),
    )
    out_specs = (
        pl.BlockSpec((1, N_HEAD * LANES, tm), col3),
        pl.BlockSpec((1, tm, N_HEAD * LANES), row3),
        pl.BlockSpec((1, N_HEAD * VT_ROWS, tm), col3),
        pl.BlockSpec((1, n_slab, tm, LANES), row4),
        pl.BlockSpec((1, n_slab, tm, LANES), row4),
    )
    return pl.pallas_call(
        functools.partial(_proj_kernel, q_lora=q_lora, kv_lora=kv_lora, d_rnn=d_rnn),
        grid=(b, nt),
        in_specs=in_specs,
        out_specs=out_specs,
        out_shape=out_shape,
        compiler_params=pltpu.CompilerParams(
            dimension_semantics=("arbitrary", "arbitrary"), vmem_limit_bytes=VMEM_LIMIT),
        name="proj",
    )(x3, ln1, win, qag, wuq, kvag, wuk, wuv, cos_t, sin_t, gq, gk, hsum)


def _scan_stride(t_len):
    s = -(-t_len // SUBLANES)
    while s % SUBLANES == 0:
        s += 1
    if s % 2:
        s += 1
        if s % SUBLANES == 0:
            s += 2
    return s


def _chunk_len(stride, cap=258):
    for c in range(min(cap, stride), 0, -1):
        if stride % c == 0:
            return c
    return 1


def _rnn_kernel(xr_ref, xg_ref, xrm_ref, cw_ref, cb_ref, wg_ref, lam_ref, g_ref,
                o_ref, t1, af, bf, ab, bb, *, seq, stride, rc, tm_out):
    n_slab = xr_ref.shape[1]
    t_len = N_META_TOK + seq
    pre = SUBLANES
    rows = t1.shape[1]
    n_chunk = stride // rc
    crow = rc * SUBLANES

    for s in range(n_slab):
        t1[s, 0:pre, :] = jnp.zeros((pre, LANES), F32)
        t1[s, pre:pre + N_META_TOK, :] = xrm_ref[0, s]
        t1[s, pre + N_META_TOK:pre + t_len, :] = xr_ref[0, s]
        t1[s, pre + t_len:rows, :] = jnp.zeros((rows - pre - t_len, LANES), F32)

    lam = lam_ref[...]
    nl = -lam
    softplus = jnp.maximum(nl, 0.0) + jnp.log(1.0 + jnp.exp(-jnp.abs(nl)))
    hcn = (0.5 * LRU_SCALE) * softplus
    cw = cw_ref[...]
    cb = cb_ref[...]
    ones_k = jnp.ones((crow, LANES), BF16)

    padded = stride * SUBLANES > t_len
    if padded:
        ridx = lax.broadcasted_iota(jnp.int32, (crow, LANES), 0)
        t_local = (ridx & (SUBLANES - 1)) * stride + (ridx >> 3)

    def gate_chunk(c, carry):
        r0 = c * rc
        row0 = pl.multiple_of(c * crow, SUBLANES)
        for s in range(n_slab):
            ls = slice(s * LANES, (s + 1) * LANES)
            taps = [t1[s, pl.ds(r0 + pre - CONV_LEFT + j, SUBLANES, stride=stride), :]
                    for j in range(rc + CONV_TAPS - 1)]
            xc_rows = []
            for rr in range(rc):
                acc = taps[rr] * cw[0:1, ls]
                for j in range(1, CONV_TAPS):
                    acc = acc + taps[rr + j] * cw[j:j + 1, ls]
                xc_rows.append(acc + cb[:, ls])
            xc = jnp.concatenate(xc_rows, axis=0)
            hx = 0.5 * xc
            z = jnp.dot(jnp.concatenate([xc.astype(BF16), ones_k], axis=1), wg_ref[s],
                        preferred_element_type=F32)
            for d, (a_ref, b_ref) in enumerate(((af, bf), (ab, bb))):
                ta = jnp.tanh(z[:, (2 * d) * LANES:(2 * d + 1) * LANES])
                ti = jnp.tanh(z[:, (2 * d + 1) * LANES:(2 * d + 2) * LANES])
                hc = hcn[d:d + 1, ls]
                nla = ta * hc + hc
                a = jnp.exp2(nla * (-LOG2E))
                one_m_a2 = jnp.tanh(nla) * (1.0 + a * a)
                root = jnp.where(one_m_a2 > 0.0, one_m_a2 * lax.rsqrt(one_m_a2), 0.0)
                bv = root * (ti * hx + hx)
                if d == 1 and padded:
                    bv = jnp.where((t_local + r0) < t_len, bv, 0.0)
                a_ref[s, pl.ds(row0, crow), :] = a
                b_ref[s, pl.ds(row0, crow), :] = bv
        return carry

    lax.fori_loop(0, n_chunk, gate_chunk, 0)

    zero = jnp.zeros((SUBLANES, LANES), F32)
    one = jnp.ones((SUBLANES, LANES), F32)

    def scan_step(r, carry):
        hf, pf, hb, pb = carry
        rf = pl.multiple_of(r * SUBLANES, SUBLANES)
        rb = pl.multiple_of((stride - 1 - r) * SUBLANES, SUBLANES)
        nhf, npf, nhb, npb = [], [], [], []
        for s in range(n_slab):
            a = af[s, pl.ds(rf, SUBLANES), :]
            h = a * hf[s] + bf[s, pl.ds(rf, SUBLANES), :]
            pr = a * pf[s]
            bf[s, pl.ds(rf, SUBLANES), :] = h
            af[s, pl.ds(rf, SUBLANES), :] = pr
            nhf.append(h)
            npf.append(pr)
            a = ab[s, pl.ds(rb, SUBLANES), :]
            h = a * hb[s] + bb[s, pl.ds(rb, SUBLANES), :]
            pr = a * pb[s]
            bb[s, pl.ds(rb, SUBLANES), :] = h
            ab[s, pl.ds(rb, SUBLANES), :] = pr
            nhb.append(h)
            npb.append(pr)
        return tuple(nhf), tuple(npf), tuple(nhb), tuple(npb)

    init = ((zero,) * n_slab, (one,) * n_slab, (zero,) * n_slab, (one,) * n_slab)
    hf, pf, hb, pb = lax.fori_loop(0, stride, scan_step, init, unroll=SCAN_UNROLL)

    sub = lax.broadcasted_iota(jnp.int32, (SUBLANES, LANES), 0)
    cf, cbk = [], []
    for s in range(n_slab):
        c = zero
        for _ in range(SUBLANES - 1):
            c = jnp.where(sub == 0, 0.0, pltpu.roll(pf[s] * c + hf[s], 1, axis=0))
        cf.append(c)
        c = zero
        for _ in range(SUBLANES - 1):
            c = jnp.where(sub == SUBLANES - 1, 0.0, pltpu.roll(pb[s] * c + hb[s], SUBLANES - 1, axis=0))
        cbk.append(c)

    def combine(r, carry):
        rp = pl.multiple_of(r * SUBLANES, SUBLANES)
        for s in range(n_slab):
            y = (bf[s, pl.ds(rp, SUBLANES), :] + af[s, pl.ds(rp, SUBLANES), :] * cf[s]
                 + bb[s, pl.ds(rp, SUBLANES), :] + ab[s, pl.ds(rp, SUBLANES), :] * cbk[s])
            t1[s, pl.ds(r + pre, SUBLANES, stride=stride), :] = y
        return carry

    lax.fori_loop(0, stride, combine, 0, unroll=COMBINE_UNROLL)

    d_rnn = n_slab * LANES
    g = g_ref[...]
    k0 = math.sqrt(2.0 / math.pi)

    def finish(j, carry):
        ro = pl.multiple_of(j * tm_out, 2 * SUBLANES)
        rt = pl.multiple_of(j * tm_out + pre + N_META_TOK, SUBLANES)
        ys = []
        ssq = jnp.zeros((tm_out, 1), F32)
        for s in range(n_slab):
            y = t1[s, pl.ds(rt, tm_out), :]
            xg = xg_ref[0, s, pl.ds(ro, tm_out), :]
            yh = y * (0.5 * xg)
            y = yh + yh * jnp.tanh(xg * (k0 + (k0 * 0.044715) * (xg * xg)))
            ssq = ssq + jnp.sum(y * y, axis=-1, keepdims=True)
            ys.append(y)
        sc = lax.rsqrt(ssq * (1.0 / d_rnn) + NORM_EPS)
        for s in range(n_slab):
            ls = slice(s * LANES, (s + 1) * LANES)
            o_ref[0, pl.ds(ro, tm_out), ls] = (ys[s] * sc * g[:, ls]).astype(BF16)
        return carry

    lax.fori_loop(0, seq // tm_out, finish, 0)


def _rnn(xr, xg, xrm, cw, cb, wg, lam, g):
    b, n_slab, seq, _ = xr.shape
    t_len = N_META_TOK + seq
    stride = _scan_stride(t_len)
    rc = _chunk_len(stride)
    rows_perm = stride * SUBLANES
    rows_time = SUBLANES + rows_perm + SUBLANES
    tm_out = 512 if seq % 512 == 0 else seq
    blk = lambda i: (i, 0, 0, 0)
    in_specs = [
        pl.BlockSpec((1, n_slab, seq, LANES), blk),
        pl.BlockSpec((1, n_slab, seq, LANES), blk),
        _const_spec(xrm.shape),
    ] + [_const_spec(w.shape) for w in (cw, cb, wg, lam, g)]
    scratch = [pltpu.VMEM((n_slab, rows_time, LANES), F32)]
    scratch += [pltpu.VMEM((n_slab, rows_perm, LANES), F32) for _ in range(4)]
    return pl.pallas_call(
        functools.partial(_rnn_kernel, seq=seq, stride=stride, rc=rc, tm_out=tm_out),
        grid=(b,),
        in_specs=in_specs,
        out_specs=pl.BlockSpec((1, seq, n_slab * LANES), lambda i: (i, 0, 0)),
        out_shape=jax.ShapeDtypeStruct((b, seq, n_slab * LANES), BF16),
        scratch_shapes=scratch,
        compiler_params=pltpu.CompilerParams(
            dimension_semantics=("arbitrary",), vmem_limit_bytes=VMEM_LIMIT),
        name="rnn",
    )(xr, xg, xrm, cw, cb, wg, lam, g)


def _attn_kernel(qt_ref, k_ref, vt_ref, km_ref, vtm_ref, o_ref, *, tq):
    n_key = k_ref.shape[1]

    n_stream = HEADS_PER_STEP * (n_key // tq)
    n_tile = n_key // KEY_TILE
    tiles = [(t * KEY_TILE, (t + 1) * KEY_TILE) for t in range(n_tile)]
    tiles[-1] = (tiles[-1][0], n_key + N_META_TOK)
    steps = [(i, t) for i in range(n_stream) for t in range(n_tile)]

    def score_tile(i, t):
        j, hh = divmod(i, HEADS_PER_STEP)
        k0, k1 = tiles[t]
        ls = slice(hh * LANES, (hh + 1) * LANES)
        keys = k_ref[0, k0:min(k1, n_key), ls]
        if k1 > n_key:
            keys = jnp.concatenate([keys, km_ref[0, :, ls]], axis=0)
        return jnp.dot(keys, qt_ref[0, ls, j * tq:(j + 1) * tq],
                       preferred_element_type=F32)

    outs = []
    pending = [score_tile(*steps[n]) for n in range(SCORE_LOOKAHEAD)]
    m, acc = None, None
    for n, (i, t) in enumerate(steps):
        if n + SCORE_LOOKAHEAD < len(steps):
            pending.append(score_tile(*steps[n + SCORE_LOOKAHEAD]))
        st = pending.pop(0)
        k0, k1 = tiles[t]
        hh = i % HEADS_PER_STEP
        vs = slice(hh * VT_ROWS, (hh + 1) * VT_ROWS)
        st3 = st.reshape((k1 - k0) // SUBLANES, SUBLANES, tq)
        tmax = jnp.max(jnp.max(st3, axis=0), axis=0, keepdims=True)
        m_new = tmax if m is None else jnp.maximum(m, tmax)
        mb = jnp.broadcast_to(m_new, (SUBLANES, tq))
        pb = jnp.exp2(st3 - mb[None]).reshape(k1 - k0, tq).astype(BF16)
        part = jnp.dot(vt_ref[0, vs, k0:k0 + KEY_TILE], pb[:KEY_TILE], preferred_element_type=F32)
        if k1 - k0 > KEY_TILE:
            part = part + jnp.dot(vtm_ref[0, vs, :], pb[KEY_TILE:], preferred_element_type=F32)
        acc = part if m is None else acc * jnp.exp2(m - m_new) + part
        m = m_new
        if t == n_tile - 1:
            outs.append(acc[:HEAD_V] * (1.0 / acc[HEAD_V:HEAD_V + 1]))
            m, acc = None, None
            if hh == HEADS_PER_STEP - 1:
                j = i // HEADS_PER_STEP
                o_ref[0, j * tq:(j + 1) * tq, :] = jnp.concatenate(outs, axis=0).T
                outs = []


def _attention(qt, k, vt, km, vtm, tq):
    b, s, _ = k.shape
    hps = HEADS_PER_STEP
    return pl.pallas_call(
        functools.partial(_attn_kernel, tq=tq),
        grid=(b, N_HEAD // hps),
        in_specs=[
            pl.BlockSpec((1, hps * LANES, s), lambda i, p: (i, p, 0)),
            pl.BlockSpec((1, s, hps * LANES), lambda i, p: (i, 0, p)),
            pl.BlockSpec((1, hps * VT_ROWS, s), lambda i, p: (i, p, 0)),
            pl.BlockSpec((1, N_META_TOK, hps * LANES), lambda i, p: (0, 0, p)),
            pl.BlockSpec((1, hps * VT_ROWS, N_META_TOK), lambda i, p: (0, p, 0)),
        ],
        out_specs=pl.BlockSpec((1, s, hps * HEAD_V), lambda i, p: (i, 0, p)),
        out_shape=jax.ShapeDtypeStruct((b, s, N_HEAD * HEAD_V), F32),
        compiler_params=pltpu.CompilerParams(
            dimension_semantics=("arbitrary", "arbitrary"), vmem_limit_bytes=VMEM_LIMIT),
        name="attn",
    )(qt, k, vt, km, vtm)


def _out_kernel(x_ref, oa_ref, on_ref, ag_ref, woa_ref, wor_ref, ln2_ref, wg_ref, wu_ref, wd_ref,
                o_ref, *, ff_chunk):
    oa = oa_ref[...]
    oan = (oa * _rsqrt_mean(oa, oa.shape[-1]) * ag_ref[...]).astype(BF16)
    h = x_ref[...] + jnp.dot(oan, woa_ref[...], preferred_element_type=F32)
    h = h + jnp.dot(on_ref[...], wor_ref[...], preferred_element_type=F32)
    hn = (h * _rsqrt_mean(h, h.shape[-1]) * ln2_ref[...]).astype(BF16)
    d_ff = wg_ref.shape[1]
    acc = h
    for c in range(d_ff // ff_chunk):
        cs = slice(c * ff_chunk, (c + 1) * ff_chunk)
        gt = jnp.dot(hn, wg_ref[:, cs], preferred_element_type=F32)
        up = jnp.dot(hn, wu_ref[:, cs], preferred_element_type=F32)
        act = (gt * jax.nn.sigmoid(gt) * up).astype(BF16)
        acc = acc + jnp.dot(act, wd_ref[cs, :], preferred_element_type=F32)
    o_ref[...] = acc


def _out_ffn(x2, oa2, on2, ag, woa, wor, ln2, wg, wu, wd, tm):
    n, d = x2.shape
    d_ff = wg.shape[1]
    ff_chunk = d_ff
    row = lambda i: (i, 0)
    single = pl.Buffered(1)
    wspec = lambda w: pl.BlockSpec(w.shape, lambda i: (0, 0), pipeline_mode=single)
    return pl.pallas_call(
        functools.partial(_out_kernel, ff_chunk=ff_chunk),
        grid=(n // tm,),
        in_specs=[
            pl.BlockSpec((tm, d), row),
            pl.BlockSpec((tm, oa2.shape[1]), row),
            pl.BlockSpec((tm, on2.shape[1]), row),
            wspec(ag), wspec(woa), wspec(wor), wspec(ln2), wspec(wg), wspec(wu), wspec(wd),
        ],
        out_specs=pl.BlockSpec((tm, d), row),
        out_shape=jax.ShapeDtypeStruct((n, d), F32),
        compiler_params=pltpu.CompilerParams(
            dimension_semantics=("arbitrary",), vmem_limit_bytes=VMEM_LIMIT),
        name="out_ffn",
    )(x2, oa2, on2, ag, woa, wor, ln2, wg, wu, wd)


def _rope_tables(t_len):
    half = ROPE // 2
    freqs = 1.0 / (THETA ** (np.arange(half, dtype=np.float64) / half))
    ang = np.arange(t_len, dtype=np.float64)[:, None] * freqs[None, :]
    pad = LANES - HEAD_QK
    cos_t = np.concatenate([np.ones((t_len, NOPE)), np.cos(ang), np.cos(ang), np.zeros((t_len, pad))], -1)
    sin_t = np.concatenate([np.zeros((t_len, NOPE)), np.sin(ang), np.sin(ang), np.zeros((t_len, pad))], -1)
    return jnp.asarray(cos_t, F32), jnp.asarray(sin_t, F32)


def _rot_half_cols(w):
    half = ROPE // 2
    return jnp.concatenate([-w[..., half:], w[..., :half]], axis=-1)


def _gain_rows(g):
    half = ROPE // 2
    pad = jnp.zeros((LANES - HEAD_QK,), F32)
    r0 = jnp.concatenate([g, pad])
    r1 = jnp.concatenate([jnp.zeros((NOPE,), F32), g[NOPE + half:], g[NOPE:NOPE + half], pad])
    return jnp.stack([r0, r1])


def _block_diag_pairs(w):
    g, n, _ = w.shape
    z = jnp.zeros((g // 2, n, n), w.dtype)
    top = jnp.concatenate([w[0::2], z], axis=2)
    bot = jnp.concatenate([z, w[1::2]], axis=2)
    return jnp.concatenate([top, bot], axis=1)


def kernel(x, meta_tokens, ln1_g, w_in, q_a_norm_g, w_uq, kv_a_norm_g, w_ukv, q_norm_g, k_norm_g,
           conv_w, conv_b, lru_wa, lru_ba, lru_wi, lru_bi, lru_lambda, attn_out_g, rnn_out_g,
           w_out, ln2_g, w_gate, w_up, w_down):
    bsz, seq, d_model = x.shape
    q_lora = q_a_norm_g.shape[-1]
    kv_lora = kv_a_norm_g.shape[-1]
    d_rnn = conv_w.shape[-1]
    d_attn = N_HEAD * HEAD_V
    t_len = N_META_TOK + seq
    l = 0

    wi = w_in[l]
    o_kr = q_lora + kv_lora
    w_kr = wi[:, o_kr:o_kr + ROPE]
    win = jnp.concatenate(
        [wi[:, :o_kr], jnp.zeros((d_model, NOPE), F32), w_kr, _rot_half_cols(w_kr),
         wi[:, o_kr + ROPE:]], axis=1).astype(BF16)
    wq = w_uq[l].reshape(q_lora, N_HEAD, HEAD_QK)
    wuq = jnp.concatenate([wq, _rot_half_cols(wq[..., NOPE:])], axis=-1)
    wuq = wuq.reshape(q_lora, N_HEAD * LANES).astype(BF16)
    wkv = w_ukv[l].reshape(kv_lora, N_HEAD, NOPE + HEAD_V)
    wuk = jnp.concatenate([wkv[..., :NOPE], jnp.zeros((kv_lora, N_HEAD, LANES - NOPE), F32)], -1)
    wuk = wuk.reshape(kv_lora, N_HEAD * LANES).astype(BF16)
    wuv = wkv[..., NOPE:].reshape(kv_lora, d_attn).astype(BF16)
    gq = _gain_rows(q_norm_g[l])
    gk = _gain_rows(k_norm_g[l])
    cos_t, sin_t = _rope_tables(t_len)
    lane2 = np.arange(2 * LANES)
    hsum = jnp.asarray((lane2[:, None] // LANES == lane2[None, :] // LANES)
                       & (lane2[:, None] % LANES < HEAD_QK), BF16)
    proj_w = (ln1_g[l][None], win, q_a_norm_g[l][None], wuq, kv_a_norm_g[l][None], wuk, wuv, gq, gk, hsum)
    dims = (q_lora, kv_lora, d_rnn)

    wgate = (0.5 * jnp.concatenate(
        [_block_diag_pairs(w) for w in (lru_wa[l, 0], lru_wi[l, 0], lru_wa[l, 1], lru_wi[l, 1])],
        axis=2)).astype(BF16)
    n_slab = d_rnn // LANES
    gate_b = 0.5 * jnp.stack([lru_ba[l, 0], lru_bi[l, 0], lru_ba[l, 1], lru_bi[l, 1]])
    gate_b = gate_b.reshape(4, n_slab, LANES).transpose(1, 0, 2).reshape(n_slab, 1, 4 * LANES)
    b_hi = gate_b.astype(BF16)
    b_lo = (gate_b - b_hi.astype(F32)).astype(BF16)
    wgate = jnp.concatenate(
        [wgate, b_hi, b_lo, jnp.zeros((n_slab, LANES - 2, 4 * LANES), BF16)], axis=1)

    tm = 512 if seq % 512 == 0 else seq
    q, k, v, xr, xg = _project(x, tm, cos_t[N_META_TOK:], sin_t[N_META_TOK:], proj_w, dims)
    _, km, vm, xrm, _ = _project(meta_tokens[None].astype(x.dtype), N_META_TOK,
                                 cos_t[:N_META_TOK], sin_t[:N_META_TOK], proj_w, dims)

    o_rnn = _rnn(xr, xg, xrm, conv_w[l], conv_b[l][None], wgate, lru_lambda[l], rnn_out_g[l][None])

    tq = 2048 if seq % 2048 == 0 else seq
    o_attn = _attention(q, k, v, km, vm, tq)

    wo = w_out[l].astype(BF16)
    n_rows = bsz * seq
    tmo = 512 if n_rows % 512 == 0 else n_rows
    out = _out_ffn(x.reshape(n_rows, d_model), o_attn.reshape(n_rows, d_attn),
                   o_rnn.reshape(n_rows, d_rnn), attn_out_g[l][None], wo[:d_attn], wo[d_attn:],
                   ln2_g[l][None], w_gate[l].astype(BF16), w_up[l].astype(BF16),
                   w_down[l].astype(BF16), tmo)
    return out.reshape(bsz, seq, d_model)
```

```python
import functools
import math

import numpy as np

import jax
import jax.numpy as jnp
from jax import lax
from jax.experimental import pallas as pl
from jax.experimental.pallas import tpu as pltpu

F32 = jnp.float32
BF16 = jnp.bfloat16

N_META_TOK = 16
N_HEAD = 8
NOPE = 64
ROPE = 32
HEAD_QK = NOPE + ROPE
HEAD_V = 64
VT_ROWS = HEAD_V + 16
LRU_SCALE = 8.0
THETA = 10000.0
NORM_EPS = 1e-6
CONV_TAPS = 4
CONV_LEFT = 2

LANES = 128
SUBLANES = 8
VMEM_LIMIT = 56 * 1024 * 1024
KEY_TILE = 512
SCAN_UNROLL = 6
COMBINE_UNROLL = 43
HEADS_PER_STEP = 4
SCORE_LOOKAHEAD = 1

LOG2E = 1.4426950408889634


def _rsqrt_mean(x, n):
    return lax.rsqrt(jnp.sum(x * x, axis=-1, keepdims=True) * (1.0 / n) + NORM_EPS)


def _proj_kernel(x_ref, ln1_ref, win_ref, qag_ref, wuq_ref, kvag_ref, wuk_ref, wuv_ref,
                 cos_ref, sin_ref, gq_ref, gk_ref, hsum_ref,
                 q_out, k_out, v_out, xr_out, xg_out, *, q_lora, kv_lora, d_rnn):
    x = x_ref[0]
    d_model = x.shape[-1]
    hn = (x * _rsqrt_mean(x, d_model) * ln1_ref[...]).astype(BF16)
    o_kv = q_lora
    o_kr = o_kv + kv_lora
    o_xr = o_kr + LANES
    p = jnp.dot(hn, win_ref[:, :o_xr], preferred_element_type=F32)
    c_q = p[:, :o_kv]
    c_kv = p[:, o_kv:o_kr]
    kr = p[:, o_kr:o_xr]
    n_slab = d_rnn // LANES

    def recurrent_inputs(c):
        px = jnp.dot(hn, win_ref[:, o_xr + 2 * c * LANES:o_xr + 2 * (c + 1) * LANES],
                     preferred_element_type=F32)
        for u in range(2):
            slab = 2 * c + u
            dst = xr_out if slab < n_slab else xg_out
            dst[0, slab % n_slab] = px[:, u * LANES:(u + 1) * LANES]

    cos_t = cos_ref[...]
    sin_t = sin_ref[...]
    eps_hd = HEAD_QK * NORM_EPS
    root_hd = math.sqrt(HEAD_QK)
    hsum = hsum_ref[...]

    def head_sums(sq_pair):
        return jnp.dot(sq_pair.astype(BF16), hsum, preferred_element_type=F32)

    cqn = (c_q * _rsqrt_mean(c_q, q_lora) * qag_ref[...]).astype(BF16)
    q = jnp.dot(cqn, wuq_ref[...], preferred_element_type=F32)
    q_gain = root_hd * (HEAD_QK ** -0.5) * LOG2E
    cq = cos_t * (gq_ref[0:1, :] * q_gain)
    sq = sin_t * (gq_ref[1:2, :] * q_gain)

    ckn = (c_kv * _rsqrt_mean(c_kv, kv_lora) * kvag_ref[...]).astype(BF16)
    kn = jnp.dot(ckn, wuk_ref[...], preferred_element_type=F32)
    vt = jnp.dot(ckn, wuv_ref[...], preferred_element_type=F32).astype(BF16).T
    tm = vt.shape[1]
    ones_rows = (lax.broadcasted_iota(jnp.int32, (VT_ROWS - HEAD_V, tm), 0) == 0).astype(BF16)
    for h in range(N_HEAD):
        v_out[0, h * VT_ROWS:h * VT_ROWS + HEAD_V, :] = vt[h * HEAD_V:(h + 1) * HEAD_V]
        v_out[0, h * VT_ROWS + HEAD_V:(h + 1) * VT_ROWS, :] = ones_rows
    ck = cos_t * (gk_ref[0:1, :] * root_hd)
    sk = sin_t * (gk_ref[1:2, :] * root_hd)
    kr_sq = kr * kr
    kr_sq2 = jnp.concatenate([kr_sq, kr_sq], axis=1)
    kr_rot = kr * ck + pltpu.roll(kr, LANES - ROPE, axis=1) * sk
    n_pair = N_HEAD // 2
    n_rec = n_slab
    for pr in range(n_pair):
        for c in range(pr * n_rec // n_pair, (pr + 1) * n_rec // n_pair):
            recurrent_inputs(c)
        qp = q[:, 2 * pr * LANES:2 * (pr + 1) * LANES]
        sc = lax.rsqrt(head_sums(qp * qp) + eps_hd)
        for hh in range(2):
            h = 2 * pr + hh
            qh = qp[:, hh * LANES:(hh + 1) * LANES]
            qr = pltpu.roll(qh, LANES - ROPE, axis=1)
            qf = (qh * cq + qr * sq) * sc[:, hh * LANES:(hh + 1) * LANES]
            q_out[0, h * LANES:(h + 1) * LANES, :] = qf.astype(BF16).T
        kp = kn[:, 2 * pr * LANES:2 * (pr + 1) * LANES]
        sc = lax.rsqrt(head_sums(kp * kp + kr_sq2) + eps_hd)
        for hh in range(2):
            h = 2 * pr + hh
            kh = kp[:, hh * LANES:(hh + 1) * LANES]
            k_out[0, :, h * LANES:(h + 1) * LANES] = (
                (kh * ck + kr_rot) * sc[:, hh * LANES:(hh + 1) * LANES]).astype(BF16)


def _const_spec(shape):
    nd = len(shape)
    return pl.BlockSpec(shape, lambda *_: (0,) * nd)


def _project(x3, tm, cos_t, sin_t, wts, dims):
    q_lora, kv_lora, d_rnn = dims
    b, s, d = x3.shape
    n_slab = d_rnn // LANES
    nt = s // tm
    ln1, win, qag, wuq, kvag, wuk, wuv, gq, gk, hsum = wts
    row3 = lambda i, j: (i, j, 0)
    row4 = lambda i, j: (i, 0, j, 0)
    tab = lambda i, j: (j, 0)
    in_specs = [pl.BlockSpec((1, tm, d), row3)]
    in_specs += [_const_spec(w.shape) for w in (ln1, win, qag, wuq, kvag, wuk, wuv)]
    in_specs += [pl.BlockSpec((tm, LANES), tab), pl.BlockSpec((tm, LANES), tab)]
    in_specs += [_const_spec(gq.shape), _const_spec(gk.shape), _const_spec(hsum.shape)]
    col3 = lambda i, j: (i, 0, j)
    out_shape = (
        jax.ShapeDtypeStruct((b, N_HEAD * LANES, s), BF16),
        jax.ShapeDtypeStruct((b, s, N_HEAD * LANES), BF16),
        jax.ShapeDtypeStruct((b, N_HEAD * VT_ROWS, s), BF16),
        jax.ShapeDtypeStruct((b, n_slab, s, LANES), F32),
        jax.ShapeDtypeStruct((b, n_slab, s, LANES), F32),
    )
    out_specs = (
        pl.BlockSpec((1, N_HEAD * LANES, tm), col3),
        pl.BlockSpec((1, tm, N_HEAD * LANES), row3),
        pl.BlockSpec((1, N_HEAD * VT_ROWS, tm), col3),
        pl.BlockSpec((1, n_slab, tm, LANES), row4),
        pl.BlockSpec((1, n_slab, tm, LANES), row4),
    )
    return pl.pallas_call(
        functools.partial(_proj_kernel, q_lora=q_lora, kv_lora=kv_lora, d_rnn=d_rnn),
        grid=(b, nt),
        in_specs=in_specs,
        out_specs=out_specs,
        out_shape=out_shape,
        compiler_params=pltpu.CompilerParams(
            dimension_semantics=("arbitrary", "arbitrary"), vmem_limit_bytes=VMEM_LIMIT),
        name="proj",
    )(x3, ln1, win, qag, wuq, kvag, wuk, wuv, cos_t, sin_t, gq, gk, hsum)


def _scan_stride(t_len):
    s = -(-t_len // SUBLANES)
    while s % SUBLANES == 0:
        s += 1
    if s % 2:
        s += 1
        if s % SUBLANES == 0:
            s += 2
    return s


def _chunk_len(stride, cap=258):
    for c in range(min(cap, stride), 0, -1):
        if stride % c == 0:
            return c
    return 1


def _rnn_kernel(xr_ref, xg_ref, xrm_ref, cw_ref, cb_ref, wg_ref, lam_ref, g_ref,
                o_ref, t1, af, bf, ab, bb, *, seq, stride, rc, tm_out):
    n_slab = xr_ref.shape[1]
    t_len = N_META_TOK + seq
    pre = SUBLANES
    rows = t1.shape[1]
    n_chunk = stride // rc
    crow = rc * SUBLANES

    for s in range(n_slab):
        t1[s, 0:pre, :] = jnp.zeros((pre, LANES), F32)
        t1[s, pre:pre + N_META_TOK, :] = xrm_ref[0, s]
        t1[s, pre + N_META_TOK:pre + t_len, :] = xr_ref[0, s]
        t1[s, pre + t_len:rows, :] = jnp.zeros((rows - pre - t_len, LANES), F32)

    lam = lam_ref[...]
    nl = -lam
    softplus = jnp.maximum(nl, 0.0) + jnp.log(1.0 + jnp.exp(-jnp.abs(nl)))
    hcn = (0.5 * LRU_SCALE) * softplus
    cw = cw_ref[...]
    cb = cb_ref[...]
    ones_k = jnp.ones((crow, LANES), BF16)

    padded = stride * SUBLANES > t_len
    if padded:
        ridx = lax.broadcasted_iota(jnp.int32, (crow, LANES), 0)
        t_local = (ridx & (SUBLANES - 1)) * stride + (ridx >> 3)

    def gate_chunk(c, carry):
        r0 = c * rc
        row0 = pl.multiple_of(c * crow, SUBLANES)
        for s in range(n_slab):
            ls = slice(s * LANES, (s + 1) * LANES)
            taps = [t1[s, pl.ds(r0 + pre - CONV_LEFT + j, SUBLANES, stride=stride), :]
                    for j in range(rc + CONV_TAPS - 1)]
            xc_rows = []
            for rr in range(rc):
                acc = taps[rr] * cw[0:1, ls]
                for j in range(1, CONV_TAPS):
                    acc = acc + taps[rr + j] * cw[j:j + 1, ls]
                xc_rows.append(acc + cb[:, ls])
            xc = jnp.concatenate(xc_rows, axis=0)
            hx = 0.5 * xc
            z = jnp.dot(jnp.concatenate([xc.astype(BF16), ones_k], axis=1), wg_ref[s],
                        preferred_element_type=F32)
            for d, (a_ref, b_ref) in enumerate(((af, bf), (ab, bb))):
                ta = jnp.tanh(z[:, (2 * d) * LANES:(2 * d + 1) * LANES])
                ti = jnp.tanh(z[:, (2 * d + 1) * LANES:(2 * d + 2) * LANES])
                hc = hcn[d:d + 1, ls]
                nla = ta * hc + hc
                a = jnp.exp2(nla * (-LOG2E))
                one_m_a2 = jnp.tanh(nla) * (1.0 + a * a)
                root = jnp.where(one_m_a2 > 0.0, one_m_a2 * lax.rsqrt(one_m_a2), 0.0)
                bv = root * (ti * hx + hx)
                if d == 1 and padded:
                    bv = jnp.where((t_local + r0) < t_len, bv, 0.0)
                a_ref[s, pl.ds(row0, crow), :] = a
                b_ref[s, pl.ds(row0, crow), :] = bv
        return carry

    lax.fori_loop(0, n_chunk, gate_chunk, 0)

    zero = jnp.zeros((SUBLANES, LANES), F32)
    one = jnp.ones((SUBLANES, LANES), F32)

    def scan_step(r, carry):
        hf, pf, hb, pb = carry
        rf = pl.multiple_of(r * SUBLANES, SUBLANES)
        rb = pl.multiple_of((stride - 1 - r) * SUBLANES, SUBLANES)
        nhf, npf, nhb, npb = [], [], [], []
        for s in range(n_slab):
            a = af[s, pl.ds(rf, SUBLANES), :]
            h = a * hf[s] + bf[s, pl.ds(rf, SUBLANES), :]
            pr = a * pf[s]
            bf[s, pl.ds(rf, SUBLANES), :] = h
            af[s, pl.ds(rf, SUBLANES), :] = pr
            nhf.append(h)
            npf.append(pr)
            a = ab[s, pl.ds(rb, SUBLANES), :]
            h = a * hb[s] + bb[s, pl.ds(rb, SUBLANES), :]
            pr = a * pb[s]
            bb[s, pl.ds(rb, SUBLANES), :] = h
            ab[s, pl.ds(rb, SUBLANES), :] = pr
            nhb.append(h)
            npb.append(pr)
        return tuple(nhf), tuple(npf), tuple(nhb), tuple(npb)

    init = ((zero,) * n_slab, (one,) * n_slab, (zero,) * n_slab, (one,) * n_slab)
    hf, pf, hb, pb = lax.fori_loop(0, stride, scan_step, init, unroll=SCAN_UNROLL)

    sub = lax.broadcasted_iota(jnp.int32, (SUBLANES, LANES), 0)
    cf, cbk = [], []
    for s in range(n_slab):
        c = zero
        for _ in range(SUBLANES - 1):
            c = jnp.where(sub == 0, 0.0, pltpu.roll(pf[s] * c + hf[s], 1, axis=0))
        cf.append(c)
        c = zero
        for _ in range(SUBLANES - 1):
            c = jnp.where(sub == SUBLANES - 1, 0.0, pltpu.roll(pb[s] * c + hb[s], SUBLANES - 1, axis=0))
        cbk.append(c)

    def combine(r, carry):
        rp = pl.multiple_of(r * SUBLANES, SUBLANES)
        for s in range(n_slab):
            y = (bf[s, pl.ds(rp, SUBLANES), :] + af[s, pl.ds(rp, SUBLANES), :] * cf[s]
                 + bb[s, pl.ds(rp, SUBLANES), :] + ab[s, pl.ds(rp, SUBLANES), :] * cbk[s])
            t1[s, pl.ds(r + pre, SUBLANES, stride=stride), :] = y
        return carry

    lax.fori_loop(0, stride, combine, 0, unroll=COMBINE_UNROLL)

    d_rnn = n_slab * LANES
    g = g_ref[...]
    k0 = math.sqrt(2.0 / math.pi)

    def finish(j, carry):
        ro = pl.multiple_of(j * tm_out, 2 * SUBLANES)
        rt = pl.multiple_of(j * tm_out + pre + N_META_TOK, SUBLANES)
        ys = []
        ssq = jnp.zeros((tm_out, 1), F32)
        for s in range(n_slab):
            y = t1[s, pl.ds(rt, tm_out), :]
            xg = xg_ref[0, s, pl.ds(ro, tm_out), :]
            yh = y * (0.5 * xg)
            y = yh + yh * jnp.tanh(xg * (k0 + (k0 * 0.044715) * (xg * xg)))
            ssq = ssq + jnp.sum(y * y, axis=-1, keepdims=True)
            ys.append(y)
        sc = lax.rsqrt(ssq * (1.0 / d_rnn) + NORM_EPS)
        for s in range(n_slab):
            ls = slice(s * LANES, (s + 1) * LANES)
            o_ref[0, pl.ds(ro, tm_out), ls] = (ys[s] * sc * g[:, ls]).astype(BF16)
        return carry

    lax.fori_loop(0, seq // tm_out, finish, 0)


def _rnn(xr, xg, xrm, cw, cb, wg, lam, g):
    b, n_slab, seq, _ = xr.shape
    t_len = N_META_TOK + seq
    stride = _scan_stride(t_len)
    rc = _chunk_len(stride)
    rows_perm = stride * SUBLANES
    rows_time = SUBLANES + rows_perm + SUBLANES
    tm_out = 512 if seq % 512 == 0 else seq
    blk = lambda i: (i, 0, 0, 0)
    in_specs = [
        pl.BlockSpec((1, n_slab, seq, LANES), blk),
        pl.BlockSpec((1, n_slab, seq, LANES), blk),
        _const_spec(xrm.shape),
    ] + [_const_spec(w.shape) for w in (cw, cb, wg, lam, g)]
    scratch = [pltpu.VMEM((n_slab, rows_time, LANES), F32)]
    scratch += [pltpu.VMEM((n_slab, rows_perm, LANES), F32) for _ in range(4)]
    return pl.pallas_call(
        functools.partial(_rnn_kernel, seq=seq, stride=stride, rc=rc, tm_out=tm_out),
        grid=(b,),
        in_specs=in_specs,
        out_specs=pl.BlockSpec((1, seq, n_slab * LANES), lambda i: (i, 0, 0)),
        out_shape=jax.ShapeDtypeStruct((b, seq, n_slab * LANES), BF16),
        scratch_shapes=scratch,
        compiler_params=pltpu.CompilerParams(
            dimension_semantics=("arbitrary",), vmem_limit_bytes=VMEM_LIMIT),
        name="rnn",
    )(xr, xg, xrm, cw, cb, wg, lam, g)


def _attn_kernel(qt_ref, k_ref, vt_ref, km_ref, vtm_ref, o_ref, *, tq):
    n_key = k_ref.shape[1]

    n_stream = HEADS_PER_STEP * (n_key // tq)
    n_tile = n_key // KEY_TILE
    tiles = [(t * KEY_TILE, (t + 1) * KEY_TILE) for t in range(n_tile)]
    tiles[-1] = (tiles[-1][0], n_key + N_META_TOK)
    steps = [(i, t) for i in range(n_stream) for t in range(n_tile)]

    def score_tile(i, t):
        j, hh = divmod(i, HEADS_PER_STEP)
        k0, k1 = tiles[t]
        ls = slice(hh * LANES, (hh + 1) * LANES)
        keys = k_ref[0, k0:min(k1, n_key), ls]
        if k1 > n_key:
            keys = jnp.concatenate([keys, km_ref[0, :, ls]], axis=0)
        return jnp.dot(keys, qt_ref[0, ls, j * tq:(j + 1) * tq],
                       preferred_element_type=F32)

    outs = []
    pending = [score_tile(*steps[n]) for n in range(SCORE_LOOKAHEAD)]
    m, acc = None, None
    for n, (i, t) in enumerate(steps):
        if n + SCORE_LOOKAHEAD < len(steps):
            pending.append(score_tile(*steps[n + SCORE_LOOKAHEAD]))
        st = pending.pop(0)
        k0, k1 = tiles[t]
        hh = i % HEADS_PER_STEP
        vs = slice(hh * VT_ROWS, (hh + 1) * VT_ROWS)
        st3 = st.reshape((k1 - k0) // SUBLANES, SUBLANES, tq)
        tmax = jnp.max(jnp.max(st3, axis=0), axis=0, keepdims=True)
        m_new = tmax if m is None else jnp.maximum(m, tmax)
        mb = jnp.broadcast_to(m_new, (SUBLANES, tq))
        pb = jnp.exp2(st3 - mb[None]).reshape(k1 - k0, tq).astype(BF16)
        part = jnp.dot(vt_ref[0, vs, k0:k0 + KEY_TILE], pb[:KEY_TILE], preferred_element_type=F32)
        if k1 - k0 > KEY_TILE:
            part = part + jnp.dot(vtm_ref[0, vs, :], pb[KEY_TILE:], preferred_element_type=F32)
        acc = part if m is None else acc * jnp.exp2(m - m_new) + part
        m = m_new
        if t == n_tile - 1:
            outs.append(acc[:HEAD_V] * (1.0 / acc[HEAD_V:HEAD_V + 1]))
            m, acc = None, None
            if hh == HEADS_PER_STEP - 1:
                j = i // HEADS_PER_STEP
                o_ref[0, j * tq:(j + 1) * tq, :] = jnp.concatenate(outs, axis=0).T
                outs = []


def _attention(qt, k, vt, km, vtm, tq):
    b, s, _ = k.shape
    hps = HEADS_PER_STEP
    return pl.pallas_call(
        functools.partial(_attn_kernel, tq=tq),
        grid=(b, N_HEAD // hps),
        in_specs=[
            pl.BlockSpec((1, hps * LANES, s), lambda i, p: (i, p, 0)),
            pl.BlockSpec((1, s, hps * LANES), lambda i, p: (i, 0, p)),
            pl.BlockSpec((1, hps * VT_ROWS, s), lambda i, p: (i, p, 0)),
            pl.BlockSpec((1, N_META_TOK, hps * LANES), lambda i, p: (0, 0, p)),
            pl.BlockSpec((1, hps * VT_ROWS, N_META_TOK), lambda i, p: (0, p, 0)),
        ],
        out_specs=pl.BlockSpec((1, s, hps * HEAD_V), lambda i, p: (i, 0, p)),
        out_shape=jax.ShapeDtypeStruct((b, s, N_HEAD * HEAD_V), F32),
        compiler_params=pltpu.CompilerParams(
            dimension_semantics=("arbitrary", "arbitrary"), vmem_limit_bytes=VMEM_LIMIT),
        name="attn",
    )(qt, k, vt, km, vtm)


def _out_kernel(x_ref, oa_ref, on_ref, ag_ref, woa_ref, wor_ref, ln2_ref, wg_ref, wu_ref, wd_ref,
                o_ref, *, ff_chunk):
    oa = oa_ref[...]
    oan = (oa * _rsqrt_mean(oa, oa.shape[-1]) * ag_ref[...]).astype(BF16)
    h = x_ref[...] + jnp.dot(oan, woa_ref[...], preferred_element_type=F32)
    h = h + jnp.dot(on_ref[...], wor_ref[...], preferred_element_type=F32)
    hn = (h * _rsqrt_mean(h, h.shape[-1]) * ln2_ref[...]).astype(BF16)
    d_ff = wg_ref.shape[1]
    acc = h
    for c in range(d_ff // ff_chunk):
        cs = slice(c * ff_chunk, (c + 1) * ff_chunk)
        gt = jnp.dot(hn, wg_ref[:, cs], preferred_element_type=F32)
        up = jnp.dot(hn, wu_ref[:, cs], preferred_element_type=F32)
        act = (gt * jax.nn.sigmoid(gt) * up).astype(BF16)
        acc = acc + jnp.dot(act, wd_ref[cs, :], preferred_element_type=F32)
    o_ref[...] = acc


def _out_ffn(x2, oa2, on2, ag, woa, wor, ln2, wg, wu, wd, tm):
    n, d = x2.shape
    d_ff = wg.shape[1]
    ff_chunk = d_ff
    row = lambda i: (i, 0)
    single = pl.Buffered(1)
    wspec = lambda w: pl.BlockSpec(w.shape, lambda i: (0, 0), pipeline_mode=single)
    return pl.pallas_call(
        functools.partial(_out_kernel, ff_chunk=ff_chunk),
        grid=(n // tm,),
        in_specs=[
            pl.BlockSpec((tm, d), row),
            pl.BlockSpec((tm, oa2.shape[1]), row),
            pl.BlockSpec((tm, on2.shape[1]), row),
            wspec(ag), wspec(woa), wspec(wor), wspec(ln2), wspec(wg), wspec(wu), wspec(wd),
        ],
        out_specs=pl.BlockSpec((tm, d), row),
        out_shape=jax.ShapeDtypeStruct((n, d), F32),
        compiler_params=pltpu.CompilerParams(
            dimension_semantics=("arbitrary",), vmem_limit_bytes=VMEM_LIMIT),
        name="out_ffn",
    )(x2, oa2, on2, ag, woa, wor, ln2, wg, wu, wd)


def _rope_tables(t_len):
    half = ROPE // 2
    freqs = 1.0 / (THETA ** (np.arange(half, dtype=np.float64) / half))
    ang = np.arange(t_len, dtype=np.float64)[:, None] * freqs[None, :]
    pad = LANES - HEAD_QK
    cos_t = np.concatenate([np.ones((t_len, NOPE)), np.cos(ang), np.cos(ang), np.zeros((t_len, pad))], -1)
    sin_t = np.concatenate([np.zeros((t_len, NOPE)), np.sin(ang), np.sin(ang), np.zeros((t_len, pad))], -1)
    return jnp.asarray(cos_t, F32), jnp.asarray(sin_t, F32)


def _rot_half_cols(w):
    half = ROPE // 2
    return jnp.concatenate([-w[..., half:], w[..., :half]], axis=-1)


def _gain_rows(g):
    half = ROPE // 2
    pad = jnp.zeros((LANES - HEAD_QK,), F32)
    r0 = jnp.concatenate([g, pad])
    r1 = jnp.concatenate([jnp.zeros((NOPE,), F32), g[NOPE + half:], g[NOPE:NOPE + half], pad])
    return jnp.stack([r0, r1])


def _block_diag_pairs(w):
    g, n, _ = w.shape
    z = jnp.zeros((g // 2, n, n), w.dtype)
    top = jnp.concatenate([w[0::2], z], axis=2)
    bot = jnp.concatenate([z, w[1::2]], axis=2)
    return jnp.concatenate([top, bot], axis=1)


def kernel(x, meta_tokens, ln1_g, w_in, q_a_norm_g, w_uq, kv_a_norm_g, w_ukv, q_norm_g, k_norm_g,
           conv_w, conv_b, lru_wa, lru_ba, lru_wi, lru_bi, lru_lambda, attn_out_g, rnn_out_g,
           w_out, ln2_g, w_gate, w_up, w_down):
    bsz, seq, d_model = x.shape
    q_lora = q_a_norm_g.shape[-1]
    kv_lora = kv_a_norm_g.shape[-1]
    d_rnn = conv_w.shape[-1]
    d_attn = N_HEAD * HEAD_V
    t_len = N_META_TOK + seq
    l = 0

    wi = w_in[l]
    o_kr = q_lora + kv_lora
    w_kr = wi[:, o_kr:o_kr + ROPE]
    win = jnp.concatenate(
        [wi[:, :o_kr], jnp.zeros((d_model, NOPE), F32), w_kr, _rot_half_cols(w_kr),
         wi[:, o_kr + ROPE:]], axis=1).astype(BF16)
    wq = w_uq[l].reshape(q_lora, N_HEAD, HEAD_QK)
    wuq = jnp.concatenate([wq, _rot_half_cols(wq[..., NOPE:])], axis=-1)
    wuq = wuq.reshape(q_lora, N_HEAD * LANES).astype(BF16)
    wkv = w_ukv[l].reshape(kv_lora, N_HEAD, NOPE + HEAD_V)
    wuk = jnp.concatenate([wkv[..., :NOPE], jnp.zeros((kv_lora, N_HEAD, LANES - NOPE), F32)], -1)
    wuk = wuk.reshape(kv_lora, N_HEAD * LANES).astype(BF16)
    wuv = wkv[..., NOPE:].reshape(kv_lora, d_attn).astype(BF16)
    gq = _gain_rows(q_norm_g[l])
    gk = _gain_rows(k_norm_g[l])
    cos_t, sin_t = _rope_tables(t_len)
    lane2 = np.arange(2 * LANES)
    hsum = jnp.asarray((lane2[:, None] // LANES == lane2[None, :] // LANES)
                       & (lane2[:, None] % LANES < HEAD_QK), BF16)
    proj_w = (ln1_g[l][None], win, q_a_norm_g[l][None], wuq, kv_a_norm_g[l][None], wuk, wuv, gq, gk, hsum)
    dims = (q_lora, kv_lora, d_rnn)

    wgate = (0.5 * jnp.concatenate(
        [_block_diag_pairs(w) for w in (lru_wa[l, 0], lru_wi[l, 0], lru_wa[l, 1], lru_wi[l, 1])],
        axis=2)).astype(BF16)
    n_slab = d_rnn // LANES
    gate_b = 0.5 * jnp.stack([lru_ba[l, 0], lru_bi[l, 0], lru_ba[l, 1], lru_bi[l, 1]])
    gate_b = gate_b.reshape(4, n_slab, LANES).transpose(1, 0, 2).reshape(n_slab, 1, 4 * LANES)
    b_hi = gate_b.astype(BF16)
    b_lo = (gate_b - b_hi.astype(F32)).astype(BF16)
    wgate = jnp.concatenate(
        [wgate, b_hi, b_lo, jnp.zeros((n_slab, LANES - 2, 4 * LANES), BF16)], axis=1)

    tm = 512 if seq % 512 == 0 else seq
    q, k, v, xr, xg = _project(x, tm, cos_t[N_META_TOK:], sin_t[N_META_TOK:], proj_w, dims)
    _, km, vm, xrm, _ = _project(meta_tokens[None].astype(x.dtype), N_META_TOK,
                                 cos_t[:N_META_TOK], sin_t[:N_META_TOK], proj_w, dims)

    o_rnn = _rnn(xr, xg, xrm, conv_w[l], conv_b[l][None], wgate, lru_lambda[l], rnn_out_g[l][None])

    tq = 2048 if seq % 2048 == 0 else seq
    o_attn = _attention(q, k, v, km, vm, tq)

    wo = w_out[l].astype(BF16)
    n_rows = bsz * seq
    tmo = 512 if n_rows % 512 == 0 else n_rows
    out = _out_ffn(x.reshape(n_rows, d_model), o_attn.reshape(n_rows, d_attn),
                   o_rnn.reshape(n_rows, d_rnn), attn_out_g[l][None], wo[:d_attn], wo[d_attn:],
                   ln2_g[l][None], w_gate[l].astype(BF16), w_up[l].astype(BF16),
                   w_down[l].astype(BF16), tmo)
    return out.reshape(bsz, seq, d_model)
```

```python
import functools
import math

import numpy as np

import jax
import jax.numpy as jnp
from jax import lax
from jax.experimental import pallas as pl
from jax.experimental.pallas import tpu as pltpu

F32 = jnp.float32
BF16 = jnp.bfloat16

N_META_TOK = 16
N_HEAD = 8
NOPE = 64
ROPE = 32
HEAD_QK = NOPE + ROPE
HEAD_V = 64
VT_ROWS = HEAD_V + 16
LRU_SCALE = 8.0
THETA = 10000.0
NORM_EPS = 1e-6
CONV_TAPS = 4
CONV_LEFT = 2

LANES = 128
SUBLANES = 8
VMEM_LIMIT = 56 * 1024 * 1024
KEY_TILE = 256
SCAN_UNROLL = 6
COMBINE_UNROLL = 43
HEADS_PER_STEP = 8
SCORE_LOOKAHEAD = 1

LOG2E = 1.4426950408889634


def _rsqrt_mean(x, n):
    return lax.rsqrt(jnp.sum(x * x, axis=-1, keepdims=True) * (1.0 / n) + NORM_EPS)


def _proj_kernel(x_ref, ln1_ref, win_ref, qag_ref, wuq_ref, kvag_ref, wuk_ref, wuv_ref,
                 cos_ref, sin_ref, gq_ref, gk_ref, hsum_ref,
                 q_out, k_out, v_out, xr_out, xg_out, *, q_lora, kv_lora, d_rnn):
    x = x_ref[0]
    d_model = x.shape[-1]
    hn = (x * _rsqrt_mean(x, d_model) * ln1_ref[...]).astype(BF16)
    o_kv = q_lora
    o_kr = o_kv + kv_lora
    o_xr = o_kr + LANES
    p = jnp.dot(hn, win_ref[:, :o_xr], preferred_element_type=F32)
    c_q = p[:, :o_kv]
    c_kv = p[:, o_kv:o_kr]
    kr = p[:, o_kr:o_xr]
    n_slab = d_rnn // LANES

    def recurrent_inputs(c):
        px = jnp.dot(hn, win_ref[:, o_xr + 2 * c * LANES:o_xr + 2 * (c + 1) * LANES],
                     preferred_element_type=F32)
        for u in range(2):
            slab = 2 * c + u
            dst = xr_out if slab < n_slab else xg_out
            dst[0, slab % n_slab] = px[:, u * LANES:(u + 1) * LANES]

    cos_t = cos_ref[...]
    sin_t = sin_ref[...]
    eps_hd = HEAD_QK * NORM_EPS
    root_hd = math.sqrt(HEAD_QK)
    hsum = hsum_ref[...]

    def head_sums(sq_pair):
        return jnp.dot(sq_pair.astype(BF16), hsum, preferred_element_type=F32)

    cqn = (c_q * _rsqrt_mean(c_q, q_lora) * qag_ref[...]).astype(BF16)
    q = jnp.dot(cqn, wuq_ref[...], preferred_element_type=F32)
    q_gain = root_hd * (HEAD_QK ** -0.5) * LOG2E
    cq = cos_t * (gq_ref[0:1, :] * q_gain)
    sq = sin_t * (gq_ref[1:2, :] * q_gain)

    ckn = (c_kv * _rsqrt_mean(c_kv, kv_lora) * kvag_ref[...]).astype(BF16)
    kn = jnp.dot(ckn, wuk_ref[...], preferred_element_type=F32)
    vt = jnp.dot(ckn, wuv_ref[...], preferred_element_type=F32).astype(BF16).T
    tm = vt.shape[1]
    ones_rows = (lax.broadcasted_iota(jnp.int32, (VT_ROWS - HEAD_V, tm), 0) == 0).astype(BF16)
    for h in range(N_HEAD):
        v_out[0, h * VT_ROWS:h * VT_ROWS + HEAD_V, :] = vt[h * HEAD_V:(h + 1) * HEAD_V]
        v_out[0, h * VT_ROWS + HEAD_V:(h + 1) * VT_ROWS, :] = ones_rows
    ck = cos_t * (gk_ref[0:1, :] * root_hd)
    sk = sin_t * (gk_ref[1:2, :] * root_hd)
    kr_sq = kr * kr
    kr_sq2 = jnp.concatenate([kr_sq, kr_sq], axis=1)
    kr_rot = kr * ck + pltpu.roll(kr, LANES - ROPE, axis=1) * sk
    n_pair = N_HEAD // 2
    n_rec = n_slab
    for pr in range(n_pair):
        for c in range(pr * n_rec // n_pair, (pr + 1) * n_rec // n_pair):
            recurrent_inputs(c)
        qp = q[:, 2 * pr * LANES:2 * (pr + 1) * LANES]
        sc = lax.rsqrt(head_sums(qp * qp) + eps_hd)
        for hh in range(2):
            h = 2 * pr + hh
            qh = qp[:, hh * LANES:(hh + 1) * LANES]
            qr = pltpu.roll(qh, LANES - ROPE, axis=1)
            qf = (qh * cq + qr * sq) * sc[:, hh * LANES:(hh + 1) * LANES]
            q_out[0, h * LANES:(h + 1) * LANES, :] = qf.astype(BF16).T
        kp = kn[:, 2 * pr * LANES:2 * (pr + 1) * LANES]
        sc = lax.rsqrt(head_sums(kp * kp + kr_sq2) + eps_hd)
        for hh in range(2):
            h = 2 * pr + hh
            kh = kp[:, hh * LANES:(hh + 1) * LANES]
            k_out[0, :, h * LANES:(h + 1) * LANES] = (
                (kh * ck + kr_rot) * sc[:, hh * LANES:(hh + 1) * LANES]).astype(BF16)


def _const_spec(shape):
    nd = len(shape)
    return pl.BlockSpec(shape, lambda *_: (0,) * nd)


def _project(x3, tm, cos_t, sin_t, wts, dims):
    q_lora, kv_lora, d_rnn = dims
    b, s, d = x3.shape
    n_slab = d_rnn // LANES
    nt = s // tm
    ln1, win, qag, wuq, kvag, wuk, wuv, gq, gk, hsum = wts
    row3 = lambda i, j: (i, j, 0)
    row4 = lambda i, j: (i, 0, j, 0)
    tab = lambda i, j: (j, 0)
    in_specs = [pl.BlockSpec((1, tm, d), row3)]
    in_specs += [_const_spec(w.shape) for w in (ln1, win, qag, wuq, kvag, wuk, wuv)]
    in_specs += [pl.BlockSpec((tm, LANES), tab), pl.BlockSpec((tm, LANES), tab)]
    in_specs += [_const_spec(gq.shape), _const_spec(gk.shape), _const_spec(hsum.shape)]
    col3 = lambda i, j: (i, 0, j)
    out_shape = (
        jax.ShapeDtypeStruct((b, N_HEAD * LANES, s), BF16),
        jax.ShapeDtypeStruct((b, s, N_HEAD * LANES), BF16),
        jax.ShapeDtypeStruct((b, N_HEAD * VT_ROWS, s), BF16),
        jax.ShapeDtypeStruct((b, n_slab, s, LANES), F32),
        jax.ShapeDtypeStruct((b, n_slab, s, LANES), F32),
    )
    out_specs = (
        pl.BlockSpec((1, N_HEAD * LANES, tm), col3),
        pl.BlockSpec((1, tm, N_HEAD * LANES), row3),
        pl.BlockSpec((1, N_HEAD * VT_ROWS, tm), col3),
        pl.BlockSpec((1, n_slab, tm, LANES), row4),
        pl.BlockSpec((1, n_slab, tm, LANES), row4),
    )
    return pl.pallas_call(
        functools.partial(_proj_kernel, q_lora=q_lora, kv_lora=kv_lora, d_rnn=d_rnn),
        grid=(b, nt),
        in_specs=in_specs,
        out_specs=out_specs,
        out_shape=out_shape,
        compiler_params=pltpu.CompilerParams(
            dimension_semantics=("arbitrary", "arbitrary"), vmem_limit_bytes=VMEM_LIMIT),
        name="proj",
    )(x3, ln1, win, qag, wuq, kvag, wuk, wuv, cos_t, sin_t, gq, gk, hsum)


def _scan_stride(t_len):
    s = -(-t_len // SUBLANES)
    while s % SUBLANES == 0:
        s += 1
    if s % 2:
        s += 1
        if s % SUBLANES == 0:
            s += 2
    return s


def _chunk_len(stride, cap=258):
    for c in range(min(cap, stride), 0, -1):
        if stride % c == 0:
            return c
    return 1


def _rnn_kernel(xr_ref, xg_ref, xrm_ref, cw_ref, cb_ref, wg_ref, lam_ref, g_ref,
                o_ref, t1, af, bf, ab, bb, *, seq, stride, rc, tm_out):
    n_slab = xr_ref.shape[1]
    t_len = N_META_TOK + seq
    pre = SUBLANES
    rows = t1.shape[1]
    n_chunk = stride // rc
    crow = rc * SUBLANES

    for s in range(n_slab):
        t1[s, 0:pre, :] = jnp.zeros((pre, LANES), F32)
        t1[s, pre:pre + N_META_TOK, :] = xrm_ref[0, s]
        t1[s, pre + N_META_TOK:pre + t_len, :] = xr_ref[0, s]
        t1[s, pre + t_len:rows, :] = jnp.zeros((rows - pre - t_len, LANES), F32)

    lam = lam_ref[...]
    nl = -lam
    softplus = jnp.maximum(nl, 0.0) + jnp.log(1.0 + jnp.exp(-jnp.abs(nl)))
    hcn = (0.5 * LRU_SCALE) * softplus
    cw = cw_ref[...]
    cb = cb_ref[...]
    ones_k = jnp.ones((crow, LANES), BF16)

    padded = stride * SUBLANES > t_len
    if padded:
        ridx = lax.broadcasted_iota(jnp.int32, (crow, LANES), 0)
        t_local = (ridx & (SUBLANES - 1)) * stride + (ridx >> 3)

    def gate_chunk(c, carry):
        r0 = c * rc
        row0 = pl.multiple_of(c * crow, SUBLANES)
        for s in range(n_slab):
            ls = slice(s * LANES, (s + 1) * LANES)
            taps = [t1[s, pl.ds(r0 + pre - CONV_LEFT + j, SUBLANES, stride=stride), :]
                    for j in range(rc + CONV_TAPS - 1)]
            xc_rows = []
            for rr in range(rc):
                acc = taps[rr] * cw[0:1, ls]
                for j in range(1, CONV_TAPS):
                    acc = acc + taps[rr + j] * cw[j:j + 1, ls]
                xc_rows.append(acc + cb[:, ls])
            xc = jnp.concatenate(xc_rows, axis=0)
            hx = 0.5 * xc
            z = jnp.dot(jnp.concatenate([xc.astype(BF16), ones_k], axis=1), wg_ref[s],
                        preferred_element_type=F32)
            for d, (a_ref, b_ref) in enumerate(((af, bf), (ab, bb))):
                ta = jnp.tanh(z[:, (2 * d) * LANES:(2 * d + 1) * LANES])
                ti = jnp.tanh(z[:, (2 * d + 1) * LANES:(2 * d + 2) * LANES])
                hc = hcn[d:d + 1, ls]
                nla = ta * hc + hc
                a = jnp.exp2(nla * (-LOG2E))
                one_m_a2 = jnp.tanh(nla) * (1.0 + a * a)
                root = jnp.where(one_m_a2 > 0.0, one_m_a2 * lax.rsqrt(one_m_a2), 0.0)
                bv = root * (ti * hx + hx)
                if d == 1 and padded:
                    bv = jnp.where((t_local + r0) < t_len, bv, 0.0)
                a_ref[s, pl.ds(row0, crow), :] = a
                b_ref[s, pl.ds(row0, crow), :] = bv
        return carry

    lax.fori_loop(0, n_chunk, gate_chunk, 0)

    zero = jnp.zeros((SUBLANES, LANES), F32)
    one = jnp.ones((SUBLANES, LANES), F32)

    def scan_step(r, carry):
        hf, pf, hb, pb = carry
        rf = pl.multiple_of(r * SUBLANES, SUBLANES)
        rb = pl.multiple_of((stride - 1 - r) * SUBLANES, SUBLANES)
        nhf, npf, nhb, npb = [], [], [], []
        for s in range(n_slab):
            a = af[s, pl.ds(rf, SUBLANES), :]
            h = a * hf[s] + bf[s, pl.ds(rf, SUBLANES), :]
            pr = a * pf[s]
            bf[s, pl.ds(rf, SUBLANES), :] = h
            af[s, pl.ds(rf, SUBLANES), :] = pr
            nhf.append(h)
            npf.append(pr)
            a = ab[s, pl.ds(rb, SUBLANES), :]
            h = a * hb[s] + bb[s, pl.ds(rb, SUBLANES), :]
            pr = a * pb[s]
            bb[s, pl.ds(rb, SUBLANES), :] = h
            ab[s, pl.ds(rb, SUBLANES), :] = pr
            nhb.append(h)
            npb.append(pr)
        return tuple(nhf), tuple(npf), tuple(nhb), tuple(npb)

    init = ((zero,) * n_slab, (one,) * n_slab, (zero,) * n_slab, (one,) * n_slab)
    hf, pf, hb, pb = lax.fori_loop(0, stride, scan_step, init, unroll=SCAN_UNROLL)

    sub = lax.broadcasted_iota(jnp.int32, (SUBLANES, LANES), 0)
    cf, cbk = [], []
    for s in range(n_slab):
        c = zero
        for _ in range(SUBLANES - 1):
            c = jnp.where(sub == 0, 0.0, pltpu.roll(pf[s] * c + hf[s], 1, axis=0))
        cf.append(c)
        c = zero
        for _ in range(SUBLANES - 1):
            c = jnp.where(sub == SUBLANES - 1, 0.0, pltpu.roll(pb[s] * c + hb[s], SUBLANES - 1, axis=0))
        cbk.append(c)

    def combine(r, carry):
        rp = pl.multiple_of(r * SUBLANES, SUBLANES)
        for s in range(n_slab):
            y = (bf[s, pl.ds(rp, SUBLANES), :] + af[s, pl.ds(rp, SUBLANES), :] * cf[s]
                 + bb[s, pl.ds(rp, SUBLANES), :] + ab[s, pl.ds(rp, SUBLANES), :] * cbk[s])
            t1[s, pl.ds(r + pre, SUBLANES, stride=stride), :] = y
        return carry

    lax.fori_loop(0, stride, combine, 0, unroll=COMBINE_UNROLL)

    d_rnn = n_slab * LANES
    g = g_ref[...]
    k0 = math.sqrt(2.0 / math.pi)

    def finish(j, carry):
        ro = pl.multiple_of(j * tm_out, 2 * SUBLANES)
        rt = pl.multiple_of(j * tm_out + pre + N_META_TOK, SUBLANES)
        ys = []
        ssq = jnp.zeros((tm_out, 1), F32)
        for s in range(n_slab):
            y = t1[s, pl.ds(rt, tm_out), :]
            xg = xg_ref[0, s, pl.ds(ro, tm_out), :]
            yh = y * (0.5 * xg)
            y = yh + yh * jnp.tanh(xg * (k0 + (k0 * 0.044715) * (xg * xg)))
            ssq = ssq + jnp.sum(y * y, axis=-1, keepdims=True)
            ys.append(y)
        sc = lax.rsqrt(ssq * (1.0 / d_rnn) + NORM_EPS)
        for s in range(n_slab):
            ls = slice(s * LANES, (s + 1) * LANES)
            o_ref[0, pl.ds(ro, tm_out), ls] = (ys[s] * sc * g[:, ls]).astype(BF16)
        return carry

    lax.fori_loop(0, seq // tm_out, finish, 0)


def _rnn(xr, xg, xrm, cw, cb, wg, lam, g):
    b, n_slab, seq, _ = xr.shape
    t_len = N_META_TOK + seq
    stride = _scan_stride(t_len)
    rc = _chunk_len(stride)
    rows_perm = stride * SUBLANES
    rows_time = SUBLANES + rows_perm + SUBLANES
    tm_out = 512 if seq % 512 == 0 else seq
    blk = lambda i: (i, 0, 0, 0)
    in_specs = [
        pl.BlockSpec((1, n_slab, seq, LANES), blk),
        pl.BlockSpec((1, n_slab, seq, LANES), blk),
        _const_spec(xrm.shape),
    ] + [_const_spec(w.shape) for w in (cw, cb, wg, lam, g)]
    scratch = [pltpu.VMEM((n_slab, rows_time, LANES), F32)]
    scratch += [pltpu.VMEM((n_slab, rows_perm, LANES), F32) for _ in range(4)]
    return pl.pallas_call(
        functools.partial(_rnn_kernel, seq=seq, stride=stride, rc=rc, tm_out=tm_out),
        grid=(b,),
        in_specs=in_specs,
        out_specs=pl.BlockSpec((1, seq, n_slab * LANES), lambda i: (i, 0, 0)),
        out_shape=jax.ShapeDtypeStruct((b, seq, n_slab * LANES), BF16),
        scratch_shapes=scratch,
        compiler_params=pltpu.CompilerParams(
            dimension_semantics=("arbitrary",), vmem_limit_bytes=VMEM_LIMIT),
        name="rnn",
    )(xr, xg, xrm, cw, cb, wg, lam, g)


def _attn_kernel(qt_ref, k_ref, vt_ref, km_ref, vtm_ref, o_ref, *, tq):
    n_key = k_ref.shape[1]

    n_stream = HEADS_PER_STEP * (n_key // tq)
    n_tile = n_key // KEY_TILE
    tiles = [(t * KEY_TILE, (t + 1) * KEY_TILE) for t in range(n_tile)]
    tiles[-1] = (tiles[-1][0], n_key + N_META_TOK)
    steps = [(i, t) for i in range(n_stream) for t in range(n_tile)]

    def score_tile(i, t):
        j, hh = divmod(i, HEADS_PER_STEP)
        k0, k1 = tiles[t]
        ls = slice(hh * LANES, (hh + 1) * LANES)
        keys = k_ref[0, k0:min(k1, n_key), ls]
        if k1 > n_key:
            keys = jnp.concatenate([keys, km_ref[0, :, ls]], axis=0)
        return jnp.dot(keys, qt_ref[0, ls, j * tq:(j + 1) * tq],
                       preferred_element_type=F32)

    outs = []
    pending = [score_tile(*steps[n]) for n in range(SCORE_LOOKAHEAD)]
    m, acc = None, None
    for n, (i, t) in enumerate(steps):
        if n + SCORE_LOOKAHEAD < len(steps):
            pending.append(score_tile(*steps[n + SCORE_LOOKAHEAD]))
        st = pending.pop(0)
        k0, k1 = tiles[t]
        hh = i % HEADS_PER_STEP
        vs = slice(hh * VT_ROWS, (hh + 1) * VT_ROWS)
        st3 = st.reshape((k1 - k0) // SUBLANES, SUBLANES, tq)
        tmax = jnp.max(jnp.max(st3, axis=0), axis=0, keepdims=True)
        m_new = tmax if m is None else jnp.maximum(m, tmax)
        mb = jnp.broadcast_to(m_new, (SUBLANES, tq))
        pb = jnp.exp2(st3 - mb[None]).reshape(k1 - k0, tq).astype(BF16)
        part = jnp.dot(vt_ref[0, vs, k0:k0 + KEY_TILE], pb[:KEY_TILE], preferred_element_type=F32)
        if k1 - k0 > KEY_TILE:
            part = part + jnp.dot(vtm_ref[0, vs, :], pb[KEY_TILE:], preferred_element_type=F32)
        acc = part if m is None else acc * jnp.exp2(m - m_new) + part
        m = m_new
        if t == n_tile - 1:
            outs.append(acc[:HEAD_V] * (1.0 / acc[HEAD_V:HEAD_V + 1]))
            m, acc = None, None
            if hh == HEADS_PER_STEP - 1:
                j = i // HEADS_PER_STEP
                o_ref[0, j * tq:(j + 1) * tq, :] = jnp.concatenate(outs, axis=0).T
                outs = []


def _attention(qt, k, vt, km, vtm, tq):
    b, s, _ = k.shape
    hps = HEADS_PER_STEP
    return pl.pallas_call(
        functools.partial(_attn_kernel, tq=tq),
        grid=(b, N_HEAD // hps),
        in_specs=[
            pl.BlockSpec((1, hps * LANES, s), lambda i, p: (i, p, 0)),
            pl.BlockSpec((1, s, hps * LANES), lambda i, p: (i, 0, p)),
            pl.BlockSpec((1, hps * VT_ROWS, s), lambda i, p: (i, p, 0)),
            pl.BlockSpec((1, N_META_TOK, hps * LANES), lambda i, p: (0, 0, p)),
            pl.BlockSpec((1, hps * VT_ROWS, N_META_TOK), lambda i, p: (0, p, 0)),
        ],
        out_specs=pl.BlockSpec((1, s, hps * HEAD_V), lambda i, p: (i, 0, p)),
        out_shape=jax.ShapeDtypeStruct((b, s, N_HEAD * HEAD_V), F32),
        compiler_params=pltpu.CompilerParams(
            dimension_semantics=("arbitrary", "arbitrary"), vmem_limit_bytes=VMEM_LIMIT),
        name="attn",
    )(qt, k, vt, km, vtm)


def _out_kernel(x_ref, oa_ref, on_ref, ag_ref, woa_ref, wor_ref, ln2_ref, wg_ref, wu_ref, wd_ref,
                o_ref, *, ff_chunk):
    oa = oa_ref[...]
    oan = (oa * _rsqrt_mean(oa, oa.shape[-1]) * ag_ref[...]).astype(BF16)
    h = x_ref[...] + jnp.dot(oan, woa_ref[...], preferred_element_type=F32)
    h = h + jnp.dot(on_ref[...], wor_ref[...], preferred_element_type=F32)
    hn = (h * _rsqrt_mean(h, h.shape[-1]) * ln2_ref[...]).astype(BF16)
    d_ff = wg_ref.shape[1]
    acc = h
    for c in range(d_ff // ff_chunk):
        cs = slice(c * ff_chunk, (c + 1) * ff_chunk)
        gt = jnp.dot(hn, wg_ref[:, cs], preferred_element_type=F32)
        up = jnp.dot(hn, wu_ref[:, cs], preferred_element_type=F32)
        act = (gt * jax.nn.sigmoid(gt) * up).astype(BF16)
        acc = acc + jnp.dot(act, wd_ref[cs, :], preferred_element_type=F32)
    o_ref[...] = acc


def _out_ffn(x2, oa2, on2, ag, woa, wor, ln2, wg, wu, wd, tm):
    n, d = x2.shape
    d_ff = wg.shape[1]
    ff_chunk = d_ff
    row = lambda i: (i, 0)
    single = pl.Buffered(1)
    wspec = lambda w: pl.BlockSpec(w.shape, lambda i: (0, 0), pipeline_mode=single)
    return pl.pallas_call(
        functools.partial(_out_kernel, ff_chunk=ff_chunk),
        grid=(n // tm,),
        in_specs=[
            pl.BlockSpec((tm, d), row),
            pl.BlockSpec((tm, oa2.shape[1]), row),
            pl.BlockSpec((tm, on2.shape[1]), row),
            wspec(ag), wspec(woa), wspec(wor), wspec(ln2), wspec(wg), wspec(wu), wspec(wd),
        ],
        out_specs=pl.BlockSpec((tm, d), row),
        out_shape=jax.ShapeDtypeStruct((n, d), F32),
        compiler_params=pltpu.CompilerParams(
            dimension_semantics=("arbitrary",), vmem_limit_bytes=VMEM_LIMIT),
        name="out_ffn",
    )(x2, oa2, on2, ag, woa, wor, ln2, wg, wu, wd)


def _rope_tables(t_len):
    half = ROPE // 2
    freqs = 1.0 / (THETA ** (np.arange(half, dtype=np.float64) / half))
    ang = np.arange(t_len, dtype=np.float64)[:, None] * freqs[None, :]
    pad = LANES - HEAD_QK
    cos_t = np.concatenate([np.ones((t_len, NOPE)), np.cos(ang), np.cos(ang), np.zeros((t_len, pad))], -1)
    sin_t = np.concatenate([np.zeros((t_len, NOPE)), np.sin(ang), np.sin(ang), np.zeros((t_len, pad))], -1)
    return jnp.asarray(cos_t, F32), jnp.asarray(sin_t, F32)


def _rot_half_cols(w):
    half = ROPE // 2
    return jnp.concatenate([-w[..., half:], w[..., :half]], axis=-1)


def _gain_rows(g):
    half = ROPE // 2
    pad = jnp.zeros((LANES - HEAD_QK,), F32)
    r0 = jnp.concatenate([g, pad])
    r1 = jnp.concatenate([jnp.zeros((NOPE,), F32), g[NOPE + half:], g[NOPE:NOPE + half], pad])
    return jnp.stack([r0, r1])


def _block_diag_pairs(w):
    g, n, _ = w.shape
    z = jnp.zeros((g // 2, n, n), w.dtype)
    top = jnp.concatenate([w[0::2], z], axis=2)
    bot = jnp.concatenate([z, w[1::2]], axis=2)
    return jnp.concatenate([top, bot], axis=1)


def kernel(x, meta_tokens, ln1_g, w_in, q_a_norm_g, w_uq, kv_a_norm_g, w_ukv, q_norm_g, k_norm_g,
           conv_w, conv_b, lru_wa, lru_ba, lru_wi, lru_bi, lru_lambda, attn_out_g, rnn_out_g,
           w_out, ln2_g, w_gate, w_up, w_down):
    bsz, seq, d_model = x.shape
    q_lora = q_a_norm_g.shape[-1]
    kv_lora = kv_a_norm_g.shape[-1]
    d_rnn = conv_w.shape[-1]
    d_attn = N_HEAD * HEAD_V
    t_len = N_META_TOK + seq
    l = 0

    wi = w_in[l]
    o_kr = q_lora + kv_lora
    w_kr = wi[:, o_kr:o_kr + ROPE]
    win = jnp.concatenate(
        [wi[:, :o_kr], jnp.zeros((d_model, NOPE), F32), w_kr, _rot_half_cols(w_kr),
         wi[:, o_kr + ROPE:]], axis=1).astype(BF16)
    wq = w_uq[l].reshape(q_lora, N_HEAD, HEAD_QK)
    wuq = jnp.concatenate([wq, _rot_half_cols(wq[..., NOPE:])], axis=-1)
    wuq = wuq.reshape(q_lora, N_HEAD * LANES).astype(BF16)
    wkv = w_ukv[l].reshape(kv_lora, N_HEAD, NOPE + HEAD_V)
    wuk = jnp.concatenate([wkv[..., :NOPE], jnp.zeros((kv_lora, N_HEAD, LANES - NOPE), F32)], -1)
    wuk = wuk.reshape(kv_lora, N_HEAD * LANES).astype(BF16)
    wuv = wkv[..., NOPE:].reshape(kv_lora, d_attn).astype(BF16)
    gq = _gain_rows(q_norm_g[l])
    gk = _gain_rows(k_norm_g[l])
    cos_t, sin_t = _rope_tables(t_len)
    lane2 = np.arange(2 * LANES)
    hsum = jnp.asarray((lane2[:, None] // LANES == lane2[None, :] // LANES)
                       & (lane2[:, None] % LANES < HEAD_QK), BF16)
    proj_w = (ln1_g[l][None], win, q_a_norm_g[l][None], wuq, kv_a_norm_g[l][None], wuk, wuv, gq, gk, hsum)
    dims = (q_lora, kv_lora, d_rnn)

    wgate = (0.5 * jnp.concatenate(
        [_block_diag_pairs(w) for w in (lru_wa[l, 0], lru_wi[l, 0], lru_wa[l, 1], lru_wi[l, 1])],
        axis=2)).astype(BF16)
    n_slab = d_rnn // LANES
    gate_b = 0.5 * jnp.stack([lru_ba[l, 0], lru_bi[l, 0], lru_ba[l, 1], lru_bi[l, 1]])
    gate_b = gate_b.reshape(4, n_slab, LANES).transpose(1, 0, 2).reshape(n_slab, 1, 4 * LANES)
    b_hi = gate_b.astype(BF16)
    b_lo = (gate_b - b_hi.astype(F32)).astype(BF16)
    wgate = jnp.concatenate(
        [wgate, b_hi, b_lo, jnp.zeros((n_slab, LANES - 2, 4 * LANES), BF16)], axis=1)

    tm = 512 if seq % 512 == 0 else seq
    q, k, v, xr, xg = _project(x, tm, cos_t[N_META_TOK:], sin_t[N_META_TOK:], proj_w, dims)
    _, km, vm, xrm, _ = _project(meta_tokens[None].astype(x.dtype), N_META_TOK,
                                 cos_t[:N_META_TOK], sin_t[:N_META_TOK], proj_w, dims)

    o_rnn = _rnn(xr, xg, xrm, conv_w[l], conv_b[l][None], wgate, lru_lambda[l], rnn_out_g[l][None])

    tq = 2048 if seq % 2048 == 0 else seq
    o_attn = _attention(q, k, v, km, vm, tq)

    wo = w_out[l].astype(BF16)
    n_rows = bsz * seq
    tmo = 512 if n_rows % 512 == 0 else n_rows
    out = _out_ffn(x.reshape(n_rows, d_model), o_attn.reshape(n_rows, d_attn),
                   o_rnn.reshape(n_rows, d_rnn), attn_out_g[l][None], wo[:d_attn], wo[d_attn:],
                   ln2_g[l][None], w_gate[l].astype(BF16), w_up[l].astype(BF16),
                   w_down[l].astype(BF16), tmo)
    return out.reshape(bsz, seq, d_model)
```

```python
import functools
import math

import numpy as np

import jax
import jax.numpy as jnp
from jax import lax
from jax.experimental import pallas as pl
from jax.experimental.pallas import tpu as pltpu

F32 = jnp.float32
BF16 = jnp.bfloat16

N_META_TOK = 16
N_HEAD = 8
NOPE = 64
ROPE = 32
HEAD_QK = NOPE + ROPE
HEAD_V = 64
VT_ROWS = HEAD_V + 16
LRU_SCALE = 8.0
THETA = 10000.0
NORM_EPS = 1e-6
CONV_TAPS = 4
CONV_LEFT = 2

LANES = 128
SUBLANES = 8
VMEM_LIMIT = 56 * 1024 * 1024
KEY_TILE = 256
SCAN_UNROLL = 6
COMBINE_UNROLL = 43
HEADS_PER_STEP = 2
SCORE_LOOKAHEAD = 1

LOG2E = 1.4426950408889634


def _rsqrt_mean(x, n):
    return lax.rsqrt(jnp.sum(x * x, axis=-1, keepdims=True) * (1.0 / n) + NORM_EPS)


def _proj_kernel(x_ref, ln1_ref, win_ref, qag_ref, wuq_ref, kvag_ref, wuk_ref, wuv_ref,
                 cos_ref, sin_ref, gq_ref, gk_ref, hsum_ref,
                 q_out, k_out, v_out, xr_out, xg_out, *, q_lora, kv_lora, d_rnn):
    x = x_ref[0]
    d_model = x.shape[-1]
    hn = (x * _rsqrt_mean(x, d_model) * ln1_ref[...]).astype(BF16)
    o_kv = q_lora
    o_kr = o_kv + kv_lora
    o_xr = o_kr + LANES
    p = jnp.dot(hn, win_ref[:, :o_xr], preferred_element_type=F32)
    c_q = p[:, :o_kv]
    c_kv = p[:, o_kv:o_kr]
    kr = p[:, o_kr:o_xr]
    n_slab = d_rnn // LANES

    def recurrent_inputs(c):
        px = jnp.dot(hn, win_ref[:, o_xr + 2 * c * LANES:o_xr + 2 * (c + 1) * LANES],
                     preferred_element_type=F32)
        for u in range(2):
            slab = 2 * c + u
            dst = xr_out if slab < n_slab else xg_out
            dst[0, slab % n_slab] = px[:, u * LANES:(u + 1) * LANES]

    cos_t = cos_ref[...]
    sin_t = sin_ref[...]
    eps_hd = HEAD_QK * NORM_EPS
    root_hd = math.sqrt(HEAD_QK)
    hsum = hsum_ref[...]

    def head_sums(sq_pair):
        return jnp.dot(sq_pair.astype(BF16), hsum, preferred_element_type=F32)

    cqn = (c_q * _rsqrt_mean(c_q, q_lora) * qag_ref[...]).astype(BF16)
    q = jnp.dot(cqn, wuq_ref[...], preferred_element_type=F32)
    q_gain = root_hd * (HEAD_QK ** -0.5) * LOG2E
    cq = cos_t * (gq_ref[0:1, :] * q_gain)
    sq = sin_t * (gq_ref[1:2, :] * q_gain)

    ckn = (c_kv * _rsqrt_mean(c_kv, kv_lora) * kvag_ref[...]).astype(BF16)
    kn = jnp.dot(ckn, wuk_ref[...], preferred_element_type=F32)
    vt = jnp.dot(ckn, wuv_ref[...], preferred_element_type=F32).astype(BF16).T
    tm = vt.shape[1]
    ones_rows = (lax.broadcasted_iota(jnp.int32, (VT_ROWS - HEAD_V, tm), 0) == 0).astype(BF16)
    for h in range(N_HEAD):
        v_out[0, h * VT_ROWS:h * VT_ROWS + HEAD_V, :] = vt[h * HEAD_V:(h + 1) * HEAD_V]
        v_out[0, h * VT_ROWS + HEAD_V:(h + 1) * VT_ROWS, :] = ones_rows
    ck = cos_t * (gk_ref[0:1, :] * root_hd)
    sk = sin_t * (gk_ref[1:2, :] * root_hd)
    kr_sq = kr * kr
    kr_sq2 = jnp.concatenate([kr_sq, kr_sq], axis=1)
    kr_rot = kr * ck + pltpu.roll(kr, LANES - ROPE, axis=1) * sk
    n_pair = N_HEAD // 2
    n_rec = n_slab
    for pr in range(n_pair):
        for c in range(pr * n_rec // n_pair, (pr + 1) * n_rec // n_pair):
            recurrent_inputs(c)
        qp = q[:, 2 * pr * LANES:2 * (pr + 1) * LANES]
        sc = lax.rsqrt(head_sums(qp * qp) + eps_hd)
        for hh in range(2):
            h = 2 * pr + hh
            qh = qp[:, hh * LANES:(hh + 1) * LANES]
            qr = pltpu.roll(qh, LANES - ROPE, axis=1)
            qf = (qh * cq + qr * sq) * sc[:, hh * LANES:(hh + 1) * LANES]
            q_out[0, h * LANES:(h + 1) * LANES, :] = qf.astype(BF16).T
        kp = kn[:, 2 * pr * LANES:2 * (pr + 1) * LANES]
        sc = lax.rsqrt(head_sums(kp * kp + kr_sq2) + eps_hd)
        for hh in range(2):
            h = 2 * pr + hh
            kh = kp[:, hh * LANES:(hh + 1) * LANES]
            k_out[0, :, h * LANES:(h + 1) * LANES] = (
                (kh * ck + kr_rot) * sc[:, hh * LANES:(hh + 1) * LANES]).astype(BF16)


def _const_spec(shape):
    nd = len(shape)
    return pl.BlockSpec(shape, lambda *_: (0,) * nd)


def _project(x3, tm, cos_t, sin_t, wts, dims):
    q_lora, kv_lora, d_rnn = dims
    b, s, d = x3.shape
    n_slab = d_rnn // LANES
    nt = s // tm
    ln1, win, qag, wuq, kvag, wuk, wuv, gq, gk, hsum = wts
    row3 = lambda i, j: (i, j, 0)
    row4 = lambda i, j: (i, 0, j, 0)
    tab = lambda i, j: (j, 0)
    in_specs = [pl.BlockSpec((1, tm, d), row3)]
    in_specs += [_const_spec(w.shape) for w in (ln1, win, qag, wuq, kvag, wuk, wuv)]
    in_specs += [pl.BlockSpec((tm, LANES), tab), pl.BlockSpec((tm, LANES), tab)]
    in_specs += [_const_spec(gq.shape), _const_spec(gk.shape), _const_spec(hsum.shape)]
    col3 = lambda i, j: (i, 0, j)
    out_shape = (
        jax.ShapeDtypeStruct((b, N_HEAD * LANES, s), BF16),
        jax.ShapeDtypeStruct((b, s, N_HEAD * LANES), BF16),
        jax.ShapeDtypeStruct((b, N_HEAD * VT_ROWS, s), BF16),
        jax.ShapeDtypeStruct((b, n_slab, s, LANES), F32),
        jax.ShapeDtypeStruct((b, n_slab, s, LANES), F32),
    )
    out_specs = (
        pl.BlockSpec((1, N_HEAD * LANES, tm), col3),
        pl.BlockSpec((1, tm, N_HEAD * LANES), row3),
        pl.BlockSpec((1, N_HEAD * VT_ROWS, tm), col3),
        pl.BlockSpec((1, n_slab, tm, LANES), row4),
        pl.BlockSpec((1, n_slab, tm, LANES), row4),
    )
    return pl.pallas_call(
        functools.partial(_proj_kernel, q_lora=q_lora, kv_lora=kv_lora, d_rnn=d_rnn),
        grid=(b, nt),
        in_specs=in_specs,
        out_specs=out_specs,
        out_shape=out_shape,
        compiler_params=pltpu.CompilerParams(
            dimension_semantics=("arbitrary", "arbitrary"), vmem_limit_bytes=VMEM_LIMIT),
        name="proj",
    )(x3, ln1, win, qag, wuq, kvag, wuk, wuv, cos_t, sin_t, gq, gk, hsum)


def _scan_stride(t_len):
    s = -(-t_len // SUBLANES)
    while s % SUBLANES == 0:
        s += 1
    if s % 2:
        s += 1
        if s % SUBLANES == 0:
            s += 2
    return s


def _chunk_len(stride, cap=258):
    for c in range(min(cap, stride), 0, -1):
        if stride % c == 0:
            return c
    return 1


def _rnn_kernel(xr_ref, xg_ref, xrm_ref, cw_ref, cb_ref, wg_ref, lam_ref, g_ref,
                o_ref, t1, af, bf, ab, bb, *, seq, stride, rc, tm_out):
    n_slab = xr_ref.shape[1]
    t_len = N_META_TOK + seq
    pre = SUBLANES
    rows = t1.shape[1]
    n_chunk = stride // rc
    crow = rc * SUBLANES

    for s in range(n_slab):
        t1[s, 0:pre, :] = jnp.zeros((pre, LANES), F32)
        t1[s, pre:pre + N_META_TOK, :] = xrm_ref[0, s]
        t1[s, pre + N_META_TOK:pre + t_len, :] = xr_ref[0, s]
        t1[s, pre + t_len:rows, :] = jnp.zeros((rows - pre - t_len, LANES), F32)

    lam = lam_ref[...]
    nl = -lam
    softplus = jnp.maximum(nl, 0.0) + jnp.log(1.0 + jnp.exp(-jnp.abs(nl)))
    hcn = (0.5 * LRU_SCALE) * softplus
    cw = cw_ref[...]
    cb = cb_ref[...]
    ones_k = jnp.ones((crow, LANES), BF16)

    padded = stride * SUBLANES > t_len
    if padded:
        ridx = lax.broadcasted_iota(jnp.int32, (crow, LANES), 0)
        t_local = (ridx & (SUBLANES - 1)) * stride + (ridx >> 3)

    def gate_chunk(c, carry):
        r0 = c * rc
        row0 = pl.multiple_of(c * crow, SUBLANES)
        for s in range(n_slab):
            ls = slice(s * LANES, (s + 1) * LANES)
            taps = [t1[s, pl.ds(r0 + pre - CONV_LEFT + j, SUBLANES, stride=stride), :]
                    for j in range(rc + CONV_TAPS - 1)]
            xc_rows = []
            for rr in range(rc):
                acc = taps[rr] * cw[0:1, ls]
                for j in range(1, CONV_TAPS):
                    acc = acc + taps[rr + j] * cw[j:j + 1, ls]
                xc_rows.append(acc + cb[:, ls])
            xc = jnp.concatenate(xc_rows, axis=0)
            hx = 0.5 * xc
            z = jnp.dot(jnp.concatenate([xc.astype(BF16), ones_k], axis=1), wg_ref[s],
                        preferred_element_type=F32)
            for d, (a_ref, b_ref) in enumerate(((af, bf), (ab, bb))):
                ta = jnp.tanh(z[:, (2 * d) * LANES:(2 * d + 1) * LANES])
                ti = jnp.tanh(z[:, (2 * d + 1) * LANES:(2 * d + 2) * LANES])
                hc = hcn[d:d + 1, ls]
                nla = ta * hc + hc
                a = jnp.exp2(nla * (-LOG2E))
                one_m_a2 = jnp.tanh(nla) * (1.0 + a * a)
                root = jnp.where(one_m_a2 > 0.0, one_m_a2 * lax.rsqrt(one_m_a2), 0.0)
                bv = root * (ti * hx + hx)
                if d == 1 and padded:
                    bv = jnp.where((t_local + r0) < t_len, bv, 0.0)
                a_ref[s, pl.ds(row0, crow), :] = a
                b_ref[s, pl.ds(row0, crow), :] = bv
        return carry

    lax.fori_loop(0, n_chunk, gate_chunk, 0)

    zero = jnp.zeros((SUBLANES, LANES), F32)
    one = jnp.ones((SUBLANES, LANES), F32)

    def scan_step(r, carry):
        hf, pf, hb, pb = carry
        rf = pl.multiple_of(r * SUBLANES, SUBLANES)
        rb = pl.multiple_of((stride - 1 - r) * SUBLANES, SUBLANES)
        nhf, npf, nhb, npb = [], [], [], []
        for s in range(n_slab):
            a = af[s, pl.ds(rf, SUBLANES), :]
            h = a * hf[s] + bf[s, pl.ds(rf, SUBLANES), :]
            pr = a * pf[s]
            bf[s, pl.ds(rf, SUBLANES), :] = h
            af[s, pl.ds(rf, SUBLANES), :] = pr
            nhf.append(h)
            npf.append(pr)
            a = ab[s, pl.ds(rb, SUBLANES), :]
            h = a * hb[s] + bb[s, pl.ds(rb, SUBLANES), :]
            pr = a * pb[s]
            bb[s, pl.ds(rb, SUBLANES), :] = h
            ab[s, pl.ds(rb, SUBLANES), :] = pr
            nhb.append(h)
            npb.append(pr)
        return tuple(nhf), tuple(npf), tuple(nhb), tuple(npb)

    init = ((zero,) * n_slab, (one,) * n_slab, (zero,) * n_slab, (one,) * n_slab)
    hf, pf, hb, pb = lax.fori_loop(0, stride, scan_step, init, unroll=SCAN_UNROLL)

    sub = lax.broadcasted_iota(jnp.int32, (SUBLANES, LANES), 0)
    cf, cbk = [], []
    for s in range(n_slab):
        c = zero
        for _ in range(SUBLANES - 1):
            c = jnp.where(sub == 0, 0.0, pltpu.roll(pf[s] * c + hf[s], 1, axis=0))
        cf.append(c)
        c = zero
        for _ in range(SUBLANES - 1):
            c = jnp.where(sub == SUBLANES - 1, 0.0, pltpu.roll(pb[s] * c + hb[s], SUBLANES - 1, axis=0))
        cbk.append(c)

    def combine(r, carry):
        rp = pl.multiple_of(r * SUBLANES, SUBLANES)
        for s in range(n_slab):
            y = (bf[s, pl.ds(rp, SUBLANES), :] + af[s, pl.ds(rp, SUBLANES), :] * cf[s]
                 + bb[s, pl.ds(rp, SUBLANES), :] + ab[s, pl.ds(rp, SUBLANES), :] * cbk[s])
            t1[s, pl.ds(r + pre, SUBLANES, stride=stride), :] = y
        return carry

    lax.fori_loop(0, stride, combine, 0, unroll=COMBINE_UNROLL)

    d_rnn = n_slab * LANES
    g = g_ref[...]
    k0 = math.sqrt(2.0 / math.pi)

    def finish(j, carry):
        ro = pl.multiple_of(j * tm_out, 2 * SUBLANES)
        rt = pl.multiple_of(j * tm_out + pre + N_META_TOK, SUBLANES)
        ys = []
        ssq = jnp.zeros((tm_out, 1), F32)
        for s in range(n_slab):
            y = t1[s, pl.ds(rt, tm_out), :]
            xg = xg_ref[0, s, pl.ds(ro, tm_out), :]
            yh = y * (0.5 * xg)
            y = yh + yh * jnp.tanh(xg * (k0 + (k0 * 0.044715) * (xg * xg)))
            ssq = ssq + jnp.sum(y * y, axis=-1, keepdims=True)
            ys.append(y)
        sc = lax.rsqrt(ssq * (1.0 / d_rnn) + NORM_EPS)
        for s in range(n_slab):
            ls = slice(s * LANES, (s + 1) * LANES)
            o_ref[0, pl.ds(ro, tm_out), ls] = (ys[s] * sc * g[:, ls]).astype(BF16)
        return carry

    lax.fori_loop(0, seq // tm_out, finish, 0)


def _rnn(xr, xg, xrm, cw, cb, wg, lam, g):
    b, n_slab, seq, _ = xr.shape
    t_len = N_META_TOK + seq
    stride = _scan_stride(t_len)
    rc = _chunk_len(stride)
    rows_perm = stride * SUBLANES
    rows_time = SUBLANES + rows_perm + SUBLANES
    tm_out = 512 if seq % 512 == 0 else seq
    blk = lambda i: (i, 0, 0, 0)
    in_specs = [
        pl.BlockSpec((1, n_slab, seq, LANES), blk),
        pl.BlockSpec((1, n_slab, seq, LANES), blk),
        _const_spec(xrm.shape),
    ] + [_const_spec(w.shape) for w in (cw, cb, wg, lam, g)]
    scratch = [pltpu.VMEM((n_slab, rows_time, LANES), F32)]
    scratch += [pltpu.VMEM((n_slab, rows_perm, LANES), F32) for _ in range(4)]
    return pl.pallas_call(
        functools.partial(_rnn_kernel, seq=seq, stride=stride, rc=rc, tm_out=tm_out),
        grid=(b,),
        in_specs=in_specs,
        out_specs=pl.BlockSpec((1, seq, n_slab * LANES), lambda i: (i, 0, 0)),
        out_shape=jax.ShapeDtypeStruct((b, seq, n_slab * LANES), BF16),
        scratch_shapes=scratch,
        compiler_params=pltpu.CompilerParams(
            dimension_semantics=("arbitrary",), vmem_limit_bytes=VMEM_LIMIT),
        name="rnn",
    )(xr, xg, xrm, cw, cb, wg, lam, g)


def _attn_kernel(qt_ref, k_ref, vt_ref, km_ref, vtm_ref, o_ref, *, tq):
    n_key = k_ref.shape[1]

    n_stream = HEADS_PER_STEP * (n_key // tq)
    n_tile = n_key // KEY_TILE
    tiles = [(t * KEY_TILE, (t + 1) * KEY_TILE) for t in range(n_tile)]
    tiles[-1] = (tiles[-1][0], n_key + N_META_TOK)
    steps = [(i, t) for i in range(n_stream) for t in range(n_tile)]

    def score_tile(i, t):
        j, hh = divmod(i, HEADS_PER_STEP)
        k0, k1 = tiles[t]
        ls = slice(hh * LANES, (hh + 1) * LANES)
        keys = k_ref[0, k0:min(k1, n_key), ls]
        if k1 > n_key:
            keys = jnp.concatenate([keys, km_ref[0, :, ls]], axis=0)
        return jnp.dot(keys, qt_ref[0, ls, j * tq:(j + 1) * tq],
                       preferred_element_type=F32)

    outs = []
    pending = [score_tile(*steps[n]) for n in range(SCORE_LOOKAHEAD)]
    m, acc = None, None
    for n, (i, t) in enumerate(steps):
        if n + SCORE_LOOKAHEAD < len(steps):
            pending.append(score_tile(*steps[n + SCORE_LOOKAHEAD]))
        st = pending.pop(0)
        k0, k1 = tiles[t]
        hh = i % HEADS_PER_STEP
        vs = slice(hh * VT_ROWS, (hh + 1) * VT_ROWS)
        st3 = st.reshape((k1 - k0) // SUBLANES, SUBLANES, tq)
        tmax = jnp.max(jnp.max(st3, axis=0), axis=0, keepdims=True)
        m_new = tmax if m is None else jnp.maximum(m, tmax)
        mb = jnp.broadcast_to(m_new, (SUBLANES, tq))
        pb = jnp.exp2(st3 - mb[None]).reshape(k1 - k0, tq).astype(BF16)
        part = jnp.dot(vt_ref[0, vs, k0:k0 + KEY_TILE], pb[:KEY_TILE], preferred_element_type=F32)
        if k1 - k0 > KEY_TILE:
            part = part + jnp.dot(vtm_ref[0, vs, :], pb[KEY_TILE:], preferred_element_type=F32)
        acc = part if m is None else acc * jnp.exp2(m - m_new) + part
        m = m_new
        if t == n_tile - 1:
            outs.append(acc[:HEAD_V] * (1.0 / acc[HEAD_V:HEAD_V + 1]))
            m, acc = None, None
            if hh == HEADS_PER_STEP - 1:
                j = i // HEADS_PER_STEP
                o_ref[0, j * tq:(j + 1) * tq, :] = jnp.concatenate(outs, axis=0).T
                outs = []


def _attention(qt, k, vt, km, vtm, tq):
    b, s, _ = k.shape
    hps = HEADS_PER_STEP
    return pl.pallas_call(
        functools.partial(_attn_kernel, tq=tq),
        grid=(b, N_HEAD // hps),
        in_specs=[
            pl.BlockSpec((1, hps * LANES, s), lambda i, p: (i, p, 0)),
            pl.BlockSpec((1, s, hps * LANES), lambda i, p: (i, 0, p)),
            pl.BlockSpec((1, hps * VT_ROWS, s), lambda i, p: (i, p, 0)),
            pl.BlockSpec((1, N_META_TOK, hps * LANES), lambda i, p: (0, 0, p)),
            pl.BlockSpec((1, hps * VT_ROWS, N_META_TOK), lambda i, p: (0, p, 0)),
        ],
        out_specs=pl.BlockSpec((1, s, hps * HEAD_V), lambda i, p: (i, 0, p)),
        out_shape=jax.ShapeDtypeStruct((b, s, N_HEAD * HEAD_V), F32),
        compiler_params=pltpu.CompilerParams(
            dimension_semantics=("arbitrary", "arbitrary"), vmem_limit_bytes=VMEM_LIMIT),
        name="attn",
    )(qt, k, vt, km, vtm)


def _out_kernel(x_ref, oa_ref, on_ref, ag_ref, woa_ref, wor_ref, ln2_ref, wg_ref, wu_ref, wd_ref,
                o_ref, *, ff_chunk):
    oa = oa_ref[...]
    oan = (oa * _rsqrt_mean(oa, oa.shape[-1]) * ag_ref[...]).astype(BF16)
    h = x_ref[...] + jnp.dot(oan, woa_ref[...], preferred_element_type=F32)
    h = h + jnp.dot(on_ref[...], wor_ref[...], preferred_element_type=F32)
    hn = (h * _rsqrt_mean(h, h.shape[-1]) * ln2_ref[...]).astype(BF16)
    d_ff = wg_ref.shape[1]
    acc = h
    for c in range(d_ff // ff_chunk):
        cs = slice(c * ff_chunk, (c + 1) * ff_chunk)
        gt = jnp.dot(hn, wg_ref[:, cs], preferred_element_type=F32)
        up = jnp.dot(hn, wu_ref[:, cs], preferred_element_type=F32)
        act = (gt * jax.nn.sigmoid(gt) * up).astype(BF16)
        acc = acc + jnp.dot(act, wd_ref[cs, :], preferred_element_type=F32)
    o_ref[...] = acc


def _out_ffn(x2, oa2, on2, ag, woa, wor, ln2, wg, wu, wd, tm):
    n, d = x2.shape
    d_ff = wg.shape[1]
    ff_chunk = d_ff
    row = lambda i: (i, 0)
    single = pl.Buffered(1)
    wspec = lambda w: pl.BlockSpec(w.shape, lambda i: (0, 0), pipeline_mode=single)
    return pl.pallas_call(
        functools.partial(_out_kernel, ff_chunk=ff_chunk),
        grid=(n // tm,),
        in_specs=[
            pl.BlockSpec((tm, d), row),
            pl.BlockSpec((tm, oa2.shape[1]), row),
            pl.BlockSpec((tm, on2.shape[1]), row),
            wspec(ag), wspec(woa), wspec(wor), wspec(ln2), wspec(wg), wspec(wu), wspec(wd),
        ],
        out_specs=pl.BlockSpec((tm, d), row),
        out_shape=jax.ShapeDtypeStruct((n, d), F32),
        compiler_params=pltpu.CompilerParams(
            dimension_semantics=("arbitrary",), vmem_limit_bytes=VMEM_LIMIT),
        name="out_ffn",
    )(x2, oa2, on2, ag, woa, wor, ln2, wg, wu, wd)


def _rope_tables(t_len):
    half = ROPE // 2
    freqs = 1.0 / (THETA ** (np.arange(half, dtype=np.float64) / half))
    ang = np.arange(t_len, dtype=np.float64)[:, None] * freqs[None, :]
    pad = LANES - HEAD_QK
    cos_t = np.concatenate([np.ones((t_len, NOPE)), np.cos(ang), np.cos(ang), np.zeros((t_len, pad))], -1)
    sin_t = np.concatenate([np.zeros((t_len, NOPE)), np.sin(ang), np.sin(ang), np.zeros((t_len, pad))], -1)
    return jnp.asarray(cos_t, F32), jnp.asarray(sin_t, F32)


def _rot_half_cols(w):
    half = ROPE // 2
    return jnp.concatenate([-w[..., half:], w[..., :half]], axis=-1)


def _gain_rows(g):
    half = ROPE // 2
    pad = jnp.zeros((LANES - HEAD_QK,), F32)
    r0 = jnp.concatenate([g, pad])
    r1 = jnp.concatenate([jnp.zeros((NOPE,), F32), g[NOPE + half:], g[NOPE:NOPE + half], pad])
    return jnp.stack([r0, r1])


def _block_diag_pairs(w):
    g, n, _ = w.shape
    z = jnp.zeros((g // 2, n, n), w.dtype)
    top = jnp.concatenate([w[0::2], z], axis=2)
    bot = jnp.concatenate([z, w[1::2]], axis=2)
    return jnp.concatenate([top, bot], axis=1)


def kernel(x, meta_tokens, ln1_g, w_in, q_a_norm_g, w_uq, kv_a_norm_g, w_ukv, q_norm_g, k_norm_g,
           conv_w, conv_b, lru_wa, lru_ba, lru_wi, lru_bi, lru_lambda, attn_out_g, rnn_out_g,
           w_out, ln2_g, w_gate, w_up, w_down):
    bsz, seq, d_model = x.shape
    q_lora = q_a_norm_g.shape[-1]
    kv_lora = kv_a_norm_g.shape[-1]
    d_rnn = conv_w.shape[-1]
    d_attn = N_HEAD * HEAD_V
    t_len = N_META_TOK + seq
    l = 0

    wi = w_in[l]
    o_kr = q_lora + kv_lora
    w_kr = wi[:, o_kr:o_kr + ROPE]
    win = jnp.concatenate(
        [wi[:, :o_kr], jnp.zeros((d_model, NOPE), F32), w_kr, _rot_half_cols(w_kr),
         wi[:, o_kr + ROPE:]], axis=1).astype(BF16)
    wq = w_uq[l].reshape(q_lora, N_HEAD, HEAD_QK)
    wuq = jnp.concatenate([wq, _rot_half_cols(wq[..., NOPE:])], axis=-1)
    wuq = wuq.reshape(q_lora, N_HEAD * LANES).astype(BF16)
    wkv = w_ukv[l].reshape(kv_lora, N_HEAD, NOPE + HEAD_V)
    wuk = jnp.concatenate([wkv[..., :NOPE], jnp.zeros((kv_lora, N_HEAD, LANES - NOPE), F32)], -1)
    wuk = wuk.reshape(kv_lora, N_HEAD * LANES).astype(BF16)
    wuv = wkv[..., NOPE:].reshape(kv_lora, d_attn).astype(BF16)
    gq = _gain_rows(q_norm_g[l])
    gk = _gain_rows(k_norm_g[l])
    cos_t, sin_t = _rope_tables(t_len)
    lane2 = np.arange(2 * LANES)
    hsum = jnp.asarray((lane2[:, None] // LANES == lane2[None, :] // LANES)
                       & (lane2[:, None] % LANES < HEAD_QK), BF16)
    proj_w = (ln1_g[l][None], win, q_a_norm_g[l][None], wuq, kv_a_norm_g[l][None], wuk, wuv, gq, gk, hsum)
    dims = (q_lora, kv_lora, d_rnn)

    wgate = (0.5 * jnp.concatenate(
        [_block_diag_pairs(w) for w in (lru_wa[l, 0], lru_wi[l, 0], lru_wa[l, 1], lru_wi[l, 1])],
        axis=2)).astype(BF16)
    n_slab = d_rnn // LANES
    gate_b = 0.5 * jnp.stack([lru_ba[l, 0], lru_bi[l, 0], lru_ba[l, 1], lru_bi[l, 1]])
    gate_b = gate_b.reshape(4, n_slab, LANES).transpose(1, 0, 2).reshape(n_slab, 1, 4 * LANES)
    b_hi = gate_b.astype(BF16)
    b_lo = (gate_b - b_hi.astype(F32)).astype(BF16)
    wgate = jnp.concatenate(
        [wgate, b_hi, b_lo, jnp.zeros((n_slab, LANES - 2, 4 * LANES), BF16)], axis=1)

    tm = 512 if seq % 512 == 0 else seq
    q, k, v, xr, xg = _project(x, tm, cos_t[N_META_TOK:], sin_t[N_META_TOK:], proj_w, dims)
    _, km, vm, xrm, _ = _project(meta_tokens[None].astype(x.dtype), N_META_TOK,
                                 cos_t[:N_META_TOK], sin_t[:N_META_TOK], proj_w, dims)

    o_rnn = _rnn(xr, xg, xrm, conv_w[l], conv_b[l][None], wgate, lru_lambda[l], rnn_out_g[l][None])

    tq = 2048 if seq % 2048 == 0 else seq
    o_attn = _attention(q, k, v, km, vm, tq)

    wo = w_out[l].astype(BF16)
    n_rows = bsz * seq
    tmo = 512 if n_rows % 512 == 0 else n_rows
    out = _out_ffn(x.reshape(n_rows, d_model), o_attn.reshape(n_rows, d_attn),
                   o_rnn.reshape(n_rows, d_rnn), attn_out_g[l][None], wo[:d_attn], wo[d_attn:],
                   ln2_g[l][None], w_gate[l].astype(BF16), w_up[l].astype(BF16),
                   w_down[l].astype(BF16), tmo)
    return out.reshape(bsz, seq, d_model)
```

```python
import functools
import math

import numpy as np

import jax
import jax.numpy as jnp
from jax import lax
from jax.experimental import pallas as pl
from jax.experimental.pallas import tpu as pltpu

F32 = jnp.float32
BF16 = jnp.bfloat16

N_META_TOK = 16
N_HEAD = 8
NOPE = 64
ROPE = 32
HEAD_QK = NOPE + ROPE
HEAD_V = 64
VT_ROWS = HEAD_V + 16
LRU_SCALE = 8.0
THETA = 10000.0
NORM_EPS = 1e-6
CONV_TAPS = 4
CONV_LEFT = 2

LANES = 128
SUBLANES = 8
VMEM_LIMIT = 56 * 1024 * 1024
KEY_TILE = 256
SCAN_UNROLL = 6
COMBINE_UNROLL = 43
HEADS_PER_STEP = 4
SCORE_LOOKAHEAD = 1

LOG2E = 1.4426950408889634


def _rsqrt_mean(x, n):
    return lax.rsqrt(jnp.sum(x * x, axis=-1, keepdims=True) * (1.0 / n) + NORM_EPS)


def _proj_kernel(x_ref, ln1_ref, win_ref, qag_ref, wuq_ref, kvag_ref, wuk_ref, wuv_ref,
                 cos_ref, sin_ref, gq_ref, gk_ref, hsum_ref,
                 q_out, k_out, v_out, xr_out, xg_out, *, q_lora, kv_lora, d_rnn):
    x = x_ref[0]
    d_model = x.shape[-1]
    hn = (x * _rsqrt_mean(x, d_model) * ln1_ref[...]).astype(BF16)
    o_kv = q_lora
    o_kr = o_kv + kv_lora
    o_xr = o_kr + LANES
    p = jnp.dot(hn, win_ref[:, :o_xr], preferred_element_type=F32)
    c_q = p[:, :o_kv]
    c_kv = p[:, o_kv:o_kr]
    kr = p[:, o_kr:o_xr]
    n_slab = d_rnn // LANES

    def recurrent_inputs(c):
        px = jnp.dot(hn, win_ref[:, o_xr + 2 * c * LANES:o_xr + 2 * (c + 1) * LANES],
                     preferred_element_type=F32)
        for u in range(2):
            slab = 2 * c + u
            dst = xr_out if slab < n_slab else xg_out
            dst[0, slab % n_slab] = px[:, u * LANES:(u + 1) * LANES]

    cos_t = cos_ref[...]
    sin_t = sin_ref[...]
    eps_hd = HEAD_QK * NORM_EPS
    root_hd = math.sqrt(HEAD_QK)
    hsum = hsum_ref[...]

    def head_sums(sq_pair):
        return jnp.dot(sq_pair.astype(BF16), hsum, preferred_element_type=F32)

    cqn = (c_q * _rsqrt_mean(c_q, q_lora) * qag_ref[...]).astype(BF16)
    q = jnp.dot(cqn, wuq_ref[...], preferred_element_type=F32)
    q_gain = root_hd * (HEAD_QK ** -0.5) * LOG2E
    cq = cos_t * (gq_ref[0:1, :] * q_gain)
    sq = sin_t * (gq_ref[1:2, :] * q_gain)

    ckn = (c_kv * _rsqrt_mean(c_kv, kv_lora) * kvag_ref[...]).astype(BF16)
    kn = jnp.dot(ckn, wuk_ref[...], preferred_element_type=F32)
    vt = jnp.dot(ckn, wuv_ref[...], preferred_element_type=F32).astype(BF16).T
    tm = vt.shape[1]
    ones_rows = (lax.broadcasted_iota(jnp.int32, (VT_ROWS - HEAD_V, tm), 0) == 0).astype(BF16)
    for h in range(N_HEAD):
        v_out[0, h * VT_ROWS:h * VT_ROWS + HEAD_V, :] = vt[h * HEAD_V:(h + 1) * HEAD_V]
        v_out[0, h * VT_ROWS + HEAD_V:(h + 1) * VT_ROWS, :] = ones_rows
    ck = cos_t * (gk_ref[0:1, :] * root_hd)
    sk = sin_t * (gk_ref[1:2, :] * root_hd)
    kr_sq = kr * kr
    kr_sq2 = jnp.concatenate([kr_sq, kr_sq], axis=1)
    kr_rot = kr * ck + pltpu.roll(kr, LANES - ROPE, axis=1) * sk
    n_pair = N_HEAD // 2
    n_rec = n_slab
    for pr in range(n_pair):
        for c in range(pr * n_rec // n_pair, (pr + 1) * n_rec // n_pair):
            recurrent_inputs(c)
        qp = q[:, 2 * pr * LANES:2 * (pr + 1) * LANES]
        sc = lax.rsqrt(head_sums(qp * qp) + eps_hd)
        for hh in range(2):
            h = 2 * pr + hh
            qh = qp[:, hh * LANES:(hh + 1) * LANES]
            qr = pltpu.roll(qh, LANES - ROPE, axis=1)
            qf = (qh * cq + qr * sq) * sc[:, hh * LANES:(hh + 1) * LANES]
            q_out[0, h * LANES:(h + 1) * LANES, :] = qf.astype(BF16).T
        kp = kn[:, 2 * pr * LANES:2 * (pr + 1) * LANES]
        sc = lax.rsqrt(head_sums(kp * kp + kr_sq2) + eps_hd)
        for hh in range(2):
            h = 2 * pr + hh
            kh = kp[:, hh * LANES:(hh + 1) * LANES]
            k_out[0, :, h * LANES:(h + 1) * LANES] = (
                (kh * ck + kr_rot) * sc[:, hh * LANES:(hh + 1) * LANES]).astype(BF16)


def _const_spec(shape):
    nd = len(shape)
    return pl.BlockSpec(shape, lambda *_: (0,) * nd)


def _project(x3, tm, cos_t, sin_t, wts, dims):
    q_lora, kv_lora, d_rnn = dims
    b, s, d = x3.shape
    n_slab = d_rnn // LANES
    nt = s // tm
    ln1, win, qag, wuq, kvag, wuk, wuv, gq, gk, hsum = wts
    row3 = lambda i, j: (i, j, 0)
    row4 = lambda i, j: (i, 0, j, 0)
    tab = lambda i, j: (j, 0)
    in_specs = [pl.BlockSpec((1, tm, d), row3)]
    in_specs += [_const_spec(w.shape) for w in (ln1, win, qag, wuq, kvag, wuk, wuv)]
    in_specs += [pl.BlockSpec((tm, LANES), tab), pl.BlockSpec((tm, LANES), tab)]
    in_specs += [_const_spec(gq.shape), _const_spec(gk.shape), _const_spec(hsum.shape)]
    col3 = lambda i, j: (i, 0, j)
    out_shape = (
        jax.ShapeDtypeStruct((b, N_HEAD * LANES, s), BF16),
        jax.ShapeDtypeStruct((b, s, N_HEAD * LANES), BF16),
        jax.ShapeDtypeStruct((b, N_HEAD * VT_ROWS, s), BF16),
        jax.ShapeDtypeStruct((b, n_slab, s, LANES), F32),
        jax.ShapeDtypeStruct((b, n_slab, s, LANES), F32),
    )
    out_specs = (
        pl.BlockSpec((1, N_HEAD * LANES, tm), col3),
        pl.BlockSpec((1, tm, N_HEAD * LANES), row3),
        pl.BlockSpec((1, N_HEAD * VT_ROWS, tm), col3),
        pl.BlockSpec((1, n_slab, tm, LANES), row4),
        pl.BlockSpec((1, n_slab, tm, LANES), row4),
    )
    return pl.pallas_call(
        functools.partial(_proj_kernel, q_lora=q_lora, kv_lora=kv_lora, d_rnn=d_rnn),
        grid=(b, nt),
        in_specs=in_specs,
        out_specs=out_specs,
        out_shape=out_shape,
        compiler_params=pltpu.CompilerParams(
            dimension_semantics=("arbitrary", "arbitrary"), vmem_limit_bytes=VMEM_LIMIT),
        name="proj",
    )(x3, ln1, win, qag, wuq, kvag, wuk, wuv, cos_t, sin_t, gq, gk, hsum)


def _scan_stride(t_len):
    s = -(-t_len // SUBLANES)
    while s % SUBLANES == 0:
        s += 1
    if s % 2:
        s += 1
        if s % SUBLANES == 0:
            s += 2
    return s


def _chunk_len(stride, cap=258):
    for c in range(min(cap, stride), 0, -1):
        if stride % c == 0:
            return c
    return 1


def _rnn_kernel(xr_ref, xg_ref, xrm_ref, cw_ref, cb_ref, wg_ref, lam_ref, g_ref,
                o_ref, t1, af, bf, ab, bb, *, seq, stride, rc, tm_out):
    n_slab = xr_ref.shape[1]
    t_len = N_META_TOK + seq
    pre = SUBLANES
    rows = t1.shape[1]
    n_chunk = stride // rc
    crow = rc * SUBLANES

    for s in range(n_slab):
        t1[s, 0:pre, :] = jnp.zeros((pre, LANES), F32)
        t1[s, pre:pre + N_META_TOK, :] = xrm_ref[0, s]
        t1[s, pre + N_META_TOK:pre + t_len, :] = xr_ref[0, s]
        t1[s, pre + t_len:rows, :] = jnp.zeros((rows - pre - t_len, LANES), F32)

    lam = lam_ref[...]
    nl = -lam
    softplus = jnp.maximum(nl, 0.0) + jnp.log(1.0 + jnp.exp(-jnp.abs(nl)))
    hcn = (0.5 * LRU_SCALE) * softplus
    cw = cw_ref[...]
    cb = cb_ref[...]
    ones_k = jnp.ones((crow, LANES), BF16)

    padded = stride * SUBLANES > t_len
    if padded:
        ridx = lax.broadcasted_iota(jnp.int32, (crow, LANES), 0)
        t_local = (ridx & (SUBLANES - 1)) * stride + (ridx >> 3)

    def gate_chunk(c, carry):
        r0 = c * rc
        row0 = pl.multiple_of(c * crow, SUBLANES)
        for s in range(n_slab):
            ls = slice(s * LANES, (s + 1) * LANES)
            taps = [t1[s, pl.ds(r0 + pre - CONV_LEFT + j, SUBLANES, stride=stride), :]
                    for j in range(rc + CONV_TAPS - 1)]
            xc_rows = []
            for rr in range(rc):
                acc = taps[rr] * cw[0:1, ls]
                for j in range(1, CONV_TAPS):
                    acc = acc + taps[rr + j] * cw[j:j + 1, ls]
                xc_rows.append(acc + cb[:, ls])
            xc = jnp.concatenate(xc_rows, axis=0)
            hx = 0.5 * xc
            z = jnp.dot(jnp.concatenate([xc.astype(BF16), ones_k], axis=1), wg_ref[s],
                        preferred_element_type=F32)
            for d, (a_ref, b_ref) in enumerate(((af, bf), (ab, bb))):
                ta = jnp.tanh(z[:, (2 * d) * LANES:(2 * d + 1) * LANES])
                ti = jnp.tanh(z[:, (2 * d + 1) * LANES:(2 * d + 2) * LANES])
                hc = hcn[d:d + 1, ls]
                nla = ta * hc + hc
                a = jnp.exp2(nla * (-LOG2E))
                one_m_a2 = jnp.tanh(nla) * (1.0 + a * a)
                root = jnp.where(one_m_a2 > 0.0, one_m_a2 * lax.rsqrt(one_m_a2), 0.0)
                bv = root * (ti * hx + hx)
                if d == 1 and padded:
                    bv = jnp.where((t_local + r0) < t_len, bv, 0.0)
                a_ref[s, pl.ds(row0, crow), :] = a
                b_ref[s, pl.ds(row0, crow), :] = bv
        return carry

    lax.fori_loop(0, n_chunk, gate_chunk, 0)

    zero = jnp.zeros((SUBLANES, LANES), F32)
    one = jnp.ones((SUBLANES, LANES), F32)

    def scan_step(r, carry):
        hf, pf, hb, pb = carry
        rf = pl.multiple_of(r * SUBLANES, SUBLANES)
        rb = pl.multiple_of((stride - 1 - r) * SUBLANES, SUBLANES)
        nhf, npf, nhb, npb = [], [], [], []
        for s in range(n_slab):
            a = af[s, pl.ds(rf, SUBLANES), :]
            h = a * hf[s] + bf[s, pl.ds(rf, SUBLANES), :]
            pr = a * pf[s]
            bf[s, pl.ds(rf, SUBLANES), :] = h
            af[s, pl.ds(rf, SUBLANES), :] = pr
            nhf.append(h)
            npf.append(pr)
            a = ab[s, pl.ds(rb, SUBLANES), :]
            h = a * hb[s] + bb[s, pl.ds(rb, SUBLANES), :]
            pr = a * pb[s]
            bb[s, pl.ds(rb, SUBLANES), :] = h
            ab[s, pl.ds(rb, SUBLANES), :] = pr
            nhb.append(h)
            npb.append(pr)
        return tuple(nhf), tuple(npf), tuple(nhb), tuple(npb)

    init = ((zero,) * n_slab, (one,) * n_slab, (zero,) * n_slab, (one,) * n_slab)
    hf, pf, hb, pb = lax.fori_loop(0, stride, scan_step, init, unroll=SCAN_UNROLL)

    sub = lax.broadcasted_iota(jnp.int32, (SUBLANES, LANES), 0)
    cf, cbk = [], []
    for s in range(n_slab):
        c = zero
        for _ in range(SUBLANES - 1):
            c = jnp.where(sub == 0, 0.0, pltpu.roll(pf[s] * c + hf[s], 1, axis=0))
        cf.append(c)
        c = zero
        for _ in range(SUBLANES - 1):
            c = jnp.where(sub == SUBLANES - 1, 0.0, pltpu.roll(pb[s] * c + hb[s], SUBLANES - 1, axis=0))
        cbk.append(c)

    def combine(r, carry):
        rp = pl.multiple_of(r * SUBLANES, SUBLANES)
        for s in range(n_slab):
            y = (bf[s, pl.ds(rp, SUBLANES), :] + af[s, pl.ds(rp, SUBLANES), :] * cf[s]
                 + bb[s, pl.ds(rp, SUBLANES), :] + ab[s, pl.ds(rp, SUBLANES), :] * cbk[s])
            t1[s, pl.ds(r + pre, SUBLANES, stride=stride), :] = y
        return carry

    lax.fori_loop(0, stride, combine, 0, unroll=COMBINE_UNROLL)

    d_rnn = n_slab * LANES
    g = g_ref[...]
    k0 = math.sqrt(2.0 / math.pi)

    def finish(j, carry):
        ro = pl.multiple_of(j * tm_out, 2 * SUBLANES)
        rt = pl.multiple_of(j * tm_out + pre + N_META_TOK, SUBLANES)
        ys = []
        ssq = jnp.zeros((tm_out, 1), F32)
        for s in range(n_slab):
            y = t1[s, pl.ds(rt, tm_out), :]
            xg = xg_ref[0, s, pl.ds(ro, tm_out), :]
            yh = y * (0.5 * xg)
            y = yh + yh * jnp.tanh(xg * (k0 + (k0 * 0.044715) * (xg * xg)))
            ssq = ssq + jnp.sum(y * y, axis=-1, keepdims=True)
            ys.append(y)
        sc = lax.rsqrt(ssq * (1.0 / d_rnn) + NORM_EPS)
        for s in range(n_slab):
            ls = slice(s * LANES, (s + 1) * LANES)
            o_ref[0, pl.ds(ro, tm_out), ls] = (ys[s] * sc * g[:, ls]).astype(BF16)
        return carry

    lax.fori_loop(0, seq // tm_out, finish, 0)


def _rnn(xr, xg, xrm, cw, cb, wg, lam, g):
    b, n_slab, seq, _ = xr.shape
    t_len = N_META_TOK + seq
    stride = _scan_stride(t_len)
    rc = _chunk_len(stride)
    rows_perm = stride * SUBLANES
    rows_time = SUBLANES + rows_perm + SUBLANES
    tm_out = 512 if seq % 512 == 0 else seq
    blk = lambda i: (i, 0, 0, 0)
    in_specs = [
        pl.BlockSpec((1, n_slab, seq, LANES), blk),
        pl.BlockSpec((1, n_slab, seq, LANES), blk),
        _const_spec(xrm.shape),
    ] + [_const_spec(w.shape) for w in (cw, cb, wg, lam, g)]
    scratch = [pltpu.VMEM((n_slab, rows_time, LANES), F32)]
    scratch += [pltpu.VMEM((n_slab, rows_perm, LANES), F32) for _ in range(4)]
    return pl.pallas_call(
        functools.partial(_rnn_kernel, seq=seq, stride=stride, rc=rc, tm_out=tm_out),
        grid=(b,),
        in_specs=in_specs,
        out_specs=pl.BlockSpec((1, seq, n_slab * LANES), lambda i: (i, 0, 0)),
        out_shape=jax.ShapeDtypeStruct((b, seq, n_slab * LANES), BF16),
        scratch_shapes=scratch,
        compiler_params=pltpu.CompilerParams(
            dimension_semantics=("arbitrary",), vmem_limit_bytes=VMEM_LIMIT),
        name="rnn",
    )(xr, xg, xrm, cw, cb, wg, lam, g)


def _attn_kernel(qt_ref, k_ref, vt_ref, km_ref, vtm_ref, o_ref, *, tq):
    n_key = k_ref.shape[1]

    n_stream = HEADS_PER_STEP * (n_key // tq)
    n_tile = n_key // KEY_TILE
    tiles = [(t * KEY_TILE, (t + 1) * KEY_TILE) for t in range(n_tile)]
    tiles[-1] = (tiles[-1][0], n_key + N_META_TOK)
    steps = [(i, t) for i in range(n_stream) for t in range(n_tile)]

    def score_tile(i, t):
        j, hh = divmod(i, HEADS_PER_STEP)
        k0, k1 = tiles[t]
        ls = slice(hh * LANES, (hh + 1) * LANES)
        keys = k_ref[0, k0:min(k1, n_key), ls]
        if k1 > n_key:
            keys = jnp.concatenate([keys, km_ref[0, :, ls]], axis=0)
        return jnp.dot(keys, qt_ref[0, ls, j * tq:(j + 1) * tq],
                       preferred_element_type=F32)

    outs = []
    pending = [score_tile(*steps[n]) for n in range(SCORE_LOOKAHEAD)]
    m, acc = None, None
    for n, (i, t) in enumerate(steps):
        st = pending.pop(0)
        k0, k1 = tiles[t]
        hh = i % HEADS_PER_STEP
        vs = slice(hh * VT_ROWS, (hh + 1) * VT_ROWS)
        st3 = st.reshape((k1 - k0) // SUBLANES, SUBLANES, tq)
        tmax = jnp.max(jnp.max(st3, axis=0), axis=0, keepdims=True)
        m_new = tmax if m is None else jnp.maximum(m, tmax)
        mb = jnp.broadcast_to(m_new, (SUBLANES, tq))
        pb = jnp.exp2(st3 - mb[None]).reshape(k1 - k0, tq).astype(BF16)
        if n + SCORE_LOOKAHEAD < len(steps):
            pending.append(score_tile(*steps[n + SCORE_LOOKAHEAD]))
        part = jnp.dot(vt_ref[0, vs, k0:k0 + KEY_TILE], pb[:KEY_TILE], preferred_element_type=F32)
        if k1 - k0 > KEY_TILE:
            part = part + jnp.dot(vtm_ref[0, vs, :], pb[KEY_TILE:], preferred_element_type=F32)
        acc = part if m is None else acc * jnp.exp2(m - m_new) + part
        m = m_new
        if t == n_tile - 1:
            outs.append(acc[:HEAD_V] * (1.0 / acc[HEAD_V:HEAD_V + 1]))
            m, acc = None, None
            if hh == HEADS_PER_STEP - 1:
                j = i // HEADS_PER_STEP
                o_ref[0, j * tq:(j + 1) * tq, :] = jnp.concatenate(outs, axis=0).T
                outs = []


def _attention(qt, k, vt, km, vtm, tq):
    b, s, _ = k.shape
    hps = HEADS_PER_STEP
    return pl.pallas_call(
        functools.partial(_attn_kernel, tq=tq),
        grid=(b, N_HEAD // hps),
        in_specs=[
            pl.BlockSpec((1, hps * LANES, s), lambda i, p: (i, p, 0)),
            pl.BlockSpec((1, s, hps * LANES), lambda i, p: (i, 0, p)),
            pl.BlockSpec((1, hps * VT_ROWS, s), lambda i, p: (i, p, 0)),
            pl.BlockSpec((1, N_META_TOK, hps * LANES), lambda i, p: (0, 0, p)),
            pl.BlockSpec((1, hps * VT_ROWS, N_META_TOK), lambda i, p: (0, p, 0)),
        ],
        out_specs=pl.BlockSpec((1, s, hps * HEAD_V), lambda i, p: (i, 0, p)),
        out_shape=jax.ShapeDtypeStruct((b, s, N_HEAD * HEAD_V), F32),
        compiler_params=pltpu.CompilerParams(
            dimension_semantics=("arbitrary", "arbitrary"), vmem_limit_bytes=VMEM_LIMIT),
        name="attn",
    )(qt, k, vt, km, vtm)


def _out_kernel(x_ref, oa_ref, on_ref, ag_ref, woa_ref, wor_ref, ln2_ref, wg_ref, wu_ref, wd_ref,
                o_ref, *, ff_chunk):
    oa = oa_ref[...]
    oan = (oa * _rsqrt_mean(oa, oa.shape[-1]) * ag_ref[...]).astype(BF16)
    h = x_ref[...] + jnp.dot(oan, woa_ref[...], preferred_element_type=F32)
    h = h + jnp.dot(on_ref[...], wor_ref[...], preferred_element_type=F32)
    hn = (h * _rsqrt_mean(h, h.shape[-1]) * ln2_ref[...]).astype(BF16)
    d_ff = wg_ref.shape[1]
    acc = h
    for c in range(d_ff // ff_chunk):
        cs = slice(c * ff_chunk, (c + 1) * ff_chunk)
        gt = jnp.dot(hn, wg_ref[:, cs], preferred_element_type=F32)
        up = jnp.dot(hn, wu_ref[:, cs], preferred_element_type=F32)
        act = (gt * jax.nn.sigmoid(gt) * up).astype(BF16)
        acc = acc + jnp.dot(act, wd_ref[cs, :], preferred_element_type=F32)
    o_ref[...] = acc


def _out_ffn(x2, oa2, on2, ag, woa, wor, ln2, wg, wu, wd, tm):
    n, d = x2.shape
    d_ff = wg.shape[1]
    ff_chunk = d_ff
    row = lambda i: (i, 0)
    single = pl.Buffered(1)
    wspec = lambda w: pl.BlockSpec(w.shape, lambda i: (0, 0), pipeline_mode=single)
    return pl.pallas_call(
        functools.partial(_out_kernel, ff_chunk=ff_chunk),
        grid=(n // tm,),
        in_specs=[
            pl.BlockSpec((tm, d), row),
            pl.BlockSpec((tm, oa2.shape[1]), row),
            pl.BlockSpec((tm, on2.shape[1]), row),
            wspec(ag), wspec(woa), wspec(wor), wspec(ln2), wspec(wg), wspec(wu), wspec(wd),
        ],
        out_specs=pl.BlockSpec((tm, d), row),
        out_shape=jax.ShapeDtypeStruct((n, d), F32),
        compiler_params=pltpu.CompilerParams(
            dimension_semantics=("arbitrary",), vmem_limit_bytes=VMEM_LIMIT),
        name="out_ffn",
    )(x2, oa2, on2, ag, woa, wor, ln2, wg, wu, wd)


def _rope_tables(t_len):
    half = ROPE // 2
    freqs = 1.0 / (THETA ** (np.arange(half, dtype=np.float64) / half))
    ang = np.arange(t_len, dtype=np.float64)[:, None] * freqs[None, :]
    pad = LANES - HEAD_QK
    cos_t = np.concatenate([np.ones((t_len, NOPE)), np.cos(ang), np.cos(ang), np.zeros((t_len, pad))], -1)
    sin_t = np.concatenate([np.zeros((t_len, NOPE)), np.sin(ang), np.sin(ang), np.zeros((t_len, pad))], -1)
    return jnp.asarray(cos_t, F32), jnp.asarray(sin_t, F32)


def _rot_half_cols(w):
    half = ROPE // 2
    return jnp.concatenate([-w[..., half:], w[..., :half]], axis=-1)


def _gain_rows(g):
    half = ROPE // 2
    pad = jnp.zeros((LANES - HEAD_QK,), F32)
    r0 = jnp.concatenate([g, pad])
    r1 = jnp.concatenate([jnp.zeros((NOPE,), F32), g[NOPE + half:], g[NOPE:NOPE + half], pad])
    return jnp.stack([r0, r1])


def _block_diag_pairs(w):
    g, n, _ = w.shape
    z = jnp.zeros((g // 2, n, n), w.dtype)
    top = jnp.concatenate([w[0::2], z], axis=2)
    bot = jnp.concatenate([z, w[1::2]], axis=2)
    return jnp.concatenate([top, bot], axis=1)


def kernel(x, meta_tokens, ln1_g, w_in, q_a_norm_g, w_uq, kv_a_norm_g, w_ukv, q_norm_g, k_norm_g,
           conv_w, conv_b, lru_wa, lru_ba, lru_wi, lru_bi, lru_lambda, attn_out_g, rnn_out_g,
           w_out, ln2_g, w_gate, w_up, w_down):
    bsz, seq, d_model = x.shape
    q_lora = q_a_norm_g.shape[-1]
    kv_lora = kv_a_norm_g.shape[-1]
    d_rnn = conv_w.shape[-1]
    d_attn = N_HEAD * HEAD_V
    t_len = N_META_TOK + seq
    l = 0

    wi = w_in[l]
    o_kr = q_lora + kv_lora
    w_kr = wi[:, o_kr:o_kr + ROPE]
    win = jnp.concatenate(
        [wi[:, :o_kr], jnp.zeros((d_model, NOPE), F32), w_kr, _rot_half_cols(w_kr),
         wi[:, o_kr + ROPE:]], axis=1).astype(BF16)
    wq = w_uq[l].reshape(q_lora, N_HEAD, HEAD_QK)
    wuq = jnp.concatenate([wq, _rot_half_cols(wq[..., NOPE:])], axis=-1)
    wuq = wuq.reshape(q_lora, N_HEAD * LANES).astype(BF16)
    wkv = w_ukv[l].reshape(kv_lora, N_HEAD, NOPE + HEAD_V)
    wuk = jnp.concatenate([wkv[..., :NOPE], jnp.zeros((kv_lora, N_HEAD, LANES - NOPE), F32)], -1)
    wuk = wuk.reshape(kv_lora, N_HEAD * LANES).astype(BF16)
    wuv = wkv[..., NOPE:].reshape(kv_lora, d_attn).astype(BF16)
    gq = _gain_rows(q_norm_g[l])
    gk = _gain_rows(k_norm_g[l])
    cos_t, sin_t = _rope_tables(t_len)
    lane2 = np.arange(2 * LANES)
    hsum = jnp.asarray((lane2[:, None] // LANES == lane2[None, :] // LANES)
                       & (lane2[:, None] % LANES < HEAD_QK), BF16)
    proj_w = (ln1_g[l][None], win, q_a_norm_g[l][None], wuq, kv_a_norm_g[l][None], wuk, wuv, gq, gk, hsum)
    dims = (q_lora, kv_lora, d_rnn)

    wgate = (0.5 * jnp.concatenate(
        [_block_diag_pairs(w) for w in (lru_wa[l, 0], lru_wi[l, 0], lru_wa[l, 1], lru_wi[l, 1])],
        axis=2)).astype(BF16)
    n_slab = d_rnn // LANES
    gate_b = 0.5 * jnp.stack([lru_ba[l, 0], lru_bi[l, 0], lru_ba[l, 1], lru_bi[l, 1]])
    gate_b = gate_b.reshape(4, n_slab, LANES).transpose(1, 0, 2).reshape(n_slab, 1, 4 * LANES)
    b_hi = gate_b.astype(BF16)
    b_lo = (gate_b - b_hi.astype(F32)).astype(BF16)
    wgate = jnp.concatenate(
        [wgate, b_hi, b_lo, jnp.zeros((n_slab, LANES - 2, 4 * LANES), BF16)], axis=1)

    tm = 512 if seq % 512 == 0 else seq
    q, k, v, xr, xg = _project(x, tm, cos_t[N_META_TOK:], sin_t[N_META_TOK:], proj_w, dims)
    _, km, vm, xrm, _ = _project(meta_tokens[None].astype(x.dtype), N_META_TOK,
                                 cos_t[:N_META_TOK], sin_t[:N_META_TOK], proj_w, dims)

    o_rnn = _rnn(xr, xg, xrm, conv_w[l], conv_b[l][None], wgate, lru_lambda[l], rnn_out_g[l][None])

    tq = 2048 if seq % 2048 == 0 else seq
    o_attn = _attention(q, k, v, km, vm, tq)

    wo = w_out[l].astype(BF16)
    n_rows = bsz * seq
    tmo = 512 if n_rows % 512 == 0 else n_rows
    out = _out_ffn(x.reshape(n_rows, d_model), o_attn.reshape(n_rows, d_attn),
                   o_rnn.reshape(n_rows, d_rnn), attn_out_g[l][None], wo[:d_attn], wo[d_attn:],
                   ln2_g[l][None], w_gate[l].astype(BF16), w_up[l].astype(BF16),
                   w_down[l].astype(BF16), tmo)
    return out.reshape(bsz, seq, d_model)
```

```python
import functools
import math

import numpy as np

import jax
import jax.numpy as jnp
from jax import lax
from jax.experimental import pallas as pl
from jax.experimental.pallas import tpu as pltpu

F32 = jnp.float32
BF16 = jnp.bfloat16

N_META_TOK = 16
N_HEAD = 8
NOPE = 64
ROPE = 32
HEAD_QK = NOPE + ROPE
HEAD_V = 64
VT_ROWS = HEAD_V + 16
LRU_SCALE = 8.0
THETA = 10000.0
NORM_EPS = 1e-6
CONV_TAPS = 4
CONV_LEFT = 2

LANES = 128
SUBLANES = 8
VMEM_LIMIT = 56 * 1024 * 1024
KEY_TILE = 256
SCAN_UNROLL = 6
COMBINE_UNROLL = 43
HEADS_PER_STEP = 4
SCORE_LOOKAHEAD = 1

LOG2E = 1.4426950408889634


def _rsqrt_mean(x, n):
    return lax.rsqrt(jnp.sum(x * x, axis=-1, keepdims=True) * (1.0 / n) + NORM_EPS)


def _proj_kernel(x_ref, ln1_ref, win_ref, qag_ref, wuq_ref, kvag_ref, wuk_ref, wuv_ref,
                 cos_ref, sin_ref, gq_ref, gk_ref, hsum_ref,
                 q_out, k_out, v_out, xr_out, xg_out, *, q_lora, kv_lora, d_rnn):
    x = x_ref[0]
    d_model = x.shape[-1]
    hn = (x * _rsqrt_mean(x, d_model) * ln1_ref[...]).astype(BF16)
    o_kv = q_lora
    o_kr = o_kv + kv_lora
    o_xr = o_kr + LANES
    p = jnp.dot(hn, win_ref[:, :o_xr], preferred_element_type=F32)
    c_q = p[:, :o_kv]
    c_kv = p[:, o_kv:o_kr]
    kr = p[:, o_kr:o_xr]
    n_slab = d_rnn // LANES

    def recurrent_inputs(c):
        px = jnp.dot(hn, win_ref[:, o_xr + 2 * c * LANES:o_xr + 2 * (c + 1) * LANES],
                     preferred_element_type=F32)
        for u in range(2):
            slab = 2 * c + u
            dst = xr_out if slab < n_slab else xg_out
            dst[0, slab % n_slab] = px[:, u * LANES:(u + 1) * LANES]

    cos_t = cos_ref[...]
    sin_t = sin_ref[...]
    eps_hd = HEAD_QK * NORM_EPS
    root_hd = math.sqrt(HEAD_QK)
    hsum = hsum_ref[...]

    def head_sums(sq_pair):
        return jnp.dot(sq_pair.astype(BF16), hsum, preferred_element_type=F32)

    cqn = (c_q * _rsqrt_mean(c_q, q_lora) * qag_ref[...]).astype(BF16)
    q = jnp.dot(cqn, wuq_ref[...], preferred_element_type=F32)
    q_gain = root_hd * (HEAD_QK ** -0.5) * LOG2E
    cq = cos_t * (gq_ref[0:1, :] * q_gain)
    sq = sin_t * (gq_ref[1:2, :] * q_gain)

    ckn = (c_kv * _rsqrt_mean(c_kv, kv_lora) * kvag_ref[...]).astype(BF16)
    kn = jnp.dot(ckn, wuk_ref[...], preferred_element_type=F32)
    vt = jnp.dot(ckn, wuv_ref[...], preferred_element_type=F32).astype(BF16).T
    tm = vt.shape[1]
    ones_rows = (lax.broadcasted_iota(jnp.int32, (VT_ROWS - HEAD_V, tm), 0) == 0).astype(BF16)
    for h in range(N_HEAD):
        v_out[0, h * VT_ROWS:h * VT_ROWS + HEAD_V, :] = vt[h * HEAD_V:(h + 1) * HEAD_V]
        v_out[0, h * VT_ROWS + HEAD_V:(h + 1) * VT_ROWS, :] = ones_rows
    ck = cos_t * (gk_ref[0:1, :] * root_hd)
    sk = sin_t * (gk_ref[1:2, :] * root_hd)
    kr_sq = kr * kr
    kr_sq2 = jnp.concatenate([kr_sq, kr_sq], axis=1)
    kr_rot = kr * ck + pltpu.roll(kr, LANES - ROPE, axis=1) * sk
    n_pair = N_HEAD // 2
    n_rec = n_slab
    for pr in range(n_pair):
        for c in range(pr * n_rec // n_pair, (pr + 1) * n_rec // n_pair):
            recurrent_inputs(c)
        qp = q[:, 2 * pr * LANES:2 * (pr + 1) * LANES]
        sc = lax.rsqrt(head_sums(qp * qp) + eps_hd)
        for hh in range(2):
            h = 2 * pr + hh
            qh = qp[:, hh * LANES:(hh + 1) * LANES]
            qr = pltpu.roll(qh, LANES - ROPE, axis=1)
            qf = (qh * cq + qr * sq) * sc[:, hh * LANES:(hh + 1) * LANES]
            q_out[0, h * LANES:(h + 1) * LANES, :] = qf.astype(BF16).T
        kp = kn[:, 2 * pr * LANES:2 * (pr + 1) * LANES]
        sc = lax.rsqrt(head_sums(kp * kp + kr_sq2) + eps_hd)
        for hh in range(2):
            h = 2 * pr + hh
            kh = kp[:, hh * LANES:(hh + 1) * LANES]
            k_out[0, :, h * LANES:(h + 1) * LANES] = (
                (kh * ck + kr_rot) * sc[:, hh * LANES:(hh + 1) * LANES]).astype(BF16)


def _const_spec(shape):
    nd = len(shape)
    return pl.BlockSpec(shape, lambda *_: (0,) * nd)


def _project(x3, tm, cos_t, sin_t, wts, dims):
    q_lora, kv_lora, d_rnn = dims
    b, s, d = x3.shape
    n_slab = d_rnn // LANES
    nt = s // tm
    ln1, win, qag, wuq, kvag, wuk, wuv, gq, gk, hsum = wts
    row3 = lambda i, j: (i, j, 0)
    row4 = lambda i, j: (i, 0, j, 0)
    tab = lambda i, j: (j, 0)
    in_specs = [pl.BlockSpec((1, tm, d), row3)]
    in_specs += [_const_spec(w.shape) for w in (ln1, win, qag, wuq, kvag, wuk, wuv)]
    in_specs += [pl.BlockSpec((tm, LANES), tab), pl.BlockSpec((tm, LANES), tab)]
    in_specs += [_const_spec(gq.shape), _const_spec(gk.shape), _const_spec(hsum.shape)]
    col3 = lambda i, j: (i, 0, j)
    out_shape = (
        jax.ShapeDtypeStruct((b, N_HEAD * LANES, s), BF16),
        jax.ShapeDtypeStruct((b, s, N_HEAD * LANES), BF16),
        jax.ShapeDtypeStruct((b, N_HEAD * VT_ROWS, s), BF16),
        jax.ShapeDtypeStruct((b, n_slab, s, LANES), F32),
        jax.ShapeDtypeStruct((b, n_slab, s, LANES), F32),
    )
    out_specs = (
        pl.BlockSpec((1, N_HEAD * LANES, tm), col3),
        pl.BlockSpec((1, tm, N_HEAD * LANES), row3),
        pl.BlockSpec((1, N_HEAD * VT_ROWS, tm), col3),
        pl.BlockSpec((1, n_slab, tm, LANES), row4),
        pl.BlockSpec((1, n_slab, tm, LANES), row4),
    )
    return pl.pallas_call(
        functools.partial(_proj_kernel, q_lora=q_lora, kv_lora=kv_lora, d_rnn=d_rnn),
        grid=(b, nt),
        in_specs=in_specs,
        out_specs=out_specs,
        out_shape=out_shape,
        compiler_params=pltpu.CompilerParams(
            dimension_semantics=("arbitrary", "arbitrary"), vmem_limit_bytes=VMEM_LIMIT),
        name="proj",
    )(x3, ln1, win, qag, wuq, kvag, wuk, wuv, cos_t, sin_t, gq, gk, hsum)


def _scan_stride(t_len):
    s = -(-t_len // SUBLANES)
    while s % SUBLANES == 0:
        s += 1
    if s % 2:
        s += 1
        if s % SUBLANES == 0:
            s += 2
    return s


def _chunk_len(stride, cap=258):
    for c in range(min(cap, stride), 0, -1):
        if stride % c == 0:
            return c
    return 1


def _rnn_kernel(xr_ref, xrm_ref, cw_ref, cb_ref, wg_ref, lam_ref,
                o_ref, t1, af, bf, ab, bb, *, seq, stride, rc, tm_out):
    n_slab = xr_ref.shape[1]
    t_len = N_META_TOK + seq
    pre = SUBLANES
    rows = t1.shape[1]
    n_chunk = stride // rc
    crow = rc * SUBLANES

    for s in range(n_slab):
        t1[s, 0:pre, :] = jnp.zeros((pre, LANES), F32)
        t1[s, pre:pre + N_META_TOK, :] = xrm_ref[0, s]
        t1[s, pre + N_META_TOK:pre + t_len, :] = xr_ref[0, s]
        t1[s, pre + t_len:rows, :] = jnp.zeros((rows - pre - t_len, LANES), F32)

    lam = lam_ref[...]
    nl = -lam
    softplus = jnp.maximum(nl, 0.0) + jnp.log(1.0 + jnp.exp(-jnp.abs(nl)))
    hcn = (0.5 * LRU_SCALE) * softplus
    cw = cw_ref[...]
    cb = cb_ref[...]
    ones_k = jnp.ones((crow, LANES), BF16)

    padded = stride * SUBLANES > t_len
    if padded:
        ridx = lax.broadcasted_iota(jnp.int32, (crow, LANES), 0)
        t_local = (ridx & (SUBLANES - 1)) * stride + (ridx >> 3)

    def gate_chunk(c, carry):
        r0 = c * rc
        row0 = pl.multiple_of(c * crow, SUBLANES)
        for s in range(n_slab):
            ls = slice(s * LANES, (s + 1) * LANES)
            taps = [t1[s, pl.ds(r0 + pre - CONV_LEFT + j, SUBLANES, stride=stride), :]
                    for j in range(rc + CONV_TAPS - 1)]
            xc_rows = []
            for rr in range(rc):
                acc = taps[rr] * cw[0:1, ls]
                for j in range(1, CONV_TAPS):
                    acc = acc + taps[rr + j] * cw[j:j + 1, ls]
                xc_rows.append(acc + cb[:, ls])
            xc = jnp.concatenate(xc_rows, axis=0)
            hx = 0.5 * xc
            z = jnp.dot(jnp.concatenate([xc.astype(BF16), ones_k], axis=1), wg_ref[s],
                        preferred_element_type=F32)
            for d, (a_ref, b_ref) in enumerate(((af, bf), (ab, bb))):
                ta = jnp.tanh(z[:, (2 * d) * LANES:(2 * d + 1) * LANES])
                ti = jnp.tanh(z[:, (2 * d + 1) * LANES:(2 * d + 2) * LANES])
                hc = hcn[d:d + 1, ls]
                nla = ta * hc + hc
                a = jnp.exp2(nla * (-LOG2E))
                one_m_a2 = jnp.tanh(nla) * (1.0 + a * a)
                root = jnp.where(one_m_a2 > 0.0, one_m_a2 * lax.rsqrt(one_m_a2), 0.0)
                bv = root * (ti * hx + hx)
                if d == 1 and padded:
                    bv = jnp.where((t_local + r0) < t_len, bv, 0.0)
                a_ref[s, pl.ds(row0, crow), :] = a
                b_ref[s, pl.ds(row0, crow), :] = bv
        return carry

    lax.fori_loop(0, n_chunk, gate_chunk, 0)

    zero = jnp.zeros((SUBLANES, LANES), F32)
    one = jnp.ones((SUBLANES, LANES), F32)

    def scan_step(r, carry):
        hf, pf, hb, pb = carry
        rf = pl.multiple_of(r * SUBLANES, SUBLANES)
        rb = pl.multiple_of((stride - 1 - r) * SUBLANES, SUBLANES)
        nhf, npf, nhb, npb = [], [], [], []
        for s in range(n_slab):
            a = af[s, pl.ds(rf, SUBLANES), :]
            h = a * hf[s] + bf[s, pl.ds(rf, SUBLANES), :]
            pr = a * pf[s]
            bf[s, pl.ds(rf, SUBLANES), :] = h
            af[s, pl.ds(rf, SUBLANES), :] = pr
            nhf.append(h)
            npf.append(pr)
            a = ab[s, pl.ds(rb, SUBLANES), :]
            h = a * hb[s] + bb[s, pl.ds(rb, SUBLANES), :]
            pr = a * pb[s]
            bb[s, pl.ds(rb, SUBLANES), :] = h
            ab[s, pl.ds(rb, SUBLANES), :] = pr
            nhb.append(h)
            npb.append(pr)
        return tuple(nhf), tuple(npf), tuple(nhb), tuple(npb)

    init = ((zero,) * n_slab, (one,) * n_slab, (zero,) * n_slab, (one,) * n_slab)
    hf, pf, hb, pb = lax.fori_loop(0, stride, scan_step, init, unroll=SCAN_UNROLL)

    sub = lax.broadcasted_iota(jnp.int32, (SUBLANES, LANES), 0)
    cf, cbk = [], []
    for s in range(n_slab):
        c = zero
        for _ in range(SUBLANES - 1):
            c = jnp.where(sub == 0, 0.0, pltpu.roll(pf[s] * c + hf[s], 1, axis=0))
        cf.append(c)
        c = zero
        for _ in range(SUBLANES - 1):
            c = jnp.where(sub == SUBLANES - 1, 0.0, pltpu.roll(pb[s] * c + hb[s], SUBLANES - 1, axis=0))
        cbk.append(c)

    def combine(r, carry):
        rp = pl.multiple_of(r * SUBLANES, SUBLANES)
        for s in range(n_slab):
            y = (bf[s, pl.ds(rp, SUBLANES), :] + af[s, pl.ds(rp, SUBLANES), :] * cf[s]
                 + bb[s, pl.ds(rp, SUBLANES), :] + ab[s, pl.ds(rp, SUBLANES), :] * cbk[s])
            t1[s, pl.ds(r + pre, SUBLANES, stride=stride), :] = y
        return carry

    lax.fori_loop(0, stride, combine, 0, unroll=COMBINE_UNROLL)

    def finish(j, carry):
        ro = pl.multiple_of(j * tm_out, SUBLANES)
        rt = pl.multiple_of(j * tm_out + pre + N_META_TOK, SUBLANES)
        for s in range(n_slab):
            o_ref[0, pl.ds(ro, tm_out), s * LANES:(s + 1) * LANES] = t1[s, pl.ds(rt, tm_out), :]
        return carry

    lax.fori_loop(0, seq // tm_out, finish, 0)


def _rnn(xr, xrm, cw, cb, wg, lam):
    b, n_slab, seq, _ = xr.shape
    t_len = N_META_TOK + seq
    stride = _scan_stride(t_len)
    rc = _chunk_len(stride)
    rows_perm = stride * SUBLANES
    rows_time = SUBLANES + rows_perm + SUBLANES
    tm_out = 512 if seq % 512 == 0 else seq
    blk = lambda i: (i, 0, 0, 0)
    in_specs = [
        pl.BlockSpec((1, n_slab, seq, LANES), blk),
        _const_spec(xrm.shape),
    ] + [_const_spec(w.shape) for w in (cw, cb, wg, lam)]
    scratch = [pltpu.VMEM((n_slab, rows_time, LANES), F32)]
    scratch += [pltpu.VMEM((n_slab, rows_perm, LANES), F32) for _ in range(4)]
    return pl.pallas_call(
        functools.partial(_rnn_kernel, seq=seq, stride=stride, rc=rc, tm_out=tm_out),
        grid=(b,),
        in_specs=in_specs,
        out_specs=pl.BlockSpec((1, seq, n_slab * LANES), lambda i: (i, 0, 0)),
        out_shape=jax.ShapeDtypeStruct((b, seq, n_slab * LANES), F32),
        scratch_shapes=scratch,
        compiler_params=pltpu.CompilerParams(
            dimension_semantics=("arbitrary",), vmem_limit_bytes=VMEM_LIMIT),
        name="rnn",
    )(xr, xrm, cw, cb, wg, lam)


def _attn_kernel(qt_ref, k_ref, vt_ref, km_ref, vtm_ref, o_ref, *, tq):
    n_key = k_ref.shape[1]

    n_stream = HEADS_PER_STEP * (n_key // tq)
    n_tile = n_key // KEY_TILE
    tiles = [(t * KEY_TILE, (t + 1) * KEY_TILE) for t in range(n_tile)]
    tiles[-1] = (tiles[-1][0], n_key + N_META_TOK)
    steps = [(i, t) for i in range(n_stream) for t in range(n_tile)]

    def score_tile(i, t):
        j, hh = divmod(i, HEADS_PER_STEP)
        k0, k1 = tiles[t]
        ls = slice(hh * LANES, (hh + 1) * LANES)
        keys = k_ref[0, k0:min(k1, n_key), ls]
        if k1 > n_key:
            keys = jnp.concatenate([keys, km_ref[0, :, ls]], axis=0)
        return jnp.dot(keys, qt_ref[0, ls, j * tq:(j + 1) * tq],
                       preferred_element_type=F32)

    outs = []
    pending = [score_tile(*steps[n]) for n in range(SCORE_LOOKAHEAD)]
    m, acc = None, None
    for n, (i, t) in enumerate(steps):
        st = pending.pop(0)
        k0, k1 = tiles[t]
        hh = i % HEADS_PER_STEP
        vs = slice(hh * VT_ROWS, (hh + 1) * VT_ROWS)
        st3 = st.reshape((k1 - k0) // SUBLANES, SUBLANES, tq)
        tmax = jnp.max(jnp.max(st3, axis=0), axis=0, keepdims=True)
        m_new = tmax if m is None else jnp.maximum(m, tmax)
        mb = jnp.broadcast_to(m_new, (SUBLANES, tq))
        pb = jnp.exp2(st3 - mb[None]).reshape(k1 - k0, tq).astype(BF16)
        if n + SCORE_LOOKAHEAD < len(steps):
            pending.append(score_tile(*steps[n + SCORE_LOOKAHEAD]))
        part = jnp.dot(vt_ref[0, vs, k0:k0 + KEY_TILE], pb[:KEY_TILE], preferred_element_type=F32)
        if k1 - k0 > KEY_TILE:
            part = part + jnp.dot(vtm_ref[0, vs, :], pb[KEY_TILE:], preferred_element_type=F32)
        acc = part if m is None else acc * jnp.exp2(m - m_new) + part
        m = m_new
        if t == n_tile - 1:
            outs.append(acc[:HEAD_V] * (1.0 / acc[HEAD_V:HEAD_V + 1]))
            m, acc = None, None
            if hh == HEADS_PER_STEP - 1:
                j = i // HEADS_PER_STEP
                o_ref[0, j * tq:(j + 1) * tq, :] = jnp.concatenate(outs, axis=0).T
                outs = []


def _attention(qt, k, vt, km, vtm, tq):
    b, s, _ = k.shape
    hps = HEADS_PER_STEP
    return pl.pallas_call(
        functools.partial(_attn_kernel, tq=tq),
        grid=(b, N_HEAD // hps),
        in_specs=[
            pl.BlockSpec((1, hps * LANES, s), lambda i, p: (i, p, 0)),
            pl.BlockSpec((1, s, hps * LANES), lambda i, p: (i, 0, p)),
            pl.BlockSpec((1, hps * VT_ROWS, s), lambda i, p: (i, p, 0)),
            pl.BlockSpec((1, N_META_TOK, hps * LANES), lambda i, p: (0, 0, p)),
            pl.BlockSpec((1, hps * VT_ROWS, N_META_TOK), lambda i, p: (0, p, 0)),
        ],
        out_specs=pl.BlockSpec((1, s, hps * HEAD_V), lambda i, p: (i, 0, p)),
        out_shape=jax.ShapeDtypeStruct((b, s, N_HEAD * HEAD_V), F32),
        compiler_params=pltpu.CompilerParams(
            dimension_semantics=("arbitrary", "arbitrary"), vmem_limit_bytes=VMEM_LIMIT),
        name="attn",
    )(qt, k, vt, km, vtm)


def _out_kernel(x_ref, oa_ref, on_ref, xg_ref, ag_ref, rg_ref, woa_ref, wor_ref, ln2_ref, wg_ref, wu_ref,
                wd_ref, o_ref, *, ff_chunk):
    oa = oa_ref[...]
    oan = (oa * _rsqrt_mean(oa, oa.shape[-1]) * ag_ref[...]).astype(BF16)
    h = x_ref[...] + jnp.dot(oan, woa_ref[...], preferred_element_type=F32)
    xg = jnp.concatenate([xg_ref[0, s] for s in range(xg_ref.shape[1])], axis=1)
    k0 = math.sqrt(2.0 / math.pi)
    yh = on_ref[...] * (0.5 * xg)
    orn = yh + yh * jnp.tanh(xg * (k0 + (k0 * 0.044715) * (xg * xg)))
    ornn = (orn * _rsqrt_mean(orn, orn.shape[-1]) * rg_ref[...]).astype(BF16)
    h = h + jnp.dot(ornn, wor_ref[...], preferred_element_type=F32)
    hn = (h * _rsqrt_mean(h, h.shape[-1]) * ln2_ref[...]).astype(BF16)
    d_ff = wg_ref.shape[1]
    acc = h
    for c in range(d_ff // ff_chunk):
        cs = slice(c * ff_chunk, (c + 1) * ff_chunk)
        gt = jnp.dot(hn, wg_ref[:, cs], preferred_element_type=F32)
        up = jnp.dot(hn, wu_ref[:, cs], preferred_element_type=F32)
        act = (gt * jax.nn.sigmoid(gt) * up).astype(BF16)
        acc = acc + jnp.dot(act, wd_ref[cs, :], preferred_element_type=F32)
    o_ref[...] = acc


def _out_ffn(x2, oa2, on2, xg, ag, rg, woa, wor, ln2, wg, wu, wd, tm):
    n, d = x2.shape
    d_ff = wg.shape[1]
    ff_chunk = d_ff
    row = lambda i: (i, 0)
    _, n_slab, seq, _ = xg.shape
    per_seq = seq // tm
    single = pl.Buffered(1)
    wspec = lambda w: pl.BlockSpec(w.shape, lambda i: (0, 0), pipeline_mode=single)
    return pl.pallas_call(
        functools.partial(_out_kernel, ff_chunk=ff_chunk),
        grid=(n // tm,),
        in_specs=[
            pl.BlockSpec((tm, d), row),
            pl.BlockSpec((tm, oa2.shape[1]), row),
            pl.BlockSpec((tm, on2.shape[1]), row),
            pl.BlockSpec((1, n_slab, tm, LANES), lambda i: (i // per_seq, 0, i % per_seq, 0)),
            wspec(ag), wspec(rg), wspec(woa), wspec(wor), wspec(ln2), wspec(wg), wspec(wu), wspec(wd),
        ],
        out_specs=pl.BlockSpec((tm, d), row),
        out_shape=jax.ShapeDtypeStruct((n, d), F32),
        compiler_params=pltpu.CompilerParams(
            dimension_semantics=("arbitrary",), vmem_limit_bytes=VMEM_LIMIT),
        name="out_ffn",
    )(x2, oa2, on2, xg, ag, rg, woa, wor, ln2, wg, wu, wd)


def _rope_tables(t_len):
    half = ROPE // 2
    freqs = 1.0 / (THETA ** (np.arange(half, dtype=np.float64) / half))
    ang = np.arange(t_len, dtype=np.float64)[:, None] * freqs[None, :]
    pad = LANES - HEAD_QK
    cos_t = np.concatenate([np.ones((t_len, NOPE)), np.cos(ang), np.cos(ang), np.zeros((t_len, pad))], -1)
    sin_t = np.concatenate([np.zeros((t_len, NOPE)), np.sin(ang), np.sin(ang), np.zeros((t_len, pad))], -1)
    return jnp.asarray(cos_t, F32), jnp.asarray(sin_t, F32)


def _rot_half_cols(w):
    half = ROPE // 2
    return jnp.concatenate([-w[..., half:], w[..., :half]], axis=-1)


def _gain_rows(g):
    half = ROPE // 2
    pad = jnp.zeros((LANES - HEAD_QK,), F32)
    r0 = jnp.concatenate([g, pad])
    r1 = jnp.concatenate([jnp.zeros((NOPE,), F32), g[NOPE + half:], g[NOPE:NOPE + half], pad])
    return jnp.stack([r0, r1])


def _block_diag_pairs(w):
    g, n, _ = w.shape
    z = jnp.zeros((g // 2, n, n), w.dtype)
    top = jnp.concatenate([w[0::2], z], axis=2)
    bot = jnp.concatenate([z, w[1::2]], axis=2)
    return jnp.concatenate([top, bot], axis=1)


def kernel(x, meta_tokens, ln1_g, w_in, q_a_norm_g, w_uq, kv_a_norm_g, w_ukv, q_norm_g, k_norm_g,
           conv_w, conv_b, lru_wa, lru_ba, lru_wi, lru_bi, lru_lambda, attn_out_g, rnn_out_g,
           w_out, ln2_g, w_gate, w_up, w_down):
    bsz, seq, d_model = x.shape
    q_lora = q_a_norm_g.shape[-1]
    kv_lora = kv_a_norm_g.shape[-1]
    d_rnn = conv_w.shape[-1]
    d_attn = N_HEAD * HEAD_V
    t_len = N_META_TOK + seq
    l = 0

    wi = w_in[l]
    o_kr = q_lora + kv_lora
    w_kr = wi[:, o_kr:o_kr + ROPE]
    win = jnp.concatenate(
        [wi[:, :o_kr], jnp.zeros((d_model, NOPE), F32), w_kr, _rot_half_cols(w_kr),
         wi[:, o_kr + ROPE:]], axis=1).astype(BF16)
    wq = w_uq[l].reshape(q_lora, N_HEAD, HEAD_QK)
    wuq = jnp.concatenate([wq, _rot_half_cols(wq[..., NOPE:])], axis=-1)
    wuq = wuq.reshape(q_lora, N_HEAD * LANES).astype(BF16)
    wkv = w_ukv[l].reshape(kv_lora, N_HEAD, NOPE + HEAD_V)
    wuk = jnp.concatenate([wkv[..., :NOPE], jnp.zeros((kv_lora, N_HEAD, LANES - NOPE), F32)], -1)
    wuk = wuk.reshape(kv_lora, N_HEAD * LANES).astype(BF16)
    wuv = wkv[..., NOPE:].reshape(kv_lora, d_attn).astype(BF16)
    gq = _gain_rows(q_norm_g[l])
    gk = _gain_rows(k_norm_g[l])
    cos_t, sin_t = _rope_tables(t_len)
    lane2 = np.arange(2 * LANES)
    hsum = jnp.asarray((lane2[:, None] // LANES == lane2[None, :] // LANES)
                       & (lane2[:, None] % LANES < HEAD_QK), BF16)
    proj_w = (ln1_g[l][None], win, q_a_norm_g[l][None], wuq, kv_a_norm_g[l][None], wuk, wuv, gq, gk, hsum)
    dims = (q_lora, kv_lora, d_rnn)

    wgate = (0.5 * jnp.concatenate(
        [_block_diag_pairs(w) for w in (lru_wa[l, 0], lru_wi[l, 0], lru_wa[l, 1], lru_wi[l, 1])],
        axis=2)).astype(BF16)
    n_slab = d_rnn // LANES
    gate_b = 0.5 * jnp.stack([lru_ba[l, 0], lru_bi[l, 0], lru_ba[l, 1], lru_bi[l, 1]])
    gate_b = gate_b.reshape(4, n_slab, LANES).transpose(1, 0, 2).reshape(n_slab, 1, 4 * LANES)
    b_hi = gate_b.astype(BF16)
    b_lo = (gate_b - b_hi.astype(F32)).astype(BF16)
    wgate = jnp.concatenate(
        [wgate, b_hi, b_lo, jnp.zeros((n_slab, LANES - 2, 4 * LANES), BF16)], axis=1)

    tm = 512 if seq % 512 == 0 else seq
    q, k, v, xr, xg = _project(x, tm, cos_t[N_META_TOK:], sin_t[N_META_TOK:], proj_w, dims)
    _, km, vm, xrm, _ = _project(meta_tokens[None].astype(x.dtype), N_META_TOK,
                                 cos_t[:N_META_TOK], sin_t[:N_META_TOK], proj_w, dims)

    o_rnn = _rnn(xr, xrm, conv_w[l], conv_b[l][None], wgate, lru_lambda[l])

    tq = 2048 if seq % 2048 == 0 else seq
    o_attn = _attention(q, k, v, km, vm, tq)

    wo = w_out[l].astype(BF16)
    n_rows = bsz * seq
    tmo = 512 if n_rows % 512 == 0 else n_rows
    out = _out_ffn(x.reshape(n_rows, d_model), o_attn.reshape(n_rows, d_attn),
                   o_rnn.reshape(n_rows, d_rnn), xg, attn_out_g[l][None], rnn_out_g[l][None],
                   wo[:d_attn], wo[d_attn:],
                   ln2_g[l][None], w_gate[l].astype(BF16), w_up[l].astype(BF16),
                   w_down[l].astype(BF16), tmo)
    return out.reshape(bsz, seq, d_model)
```
